```python
import math
import jax, jax.numpy as jnp
from jax import lax
import numpy as np

D_MODEL = 1024
BATCH = 8
SEQ = 2048
DEPTH = 1
DEC_BATCH = 128
DEC_SEQ = 1
PAST_LEN = 16384
PAGE_SIZE = 128

EPS = 1e-6
S5_WIDTH = D_MODEL
S5_GROUP = 16
S5_GROUPS = S5_WIDTH // S5_GROUP
S5_STATE = 64
S5_DT_MIN = 1e-3
S5_DT_MAX = 1e-1
M_EXPAND = 2
M_INNER = M_EXPAND * D_MODEL
M_HEADDIM = 64
M_HEADS = M_INNER // M_HEADDIM
M_GROUPS = 4
M_HPG = M_HEADS // M_GROUPS
M_STATE = 128
M_CONV = 4
M_CONV_DIM = M_INNER + 2 * M_GROUPS * M_STATE
M_CHUNK = 128
M_DT_MIN = 1e-3
M_DT_MAX = 1e-1
D_FF = -(-8 * D_MODEL // (3 * 256)) * 256
OFF_Z = S5_WIDTH
OFF_XBC = OFF_Z + M_INNER
OFF_DT = OFF_XBC + M_CONV_DIM
OFF_GA = OFF_DT + M_HEADS
OFF_GB = OFF_GA + D_MODEL
IN_COLS = OFF_GB + D_MODEL

kernel_name = "hybrid_s5_ssd_gated_decoder_step"


def _rmsnorm(x, g):
    x32 = x.astype(jnp.float32)
    r = x32 * lax.rsqrt(jnp.mean(x32 * x32, axis=-1, keepdims=True) + EPS)
    return (r * g.astype(jnp.float32)).astype(x.dtype)


def _modulate(h, shift, scale):
    return h * (1.0 + scale[:, None, :]) + shift[:, None, :]


def _cplx_affine(e1, e2):
    a1r, a1i, b1r, b1i = e1
    a2r, a2i, b2r, b2i = e2
    return (a2r * a1r - a2i * a1i,
            a2r * a1i + a2i * a1r,
            a2r * b1r - a2i * b1i + b2r,
            a2r * b1i + a2i * b1r + b2i)


def _s5_mixer(u, h0_re, h0_im, lam_re, lam_im, log_dt, b_re, b_im, c_re, c_im, d_skip, w_glu, b_glu):
    bsz, L, _ = u.shape
    u32 = u.astype(jnp.float32)
    lr = lam_re.astype(jnp.float32)
    li = lam_im.astype(jnp.float32)
    dt = jnp.exp(log_dt.astype(jnp.float32))[:, None]
    mag = jnp.exp(lr * dt)
    abar_re = mag * jnp.cos(li * dt)
    abar_im = mag * jnp.sin(li * dt)
    den = lr * lr + li * li
    nr = abar_re - 1.0
    ni = abar_im
    f_re = (nr * lr + ni * li) / den
    f_im = (ni * lr - nr * li) / den
    br = b_re.astype(jnp.float32)
    bi = b_im.astype(jnp.float32)
    bbar_re = f_re[..., None] * br - f_im[..., None] * bi
    bbar_im = f_re[..., None] * bi + f_im[..., None] * br
    ug = u32.reshape(bsz, L, S5_GROUPS, S5_GROUP)
    bu_re = jnp.einsum("blgh,gnh->blgn", ug, bbar_re)
    bu_im = jnp.einsum("blgh,gnh->blgn", ug, bbar_im)
    shp = (bsz, L, S5_GROUPS, S5_STATE)
    a_re = jnp.broadcast_to(abar_re, shp)
    a_im = jnp.broadcast_to(abar_im, shp)
    acr, aci, bcr, bci = lax.associative_scan(_cplx_affine, (a_re, a_im, bu_re, bu_im), axis=1)
    h0r = h0_re.astype(jnp.float32)[:, None]
    h0i = h0_im.astype(jnp.float32)[:, None]
    h_re = bcr + acr * h0r - aci * h0i
    h_im = bci + acr * h0i + aci * h0r
    y = (jnp.einsum("ghn,blgn->blgh", c_re.astype(jnp.float32), h_re)
         - jnp.einsum("ghn,blgn->blgh", c_im.astype(jnp.float32), h_im)).reshape(bsz, L, S5_WIDTH)
    y = (y + d_skip.astype(jnp.float32) * u32).astype(u.dtype)
    ya = jax.nn.gelu(y)
    out = ya * jax.nn.sigmoid(ya @ w_glu + b_glu)
    return out, h_re[:, -1], h_im[:, -1]


def _ssd(xs, dt, a_head, bm, cm, h0):
    bsz, L = xs.shape[0], xs.shape[1]
    q = M_CHUNK if L % M_CHUNK == 0 else L
    nc = L // q
    xdt = (xs * dt[..., None]).reshape(bsz, nc, q, M_GROUPS, M_HPG, M_HEADDIM)
    a = jnp.moveaxis((dt * a_head).reshape(bsz, nc, q, M_GROUPS, M_HPG), 2, -1)
    bc = bm.reshape(bsz, nc, q, M_GROUPS, M_STATE)
    cc = cm.reshape(bsz, nc, q, M_GROUPS, M_STATE)
    a_cs = jnp.cumsum(a, axis=-1)
    causal = jnp.tril(jnp.ones((q, q), dtype=bool))
    decay_in = jnp.exp(jnp.where(causal, a_cs[..., :, None] - a_cs[..., None, :], -jnp.inf))
    cb = jnp.einsum("bclgn,bcsgn->bcgls", cc, bc)
    y_diag = jnp.einsum("bcgls,bcgrls,bcsgrp->bclgrp", cb, decay_in, xdt)
    decay_to_end = jnp.exp(a_cs[..., -1:] - a_cs)
    chunk_states = jnp.einsum("bclgn,bcgrl,bclgrp->bcgrpn", bc, decay_to_end, xdt)
    chunk_decay = jnp.exp(a_cs[..., -1])

    def step(h, inp):
        dec, st = inp
        return dec[..., None, None] * h + st, h

    h_final, h_enter = lax.scan(step, h0, (jnp.moveaxis(chunk_decay, 1, 0), jnp.moveaxis(chunk_states, 1, 0)))
    h_enter = jnp.moveaxis(h_enter, 0, 1)
    y_off = jnp.einsum("bclgn,bcgrpn,bcgrl->bclgrp", cc, h_enter, jnp.exp(a_cs))
    y = (y_diag + y_off).reshape(bsz, L, M_GROUPS, M_HPG, M_HEADDIM)
    return y, h_final


def _ssd_mixer(z, xbc, dt_raw, conv_buf, h0, conv_w, conv_b, dt_bias, a_log, d_skip, norm_w):
    bsz, L, _ = xbc.shape
    xbc_full = jnp.concatenate([conv_buf.astype(xbc.dtype), xbc], axis=1)
    conv = conv_b + sum(xbc_full[:, k:k + L] * conv_w[k] for k in range(M_CONV))
    new_buf = xbc_full[:, -(M_CONV - 1):]
    act = jax.nn.silu(conv).astype(jnp.float32)
    xs = act[..., :M_INNER].reshape(bsz, L, M_GROUPS, M_HPG, M_HEADDIM)
    bm = act[..., M_INNER:M_INNER + M_GROUPS * M_STATE].reshape(bsz, L, M_GROUPS, M_STATE)
    cm = act[..., M_INNER + M_GROUPS * M_STATE:].reshape(bsz, L, M_GROUPS, M_STATE)
    dt = jax.nn.softplus(dt_raw.astype(jnp.float32) + dt_bias.astype(jnp.float32)).reshape(bsz, L, M_GROUPS, M_HPG)
    a_head = -jnp.exp(a_log.astype(jnp.float32)).reshape(M_GROUPS, M_HPG)
    h0g = h0.astype(jnp.float32).reshape(bsz, M_GROUPS, M_HPG, M_HEADDIM, M_STATE)
    y, h_final = _ssd(xs, dt, a_head, bm, cm, h0g)
    y = y + d_skip.astype(jnp.float32).reshape(M_GROUPS, M_HPG)[:, :, None] * xs
    y = y.reshape(bsz, L, M_INNER).astype(z.dtype)
    y = _rmsnorm(y * jax.nn.silu(z), norm_w)
    return y, h_final.reshape(bsz, M_HEADS, M_HEADDIM, M_STATE), new_buf


def _layer(x, c, h5_re, h5_im, h_ssm, conv_buf, p):
    mod = jax.nn.silu(c) @ p["w_ada"] + p["b_ada"]
    sh_m, sc_m, gt_m, sh_f, sc_f, gt_f = jnp.split(mod, 6, axis=-1)
    h = _modulate(_rmsnorm(x, p["norm_pre_mix"]), sh_m, sc_m)
    proj = h @ p["w_in"]
    u = proj[..., :OFF_Z]
    z = proj[..., OFF_Z:OFF_XBC]
    xbc = proj[..., OFF_XBC:OFF_DT]
    dt_raw = proj[..., OFF_DT:OFF_GA]
    g_a = jax.nn.sigmoid(proj[..., OFF_GA:OFF_GB])
    g_b = jax.nn.sigmoid(proj[..., OFF_GB:])
    ya, h5_re_new, h5_im_new = _s5_mixer(u, h5_re, h5_im, p["s5_lam_re"], p["s5_lam_im"], p["s5_log_dt"],
                                         p["s5_b_re"], p["s5_b_im"], p["s5_c_re"], p["s5_c_im"],
                                         p["s5_d"], p["s5_w_glu"], p["s5_b_glu"])
    yb, h_ssm_new, conv_new = _ssd_mixer(z, xbc, dt_raw, conv_buf, h_ssm, p["m_conv_w"], p["m_conv_b"],
                                         p["m_dt_bias"], p["m_a_log"], p["m_d"], p["m_norm"])
    merged = g_a * (ya @ p["w_branch_s5"]) + g_b * (yb @ p["w_branch_ssd"])
    mix = merged @ p["w_out"]
    x = x + gt_m[:, None, :] * _rmsnorm(mix, p["norm_post_mix"])
    h = _modulate(_rmsnorm(x, p["norm_pre_ffn"]), sh_f, sc_f)
    gu = h @ p["w_ffn_in"]
    f = (jax.nn.silu(gu[..., :D_FF]) * gu[..., D_FF:]) @ p["w_ffn_out"]
    x = x + gt_f[:, None, :] * _rmsnorm(f, p["norm_post_ffn"])
    return x, h5_re_new.astype(x.dtype), h5_im_new.astype(x.dtype), h_ssm_new.astype(x.dtype), conv_new.astype(x.dtype)


def setup_inputs(seed: int = 0) -> dict:
    key = jax.random.key(seed)
    ks = jax.random.split(key, 40)
    f32 = jnp.float32
    nrm = lambda k, shape, s: jax.random.normal(k, shape, f32) * s
    D = DEPTH
    n_idx = jnp.arange(S5_STATE, dtype=f32)
    dt_m = jnp.exp(jax.random.uniform(ks[30], (D, M_HEADS), f32, math.log(M_DT_MIN), math.log(M_DT_MAX)))
    return {
        "x_prompt": nrm(ks[0], (BATCH, SEQ, D_MODEL), 1.0),
        "x_sample": nrm(ks[1], (DEC_BATCH, DEC_SEQ, D_MODEL), 1.0),
        "state_s5_re": nrm(ks[2], (D, DEC_BATCH, S5_GROUPS, S5_STATE), 0.3),
        "state_s5_im": nrm(ks[3], (D, DEC_BATCH, S5_GROUPS, S5_STATE), 0.3),
        "state_ssm": nrm(ks[4], (D, DEC_BATCH, M_HEADS, M_HEADDIM, M_STATE), 0.1),
        "state_conv": nrm(ks[5], (D, DEC_BATCH, M_CONV - 1, M_CONV_DIM), 1.0),
        "c_prompt": nrm(ks[6], (BATCH, D_MODEL), 1.0),
        "c_sample": nrm(ks[7], (DEC_BATCH, D_MODEL), 1.0),
        "w_ada": nrm(ks[8], (D, D_MODEL, 6 * D_MODEL), 0.5 * D_MODEL ** -0.5),
        "b_ada": nrm(ks[9], (D, 6 * D_MODEL), 0.01),
        "norm_pre_mix": 1.0 + nrm(ks[10], (D, D_MODEL), 0.01),
        "norm_post_mix": 1.0 + nrm(ks[11], (D, D_MODEL), 0.01),
        "norm_pre_ffn": 1.0 + nrm(ks[12], (D, D_MODEL), 0.01),
        "norm_post_ffn": 1.0 + nrm(ks[13], (D, D_MODEL), 0.01),
        "w_in": nrm(ks[14], (D, D_MODEL, IN_COLS), D_MODEL ** -0.5),
        "s5_lam_re": -0.5 + nrm(ks[15], (D, S5_GROUPS, S5_STATE), 0.01),
        "s5_lam_im": math.pi * n_idx + nrm(ks[16], (D, S5_GROUPS, S5_STATE), 0.01),
        "s5_log_dt": jax.random.uniform(ks[17], (D, S5_GROUPS), f32, math.log(S5_DT_MIN), math.log(S5_DT_MAX)),
        "s5_b_re": nrm(ks[18], (D, S5_GROUPS, S5_STATE, S5_GROUP), (2 * S5_GROUP) ** -0.5),
        "s5_b_im": nrm(ks[19], (D, S5_GROUPS, S5_STATE, S5_GROUP), (2 * S5_GROUP) ** -0.5),
        "s5_c_re": nrm(ks[20], (D, S5_GROUPS, S5_GROUP, S5_STATE), (2 * S5_STATE) ** -0.5),
        "s5_c_im": nrm(ks[21], (D, S5_GROUPS, S5_GROUP, S5_STATE), (2 * S5_STATE) ** -0.5),
        "s5_d": nrm(ks[22], (D, S5_WIDTH), 1.0),
        "s5_w_glu": nrm(ks[23], (D, S5_WIDTH, S5_WIDTH), S5_WIDTH ** -0.5),
        "s5_b_glu": nrm(ks[24], (D, S5_WIDTH), 0.01),
        "m_conv_w": nrm(ks[25], (D, M_CONV, M_CONV_DIM), M_CONV ** -0.5),
        "m_conv_b": nrm(ks[26], (D, M_CONV_DIM), 0.01),
        "m_dt_bias": dt_m + jnp.log(-jnp.expm1(-dt_m)),
        "m_a_log": jnp.log(jax.random.uniform(ks[27], (D, M_HEADS), f32, 1.0, 16.0)),
        "m_d": 1.0 + nrm(ks[28], (D, M_HEADS), 0.1),
        "m_norm": 1.0 + nrm(ks[29], (D, M_INNER), 0.01),
        "w_branch_s5": nrm(ks[31], (D, S5_WIDTH, D_MODEL), S5_WIDTH ** -0.5),
        "w_branch_ssd": nrm(ks[32], (D, M_INNER, D_MODEL), M_INNER ** -0.5),
        "w_out": nrm(ks[33], (D, D_MODEL, D_MODEL), D_MODEL ** -0.5),
        "w_ffn_in": nrm(ks[34], (D, D_MODEL, 2 * D_FF), D_MODEL ** -0.5),
        "w_ffn_out": nrm(ks[35], (D, D_FF, D_MODEL), D_FF ** -0.5),
    }


def reference(x_prompt, x_sample, state_s5_re, state_s5_im, state_ssm, state_conv, c_prompt, c_sample,
              w_ada, b_ada, norm_pre_mix, norm_post_mix, norm_pre_ffn, norm_post_ffn, w_in,
              s5_lam_re, s5_lam_im, s5_log_dt, s5_b_re, s5_b_im, s5_c_re, s5_c_im, s5_d, s5_w_glu, s5_b_glu,
              m_conv_w, m_conv_b, m_dt_bias, m_a_log, m_d, m_norm,
              w_branch_s5, w_branch_ssd, w_out, w_ffn_in, w_ffn_out):
    bp = x_prompt.shape[0]
    dt_ = x_prompt.dtype
    xp, xs = x_prompt, x_sample
    p_re, p_im, p_ssm, p_conv = [], [], [], []
    s_re, s_im, s_ssm, s_conv = [], [], [], []
    for l in range(DEPTH):
        p = {
            "w_ada": w_ada[l], "b_ada": b_ada[l],
            "norm_pre_mix": norm_pre_mix[l], "norm_post_mix": norm_post_mix[l],
            "norm_pre_ffn": norm_pre_ffn[l], "norm_post_ffn": norm_post_ffn[l],
            "w_in": w_in[l],
            "s5_lam_re": s5_lam_re[l], "s5_lam_im": s5_lam_im[l], "s5_log_dt": s5_log_dt[l],
            "s5_b_re": s5_b_re[l], "s5_b_im": s5_b_im[l], "s5_c_re": s5_c_re[l], "s5_c_im": s5_c_im[l],
            "s5_d": s5_d[l], "s5_w_glu": s5_w_glu[l], "s5_b_glu": s5_b_glu[l],
            "m_conv_w": m_conv_w[l], "m_conv_b": m_conv_b[l], "m_dt_bias": m_dt_bias[l],
            "m_a_log": m_a_log[l], "m_d": m_d[l], "m_norm": m_norm[l],
            "w_branch_s5": w_branch_s5[l], "w_branch_ssd": w_branch_ssd[l], "w_out": w_out[l],
            "w_ffn_in": w_ffn_in[l], "w_ffn_out": w_ffn_out[l],
        }
        xp, a1, a2, a3, a4 = _layer(
            xp, c_prompt,
            jnp.zeros((bp, S5_GROUPS, S5_STATE), dt_), jnp.zeros((bp, S5_GROUPS, S5_STATE), dt_),
            jnp.zeros((bp, M_HEADS, M_HEADDIM, M_STATE), dt_), jnp.zeros((bp, M_CONV - 1, M_CONV_DIM), dt_), p)
        p_re.append(a1); p_im.append(a2); p_ssm.append(a3); p_conv.append(a4)
        xs, b1, b2, b3, b4 = _layer(xs, c_sample, state_s5_re[l], state_s5_im[l], state_ssm[l], state_conv[l], p)
        s_re.append(b1); s_im.append(b2); s_ssm.append(b3); s_conv.append(b4)
    y_prompt = xp
    y_sample = xs
    s5_re_prompt = jnp.stack(p_re, 0)
    s5_im_prompt = jnp.stack(p_im, 0)
    ssm_prompt = jnp.stack(p_ssm, 0)
    conv_prompt = jnp.stack(p_conv, 0)
    s5_re_sample = jnp.stack(s_re, 0)
    s5_im_sample = jnp.stack(s_im, 0)
    ssm_sample = jnp.stack(s_ssm, 0)
    conv_sample = jnp.stack(s_conv, 0)
    return (y_prompt, y_sample, s5_re_prompt, s5_im_prompt, ssm_prompt, conv_prompt,
            s5_re_sample, s5_im_sample, ssm_sample, conv_sample)
```

```python
import functools

import jax
import jax.numpy as jnp
from jax import lax
from jax.experimental import pallas as pl
from jax.experimental.pallas import tpu as pltpu

F32 = jnp.float32
BF16 = jnp.bfloat16

D_MODEL = 1024
EPS = 1e-6
S5_GROUP = 16
S5_GROUPS = 64
S5_STATE = 64
M_INNER = 2048
M_HEADDIM = 64
M_HEADS = 32
M_GROUPS = 4
M_HPG = 8
M_STATE = 128
M_CONV = 4
M_CONV_DIM = 3072
M_CHUNK = 128
D_FF = 2816
OFF_Z = D_MODEL
OFF_XBC = OFF_Z + M_INNER
OFF_DT = OFF_XBC + M_CONV_DIM
OFF_GA = OFF_DT + M_HEADS
IN_COLS = OFF_GA + 2 * D_MODEL

LANES = 128
SUBLANES = 8
VMEM_LIMIT = 56 * 1024 * 1024

S5_CH = 128
S5_NS = 512
S5_CHUNKS = D_MODEL // S5_CH
S5_TIME_TILE = 256
ROW_TILE = 256

_NT = (((1,), (1,)), ((), ()))
_TN = (((0,), (0,)), ((), ()))


def _const_spec(shape):
    nd = len(shape)
    return pl.BlockSpec(shape, lambda *_: (0,) * nd, pipeline_mode=pl.Buffered(1))


def _params(sem):
    return pltpu.CompilerParams(dimension_semantics=sem, vmem_limit_bytes=VMEM_LIMIT)


def _sigmoid(x):
    return 1.0 / (1.0 + jnp.exp(-x))


def _silu(x):
    return x * _sigmoid(x)


def _softplus(x):
    return jnp.maximum(x, 0.0) + jnp.log1p(jnp.exp(-jnp.abs(x)))


def _gelu_tanh(x):
    return 0.5 * x * (1.0 + jnp.tanh(0.7978845608028654 * (x + 0.044715 * (x * x * x))))


def _rms(x, g):
    return x * lax.rsqrt(jnp.mean(x * x, axis=-1, keepdims=True) + EPS) * g


def _dot(a, b):
    return jnp.dot(a, b, preferred_element_type=F32)


def _pieces3(x):
    hi = x.astype(BF16).astype(F32)
    r1 = x - hi
    mid = r1.astype(BF16).astype(F32)
    lo = (r1 - mid).astype(BF16).astype(F32)
    return hi, mid, lo


def _split3(x):
    return jnp.concatenate(_pieces3(x), axis=-1).astype(BF16)


def _ada_kernel(c_ref, w_ref, b_ref, o_ref):
    s = _silu(c_ref[...]).astype(BF16)
    o_ref[...] = _dot(s, w_ref[...].astype(BF16)) + b_ref[...]


def _ada(c, w, b):
    n = c.shape[0]
    tn = 512
    return pl.pallas_call(
        _ada_kernel,
        grid=(6 * D_MODEL // tn,),
        in_specs=[_const_spec((n, D_MODEL)),
                  pl.BlockSpec((D_MODEL, tn), lambda j: (0, j)),
                  pl.BlockSpec((1, tn), lambda j: (0, j))],
        out_specs=pl.BlockSpec((n, tn), lambda j: (0, j)),
        out_shape=jax.ShapeDtypeStruct((n, 6 * D_MODEL), F32),
        compiler_params=_params(("arbitrary",)),
        name="ada",
    )(c, w, b.reshape(1, -1))


def _s5_disc_kernel(lr_ref, li_ref, ldt_ref, br_ref, bi_ref, ar_ref, ai_ref, bbr_ref, bbi_ref):
    lr = lr_ref[...]
    li = li_ref[...]
    dt = jnp.exp(ldt_ref[...])
    mag = jnp.exp(lr * dt)
    ar = mag * jnp.cos(li * dt)
    ai = mag * jnp.sin(li * dt)
    den = lr * lr + li * li
    nr = ar - 1.0
    f_re = (nr * lr + ai * li) / den
    f_im = (ai * lr - nr * li) / den
    br = br_ref[...]
    bi = bi_ref[...]
    ar_ref[...] = ar
    ai_ref[...] = ai
    bbr_ref[...] = f_re * br - f_im * bi
    bbi_ref[...] = f_re * bi + f_im * br


def _s5_disc(lam_re, lam_im, log_dt, b_re, b_im):
    g, n = lam_re.shape
    w = S5_GROUP * n
    tile = lambda a: jnp.tile(a, (1, S5_GROUP))
    ldt = jnp.broadcast_to(log_dt[:, None], (g, w))
    brt = jnp.swapaxes(b_re, 1, 2).reshape(g, w)
    bit = jnp.swapaxes(b_im, 1, 2).reshape(g, w)
    shp = jax.ShapeDtypeStruct((g, w), F32)
    ar, ai, bbr, bbi = pl.pallas_call(
        _s5_disc_kernel,
        out_shape=(shp, shp, shp, shp),
        name="s5_disc",
    )(tile(lam_re), tile(lam_im), ldt, brt, bit)
    return (ar[:, :n], ai[:, :n], bbr.reshape(g, S5_GROUP, n), bbi.reshape(g, S5_GROUP, n))


def _s5_matrices(abar_re, abar_im, bbar_re, bbar_im, c_re, c_im, d_skip):
    gl = S5_CH // S5_GROUP
    eye = jnp.eye(gl, dtype=bool)

    def bmat(bb):
        t = bb.reshape(S5_CHUNKS, gl, S5_GROUP, 1, S5_STATE)
        t = jnp.where(eye[None, :, None, :, None], t, 0.0)
        return t.reshape(S5_CHUNKS, S5_CH, S5_NS)

    def cmat(cc):
        t = jnp.swapaxes(cc, 1, 2).reshape(S5_CHUNKS, gl, S5_STATE, 1, S5_GROUP)
        t = jnp.where(eye[None, :, None, :, None], t, 0.0)
        return t.reshape(S5_CHUNKS, S5_NS, S5_CH)

    bm = jnp.concatenate([bmat(bbar_re), bmat(bbar_im)], axis=2).astype(BF16)
    cm = jnp.concatenate([cmat(c_re), cmat(-c_im)], axis=1).astype(BF16)
    a = jnp.concatenate([abar_re.reshape(S5_CHUNKS, 1, S5_NS), abar_im.reshape(S5_CHUNKS, 1, S5_NS)], axis=2)
    d = d_skip.reshape(S5_CHUNKS, 1, S5_CH)
    return bm, a, cm, d


def _inproj_kernel(x_ref, sh_ref, sc_ref, g_ref, wu_ref, wz_ref, wx_ref, wdt_ref, wg_ref,
                   u_ref, z_ref, xbc_ref, dt_ref, gate_ref):
    h = _rms(x_ref[...], g_ref[...]) * (1.0 + sc_ref[...]) + sh_ref[...]
    hb = h.astype(BF16)
    step = 512

    def mm(o_ref, w_ref, fn=None):
        n = o_ref.shape[-1]
        for j in range(0, n, step):
            w = min(step, n - j)
            r = _dot(hb, w_ref[:, j:j + w])
            o_ref[:, j:j + w] = r if fn is None else fn(r)

    mm(u_ref, wu_ref)
    mm(z_ref, wz_ref)
    mm(xbc_ref, wx_ref)
    mm(dt_ref, wdt_ref)
    mm(gate_ref, wg_ref, _sigmoid)


def _mod_specs(per_row, tm, rows_per_mod, cols):
    if per_row:
        return [pl.BlockSpec((tm, D_MODEL), functools.partial(lambda i, c: (i, c), c=c)) for c in cols]
    return [pl.BlockSpec((None, 1, D_MODEL), functools.partial(lambda i, c: (i // rows_per_mod, 0, c), c=c))
            for c in cols]


def _in_proj(x, mod, g, ws, *, per_row, tm, rows_per_batch):
    t = x.shape[0]
    wu, wz, wx, wdt, wg = ws
    row = lambda n: pl.BlockSpec((tm, n), lambda i: (i, 0))
    outs = (D_MODEL, M_INNER, M_CONV_DIM, LANES, 2 * D_MODEL)
    return pl.pallas_call(
        _inproj_kernel,
        grid=(t // tm,),
        in_specs=[row(D_MODEL)] + _mod_specs(per_row, tm, rows_per_batch // tm if not per_row else 1, (0, 1))
                 + [_const_spec((1, D_MODEL))] + [_const_spec(w.shape) for w in ws],
        out_specs=[row(n) for n in outs],
        out_shape=[jax.ShapeDtypeStruct((t, n), F32) for n in outs],
        compiler_params=_params(("parallel",)),
        name="in_proj",
    )(x, mod, mod, g, wu, wz, wx, wdt, wg)


def _s5_kernel(u_ref, h0r_ref, h0i_ref, bm_ref, a_ref, cm_ref, d_ref,
               y_ref, hr_ref, hi_ref, us_ref, bu_ref, hst_ref, *, nb, tl):
    tb = pl.program_id(1)

    @pl.when(tb == 0)
    def _():
        hst_ref[:, :S5_NS] = h0r_ref[...]
        hst_ref[:, S5_NS:] = h0i_ref[...]

    def gather(t, c):
        us_ref[pl.ds(pl.multiple_of(t * nb, nb), nb), :] = u_ref[:, t, :]
        return c
    lax.fori_loop(0, tl, gather, 0)

    bu_ref[...] = _dot(us_ref[...].astype(BF16), bm_ref[...])

    ar = jnp.broadcast_to(a_ref[:, :S5_NS], (nb, S5_NS))
    ai = jnp.broadcast_to(a_ref[:, S5_NS:], (nb, S5_NS))

    def step(t, carry):
        hr, hi = carry
        rows = pl.ds(pl.multiple_of(t * nb, nb), nb)
        nhr = ar * hr - ai * hi + bu_ref[rows, :S5_NS]
        nhi = ar * hi + ai * hr + bu_ref[rows, S5_NS:]
        bu_ref[rows, :S5_NS] = nhr
        bu_ref[rows, S5_NS:] = nhi
        return nhr, nhi

    hr, hi = lax.fori_loop(0, tl, step, (hst_ref[:, :S5_NS], hst_ref[:, S5_NS:]))
    hst_ref[:, :S5_NS] = hr
    hst_ref[:, S5_NS:] = hi

    us_ref[...] = _dot(bu_ref[...].astype(BF16), cm_ref[...]) + d_ref[...] * us_ref[...]

    def scatter(t, c):
        y_ref[:, t, :] = us_ref[pl.ds(pl.multiple_of(t * nb, nb), nb), :]
        return c
    lax.fori_loop(0, tl, scatter, 0)

    @pl.when(tb == pl.num_programs(1) - 1)
    def _():
        hr_ref[...] = hr
        hi_ref[...] = hi


def _s5(u, h0r, h0i, bm, a, cm, d, *, tl):
    nb, l, _ = u.shape
    chunk3 = lambda s: pl.BlockSpec((None,) + s, lambda c, t: (c, 0, 0))
    st_spec = pl.BlockSpec((nb, S5_NS), lambda c, t: (0, c))
    return pl.pallas_call(
        functools.partial(_s5_kernel, nb=nb, tl=tl),
        grid=(S5_CHUNKS, l // tl),
        in_specs=[pl.BlockSpec((nb, tl, S5_CH), lambda c, t: (0, t, c)), st_spec, st_spec,
                  chunk3((S5_CH, 2 * S5_NS)), chunk3((1, 2 * S5_NS)), chunk3((2 * S5_NS, S5_CH)), chunk3((1, S5_CH))],
        out_specs=[pl.BlockSpec((nb, tl, S5_CH), lambda c, t: (0, t, c)), st_spec, st_spec],
        out_shape=[jax.ShapeDtypeStruct(u.shape, F32),
                   jax.ShapeDtypeStruct(h0r.shape, F32), jax.ShapeDtypeStruct(h0i.shape, F32)],
        scratch_shapes=[pltpu.VMEM((tl * nb, S5_CH), F32), pltpu.VMEM((tl * nb, 2 * S5_NS), F32),
                        pltpu.VMEM((nb, 2 * S5_NS), F32)],
        compiler_params=_params(("parallel", "arbitrary")),
        name="s5",
    )(u, h0r, h0i, bm, a, cm, d)


def _expand_mats():
    r = lax.broadcasted_iota(jnp.int32, (3 * LANES, M_HEADS * M_HEADDIM), 0) % LANES
    c = lax.broadcasted_iota(jnp.int32, (3 * LANES, M_HEADS * M_HEADDIM), 1) // M_HEADDIM
    e64 = (r == c).astype(BF16)
    r = lax.broadcasted_iota(jnp.int32, (3 * LANES, M_HEADS * M_CHUNK), 0) % LANES
    c = lax.broadcasted_iota(jnp.int32, (3 * LANES, M_HEADS * M_CHUNK), 1) // M_CHUNK
    e128 = (r == c).astype(BF16)
    return e64, e128


def _ssd_kernel(z_ref, xbc_ref, dt_ref, cw_ref, cb_ref, dtb_ref, alog_ref, dsk_ref, nw_ref, e64_ref, e128_ref,
                y_ref, hout_ref, convout_ref,
                xf_ref, act_ref, h_ref, xdt_ref, xdte_ref, eacs_ref, rall_ref, yacc_ref):
    c = pl.program_id(1)
    last = pl.num_programs(1) - 1
    q = M_CHUNK

    @pl.when(c == 0)
    def _():
        xf_ref[0:SUBLANES, :] = jnp.zeros((SUBLANES, M_CONV_DIM), F32)
        h_ref[...] = jnp.zeros_like(h_ref)

    xf_ref[SUBLANES:SUBLANES + q, :] = xbc_ref[...]
    cs = 512
    for j in range(0, M_CONV_DIM, cs):
        sl = slice(j, j + cs)
        conv = cb_ref[:, sl]
        for k in range(M_CONV):
            o = SUBLANES - (M_CONV - 1) + k
            conv = conv + xf_ref[o:o + q, sl] * cw_ref[k:k + 1, sl]
        act_ref[:, sl] = _silu(conv)

    @pl.when(c == last)
    def _():
        convout_ref[...] = xf_ref[SUBLANES + q - (M_CONV - 1):SUBLANES + q, :]

    xf_ref[0:SUBLANES, :] = xf_ref[q:q + SUBLANES, :]

    dt = _softplus(dt_ref[...] + dtb_ref[...])
    a = dt * (-jnp.exp(alog_ref[...]))
    ri = lax.broadcasted_iota(jnp.int32, (q, q), 0)
    ci = lax.broadcasted_iota(jnp.int32, (q, q), 1)
    causal = ri >= ci
    acs = jnp.dot(causal.astype(F32), a, preferred_element_type=F32, precision=lax.Precision.HIGHEST)
    acs_t = acs.T
    acs3 = _split3(acs)
    dt3 = _split3(dt)

    for j in range(0, M_INNER, cs):
        sl = slice(j, j + cs)
        dt_e = _dot(dt3, e64_ref[:, sl])
        acs_e = _dot(acs3, e64_ref[:, sl])
        xdt = act_ref[:, sl] * dt_e
        xdt_ref[:, sl] = xdt.astype(BF16)
        xdte_ref[:, sl] = (xdt * jnp.exp(acs_e[q - 1:q, :] - acs_e)).astype(BF16)
        eacs_ref[:, sl] = jnp.exp(acs_e)
    for j in range(0, M_HEADS * q, cs):
        sl = slice(j, j + cs)
        rall_ref[:, sl] = _dot(acs3, e128_ref[:, sl])

    hp = M_HPG * M_HEADDIM
    for g in range(M_GROUPS):
        gs = slice(g * hp, (g + 1) * hp)
        bm = act_ref[:, M_INNER + g * M_STATE:M_INNER + (g + 1) * M_STATE].astype(BF16)
        cm = act_ref[:, M_INNER + (M_GROUPS + g) * M_STATE:M_INNER + (M_GROUPS + g + 1) * M_STATE].astype(BF16)
        cbm = lax.dot_general(cm, bm, _NT, preferred_element_type=F32)
        y_off = lax.dot_general(cm, h_ref[gs, :].astype(BF16), _NT, preferred_element_type=F32)
        st = lax.dot_general(xdte_ref[:, gs], bm, _TN, preferred_element_type=F32)
        yd = []
        for r in range(M_HPG):
            h = g * M_HPG + r
            rh = rall_ref[:, h * q:(h + 1) * q]
            dec = jnp.where(causal, jnp.exp(rh - acs_t[h:h + 1, :]), 0.0)
            m = (cbm * dec).astype(BF16)
            yd.append(_dot(m, xdt_ref[:, h * M_HEADDIM:(h + 1) * M_HEADDIM]))
            hs = slice(h * M_HEADDIM, (h + 1) * M_HEADDIM)
            h_ref[hs, :] = jnp.exp(rh[q - 1:q, :]) * h_ref[hs, :] + st[r * M_HEADDIM:(r + 1) * M_HEADDIM, :]
        yacc_ref[:, gs] = jnp.concatenate(yd, axis=1) + y_off * eacs_ref[:, gs] + dsk_ref[:, gs] * act_ref[:, gs]

    ss = jnp.zeros((q, 1), F32)
    for j in range(0, M_INNER, cs):
        sl = slice(j, j + cs)
        gt = yacc_ref[:, sl] * _silu(z_ref[:, sl])
        yacc_ref[:, sl] = gt
        ss = ss + jnp.sum(gt * gt, axis=-1, keepdims=True)
    inv = lax.rsqrt(ss * (1.0 / M_INNER) + EPS)
    for j in range(0, M_INNER, cs):
        sl = slice(j, j + cs)
        y_ref[:, sl] = yacc_ref[:, sl] * inv * nw_ref[:, sl]

    @pl.when(c == last)
    def _():
        hout_ref[...] = h_ref[...]


def _ssd(z, xbc, dtr, cw, cb, dtb, alog, dsk, nw, e64, e128, *, nbatch, seq):
    nc = seq // M_CHUNK
    q = M_CHUNK
    row = lambda n: pl.BlockSpec((q, n), lambda b, c: (b * nc + c, 0))
    return pl.pallas_call(
        _ssd_kernel,
        grid=(nbatch, nc),
        in_specs=[row(M_INNER), row(M_CONV_DIM), row(LANES),
                  _const_spec(cw.shape), _const_spec(cb.shape), _const_spec(dtb.shape), _const_spec(alog.shape),
                  _const_spec(dsk.shape), _const_spec(nw.shape), _const_spec(e64.shape), _const_spec(e128.shape)],
        out_specs=[row(M_INNER),
                   pl.BlockSpec((None, M_INNER, M_STATE), lambda b, c: (b, 0, 0)),
                   pl.BlockSpec((None, M_CONV - 1, M_CONV_DIM), lambda b, c: (b, 0, 0))],
        out_shape=[jax.ShapeDtypeStruct((nbatch * seq, M_INNER), F32),
                   jax.ShapeDtypeStruct((nbatch, M_INNER, M_STATE), F32),
                   jax.ShapeDtypeStruct((nbatch, M_CONV - 1, M_CONV_DIM), F32)],
        scratch_shapes=[pltpu.VMEM((q + 2 * SUBLANES, M_CONV_DIM), F32), pltpu.VMEM((q, M_CONV_DIM), F32),
                        pltpu.VMEM((M_INNER, M_STATE), F32), pltpu.VMEM((q, M_INNER), BF16),
                        pltpu.VMEM((q, M_INNER), BF16), pltpu.VMEM((q, M_INNER), F32),
                        pltpu.VMEM((q, M_HEADS * q), F32), pltpu.VMEM((q, M_INNER), F32)],
        compiler_params=_params(("parallel", "arbitrary")),
        name="ssd",
    )(z, xbc, dtr, cw, cb, dtb, alog, dsk, nw, e64, e128)


SSD_STEP_BB = 8


def _ssd_step_kernel(z_ref, xbc_ref, dt_ref, cbuf_ref, h0_ref, cw_ref, cb_ref, dtb_ref, alog_ref, dsk_ref, nw_ref,
                     e64_ref, y_ref, hout_ref, convout_ref):
    bb = SSD_STEP_BB
    x_new = xbc_ref[...]
    conv = cb_ref[...] + x_new * cw_ref[M_CONV - 1:M_CONV, :]
    for k in range(M_CONV - 1):
        conv = conv + cbuf_ref[:, k, :] * cw_ref[k:k + 1, :]
    act = _silu(conv)
    for k in range(M_CONV - 2):
        convout_ref[:, k, :] = cbuf_ref[:, k + 1, :]
    convout_ref[:, M_CONV - 2, :] = x_new

    xs = act[:, :M_INNER]
    bmat = act[:, M_INNER:M_INNER + M_GROUPS * M_STATE]
    cmat = act[:, M_INNER + M_GROUPS * M_STATE:]
    dt = _softplus(dt_ref[...] + dtb_ref[...])
    da = jnp.exp(dt * (-jnp.exp(alog_ref[...])))
    dt_e = _dot(_split3(dt), e64_ref[...])
    da_e = _dot(_split3(da), e64_ref[...])
    xdt = xs * dt_e

    hp = M_HPG * M_HEADDIM
    lane_group = lax.broadcasted_iota(jnp.int32, (M_GROUPS, M_INNER), 1) // hp
    row_group = lax.broadcasted_iota(jnp.int32, (M_GROUPS, M_INNER), 0)
    gmask = lane_group == row_group
    ones_rows = lax.broadcasted_iota(jnp.int32, (SUBLANES, M_STATE), 0) >= M_GROUPS

    cbx = []
    for g in range(M_GROUPS):
        cbg = jnp.sum(cmat[:, g * M_STATE:(g + 1) * M_STATE] * bmat[:, g * M_STATE:(g + 1) * M_STATE],
                      axis=-1, keepdims=True)
        cbx.append(jnp.broadcast_to(cbg, (bb, hp)))
    y = jnp.concatenate(cbx, axis=1) * xdt + dsk_ref[...] * xs

    y_off_rows = []
    for b in range(bb):
        xsel = jnp.where(gmask, jnp.broadcast_to(xdt[b:b + 1, :], (M_GROUPS, M_INNER)), 0.0)
        lhs = jnp.concatenate([xsel, *_pieces3(da_e[b:b + 1, :]), jnp.zeros((1, M_INNER), F32)],
                              axis=0).astype(BF16)
        bm4 = jnp.concatenate([bmat[b:b + 1, g * M_STATE:(g + 1) * M_STATE] for g in range(M_GROUPS)], axis=0)
        cm4 = jnp.concatenate([cmat[b:b + 1, g * M_STATE:(g + 1) * M_STATE] for g in range(M_GROUPS)], axis=0)
        zeros4 = jnp.zeros((SUBLANES - M_GROUPS, M_STATE), F32)
        rhs = jnp.concatenate([jnp.concatenate([bm4, zeros4], axis=0),
                               jnp.where(ones_rows, 1.0, 0.0)], axis=1).astype(BF16)
        res = lax.dot_general(lhs, rhs, _TN, preferred_element_type=F32)
        h0 = h0_ref[b]
        hout_ref[b] = res[:, M_STATE:] * h0 + res[:, :M_STATE]
        cm8 = jnp.concatenate([cm4, zeros4], axis=0).astype(BF16)
        yo = lax.dot_general(cm8, h0.astype(BF16), _NT, preferred_element_type=F32)
        y_off_rows.append(jnp.sum(jnp.where(gmask, yo[:M_GROUPS, :], 0.0), axis=0, keepdims=True))
    y = y + da_e * jnp.concatenate(y_off_rows, axis=0)

    gt = y * _silu(z_ref[...])
    y_ref[...] = _rms(gt, nw_ref[...])


def _ssd_step(z, xbc, dtr, cbuf, h0, cw, cb, dtb, alog, dsk, nw, e64):
    n = z.shape[0]
    bb = SSD_STEP_BB
    row = lambda w: pl.BlockSpec((bb, w), lambda i: (i, 0))
    blk3 = lambda s: pl.BlockSpec((bb,) + s, lambda i: (i, 0, 0))
    return pl.pallas_call(
        _ssd_step_kernel,
        grid=(n // bb,),
        in_specs=[row(M_INNER), row(M_CONV_DIM), row(LANES), blk3((M_CONV - 1, M_CONV_DIM)),
                  blk3((M_INNER, M_STATE)),
                  _const_spec(cw.shape), _const_spec(cb.shape), _const_spec(dtb.shape), _const_spec(alog.shape),
                  _const_spec(dsk.shape), _const_spec(nw.shape), _const_spec(e64.shape)],
        out_specs=[row(M_INNER), blk3((M_INNER, M_STATE)), blk3((M_CONV - 1, M_CONV_DIM))],
        out_shape=[jax.ShapeDtypeStruct((n, M_INNER), F32),
                   jax.ShapeDtypeStruct((n, M_INNER, M_STATE), F32),
                   jax.ShapeDtypeStruct((n, M_CONV - 1, M_CONV_DIM), F32)],
        compiler_params=_params(("parallel",)),
        name="ssd_step",
    )(z, xbc, dtr, cbuf, h0, cw, cb, dtb, alog, dsk, nw, e64)


FFN_CHUNK = 1408


def _out_kernel(x_ref, y5_ref, yb_ref, gate_ref, gtm_ref, shf_ref, scf_ref, gtf_ref,
                wglu_ref, bglu_ref, wb5_ref, wbs_ref, wout_ref, npm_ref, npf_ref, npo_ref, wfi_ref, wfo_ref,
                o_ref):
    ya = _gelu_tanh(y5_ref[...])
    glu = ya * _sigmoid(_dot(ya.astype(BF16), wglu_ref[...]) + bglu_ref[...])
    merged = (gate_ref[:, :D_MODEL] * _dot(glu.astype(BF16), wb5_ref[...])
              + gate_ref[:, D_MODEL:] * _dot(yb_ref[...].astype(BF16), wbs_ref[...]))
    mix = _dot(merged.astype(BF16), wout_ref[...])
    x1 = x_ref[...] + gtm_ref[...] * _rms(mix, npm_ref[...])

    hb = (_rms(x1, npf_ref[...]) * (1.0 + scf_ref[...]) + shf_ref[...]).astype(BF16)
    f = jnp.zeros_like(x1)
    for j in range(0, D_FF, FFN_CHUNK):
        gg = _dot(hb, wfi_ref[:, j:j + FFN_CHUNK])
        uu = _dot(hb, wfi_ref[:, D_FF + j:D_FF + j + FFN_CHUNK])
        f = f + _dot((_silu(gg) * uu).astype(BF16), wfo_ref[j:j + FFN_CHUNK, :])
    o_ref[...] = x1 + gtf_ref[...] * _rms(f, npo_ref[...])


def _out_stage(x, y5, yb, gates, mod, ws, *, per_row, tm, rows_per_batch):
    t = x.shape[0]
    row = lambda n: pl.BlockSpec((tm, n), lambda i: (i, 0))
    mods = _mod_specs(per_row, tm, rows_per_batch // tm if not per_row else 1, (2, 3, 4, 5))
    return pl.pallas_call(
        _out_kernel,
        grid=(t // tm,),
        in_specs=[row(D_MODEL), row(D_MODEL), row(M_INNER), row(2 * D_MODEL)] + mods
                 + [_const_spec(w.shape) for w in ws],
        out_specs=row(D_MODEL),
        out_shape=jax.ShapeDtypeStruct((t, D_MODEL), F32),
        compiler_params=_params(("parallel",)),
        name="out_stage",
    )(x, y5, yb, gates, mod, mod, mod, mod, *ws)


def _pad_lanes(v):
    return jnp.pad(v.reshape(1, -1), ((0, 0), (0, LANES - v.shape[-1])))


def kernel(x_prompt, x_sample, state_s5_re, state_s5_im, state_ssm, state_conv, c_prompt, c_sample, w_ada, b_ada, norm_pre_mix, norm_post_mix, norm_pre_ffn, norm_post_ffn, w_in, s5_lam_re, s5_lam_im, s5_log_dt, s5_b_re, s5_b_im, s5_c_re, s5_c_im, s5_d, s5_w_glu, s5_b_glu, m_conv_w, m_conv_b, m_dt_bias, m_a_log, m_d, m_norm, w_branch_s5, w_branch_ssd, w_out, w_ffn_in, w_ffn_out):
    bp, seq, _ = x_prompt.shape
    bs = x_sample.shape[0]
    assert x_sample.shape[1] == 1 and w_ada.shape[0] == 1 and seq % M_CHUNK == 0
    row1 = lambda v: v.reshape(1, -1)

    w_in0 = w_in[0]
    ws_in = (w_in0[:, :OFF_Z].astype(BF16), w_in0[:, OFF_Z:OFF_XBC].astype(BF16),
             w_in0[:, OFF_XBC:OFF_DT].astype(BF16),
             jnp.pad(w_in0[:, OFF_DT:OFF_GA], ((0, 0), (0, LANES - M_HEADS))).astype(BF16),
             w_in0[:, OFF_GA:].astype(BF16))
    ws_out = (s5_w_glu[0].astype(BF16), row1(s5_b_glu[0]), w_branch_s5[0].astype(BF16),
              w_branch_ssd[0].astype(BF16), w_out[0].astype(BF16), row1(norm_post_mix[0]),
              row1(norm_pre_ffn[0]), row1(norm_post_ffn[0]), w_ffn_in[0].astype(BF16), w_ffn_out[0].astype(BF16))
    cw, cb = m_conv_w[0], row1(m_conv_b[0])
    dtb, alog = _pad_lanes(m_dt_bias[0]), _pad_lanes(m_a_log[0])
    dsk = row1(jnp.repeat(m_d[0], M_HEADDIM))
    nw = row1(m_norm[0])
    e64, e128 = _expand_mats()

    mod = _ada(jnp.concatenate([c_prompt, c_sample], axis=0), w_ada[0], b_ada[0])
    mod_p = mod[:bp].reshape(bp, 1, 6 * D_MODEL)
    mod_s = mod[bp:]

    ar, ai, bbr, bbi = _s5_disc(s5_lam_re[0], s5_lam_im[0], s5_log_dt[0], s5_b_re[0], s5_b_im[0])
    s5m = _s5_matrices(ar, ai, bbr, bbi, s5_c_re[0], s5_c_im[0], s5_d[0])
    g_pre = row1(norm_pre_mix[0])

    xp = x_prompt.reshape(bp * seq, D_MODEL)
    tm = min(ROW_TILE, seq)
    u, z, xbc, dtr, gates = _in_proj(xp, mod_p, g_pre, ws_in, per_row=False, tm=tm, rows_per_batch=seq)
    zeros_s5 = jnp.zeros((bp, S5_GROUPS * S5_STATE), F32)
    y5, p_re, p_im = _s5(u.reshape(bp, seq, D_MODEL), zeros_s5, zeros_s5, *s5m, tl=min(S5_TIME_TILE, seq))
    yb, p_ssm, p_conv = _ssd(z, xbc, dtr, cw, cb, dtb, alog, dsk, nw, e64, e128, nbatch=bp, seq=seq)
    y_prompt = _out_stage(xp, y5.reshape(bp * seq, D_MODEL), yb, gates, mod_p, ws_out,
                          per_row=False, tm=tm, rows_per_batch=seq)

    xs = x_sample.reshape(bs, D_MODEL)
    u, z, xbc, dtr, gates = _in_proj(xs, mod_s, g_pre, ws_in, per_row=True, tm=bs, rows_per_batch=1)
    y5, s_re, s_im = _s5(u.reshape(bs, 1, D_MODEL), state_s5_re[0].reshape(bs, -1), state_s5_im[0].reshape(bs, -1),
                         *s5m, tl=1)
    yb, s_ssm, s_conv = _ssd_step(z, xbc, dtr, state_conv[0], state_ssm[0].reshape(bs, M_INNER, M_STATE),
                                  cw, cb, dtb, alog, dsk, nw, e64)
    y_sample = _out_stage(xs, y5.reshape(bs, D_MODEL), yb, gates, mod_s, ws_out,
                          per_row=True, tm=bs, rows_per_batch=1)

    s5_shape = (1, -1, S5_GROUPS, S5_STATE)
    ssm_shape = (1, -1, M_HEADS, M_HEADDIM, M_STATE)
    return (y_prompt.reshape(bp, seq, D_MODEL), y_sample.reshape(bs, 1, D_MODEL),
            p_re.reshape(s5_shape), p_im.reshape(s5_shape), p_ssm.reshape(ssm_shape), p_conv[None],
            s_re.reshape(s5_shape), s_im.reshape(s5_shape), s_ssm.reshape(ssm_shape), s_conv[None])
```

```python
import functools

import jax
import jax.numpy as jnp
from jax import lax
from jax.experimental import pallas as pl
from jax.experimental.pallas import tpu as pltpu

F32 = jnp.float32
BF16 = jnp.bfloat16

D_MODEL = 1024
EPS = 1e-6
S5_GROUP = 16
S5_GROUPS = 64
S5_STATE = 64
M_INNER = 2048
M_HEADDIM = 64
M_HEADS = 32
M_GROUPS = 4
M_HPG = 8
M_STATE = 128
M_CONV = 4
M_CONV_DIM = 3072
M_CHUNK = 128
D_FF = 2816
OFF_Z = D_MODEL
OFF_XBC = OFF_Z + M_INNER
OFF_DT = OFF_XBC + M_CONV_DIM
OFF_GA = OFF_DT + M_HEADS
IN_COLS = OFF_GA + 2 * D_MODEL

LANES = 128
SUBLANES = 8
VMEM_LIMIT = 56 * 1024 * 1024

S5_CH = 256
S5_NS = 1024
S5_CHUNKS = D_MODEL // S5_CH
S5_TILES = S5_NS // LANES
S5_PIECES = S5_CH // LANES
S5_SUB_ROWS = 128
ROW_TILE = 256

_NT = (((1,), (1,)), ((), ()))
_TN = (((0,), (0,)), ((), ()))


def _const_spec(shape):
    nd = len(shape)
    return pl.BlockSpec(shape, lambda *_: (0,) * nd, pipeline_mode=pl.Buffered(1))


def _params(sem):
    return pltpu.CompilerParams(dimension_semantics=sem, vmem_limit_bytes=VMEM_LIMIT)


def _sigmoid(x):
    return 1.0 / (1.0 + jnp.exp(-x))


def _silu(x):
    return x * _sigmoid(x)


def _softplus(x):
    return jnp.maximum(x, 0.0) + jnp.log1p(jnp.exp(-jnp.abs(x)))


def _gelu_tanh(x):
    return 0.5 * x * (1.0 + jnp.tanh(0.7978845608028654 * (x + 0.044715 * (x * x * x))))


def _rms(x, g):
    return x * lax.rsqrt(jnp.mean(x * x, axis=-1, keepdims=True) + EPS) * g


def _dot(a, b):
    return jnp.dot(a, b, preferred_element_type=F32)


def _pieces3(x):
    hi = x.astype(BF16).astype(F32)
    r1 = x - hi
    mid = r1.astype(BF16).astype(F32)
    lo = (r1 - mid).astype(BF16).astype(F32)
    return hi, mid, lo


def _split3(x):
    return jnp.concatenate(_pieces3(x), axis=-1).astype(BF16)


def _ada_kernel(c_ref, w_ref, b_ref, o_ref):
    s = _silu(c_ref[...]).astype(BF16)
    o_ref[...] = _dot(s, w_ref[...].astype(BF16)) + b_ref[...]


def _ada(c, w, b):
    n = c.shape[0]
    tn = 512
    return pl.pallas_call(
        _ada_kernel,
        grid=(6 * D_MODEL // tn,),
        in_specs=[_const_spec((n, D_MODEL)),
                  pl.BlockSpec((D_MODEL, tn), lambda j: (0, j)),
                  pl.BlockSpec((1, tn), lambda j: (0, j))],
        out_specs=pl.BlockSpec((n, tn), lambda j: (0, j)),
        out_shape=jax.ShapeDtypeStruct((n, 6 * D_MODEL), F32),
        compiler_params=_params(("arbitrary",)),
        name="ada",
    )(c, w, b.reshape(1, -1))


def _s5_disc_kernel(lr_ref, li_ref, ldt_ref, br_ref, bi_ref, ar_ref, ai_ref, bbr_ref, bbi_ref):
    lr = lr_ref[...]
    li = li_ref[...]
    dt = jnp.exp(ldt_ref[...])
    mag = jnp.exp(lr * dt)
    ar = mag * jnp.cos(li * dt)
    ai = mag * jnp.sin(li * dt)
    den = lr * lr + li * li
    nr = ar - 1.0
    f_re = (nr * lr + ai * li) / den
    f_im = (ai * lr - nr * li) / den
    br = br_ref[...]
    bi = bi_ref[...]
    ar_ref[...] = ar
    ai_ref[...] = ai
    bbr_ref[...] = f_re * br - f_im * bi
    bbi_ref[...] = f_re * bi + f_im * br


def _s5_disc(lam_re, lam_im, log_dt, b_re, b_im):
    g, n = lam_re.shape
    w = S5_GROUP * n
    tile = lambda a: jnp.tile(a, (1, S5_GROUP))
    ldt = jnp.broadcast_to(log_dt[:, None], (g, w))
    brt = jnp.swapaxes(b_re, 1, 2).reshape(g, w)
    bit = jnp.swapaxes(b_im, 1, 2).reshape(g, w)
    shp = jax.ShapeDtypeStruct((g, w), F32)
    ar, ai, bbr, bbi = pl.pallas_call(
        _s5_disc_kernel,
        out_shape=(shp, shp, shp, shp),
        name="s5_disc",
    )(tile(lam_re), tile(lam_im), ldt, brt, bit)
    return (ar[:, :n], ai[:, :n], bbr.reshape(g, S5_GROUP, n), bbi.reshape(g, S5_GROUP, n))


def _s5_matrices(abar_re, abar_im, bbar_re, bbar_im, c_re, c_im, d_skip):
    gl = S5_CH // S5_GROUP
    eye = jnp.eye(gl, dtype=bool)

    def bmat(bb):
        t = bb.reshape(S5_CHUNKS, gl, S5_GROUP, 1, S5_STATE)
        t = jnp.where(eye[None, :, None, :, None], t, 0.0)
        return t.reshape(S5_CHUNKS, S5_CH, S5_NS)

    def cmat(cc):
        t = jnp.swapaxes(cc, 1, 2).reshape(S5_CHUNKS, gl, S5_STATE, 1, S5_GROUP)
        t = jnp.where(eye[None, :, None, :, None], t, 0.0)
        return t.reshape(S5_CHUNKS, S5_NS, S5_CH)

    def tiles(re, im, axis):
        shp = re.shape[:axis] + (S5_TILES, 1, LANES) + re.shape[axis + 1:]
        full = re.shape[:axis] + (2 * S5_NS,) + re.shape[axis + 1:]
        return jnp.concatenate([re.reshape(shp), im.reshape(shp)], axis=axis + 1).reshape(full)

    bm = tiles(bmat(bbar_re), bmat(bbar_im), 2).astype(BF16)
    cm = tiles(cmat(c_re), cmat(-c_im), 1).astype(BF16)
    a = tiles(abar_re.reshape(S5_CHUNKS, 1, S5_NS), abar_im.reshape(S5_CHUNKS, 1, S5_NS), 2)
    d = d_skip.reshape(S5_CHUNKS, 1, S5_CH)
    return bm, a, cm, d


def _inproj_kernel(x_ref, sh_ref, sc_ref, g_ref, wu_ref, wz_ref, wx_ref, wdt_ref, wg_ref,
                   u_ref, z_ref, xbc_ref, dt_ref, gate_ref):
    h = _rms(x_ref[...], g_ref[...]) * (1.0 + sc_ref[...]) + sh_ref[...]
    hb = h.astype(BF16)
    step = 512

    rows = hb.shape[0]

    def mm(o_ref, w_ref, fn=None):
        n = o_ref.shape[-1]
        for j in range(0, n, step):
            w = min(step, n - j)
            r = _dot(hb, w_ref[:, j:j + w])
            o_ref[0:rows, j:j + w] = r if fn is None else fn(r)

    if u_ref.shape[0] > rows:
        u_ref[rows:, :] = jnp.zeros((u_ref.shape[0] - rows, D_MODEL), F32)
    mm(u_ref, wu_ref)
    mm(z_ref, wz_ref)
    mm(xbc_ref, wx_ref)
    mm(dt_ref, wdt_ref)
    mm(gate_ref, wg_ref, _sigmoid)


def _mod_specs(per_row, tm, rows_per_mod, cols):
    if per_row:
        return [pl.BlockSpec((tm, D_MODEL), functools.partial(lambda i, c: (i, c), c=c)) for c in cols]
    return [pl.BlockSpec((None, 1, D_MODEL), functools.partial(lambda i, c: (i // rows_per_mod, 0, c), c=c))
            for c in cols]


def _s5_layout(per_row, t, tm, rows_per_batch, width):
    if per_row:
        return (1, t, width), pl.BlockSpec((None, tm, width), lambda i: (0, i, 0))
    nblk, pitch = rows_per_batch // tm, _s5_pitch(tm)
    return ((nblk, (t // rows_per_batch) * pitch, width),
            pl.BlockSpec((None, pitch, width), lambda i: (i % nblk, i // nblk, 0)))


def _in_proj(x, mod, g, ws, *, per_row, tm, rows_per_batch):
    t = x.shape[0]
    wu, wz, wx, wdt, wg = ws
    row = lambda n: pl.BlockSpec((tm, n), lambda i: (i, 0))
    outs = (M_INNER, M_CONV_DIM, LANES, 2 * D_MODEL)
    u_shape, u_spec = _s5_layout(per_row, t, tm, rows_per_batch, D_MODEL)
    return pl.pallas_call(
        _inproj_kernel,
        grid=(t // tm,),
        in_specs=[row(D_MODEL)] + _mod_specs(per_row, tm, rows_per_batch // tm if not per_row else 1, (0, 1))
                 + [_const_spec((1, D_MODEL))] + [_const_spec(w.shape) for w in ws],
        out_specs=[u_spec] + [row(n) for n in outs],
        out_shape=[jax.ShapeDtypeStruct(u_shape, F32)] + [jax.ShapeDtypeStruct((t, n), F32) for n in outs],
        compiler_params=_params(("parallel",)),
        name="in_proj",
    )(x, mod, mod, g, wu, wz, wx, wdt, wg)


def _s5_pitch(tl):
    return tl if tl == 1 else tl + SUBLANES


def _s5_kernel(*refs, nb, tl):
    nu = S5_PIECES
    u_refs = refs[:nu]
    h0r_ref, h0i_ref, bm_ref, a_ref, cm_ref, d_ref = refs[nu:nu + 6]
    y_refs = refs[nu + 6:2 * nu + 6]
    hr_ref, hi_ref, hst_ref = refs[2 * nu + 6:]
    tb = pl.program_id(1)
    tsub = S5_SUB_ROWS // nb
    tw = 2 * LANES
    pitch = _s5_pitch(tl)

    def step_rows(t):
        return pl.ds(t, nb, stride=pitch) if pitch > 1 else pl.ds(0, nb)

    @pl.when(tb == 0)
    def _():
        for j in range(S5_TILES):
            hst_ref[:, j * tw:j * tw + LANES] = h0r_ref[:, j * LANES:(j + 1) * LANES]
            hst_ref[:, j * tw + LANES:(j + 1) * tw] = h0i_ref[:, j * LANES:(j + 1) * LANES]

    ar = [jnp.broadcast_to(a_ref[:, j * tw:j * tw + LANES], (nb, LANES)) for j in range(S5_TILES)]
    ai = [jnp.broadcast_to(a_ref[:, j * tw + LANES:(j + 1) * tw], (nb, LANES)) for j in range(S5_TILES)]
    dsk = d_ref[...]

    def sub_block(s, carry):
        t0 = s * tsub
        xs = jnp.concatenate([jnp.concatenate([r[step_rows(t0 + i), :] for r in u_refs], axis=1)
                              for i in range(tsub)], axis=0)
        lhs = xs.astype(BF16)
        new, hs = [], []
        for j in range(S5_TILES):
            bu = _dot(lhs, bm_ref[:, j * tw:(j + 1) * tw])
            hr, hi = carry[2 * j], carry[2 * j + 1]
            rows = []
            for i in range(tsub):
                br = bu[i * nb:(i + 1) * nb, :LANES]
                bi = bu[i * nb:(i + 1) * nb, LANES:]
                hr, hi = ar[j] * hr - ai[j] * hi + br, ar[j] * hi + ai[j] * hr + bi
                rows.append(jnp.concatenate([hr, hi], axis=1))
            new += [hr, hi]
            hs.append(jnp.concatenate(rows, axis=0).astype(BF16))
        y = _dot(jnp.concatenate(hs, axis=1), cm_ref[...]) + dsk * xs
        for i in range(tsub):
            for k, y_ref in enumerate(y_refs):
                y_ref[step_rows(t0 + i), :] = y[i * nb:(i + 1) * nb, k * LANES:(k + 1) * LANES]
        return tuple(new)

    h_in = []
    for j in range(S5_TILES):
        h_in += [hst_ref[:, j * tw:j * tw + LANES], hst_ref[:, j * tw + LANES:(j + 1) * tw]]
    h_out = lax.fori_loop(0, tl // tsub, sub_block, tuple(h_in))
    for j in range(S5_TILES):
        hst_ref[:, j * tw:j * tw + LANES] = h_out[2 * j]
        hst_ref[:, j * tw + LANES:(j + 1) * tw] = h_out[2 * j + 1]

    if pitch > tl:
        for y_ref in y_refs:
            for b in range(nb):
                y_ref[b * pitch + tl:(b + 1) * pitch, :] = jnp.zeros((pitch - tl, LANES), F32)

    @pl.when(tb == pl.num_programs(1) - 1)
    def _():
        for j in range(S5_TILES):
            hr_ref[:, j * LANES:(j + 1) * LANES] = h_out[2 * j]
            hi_ref[:, j * LANES:(j + 1) * LANES] = h_out[2 * j + 1]


def _s5(u, h0r, h0i, bm, a, cm, d, *, tl):
    nblk, rows, _ = u.shape
    pitch = _s5_pitch(tl)
    nb = rows // pitch
    assert S5_SUB_ROWS % nb == 0 and tl % (S5_SUB_ROWS // nb) == 0
    chunk3 = lambda s: pl.BlockSpec((None,) + s, lambda c, t: (c, 0, 0))
    st_spec = pl.BlockSpec((nb, S5_NS), lambda c, t: (0, c))
    nu = S5_PIECES
    u_specs = [pl.BlockSpec((None, rows, LANES), functools.partial(lambda c, t, k: (t, 0, nu * c + k), k=k))
               for k in range(nu)]
    y_spec = pl.BlockSpec((None, rows, LANES), lambda c, t: (t, 0, c))
    y_shape = jax.ShapeDtypeStruct((nblk, rows, D_MODEL // nu), F32)
    res = pl.pallas_call(
        functools.partial(_s5_kernel, nb=nb, tl=tl),
        grid=(S5_CHUNKS, nblk),
        in_specs=u_specs + [st_spec, st_spec, chunk3((S5_CH, 2 * S5_NS)), chunk3((1, 2 * S5_NS)),
                            chunk3((2 * S5_NS, S5_CH)), chunk3((1, S5_CH))],
        out_specs=[y_spec] * nu + [st_spec, st_spec],
        out_shape=[y_shape] * nu + [jax.ShapeDtypeStruct(h0r.shape, F32), jax.ShapeDtypeStruct(h0i.shape, F32)],
        scratch_shapes=[pltpu.VMEM((nb, 2 * S5_NS), F32)],
        compiler_params=_params(("parallel", "arbitrary")),
        name="s5",
    )(*([u] * nu), h0r, h0i, bm, a, cm, d)
    return res[:nu], res[nu], res[nu + 1]


def _expand_mats():
    r = lax.broadcasted_iota(jnp.int32, (3 * LANES, M_HEADS * M_HEADDIM), 0) % LANES
    c = lax.broadcasted_iota(jnp.int32, (3 * LANES, M_HEADS * M_HEADDIM), 1) // M_HEADDIM
    e64 = (r == c).astype(BF16)
    r = lax.broadcasted_iota(jnp.int32, (3 * LANES, M_HEADS * M_CHUNK), 0) % LANES
    c = lax.broadcasted_iota(jnp.int32, (3 * LANES, M_HEADS * M_CHUNK), 1) // M_CHUNK
    e128 = (r == c).astype(BF16)
    return e64, e128


def _ssd_kernel(z_ref, xbc_ref, dt_ref, cw_ref, cb_ref, dtb_ref, alog_ref, dsk_ref, nw_ref, e64_ref, e128_ref,
                y_ref, hout_ref, convout_ref,
                xf_ref, act_ref, h_ref, xdt_ref, xdte_ref, eacs_ref, rall_ref, yacc_ref):
    c = pl.program_id(1)
    last = pl.num_programs(1) - 1
    q = M_CHUNK

    @pl.when(c == 0)
    def _():
        xf_ref[0:SUBLANES, :] = jnp.zeros((SUBLANES, M_CONV_DIM), F32)
        h_ref[...] = jnp.zeros_like(h_ref)

    xf_ref[SUBLANES:SUBLANES + q, :] = xbc_ref[...]
    cs = 512
    for j in range(0, M_CONV_DIM, cs):
        sl = slice(j, j + cs)
        conv = cb_ref[:, sl]
        for k in range(M_CONV):
            o = SUBLANES - (M_CONV - 1) + k
            conv = conv + xf_ref[o:o + q, sl] * cw_ref[k:k + 1, sl]
        act_ref[:, sl] = _silu(conv)

    @pl.when(c == last)
    def _():
        convout_ref[...] = xf_ref[SUBLANES + q - (M_CONV - 1):SUBLANES + q, :]

    xf_ref[0:SUBLANES, :] = xf_ref[q:q + SUBLANES, :]

    dt = _softplus(dt_ref[...] + dtb_ref[...])
    a = dt * (-jnp.exp(alog_ref[...]))
    ri = lax.broadcasted_iota(jnp.int32, (q, q), 0)
    ci = lax.broadcasted_iota(jnp.int32, (q, q), 1)
    causal = ri >= ci
    acs = jnp.dot(causal.astype(F32), a, preferred_element_type=F32, precision=lax.Precision.HIGHEST)
    acs_t = acs.T
    acs3 = _split3(acs)
    dt3 = _split3(dt)

    for j in range(0, M_INNER, cs):
        sl = slice(j, j + cs)
        dt_e = _dot(dt3, e64_ref[:, sl])
        acs_e = _dot(acs3, e64_ref[:, sl])
        xdt = act_ref[:, sl] * dt_e
        xdt_ref[:, sl] = xdt.astype(BF16)
        xdte_ref[:, sl] = (xdt * jnp.exp(acs_e[q - 1:q, :] - acs_e)).astype(BF16)
        eacs_ref[:, sl] = jnp.exp(acs_e)
    for j in range(0, M_HEADS * q, cs):
        sl = slice(j, j + cs)
        rall_ref[:, sl] = _dot(acs3, e128_ref[:, sl])

    hp = M_HPG * M_HEADDIM
    for g in range(M_GROUPS):
        gs = slice(g * hp, (g + 1) * hp)
        bm = act_ref[:, M_INNER + g * M_STATE:M_INNER + (g + 1) * M_STATE].astype(BF16)
        cm = act_ref[:, M_INNER + (M_GROUPS + g) * M_STATE:M_INNER + (M_GROUPS + g + 1) * M_STATE].astype(BF16)
        cbm = lax.dot_general(cm, bm, _NT, preferred_element_type=F32)
        y_off = lax.dot_general(cm, h_ref[gs, :].astype(BF16), _NT, preferred_element_type=F32)
        st = lax.dot_general(xdte_ref[:, gs], bm, _TN, preferred_element_type=F32)
        yd = []
        for r in range(M_HPG):
            h = g * M_HPG + r
            rh = rall_ref[:, h * q:(h + 1) * q]
            dec = jnp.where(causal, jnp.exp(rh - acs_t[h:h + 1, :]), 0.0)
            m = (cbm * dec).astype(BF16)
            yd.append(_dot(m, xdt_ref[:, h * M_HEADDIM:(h + 1) * M_HEADDIM]))
            hs = slice(h * M_HEADDIM, (h + 1) * M_HEADDIM)
            h_ref[hs, :] = jnp.exp(rh[q - 1:q, :]) * h_ref[hs, :] + st[r * M_HEADDIM:(r + 1) * M_HEADDIM, :]
        yacc_ref[:, gs] = jnp.concatenate(yd, axis=1) + y_off * eacs_ref[:, gs] + dsk_ref[:, gs] * act_ref[:, gs]

    ss = jnp.zeros((q, 1), F32)
    for j in range(0, M_INNER, cs):
        sl = slice(j, j + cs)
        gt = yacc_ref[:, sl] * _silu(z_ref[:, sl])
        yacc_ref[:, sl] = gt
        ss = ss + jnp.sum(gt * gt, axis=-1, keepdims=True)
    inv = lax.rsqrt(ss * (1.0 / M_INNER) + EPS)
    for j in range(0, M_INNER, cs):
        sl = slice(j, j + cs)
        y_ref[:, sl] = yacc_ref[:, sl] * inv * nw_ref[:, sl]

    @pl.when(c == last)
    def _():
        hout_ref[...] = h_ref[...]


def _ssd(z, xbc, dtr, cw, cb, dtb, alog, dsk, nw, e64, e128, *, nbatch, seq):
    nc = seq // M_CHUNK
    q = M_CHUNK
    row = lambda n: pl.BlockSpec((q, n), lambda b, c: (b * nc + c, 0))
    return pl.pallas_call(
        _ssd_kernel,
        grid=(nbatch, nc),
        in_specs=[row(M_INNER), row(M_CONV_DIM), row(LANES),
                  _const_spec(cw.shape), _const_spec(cb.shape), _const_spec(dtb.shape), _const_spec(alog.shape),
                  _const_spec(dsk.shape), _const_spec(nw.shape), _const_spec(e64.shape), _const_spec(e128.shape)],
        out_specs=[row(M_INNER),
                   pl.BlockSpec((None, M_INNER, M_STATE), lambda b, c: (b, 0, 0)),
                   pl.BlockSpec((None, M_CONV - 1, M_CONV_DIM), lambda b, c: (b, 0, 0))],
        out_shape=[jax.ShapeDtypeStruct((nbatch * seq, M_INNER), F32),
                   jax.ShapeDtypeStruct((nbatch, M_INNER, M_STATE), F32),
                   jax.ShapeDtypeStruct((nbatch, M_CONV - 1, M_CONV_DIM), F32)],
        scratch_shapes=[pltpu.VMEM((q + 2 * SUBLANES, M_CONV_DIM), F32), pltpu.VMEM((q, M_CONV_DIM), F32),
                        pltpu.VMEM((M_INNER, M_STATE), F32), pltpu.VMEM((q, M_INNER), BF16),
                        pltpu.VMEM((q, M_INNER), BF16), pltpu.VMEM((q, M_INNER), F32),
                        pltpu.VMEM((q, M_HEADS * q), F32), pltpu.VMEM((q, M_INNER), F32)],
        compiler_params=_params(("parallel", "arbitrary")),
        name="ssd",
    )(z, xbc, dtr, cw, cb, dtb, alog, dsk, nw, e64, e128)


SSD_STEP_BB = 8


def _ssd_step_kernel(z_ref, xbc_ref, dt_ref, cbuf_ref, h0_ref, cw_ref, cb_ref, dtb_ref, alog_ref, dsk_ref, nw_ref,
                     e64_ref, y_ref, hout_ref, convout_ref):
    bb = SSD_STEP_BB
    x_new = xbc_ref[...]
    conv = cb_ref[...] + x_new * cw_ref[M_CONV - 1:M_CONV, :]
    for k in range(M_CONV - 1):
        conv = conv + cbuf_ref[:, k, :] * cw_ref[k:k + 1, :]
    act = _silu(conv)
    for k in range(M_CONV - 2):
        convout_ref[:, k, :] = cbuf_ref[:, k + 1, :]
    convout_ref[:, M_CONV - 2, :] = x_new

    xs = act[:, :M_INNER]
    bmat = act[:, M_INNER:M_INNER + M_GROUPS * M_STATE]
    cmat = act[:, M_INNER + M_GROUPS * M_STATE:]
    dt = _softplus(dt_ref[...] + dtb_ref[...])
    da = jnp.exp(dt * (-jnp.exp(alog_ref[...])))
    dt_e = _dot(_split3(dt), e64_ref[...])
    da_e = _dot(_split3(da), e64_ref[...])
    xdt = xs * dt_e

    hp = M_HPG * M_HEADDIM
    lane_group = lax.broadcasted_iota(jnp.int32, (M_GROUPS, M_INNER), 1) // hp
    row_group = lax.broadcasted_iota(jnp.int32, (M_GROUPS, M_INNER), 0)
    gmask = lane_group == row_group
    ones_rows = lax.broadcasted_iota(jnp.int32, (SUBLANES, M_STATE), 0) >= M_GROUPS

    cbx = []
    for g in range(M_GROUPS):
        cbg = jnp.sum(cmat[:, g * M_STATE:(g + 1) * M_STATE] * bmat[:, g * M_STATE:(g + 1) * M_STATE],
                      axis=-1, keepdims=True)
        cbx.append(jnp.broadcast_to(cbg, (bb, hp)))
    y = jnp.concatenate(cbx, axis=1) * xdt + dsk_ref[...] * xs

    y_off_rows = []
    for b in range(bb):
        xsel = jnp.where(gmask, jnp.broadcast_to(xdt[b:b + 1, :], (M_GROUPS, M_INNER)), 0.0)
        lhs = jnp.concatenate([xsel, *_pieces3(da_e[b:b + 1, :]), jnp.zeros((1, M_INNER), F32)],
                              axis=0).astype(BF16)
        bm4 = jnp.concatenate([bmat[b:b + 1, g * M_STATE:(g + 1) * M_STATE] for g in range(M_GROUPS)], axis=0)
        cm4 = jnp.concatenate([cmat[b:b + 1, g * M_STATE:(g + 1) * M_STATE] for g in range(M_GROUPS)], axis=0)
        zeros4 = jnp.zeros((SUBLANES - M_GROUPS, M_STATE), F32)
        rhs = jnp.concatenate([jnp.concatenate([bm4, zeros4], axis=0),
                               jnp.where(ones_rows, 1.0, 0.0)], axis=1).astype(BF16)
        res = lax.dot_general(lhs, rhs, _TN, preferred_element_type=F32)
        h0 = h0_ref[b]
        hout_ref[b] = res[:, M_STATE:] * h0 + res[:, :M_STATE]
        cm8 = jnp.concatenate([cm4, zeros4], axis=0).astype(BF16)
        yo = lax.dot_general(cm8, h0.astype(BF16), _NT, preferred_element_type=F32)
        y_off_rows.append(jnp.sum(jnp.where(gmask, yo[:M_GROUPS, :], 0.0), axis=0, keepdims=True))
    y = y + da_e * jnp.concatenate(y_off_rows, axis=0)

    gt = y * _silu(z_ref[...])
    y_ref[...] = _rms(gt, nw_ref[...])


def _ssd_step(z, xbc, dtr, cbuf, h0, cw, cb, dtb, alog, dsk, nw, e64):
    n = z.shape[0]
    bb = SSD_STEP_BB
    row = lambda w: pl.BlockSpec((bb, w), lambda i: (i, 0))
    blk3 = lambda s: pl.BlockSpec((bb,) + s, lambda i: (i, 0, 0))
    return pl.pallas_call(
        _ssd_step_kernel,
        grid=(n // bb,),
        in_specs=[row(M_INNER), row(M_CONV_DIM), row(LANES), blk3((M_CONV - 1, M_CONV_DIM)),
                  blk3((M_INNER, M_STATE)),
                  _const_spec(cw.shape), _const_spec(cb.shape), _const_spec(dtb.shape), _const_spec(alog.shape),
                  _const_spec(dsk.shape), _const_spec(nw.shape), _const_spec(e64.shape)],
        out_specs=[row(M_INNER), blk3((M_INNER, M_STATE)), blk3((M_CONV - 1, M_CONV_DIM))],
        out_shape=[jax.ShapeDtypeStruct((n, M_INNER), F32),
                   jax.ShapeDtypeStruct((n, M_INNER, M_STATE), F32),
                   jax.ShapeDtypeStruct((n, M_CONV - 1, M_CONV_DIM), F32)],
        compiler_params=_params(("parallel",)),
        name="ssd_step",
    )(z, xbc, dtr, cbuf, h0, cw, cb, dtb, alog, dsk, nw, e64)


FFN_CHUNK = 1408


def _out_kernel(x_ref, *refs):
    y5_refs = refs[:S5_PIECES]
    (yb_ref, gate_ref, gtm_ref, shf_ref, scf_ref, gtf_ref, wglu_ref, bglu_ref, wb5_ref, wbs_ref, wout_ref,
     npm_ref, npf_ref, npo_ref, wfi_ref, wfo_ref, o_ref) = refs[S5_PIECES:]
    rows = x_ref.shape[0]
    y5 = jnp.concatenate([y5_refs[k][0:rows, c * LANES:(c + 1) * LANES]
                          for c in range(S5_CHUNKS) for k in range(S5_PIECES)], axis=1)
    ya = _gelu_tanh(y5)
    glu = ya * _sigmoid(_dot(ya.astype(BF16), wglu_ref[...]) + bglu_ref[...])
    merged = (gate_ref[:, :D_MODEL] * _dot(glu.astype(BF16), wb5_ref[...])
              + gate_ref[:, D_MODEL:] * _dot(yb_ref[...].astype(BF16), wbs_ref[...]))
    mix = _dot(merged.astype(BF16), wout_ref[...])
    x1 = x_ref[...] + gtm_ref[...] * _rms(mix, npm_ref[...])

    hb = (_rms(x1, npf_ref[...]) * (1.0 + scf_ref[...]) + shf_ref[...]).astype(BF16)
    f = jnp.zeros_like(x1)
    for j in range(0, D_FF, FFN_CHUNK):
        gg = _dot(hb, wfi_ref[:, j:j + FFN_CHUNK])
        uu = _dot(hb, wfi_ref[:, D_FF + j:D_FF + j + FFN_CHUNK])
        f = f + _dot((_silu(gg) * uu).astype(BF16), wfo_ref[j:j + FFN_CHUNK, :])
    o_ref[...] = x1 + gtf_ref[...] * _rms(f, npo_ref[...])


def _out_stage(x, y5, yb, gates, mod, ws, *, per_row, tm, rows_per_batch):
    t = x.shape[0]
    row = lambda n: pl.BlockSpec((tm, n), lambda i: (i, 0))
    mods = _mod_specs(per_row, tm, rows_per_batch // tm if not per_row else 1, (2, 3, 4, 5))
    _, y5_spec = _s5_layout(per_row, t, tm, rows_per_batch, D_MODEL // S5_PIECES)
    return pl.pallas_call(
        _out_kernel,
        grid=(t // tm,),
        in_specs=[row(D_MODEL)] + [y5_spec] * S5_PIECES + [row(M_INNER), row(2 * D_MODEL)] + mods
                 + [_const_spec(w.shape) for w in ws],
        out_specs=row(D_MODEL),
        out_shape=jax.ShapeDtypeStruct((t, D_MODEL), F32),
        compiler_params=_params(("parallel",)),
        name="out_stage",
    )(x, *y5, yb, gates, mod, mod, mod, mod, *ws)


def _pad_lanes(v):
    return jnp.pad(v.reshape(1, -1), ((0, 0), (0, LANES - v.shape[-1])))


def kernel(x_prompt, x_sample, state_s5_re, state_s5_im, state_ssm, state_conv, c_prompt, c_sample, w_ada, b_ada, norm_pre_mix, norm_post_mix, norm_pre_ffn, norm_post_ffn, w_in, s5_lam_re, s5_lam_im, s5_log_dt, s5_b_re, s5_b_im, s5_c_re, s5_c_im, s5_d, s5_w_glu, s5_b_glu, m_conv_w, m_conv_b, m_dt_bias, m_a_log, m_d, m_norm, w_branch_s5, w_branch_ssd, w_out, w_ffn_in, w_ffn_out):
    bp, seq, _ = x_prompt.shape
    bs = x_sample.shape[0]
    assert x_sample.shape[1] == 1 and w_ada.shape[0] == 1 and seq % M_CHUNK == 0
    row1 = lambda v: v.reshape(1, -1)

    w_in0 = w_in[0]
    ws_in = (w_in0[:, :OFF_Z].astype(BF16), w_in0[:, OFF_Z:OFF_XBC].astype(BF16),
             w_in0[:, OFF_XBC:OFF_DT].astype(BF16),
             jnp.pad(w_in0[:, OFF_DT:OFF_GA], ((0, 0), (0, LANES - M_HEADS))).astype(BF16),
             w_in0[:, OFF_GA:].astype(BF16))
    ws_out = (s5_w_glu[0].astype(BF16), row1(s5_b_glu[0]), w_branch_s5[0].astype(BF16),
              w_branch_ssd[0].astype(BF16), w_out[0].astype(BF16), row1(norm_post_mix[0]),
              row1(norm_pre_ffn[0]), row1(norm_post_ffn[0]), w_ffn_in[0].astype(BF16), w_ffn_out[0].astype(BF16))
    cw, cb = m_conv_w[0], row1(m_conv_b[0])
    dtb, alog = _pad_lanes(m_dt_bias[0]), _pad_lanes(m_a_log[0])
    dsk = row1(jnp.repeat(m_d[0], M_HEADDIM))
    nw = row1(m_norm[0])
    e64, e128 = _expand_mats()

    mod = _ada(jnp.concatenate([c_prompt, c_sample], axis=0), w_ada[0], b_ada[0])
    mod_p = mod[:bp].reshape(bp, 1, 6 * D_MODEL)
    mod_s = mod[bp:]

    ar, ai, bbr, bbi = _s5_disc(s5_lam_re[0], s5_lam_im[0], s5_log_dt[0], s5_b_re[0], s5_b_im[0])
    s5m = _s5_matrices(ar, ai, bbr, bbi, s5_c_re[0], s5_c_im[0], s5_d[0])
    g_pre = row1(norm_pre_mix[0])

    xp = x_prompt.reshape(bp * seq, D_MODEL)
    tm = min(ROW_TILE, seq)
    u, z, xbc, dtr, gates = _in_proj(xp, mod_p, g_pre, ws_in, per_row=False, tm=tm, rows_per_batch=seq)
    zeros_s5 = jnp.zeros((bp, S5_GROUPS * S5_STATE), F32)
    y5, p_re, p_im = _s5(u, zeros_s5, zeros_s5, *s5m, tl=tm)
    yb, p_ssm, p_conv = _ssd(z, xbc, dtr, cw, cb, dtb, alog, dsk, nw, e64, e128, nbatch=bp, seq=seq)
    y_prompt = _out_stage(xp, y5, yb, gates, mod_p, ws_out, per_row=False, tm=tm, rows_per_batch=seq)

    xs = x_sample.reshape(bs, D_MODEL)
    u, z, xbc, dtr, gates = _in_proj(xs, mod_s, g_pre, ws_in, per_row=True, tm=bs, rows_per_batch=1)
    y5, s_re, s_im = _s5(u, state_s5_re[0].reshape(bs, -1), state_s5_im[0].reshape(bs, -1), *s5m, tl=1)
    yb, s_ssm, s_conv = _ssd_step(z, xbc, dtr, state_conv[0], state_ssm[0].reshape(bs, M_INNER, M_STATE),
                                  cw, cb, dtb, alog, dsk, nw, e64)
    y_sample = _out_stage(xs, y5, yb, gates, mod_s, ws_out, per_row=True, tm=bs, rows_per_batch=1)

    s5_shape = (1, -1, S5_GROUPS, S5_STATE)
    ssm_shape = (1, -1, M_HEADS, M_HEADDIM, M_STATE)
    return (y_prompt.reshape(bp, seq, D_MODEL), y_sample.reshape(bs, 1, D_MODEL),
            p_re.reshape(s5_shape), p_im.reshape(s5_shape), p_ssm.reshape(ssm_shape), p_conv[None],
            s_re.reshape(s5_shape), s_im.reshape(s5_shape), s_ssm.reshape(ssm_shape), s_conv[None])
```

```python
import functools

import jax
import jax.numpy as jnp
from jax import lax
from jax.experimental import pallas as pl
from jax.experimental.pallas import tpu as pltpu

F32 = jnp.float32
BF16 = jnp.bfloat16

D_MODEL = 1024
EPS = 1e-6
S5_GROUP = 16
S5_GROUPS = 64
S5_STATE = 64
M_INNER = 2048
M_HEADDIM = 64
M_HEADS = 32
M_GROUPS = 4
M_HPG = 8
M_STATE = 128
M_CONV = 4
M_CONV_DIM = 3072
M_CHUNK = 128
D_FF = 2816
OFF_Z = D_MODEL
OFF_XBC = OFF_Z + M_INNER
OFF_DT = OFF_XBC + M_CONV_DIM
OFF_GA = OFF_DT + M_HEADS
IN_COLS = OFF_GA + 2 * D_MODEL

LANES = 128
SUBLANES = 8
VMEM_LIMIT = 56 * 1024 * 1024

S5_CH = 256
S5_NS = 1024
S5_CHUNKS = D_MODEL // S5_CH
S5_TILES = S5_NS // LANES
S5_PIECES = S5_CH // LANES
S5_SUB_ROWS = 128
ROW_TILE = 256

_NT = (((1,), (1,)), ((), ()))
_TN = (((0,), (0,)), ((), ()))


def _const_spec(shape):
    nd = len(shape)
    return pl.BlockSpec(shape, lambda *_: (0,) * nd, pipeline_mode=pl.Buffered(1))


def _params(sem):
    return pltpu.CompilerParams(dimension_semantics=sem, vmem_limit_bytes=VMEM_LIMIT)


def _sigmoid(x):
    return 0.5 * jnp.tanh(0.5 * x) + 0.5


def _silu(x):
    hx = 0.5 * x
    return hx + hx * jnp.tanh(hx)


def _softplus(x):
    return jnp.maximum(x, 0.0) + jnp.log1p(jnp.exp(-jnp.abs(x)))


def _gelu_tanh(x):
    return 0.5 * x * (1.0 + jnp.tanh(0.7978845608028654 * (x + 0.044715 * (x * x * x))))


def _rms(x, g):
    return x * lax.rsqrt(jnp.mean(x * x, axis=-1, keepdims=True) + EPS) * g


def _dot(a, b):
    return jnp.dot(a, b, preferred_element_type=F32)


def _pieces3(x):
    hi = x.astype(BF16).astype(F32)
    r1 = x - hi
    mid = r1.astype(BF16).astype(F32)
    lo = (r1 - mid).astype(BF16).astype(F32)
    return hi, mid, lo


def _split2(x):
    return jnp.concatenate(_pieces3(x)[:2], axis=-1).astype(BF16)


def _split3(x):
    return jnp.concatenate(_pieces3(x), axis=-1).astype(BF16)


def _ada_kernel(c_ref, w_ref, b_ref, o_ref):
    s = _silu(c_ref[...]).astype(BF16)
    o_ref[...] = _dot(s, w_ref[...].astype(BF16)) + b_ref[...]


def _ada(c, w, b):
    n = c.shape[0]
    tn = 512
    return pl.pallas_call(
        _ada_kernel,
        grid=(6 * D_MODEL // tn,),
        in_specs=[_const_spec((n, D_MODEL)),
                  pl.BlockSpec((D_MODEL, tn), lambda j: (0, j)),
                  pl.BlockSpec((1, tn), lambda j: (0, j))],
        out_specs=pl.BlockSpec((n, tn), lambda j: (0, j)),
        out_shape=jax.ShapeDtypeStruct((n, 6 * D_MODEL), F32),
        compiler_params=_params(("arbitrary",)),
        name="ada",
    )(c, w, b.reshape(1, -1))


def _s5_disc_kernel(lr_ref, li_ref, ldt_ref, br_ref, bi_ref, ar_ref, ai_ref, bbr_ref, bbi_ref):
    lr = lr_ref[...]
    li = li_ref[...]
    dt = jnp.exp(ldt_ref[...])
    mag = jnp.exp(lr * dt)
    ar = mag * jnp.cos(li * dt)
    ai = mag * jnp.sin(li * dt)
    den = lr * lr + li * li
    nr = ar - 1.0
    f_re = (nr * lr + ai * li) / den
    f_im = (ai * lr - nr * li) / den
    br = br_ref[...]
    bi = bi_ref[...]
    ar_ref[...] = ar
    ai_ref[...] = ai
    bbr_ref[...] = f_re * br - f_im * bi
    bbi_ref[...] = f_re * bi + f_im * br


def _s5_disc(lam_re, lam_im, log_dt, b_re, b_im):
    g, n = lam_re.shape
    w = S5_GROUP * n
    tile = lambda a: jnp.tile(a, (1, S5_GROUP))
    ldt = jnp.broadcast_to(log_dt[:, None], (g, w))
    brt = jnp.swapaxes(b_re, 1, 2).reshape(g, w)
    bit = jnp.swapaxes(b_im, 1, 2).reshape(g, w)
    shp = jax.ShapeDtypeStruct((g, w), F32)
    ar, ai, bbr, bbi = pl.pallas_call(
        _s5_disc_kernel,
        out_shape=(shp, shp, shp, shp),
        name="s5_disc",
    )(tile(lam_re), tile(lam_im), ldt, brt, bit)
    return (ar[:, :n], ai[:, :n], bbr.reshape(g, S5_GROUP, n), bbi.reshape(g, S5_GROUP, n))


def _s5_matrices(abar_re, abar_im, bbar_re, bbar_im, c_re, c_im, d_skip):
    gl = S5_CH // S5_GROUP
    eye = jnp.eye(gl, dtype=bool)

    def bmat(bb):
        t = bb.reshape(S5_CHUNKS, gl, S5_GROUP, 1, S5_STATE)
        t = jnp.where(eye[None, :, None, :, None], t, 0.0)
        return t.reshape(S5_CHUNKS, S5_CH, S5_NS)

    def cmat(cc):
        t = jnp.swapaxes(cc, 1, 2).reshape(S5_CHUNKS, gl, S5_STATE, 1, S5_GROUP)
        t = jnp.where(eye[None, :, None, :, None], t, 0.0)
        return t.reshape(S5_CHUNKS, S5_NS, S5_CH)

    def tiles(re, im, axis):
        shp = re.shape[:axis] + (S5_TILES, 1, LANES) + re.shape[axis + 1:]
        full = re.shape[:axis] + (2 * S5_NS,) + re.shape[axis + 1:]
        return jnp.concatenate([re.reshape(shp), im.reshape(shp)], axis=axis + 1).reshape(full)

    bm = tiles(bmat(bbar_re), bmat(bbar_im), 2).astype(BF16)
    cm = tiles(cmat(c_re), cmat(-c_im), 1).astype(BF16)
    a = tiles(abar_re.reshape(S5_CHUNKS, 1, S5_NS), abar_im.reshape(S5_CHUNKS, 1, S5_NS), 2)
    d = d_skip.reshape(S5_CHUNKS, 1, S5_CH)
    return bm, a, cm, d


def _inproj_kernel(x_ref, sh_ref, sc_ref, g_ref, wu_ref, wz_ref, wx_ref, wdt_ref, wg_ref, dtb_ref, cw_ref, cb_ref,
                   u_ref, zs_ref, xbc_ref, dt_ref, gate_ref, *rest, tiles_per_seq):
    h = _rms(x_ref[...], g_ref[...]) * (1.0 + sc_ref[...]) + sh_ref[...]
    hb = h.astype(BF16)
    step = 512
    rows = hb.shape[0]

    def mm(o_ref, w_ref, fn=None):
        n = o_ref.shape[-1]
        for j in range(0, n, step):
            w = min(step, n - j)
            r = _dot(hb, w_ref[:, j:j + w])
            o_ref[0:rows, j:j + w] = r if fn is None else fn(r, slice(j, j + w))

    if u_ref.shape[0] > rows:
        u_ref[rows:, :] = jnp.zeros((u_ref.shape[0] - rows, D_MODEL), F32)
    mm(u_ref, wu_ref)
    mm(zs_ref, wz_ref, lambda r, sl: _silu(r))
    mm(dt_ref, wdt_ref, lambda r, sl: _softplus(r + dtb_ref[:, sl]))
    mm(gate_ref, wg_ref, lambda r, sl: _sigmoid(r))

    if tiles_per_seq is None:
        mm(xbc_ref, wx_ref)
        return

    tail_ref, carry_ref = rest

    @pl.when(pl.program_id(0) % tiles_per_seq == 0)
    def _():
        carry_ref[...] = jnp.zeros_like(carry_ref)

    def conv_silu(r, sl):
        ext = jnp.concatenate([carry_ref[:, sl], r], axis=0)
        conv = cb_ref[:, sl] + r * cw_ref[M_CONV - 1:M_CONV, sl]
        for k in range(1, M_CONV):
            conv = conv + pltpu.roll(ext, k, axis=0)[SUBLANES:, :] * cw_ref[M_CONV - 1 - k:M_CONV - k, sl]
        carry_ref[:, sl] = r[rows - SUBLANES:, :]
        tail_ref[:, sl] = r[rows - SUBLANES:, :]
        return _silu(conv)

    mm(xbc_ref, wx_ref, conv_silu)


def _mod_specs(per_row, tm, rows_per_mod, cols):
    if per_row:
        return [pl.BlockSpec((tm, D_MODEL), functools.partial(lambda i, c: (i, c), c=c)) for c in cols]
    return [pl.BlockSpec((None, 1, D_MODEL), functools.partial(lambda i, c: (i // rows_per_mod, 0, c), c=c))
            for c in cols]


def _s5_layout(per_row, t, tm, rows_per_batch, width):
    if per_row:
        return (1, t, width), pl.BlockSpec((None, tm, width), lambda i: (0, i, 0))
    nblk, pitch = rows_per_batch // tm, _s5_pitch(tm)
    return ((nblk, (t // rows_per_batch) * pitch, width),
            pl.BlockSpec((None, pitch, width), lambda i: (i % nblk, i // nblk, 0)))


def _in_proj(x, mod, g, ws, dtb, cw, cb, *, per_row, tm, rows_per_batch):
    t = x.shape[0]
    row = lambda n: pl.BlockSpec((tm, n), lambda i: (i, 0))
    outs = (M_INNER, M_CONV_DIM, LANES, 2 * D_MODEL)
    u_shape, u_spec = _s5_layout(per_row, t, tm, rows_per_batch, D_MODEL)
    out_specs = [u_spec] + [row(n) for n in outs]
    out_shape = [jax.ShapeDtypeStruct(u_shape, F32)] + [jax.ShapeDtypeStruct((t, n), F32) for n in outs]
    scratch = []
    tiles_per_seq = None
    if not per_row:
        tiles_per_seq = rows_per_batch // tm
        out_specs.append(pl.BlockSpec((None, SUBLANES, M_CONV_DIM), lambda i: (i // tiles_per_seq, 0, 0)))
        out_shape.append(jax.ShapeDtypeStruct((t // rows_per_batch, SUBLANES, M_CONV_DIM), F32))
        scratch.append(pltpu.VMEM((SUBLANES, M_CONV_DIM), F32))
    consts = (g,) + tuple(ws) + (dtb, cw, cb)
    return pl.pallas_call(
        functools.partial(_inproj_kernel, tiles_per_seq=tiles_per_seq),
        grid=(t // tm,),
        in_specs=[row(D_MODEL)] + _mod_specs(per_row, tm, rows_per_batch // tm if not per_row else 1, (0, 1))
                 + [_const_spec(w.shape) for w in consts],
        out_specs=out_specs,
        out_shape=out_shape,
        scratch_shapes=scratch,
        compiler_params=_params(("arbitrary",)),
        name="in_proj",
    )(x, mod, mod, *consts)


def _s5_pitch(tl):
    return tl if tl == 1 else tl + SUBLANES


def _s5_kernel(*refs, nb, tl):
    nu = S5_PIECES
    u_refs = refs[:nu]
    h0r_ref, h0i_ref, bm_ref, a_ref, cm_ref, d_ref = refs[nu:nu + 6]
    y_refs = refs[nu + 6:2 * nu + 6]
    hr_ref, hi_ref, hst_ref = refs[2 * nu + 6:]
    tb = pl.program_id(1)
    tsub = S5_SUB_ROWS // nb
    tw = 2 * LANES
    pitch = _s5_pitch(tl)

    def step_rows(t):
        return pl.ds(t, nb, stride=pitch) if pitch > 1 else pl.ds(0, nb)

    @pl.when(tb == 0)
    def _():
        for j in range(S5_TILES):
            hst_ref[:, j * tw:j * tw + LANES] = h0r_ref[:, j * LANES:(j + 1) * LANES]
            hst_ref[:, j * tw + LANES:(j + 1) * tw] = h0i_ref[:, j * LANES:(j + 1) * LANES]

    ar = [jnp.broadcast_to(a_ref[:, j * tw:j * tw + LANES], (nb, LANES)) for j in range(S5_TILES)]
    ai = [jnp.broadcast_to(a_ref[:, j * tw + LANES:(j + 1) * tw], (nb, LANES)) for j in range(S5_TILES)]
    dsk = d_ref[...]

    def sub_block(s, carry):
        t0 = s * tsub
        xs = jnp.concatenate([jnp.concatenate([r[step_rows(t0 + i), :] for r in u_refs], axis=1)
                              for i in range(tsub)], axis=0)
        lhs = xs.astype(BF16)
        new, hs = [], []
        for j in range(S5_TILES):
            bu = _dot(lhs, bm_ref[:, j * tw:(j + 1) * tw])
            hr, hi = carry[2 * j], carry[2 * j + 1]
            rows = []
            for i in range(tsub):
                br = bu[i * nb:(i + 1) * nb, :LANES]
                bi = bu[i * nb:(i + 1) * nb, LANES:]
                hr, hi = ar[j] * hr - ai[j] * hi + br, ar[j] * hi + ai[j] * hr + bi
                rows.append(jnp.concatenate([hr, hi], axis=1))
            new += [hr, hi]
            hs.append(jnp.concatenate(rows, axis=0).astype(BF16))
        y = _dot(jnp.concatenate(hs, axis=1), cm_ref[...]) + dsk * xs
        for i in range(tsub):
            for k, y_ref in enumerate(y_refs):
                y_ref[step_rows(t0 + i), :] = y[i * nb:(i + 1) * nb, k * LANES:(k + 1) * LANES]
        return tuple(new)

    h_in = []
    for j in range(S5_TILES):
        h_in += [hst_ref[:, j * tw:j * tw + LANES], hst_ref[:, j * tw + LANES:(j + 1) * tw]]
    h_out = lax.fori_loop(0, tl // tsub, sub_block, tuple(h_in))
    for j in range(S5_TILES):
        hst_ref[:, j * tw:j * tw + LANES] = h_out[2 * j]
        hst_ref[:, j * tw + LANES:(j + 1) * tw] = h_out[2 * j + 1]

    if pitch > tl:
        for y_ref in y_refs:
            for b in range(nb):
                y_ref[b * pitch + tl:(b + 1) * pitch, :] = jnp.zeros((pitch - tl, LANES), F32)

    @pl.when(tb == pl.num_programs(1) - 1)
    def _():
        for j in range(S5_TILES):
            hr_ref[:, j * LANES:(j + 1) * LANES] = h_out[2 * j]
            hi_ref[:, j * LANES:(j + 1) * LANES] = h_out[2 * j + 1]


def _s5(u, h0r, h0i, bm, a, cm, d, *, tl):
    nblk, rows, _ = u.shape
    pitch = _s5_pitch(tl)
    nb = rows // pitch
    assert S5_SUB_ROWS % nb == 0 and tl % (S5_SUB_ROWS // nb) == 0
    chunk3 = lambda s: pl.BlockSpec((None,) + s, lambda c, t: (c, 0, 0))
    st_spec = pl.BlockSpec((nb, S5_NS), lambda c, t: (0, c))
    nu = S5_PIECES
    u_specs = [pl.BlockSpec((None, rows, LANES), functools.partial(lambda c, t, k: (t, 0, nu * c + k), k=k))
               for k in range(nu)]
    y_spec = pl.BlockSpec((None, rows, LANES), lambda c, t: (t, 0, c))
    y_shape = jax.ShapeDtypeStruct((nblk, rows, D_MODEL // nu), F32)
    res = pl.pallas_call(
        functools.partial(_s5_kernel, nb=nb, tl=tl),
        grid=(S5_CHUNKS, nblk),
        in_specs=u_specs + [st_spec, st_spec, chunk3((S5_CH, 2 * S5_NS)), chunk3((1, 2 * S5_NS)),
                            chunk3((2 * S5_NS, S5_CH)), chunk3((1, S5_CH))],
        out_specs=[y_spec] * nu + [st_spec, st_spec],
        out_shape=[y_shape] * nu + [jax.ShapeDtypeStruct(h0r.shape, F32), jax.ShapeDtypeStruct(h0i.shape, F32)],
        scratch_shapes=[pltpu.VMEM((nb, 2 * S5_NS), F32)],
        compiler_params=_params(("parallel", "arbitrary")),
        name="s5",
    )(*([u] * nu), h0r, h0i, bm, a, cm, d)
    return res[:nu], res[nu], res[nu + 1]


def _expand_mats():
    r = lax.broadcasted_iota(jnp.int32, (3 * LANES, M_HEADS * M_HEADDIM), 0) % LANES
    c = lax.broadcasted_iota(jnp.int32, (3 * LANES, M_HEADS * M_HEADDIM), 1) // M_HEADDIM
    e64 = (r == c).astype(BF16)
    r = lax.broadcasted_iota(jnp.int32, (3 * LANES, M_HEADS * M_CHUNK), 0) % LANES
    c = lax.broadcasted_iota(jnp.int32, (3 * LANES, M_HEADS * M_CHUNK), 1) // M_CHUNK
    e128 = (r == c).astype(BF16)
    return e64, e128


LOG2E = 1.4426950408889634


def _ssd_kernel(zs_ref, act_ref, dt_ref, alog_ref, dsk_ref, nw_ref, e64_ref, e128_ref,
                y_ref, hout_ref,
                ht_ref, xdt_ref, xdte_ref, eacs_ref, rall_ref, yacc_ref):
    c = pl.program_id(1)
    last = pl.num_programs(1) - 1
    q = M_CHUNK
    cs = 512

    @pl.when(c == 0)
    def _():
        ht_ref[...] = jnp.zeros_like(ht_ref)

    dt = dt_ref[...]
    a2 = dt * (-jnp.exp(alog_ref[...]) * LOG2E)
    ri = lax.broadcasted_iota(jnp.int32, (q, q), 0)
    ci = lax.broadcasted_iota(jnp.int32, (q, q), 1)
    causal = ri >= ci
    acs = jnp.dot(causal.astype(F32), a2, preferred_element_type=F32, precision=lax.Precision.HIGHEST)
    acs_t = acs.T
    dt2 = _split2(dt)
    eacs2 = _split2(jnp.exp2(acs))
    dte2 = _split2(jnp.exp2(acs[q - 1:q, :] - acs))
    for j in range(0, M_INNER, cs):
        sl = slice(j, j + cs)
        e = e64_ref[0:2 * LANES, sl]
        xdt = act_ref[:, sl] * _dot(dt2, e)
        xdt_ref[:, sl] = xdt.astype(BF16)
        xdte_ref[:, sl] = (xdt * _dot(dte2, e)).astype(BF16)
        eacs_ref[:, sl] = _dot(eacs2, e)
    acs3 = _split3(acs)
    for j in range(0, M_HEADS * q, cs):
        sl = slice(j, j + cs)
        rall_ref[:, sl] = _dot(acs3, e128_ref[:, sl])

    hp = M_HPG * M_HEADDIM
    lane = lax.broadcasted_iota(jnp.int32, (q, LANES), 1)
    first_head = lane < M_HEADDIM
    for g in range(M_GROUPS):
        gs = slice(g * hp, (g + 1) * hp)
        bm_t = act_ref[:, M_INNER + g * M_STATE:M_INNER + (g + 1) * M_STATE].T.astype(BF16)
        cm = act_ref[:, M_INNER + (M_GROUPS + g) * M_STATE:M_INNER + (M_GROUPS + g + 1) * M_STATE].astype(BF16)
        cbm = _dot(cm, bm_t)
        y_off = _dot(cm, ht_ref[:, gs].astype(BF16))
        st_t = _dot(bm_t, xdte_ref[:, gs])
        for pr in range(M_HPG // 2):
            h0 = g * M_HPG + 2 * pr
            ps = slice(h0 * M_HEADDIM, (h0 + 2) * M_HEADDIM)
            ms = []
            for h in (h0, h0 + 1):
                d2 = rall_ref[:, h * q:(h + 1) * q] - acs_t[h:h + 1, :]
                ms.append((cbm * jnp.where(causal, jnp.exp2(d2), 0.0)).astype(BF16))
            xp = xdt_ref[:, ps]
            zero = jnp.zeros_like(xp)
            rhs = jnp.concatenate([jnp.where(first_head, xp, zero), jnp.where(first_head, zero, xp)], axis=0)
            yd = _dot(jnp.concatenate(ms, axis=1), rhs)
            yacc_ref[:, ps] = yd + y_off[:, ps.start - gs.start:ps.stop - gs.start] * eacs_ref[:, ps] \
                + dsk_ref[:, ps] * act_ref[:, ps]
        ht_ref[:, gs] = eacs_ref[q - 1:q, gs] * ht_ref[:, gs] + st_t

    ss = jnp.zeros((q, 1), F32)
    for j in range(0, M_INNER, cs):
        sl = slice(j, j + cs)
        gt = yacc_ref[:, sl] * zs_ref[:, sl]
        yacc_ref[:, sl] = gt
        ss = ss + jnp.sum(gt * gt, axis=-1, keepdims=True)
    inv = lax.rsqrt(ss * (1.0 / M_INNER) + EPS)
    for j in range(0, M_INNER, cs):
        sl = slice(j, j + cs)
        y_ref[:, sl] = yacc_ref[:, sl] * inv * nw_ref[:, sl]

    @pl.when(c == last)
    def _():
        for j in range(0, M_INNER, LANES):
            hout_ref[j:j + LANES, :] = ht_ref[:, j:j + LANES].T


def _ssd(zs, act, dt, alog, dsk, nw, e64, e128, *, nbatch, seq):
    nc = seq // M_CHUNK
    q = M_CHUNK
    row = lambda n: pl.BlockSpec((q, n), lambda b, c: (b * nc + c, 0))
    return pl.pallas_call(
        _ssd_kernel,
        grid=(nbatch, nc),
        in_specs=[row(M_INNER), row(M_CONV_DIM), row(LANES), _const_spec(alog.shape),
                  _const_spec(dsk.shape), _const_spec(nw.shape), _const_spec(e64.shape), _const_spec(e128.shape)],
        out_specs=[row(M_INNER), pl.BlockSpec((None, M_INNER, M_STATE), lambda b, c: (b, 0, 0))],
        out_shape=[jax.ShapeDtypeStruct((nbatch * seq, M_INNER), F32),
                   jax.ShapeDtypeStruct((nbatch, M_INNER, M_STATE), F32)],
        scratch_shapes=[pltpu.VMEM((M_STATE, M_INNER), F32), pltpu.VMEM((q, M_INNER), BF16),
                        pltpu.VMEM((q, M_INNER), BF16), pltpu.VMEM((q, M_INNER), F32),
                        pltpu.VMEM((q, M_HEADS * q), F32), pltpu.VMEM((q, M_INNER), F32)],
        compiler_params=_params(("parallel", "arbitrary")),
        name="ssd",
    )(zs, act, dt, alog, dsk, nw, e64, e128)


SSD_STEP_BB = 8


def _ssd_step_kernel(zs_ref, xbc_ref, dt_ref, cbuf_ref, h0_ref, cw_ref, cb_ref, alog_ref, dsk_ref, nw_ref,
                     e64_ref, y_ref, hout_ref, convout_ref):
    bb = SSD_STEP_BB
    x_new = xbc_ref[...]
    conv = cb_ref[...] + x_new * cw_ref[M_CONV - 1:M_CONV, :]
    for k in range(M_CONV - 1):
        conv = conv + cbuf_ref[:, k, :] * cw_ref[k:k + 1, :]
    act = _silu(conv)
    for k in range(M_CONV - 2):
        convout_ref[:, k, :] = cbuf_ref[:, k + 1, :]
    convout_ref[:, M_CONV - 2, :] = x_new

    xs = act[:, :M_INNER]
    bmat = act[:, M_INNER:M_INNER + M_GROUPS * M_STATE]
    cmat = act[:, M_INNER + M_GROUPS * M_STATE:]
    dt = dt_ref[...]
    da = jnp.exp(dt * (-jnp.exp(alog_ref[...])))
    dt_e = _dot(_split3(dt), e64_ref[...])
    da_e = _dot(_split3(da), e64_ref[...])
    xdt = xs * dt_e

    hp = M_HPG * M_HEADDIM
    lane_group = lax.broadcasted_iota(jnp.int32, (M_GROUPS, M_INNER), 1) // hp
    row_group = lax.broadcasted_iota(jnp.int32, (M_GROUPS, M_INNER), 0)
    gmask = lane_group == row_group
    ones_rows = lax.broadcasted_iota(jnp.int32, (SUBLANES, M_STATE), 0) >= M_GROUPS

    cbx = []
    for g in range(M_GROUPS):
        cbg = jnp.sum(cmat[:, g * M_STATE:(g + 1) * M_STATE] * bmat[:, g * M_STATE:(g + 1) * M_STATE],
                      axis=-1, keepdims=True)
        cbx.append(jnp.broadcast_to(cbg, (bb, hp)))
    y = jnp.concatenate(cbx, axis=1) * xdt + dsk_ref[...] * xs

    y_off_rows = []
    for b in range(bb):
        xsel = jnp.where(gmask, jnp.broadcast_to(xdt[b:b + 1, :], (M_GROUPS, M_INNER)), 0.0)
        lhs = jnp.concatenate([xsel, *_pieces3(da_e[b:b + 1, :]), jnp.zeros((1, M_INNER), F32)],
                              axis=0).astype(BF16)
        bm4 = jnp.concatenate([bmat[b:b + 1, g * M_STATE:(g + 1) * M_STATE] for g in range(M_GROUPS)], axis=0)
        cm4 = jnp.concatenate([cmat[b:b + 1, g * M_STATE:(g + 1) * M_STATE] for g in range(M_GROUPS)], axis=0)
        zeros4 = jnp.zeros((SUBLANES - M_GROUPS, M_STATE), F32)
        rhs = jnp.concatenate([jnp.concatenate([bm4, zeros4], axis=0),
                               jnp.where(ones_rows, 1.0, 0.0)], axis=1).astype(BF16)
        res = lax.dot_general(lhs, rhs, _TN, preferred_element_type=F32)
        h0 = h0_ref[b]
        hout_ref[b] = res[:, M_STATE:] * h0 + res[:, :M_STATE]
        cm8 = jnp.concatenate([cm4, zeros4], axis=0).astype(BF16)
        yo = lax.dot_general(cm8, h0.astype(BF16), _NT, preferred_element_type=F32)
        y_off_rows.append(jnp.sum(jnp.where(gmask, yo[:M_GROUPS, :], 0.0), axis=0, keepdims=True))
    y = y + da_e * jnp.concatenate(y_off_rows, axis=0)

    y_ref[...] = _rms(y * zs_ref[...], nw_ref[...])


def _ssd_step(zs, xbc, dt, cbuf, h0, cw, cb, alog, dsk, nw, e64):
    n = zs.shape[0]
    bb = SSD_STEP_BB
    row = lambda w: pl.BlockSpec((bb, w), lambda i: (i, 0))
    blk3 = lambda s: pl.BlockSpec((bb,) + s, lambda i: (i, 0, 0))
    return pl.pallas_call(
        _ssd_step_kernel,
        grid=(n // bb,),
        in_specs=[row(M_INNER), row(M_CONV_DIM), row(LANES), blk3((M_CONV - 1, M_CONV_DIM)),
                  blk3((M_INNER, M_STATE)),
                  _const_spec(cw.shape), _const_spec(cb.shape), _const_spec(alog.shape),
                  _const_spec(dsk.shape), _const_spec(nw.shape), _const_spec(e64.shape)],
        out_specs=[row(M_INNER), blk3((M_INNER, M_STATE)), blk3((M_CONV - 1, M_CONV_DIM))],
        out_shape=[jax.ShapeDtypeStruct((n, M_INNER), F32),
                   jax.ShapeDtypeStruct((n, M_INNER, M_STATE), F32),
                   jax.ShapeDtypeStruct((n, M_CONV - 1, M_CONV_DIM), F32)],
        compiler_params=_params(("parallel",)),
        name="ssd_step",
    )(zs, xbc, dt, cbuf, h0, cw, cb, alog, dsk, nw, e64)


FFN_CHUNK = 1408


def _out_kernel(x_ref, *refs):
    y5_refs = refs[:S5_PIECES]
    (yb_ref, gate_ref, gtm_ref, shf_ref, scf_ref, gtf_ref, wglu_ref, bglu_ref, wb5_ref, wbs_ref, wout_ref,
     npm_ref, npf_ref, npo_ref, wfi_ref, wfo_ref, o_ref) = refs[S5_PIECES:]
    rows = x_ref.shape[0]
    y5 = jnp.concatenate([y5_refs[k][0:rows, c * LANES:(c + 1) * LANES]
                          for c in range(S5_CHUNKS) for k in range(S5_PIECES)], axis=1)
    ya = _gelu_tanh(y5)
    glu = ya * _sigmoid(_dot(ya.astype(BF16), wglu_ref[...]) + bglu_ref[...])
    merged = (gate_ref[:, :D_MODEL] * _dot(glu.astype(BF16), wb5_ref[...])
              + gate_ref[:, D_MODEL:] * _dot(yb_ref[...].astype(BF16), wbs_ref[...]))
    mix = _dot(merged.astype(BF16), wout_ref[...])
    x1 = x_ref[...] + gtm_ref[...] * _rms(mix, npm_ref[...])

    hb = (_rms(x1, npf_ref[...]) * (1.0 + scf_ref[...]) + shf_ref[...]).astype(BF16)
    f = jnp.zeros_like(x1)
    for j in range(0, D_FF, FFN_CHUNK):
        gg = _dot(hb, wfi_ref[:, j:j + FFN_CHUNK])
        uu = _dot(hb, wfi_ref[:, D_FF + j:D_FF + j + FFN_CHUNK])
        f = f + _dot((_silu(gg) * uu).astype(BF16), wfo_ref[j:j + FFN_CHUNK, :])
    o_ref[...] = x1 + gtf_ref[...] * _rms(f, npo_ref[...])


def _out_stage(x, y5, yb, gates, mod, ws, *, per_row, tm, rows_per_batch):
    t = x.shape[0]
    row = lambda n: pl.BlockSpec((tm, n), lambda i: (i, 0))
    mods = _mod_specs(per_row, tm, rows_per_batch // tm if not per_row else 1, (2, 3, 4, 5))
    _, y5_spec = _s5_layout(per_row, t, tm, rows_per_batch, D_MODEL // S5_PIECES)
    return pl.pallas_call(
        _out_kernel,
        grid=(t // tm,),
        in_specs=[row(D_MODEL)] + [y5_spec] * S5_PIECES + [row(M_INNER), row(2 * D_MODEL)] + mods
                 + [_const_spec(w.shape) for w in ws],
        out_specs=row(D_MODEL),
        out_shape=jax.ShapeDtypeStruct((t, D_MODEL), F32),
        compiler_params=_params(("parallel",)),
        name="out_stage",
    )(x, *y5, yb, gates, mod, mod, mod, mod, *ws)


def _pad_lanes(v):
    return jnp.pad(v.reshape(1, -1), ((0, 0), (0, LANES - v.shape[-1])))


def kernel(x_prompt, x_sample, state_s5_re, state_s5_im, state_ssm, state_conv, c_prompt, c_sample, w_ada, b_ada, norm_pre_mix, norm_post_mix, norm_pre_ffn, norm_post_ffn, w_in, s5_lam_re, s5_lam_im, s5_log_dt, s5_b_re, s5_b_im, s5_c_re, s5_c_im, s5_d, s5_w_glu, s5_b_glu, m_conv_w, m_conv_b, m_dt_bias, m_a_log, m_d, m_norm, w_branch_s5, w_branch_ssd, w_out, w_ffn_in, w_ffn_out):
    bp, seq, _ = x_prompt.shape
    bs = x_sample.shape[0]
    assert x_sample.shape[1] == 1 and w_ada.shape[0] == 1 and seq % M_CHUNK == 0
    row1 = lambda v: v.reshape(1, -1)

    w_in0 = w_in[0]
    ws_in = (w_in0[:, :OFF_Z].astype(BF16), w_in0[:, OFF_Z:OFF_XBC].astype(BF16),
             w_in0[:, OFF_XBC:OFF_DT].astype(BF16),
             jnp.pad(w_in0[:, OFF_DT:OFF_GA], ((0, 0), (0, LANES - M_HEADS))).astype(BF16),
             w_in0[:, OFF_GA:].astype(BF16))
    ws_out = (s5_w_glu[0].astype(BF16), row1(s5_b_glu[0]), w_branch_s5[0].astype(BF16),
              w_branch_ssd[0].astype(BF16), w_out[0].astype(BF16), row1(norm_post_mix[0]),
              row1(norm_pre_ffn[0]), row1(norm_post_ffn[0]), w_ffn_in[0].astype(BF16), w_ffn_out[0].astype(BF16))
    cw, cb = m_conv_w[0], row1(m_conv_b[0])
    dtb, alog = _pad_lanes(m_dt_bias[0]), _pad_lanes(m_a_log[0])
    dsk = row1(jnp.repeat(m_d[0], M_HEADDIM))
    nw = row1(m_norm[0])
    e64, e128 = _expand_mats()

    mod = _ada(jnp.concatenate([c_prompt, c_sample], axis=0), w_ada[0], b_ada[0])
    mod_p = mod[:bp].reshape(bp, 1, 6 * D_MODEL)
    mod_s = mod[bp:]

    ar, ai, bbr, bbi = _s5_disc(s5_lam_re[0], s5_lam_im[0], s5_log_dt[0], s5_b_re[0], s5_b_im[0])
    s5m = _s5_matrices(ar, ai, bbr, bbi, s5_c_re[0], s5_c_im[0], s5_d[0])
    g_pre = row1(norm_pre_mix[0])

    xp = x_prompt.reshape(bp * seq, D_MODEL)
    tm = min(ROW_TILE, seq)
    u, zs, act, dt, gates, tail = _in_proj(xp, mod_p, g_pre, ws_in, dtb, cw, cb,
                                           per_row=False, tm=tm, rows_per_batch=seq)
    p_conv = tail[:, SUBLANES - (M_CONV - 1):, :]
    zeros_s5 = jnp.zeros((bp, S5_GROUPS * S5_STATE), F32)
    y5, p_re, p_im = _s5(u, zeros_s5, zeros_s5, *s5m, tl=tm)
    yb, p_ssm = _ssd(zs, act, dt, alog, dsk, nw, e64, e128, nbatch=bp, seq=seq)
    y_prompt = _out_stage(xp, y5, yb, gates, mod_p, ws_out, per_row=False, tm=tm, rows_per_batch=seq)

    xs = x_sample.reshape(bs, D_MODEL)
    u, zs, xbc, dt, gates = _in_proj(xs, mod_s, g_pre, ws_in, dtb, cw, cb, per_row=True, tm=bs, rows_per_batch=1)
    y5, s_re, s_im = _s5(u, state_s5_re[0].reshape(bs, -1), state_s5_im[0].reshape(bs, -1), *s5m, tl=1)
    yb, s_ssm, s_conv = _ssd_step(zs, xbc, dt, state_conv[0], state_ssm[0].reshape(bs, M_INNER, M_STATE),
                                  cw, cb, alog, dsk, nw, e64)
    y_sample = _out_stage(xs, y5, yb, gates, mod_s, ws_out, per_row=True, tm=bs, rows_per_batch=1)

    s5_shape = (1, -1, S5_GROUPS, S5_STATE)
    ssm_shape = (1, -1, M_HEADS, M_HEADDIM, M_STATE)
    return (y_prompt.reshape(bp, seq, D_MODEL), y_sample.reshape(bs, 1, D_MODEL),
            p_re.reshape(s5_shape), p_im.reshape(s5_shape), p_ssm.reshape(ssm_shape), p_conv[None],
            s_re.reshape(s5_shape), s_im.reshape(s5_shape), s_ssm.reshape(ssm_shape), s_conv[None])
```

```python
import functools

import jax
import jax.numpy as jnp
from jax import lax
from jax.experimental import pallas as pl
from jax.experimental.pallas import tpu as pltpu

F32 = jnp.float32
BF16 = jnp.bfloat16

D_MODEL = 1024
EPS = 1e-6
S5_GROUP = 16
S5_GROUPS = 64
S5_STATE = 64
M_INNER = 2048
M_HEADDIM = 64
M_HEADS = 32
M_GROUPS = 4
M_HPG = 8
M_STATE = 128
M_CONV = 4
M_CONV_DIM = 3072
M_CHUNK = 128
D_FF = 2816
OFF_Z = D_MODEL
OFF_XBC = OFF_Z + M_INNER
OFF_DT = OFF_XBC + M_CONV_DIM
OFF_GA = OFF_DT + M_HEADS
IN_COLS = OFF_GA + 2 * D_MODEL

LANES = 128
SUBLANES = 8
VMEM_LIMIT = 56 * 1024 * 1024

S5_CH = 256
S5_NS = 1024
S5_CHUNKS = D_MODEL // S5_CH
S5_TILES = S5_NS // LANES
S5_PIECES = S5_CH // LANES
S5_SUB_ROWS = 128
ROW_TILE = 256
INPROJ_SLAB = 512
INPROJ_DEPTH = 3

_NT = (((1,), (1,)), ((), ()))
_TN = (((0,), (0,)), ((), ()))


def _const_spec(shape):
    nd = len(shape)
    return pl.BlockSpec(shape, lambda *_: (0,) * nd, pipeline_mode=pl.Buffered(1))


def _params(sem):
    return pltpu.CompilerParams(dimension_semantics=sem, vmem_limit_bytes=VMEM_LIMIT)


def _sigmoid(x):
    return 0.5 * jnp.tanh(0.5 * x) + 0.5


def _silu(x):
    hx = 0.5 * x
    return hx + hx * jnp.tanh(hx)


def _softplus(x):
    return jnp.maximum(x, 0.0) + jnp.log1p(jnp.exp(-jnp.abs(x)))


def _gelu_tanh(x):
    return 0.5 * x * (1.0 + jnp.tanh(0.7978845608028654 * (x + 0.044715 * (x * x * x))))


def _rms(x, g):
    return x * lax.rsqrt(jnp.mean(x * x, axis=-1, keepdims=True) + EPS) * g


def _dot(a, b):
    return jnp.dot(a, b, preferred_element_type=F32)


def _pieces3(x):
    hi = x.astype(BF16).astype(F32)
    r1 = x - hi
    mid = r1.astype(BF16).astype(F32)
    lo = (r1 - mid).astype(BF16).astype(F32)
    return hi, mid, lo


def _split2(x):
    return jnp.concatenate(_pieces3(x)[:2], axis=-1).astype(BF16)


def _split3(x):
    return jnp.concatenate(_pieces3(x), axis=-1).astype(BF16)


def _ada_kernel(c_ref, w_ref, b_ref, o_ref):
    s = _silu(c_ref[...]).astype(BF16)
    o_ref[...] = _dot(s, w_ref[...].astype(BF16)) + b_ref[...]


def _ada(c, w, b):
    n = c.shape[0]
    tn = 512
    return pl.pallas_call(
        _ada_kernel,
        grid=(6 * D_MODEL // tn,),
        in_specs=[_const_spec((n, D_MODEL)),
                  pl.BlockSpec((D_MODEL, tn), lambda j: (0, j)),
                  pl.BlockSpec((1, tn), lambda j: (0, j))],
        out_specs=pl.BlockSpec((n, tn), lambda j: (0, j)),
        out_shape=jax.ShapeDtypeStruct((n, 6 * D_MODEL), F32),
        compiler_params=_params(("arbitrary",)),
        name="ada",
    )(c, w, b.reshape(1, -1))


def _s5_disc_kernel(lr_ref, li_ref, ldt_ref, br_ref, bi_ref, ar_ref, ai_ref, bbr_ref, bbi_ref):
    lr = lr_ref[...]
    li = li_ref[...]
    dt = jnp.exp(ldt_ref[...])
    mag = jnp.exp(lr * dt)
    ar = mag * jnp.cos(li * dt)
    ai = mag * jnp.sin(li * dt)
    den = lr * lr + li * li
    nr = ar - 1.0
    f_re = (nr * lr + ai * li) / den
    f_im = (ai * lr - nr * li) / den
    br = br_ref[...]
    bi = bi_ref[...]
    ar_ref[...] = ar
    ai_ref[...] = ai
    bbr_ref[...] = f_re * br - f_im * bi
    bbi_ref[...] = f_re * bi + f_im * br


def _s5_disc(lam_re, lam_im, log_dt, b_re, b_im):
    g, n = lam_re.shape
    w = S5_GROUP * n
    tile = lambda a: jnp.tile(a, (1, S5_GROUP))
    ldt = jnp.broadcast_to(log_dt[:, None], (g, w))
    brt = jnp.swapaxes(b_re, 1, 2).reshape(g, w)
    bit = jnp.swapaxes(b_im, 1, 2).reshape(g, w)
    shp = jax.ShapeDtypeStruct((g, w), F32)
    ar, ai, bbr, bbi = pl.pallas_call(
        _s5_disc_kernel,
        out_shape=(shp, shp, shp, shp),
        name="s5_disc",
    )(tile(lam_re), tile(lam_im), ldt, brt, bit)
    return (ar[:, :n], ai[:, :n], bbr.reshape(g, S5_GROUP, n), bbi.reshape(g, S5_GROUP, n))


def _s5_matrices(abar_re, abar_im, bbar_re, bbar_im, c_re, c_im, d_skip):
    gl = S5_CH // S5_GROUP
    tw = 2 * LANES
    ns = 2 * S5_NS

    def state_group(i):
        return 2 * (i // tw) + (i % LANES) // S5_STATE

    bre, bim = bbar_re.reshape(D_MODEL, S5_STATE), bbar_im.reshape(D_MODEL, S5_STATE)
    pattern = jnp.tile(jnp.concatenate([bre, bre, bim, bim], axis=1), (1, S5_TILES))
    row_g = (lax.broadcasted_iota(jnp.int32, (D_MODEL, ns), 0) // S5_GROUP) % gl
    col_g = state_group(lax.broadcasted_iota(jnp.int32, (D_MODEL, ns), 1))
    bm = jnp.where(row_g == col_g, pattern, 0.0).astype(BF16).reshape(S5_CHUNKS, S5_CH, ns)

    def by_tile(c):
        return jnp.swapaxes(c, 1, 2).reshape(S5_CHUNKS, S5_TILES, 1, 2, S5_STATE, S5_GROUP)
    cols = jnp.concatenate([by_tile(c_re), by_tile(-c_im)], axis=2).reshape(S5_CHUNKS * ns, S5_GROUP)
    pattern = jnp.tile(cols, (1, gl))
    row_g = state_group(lax.broadcasted_iota(jnp.int32, (S5_CHUNKS * ns, S5_CH), 0) % ns)
    col_g = lax.broadcasted_iota(jnp.int32, (S5_CHUNKS * ns, S5_CH), 1) // S5_GROUP
    cm = jnp.where(row_g == col_g, pattern, 0.0).astype(BF16).reshape(S5_CHUNKS, ns, S5_CH)

    by_tile_a = lambda v: v.reshape(S5_CHUNKS, S5_TILES, 1, LANES)
    a = jnp.concatenate([by_tile_a(abar_re), by_tile_a(abar_im)], axis=2).reshape(S5_CHUNKS, 1, ns)
    d = d_skip.reshape(S5_CHUNKS, 1, S5_CH)
    return bm, a, cm, d


def _inproj_kernel(x_ref, sh_ref, sc_ref, g_ref, wm_ref, wdt_ref, wg_ref, dtb_ref, cw_ref, cb_ref,
                   u_ref, zs_ref, xbc_ref, dt_ref, gate_ref, *rest, tiles_per_seq):
    accs = rest[-INPROJ_DEPTH:]
    h = _rms(x_ref[...], g_ref[...]) * (1.0 + sc_ref[...]) + sh_ref[...]
    hb = h.astype(BF16)
    rows = hb.shape[0]
    step = INPROJ_SLAB

    if u_ref.shape[0] > rows:
        u_ref[rows:, :] = jnp.zeros((u_ref.shape[0] - rows, D_MODEL), F32)

    if tiles_per_seq is None:
        conv_silu = None
    else:
        tail_ref, carry_ref = rest[:2]

        @pl.when(pl.program_id(0) % tiles_per_seq == 0)
        def _():
            carry_ref[...] = jnp.zeros_like(carry_ref)

        def conv_silu(r, sl):
            ext = jnp.concatenate([carry_ref[:, sl], r], axis=0)
            t = ext * cw_ref[0:1, sl]
            for k in range(1, M_CONV):
                t = ext * cw_ref[k:k + 1, sl] + pltpu.roll(t, 1, axis=0)
            carry_ref[:, sl] = r[rows - SUBLANES:, :]
            tail_ref[:, sl] = r[rows - SUBLANES:, :]
            return _silu(t[SUBLANES:, :] + cb_ref[:, sl])

    def slabs(o_ref, w_ref, w_off, fn):
        n = o_ref.shape[-1]
        return [(o_ref, slice(j, j + min(step, n - j)), w_ref, w_off + j, fn) for j in range(0, n, step)]

    su = slabs(u_ref, wm_ref, 0, None)
    sz = slabs(zs_ref, wm_ref, OFF_Z, lambda r, sl: _silu(r))
    sx = slabs(xbc_ref, wm_ref, OFF_XBC, conv_silu)
    sd = slabs(dt_ref, wdt_ref, 0, lambda r, sl: _softplus(r + dtb_ref[:, sl]))
    sg = slabs(gate_ref, wg_ref, 0, lambda r, sl: _sigmoid(r))
    light = su + sz + sg + sd
    order = []
    while sx or light:
        if sx:
            order.append(sx.pop(0))
        if light:
            order.append(light.pop(0))

    def matmul(s):
        o_ref, sl, w_ref, w_col, _ = order[s]
        accs[s % INPROJ_DEPTH][:, 0:sl.stop - sl.start] = _dot(hb, w_ref[:, w_col:w_col + sl.stop - sl.start])

    for s in range(min(INPROJ_DEPTH - 1, len(order))):
        matmul(s)
    for s, (o_ref, sl, _, _, fn) in enumerate(order):
        if s + INPROJ_DEPTH - 1 < len(order):
            matmul(s + INPROJ_DEPTH - 1)
        r = accs[s % INPROJ_DEPTH][:, 0:sl.stop - sl.start]
        o_ref[0:rows, sl] = r if fn is None else fn(r, sl)


def _mod_specs(per_row, tm, rows_per_mod, cols):
    if per_row:
        return [pl.BlockSpec((tm, D_MODEL), functools.partial(lambda i, c: (i, c), c=c)) for c in cols]
    return [pl.BlockSpec((None, 1, D_MODEL), functools.partial(lambda i, c: (i // rows_per_mod, 0, c), c=c))
            for c in cols]


def _s5_layout(per_row, t, tm, rows_per_batch, width):
    if per_row:
        return (1, t, width), pl.BlockSpec((None, tm, width), lambda i: (0, i, 0))
    nblk, pitch = rows_per_batch // tm, _s5_pitch(tm)
    return ((nblk, (t // rows_per_batch) * pitch, width),
            pl.BlockSpec((None, pitch, width), lambda i: (i % nblk, i // nblk, 0)))


def _in_proj(x, mod, g, ws, dtb, cw, cb, *, per_row, tm, rows_per_batch):
    t = x.shape[0]
    row = lambda n: pl.BlockSpec((tm, n), lambda i: (i, 0))
    outs = (M_INNER, M_CONV_DIM, LANES, 2 * D_MODEL)
    u_shape, u_spec = _s5_layout(per_row, t, tm, rows_per_batch, D_MODEL)
    out_specs = [u_spec] + [row(n) for n in outs]
    out_shape = [jax.ShapeDtypeStruct(u_shape, F32)] + [jax.ShapeDtypeStruct((t, n), F32) for n in outs]
    scratch = []
    tiles_per_seq = None
    if not per_row:
        tiles_per_seq = rows_per_batch // tm
        out_specs.append(pl.BlockSpec((None, SUBLANES, M_CONV_DIM), lambda i: (i // tiles_per_seq, 0, 0)))
        out_shape.append(jax.ShapeDtypeStruct((t // rows_per_batch, SUBLANES, M_CONV_DIM), F32))
        scratch.append(pltpu.VMEM((SUBLANES, M_CONV_DIM), F32))
    scratch += [pltpu.VMEM((tm, INPROJ_SLAB), F32)] * INPROJ_DEPTH
    consts = (g,) + tuple(ws) + (dtb, cw, cb)
    return pl.pallas_call(
        functools.partial(_inproj_kernel, tiles_per_seq=tiles_per_seq),
        grid=(t // tm,),
        in_specs=[row(D_MODEL)] + _mod_specs(per_row, tm, rows_per_batch // tm if not per_row else 1, (0, 1))
                 + [_const_spec(w.shape) for w in consts],
        out_specs=out_specs,
        out_shape=out_shape,
        scratch_shapes=scratch,
        compiler_params=_params(("arbitrary",)),
        name="in_proj",
    )(x, mod, mod, *consts)


def _s5_pitch(tl):
    return tl if tl == 1 else tl + SUBLANES


def _s5_kernel(*refs, nb, tl):
    nu = S5_PIECES
    u_refs = refs[:nu]
    h0r_ref, h0i_ref, bm_ref, a_ref, cm_ref, d_ref = refs[nu:nu + 6]
    y_refs = refs[nu + 6:2 * nu + 6]
    hr_ref, hi_ref, hst_ref, bu0_ref, xs0_ref, bu1_ref, xs1_ref = refs[2 * nu + 6:]
    bufs = ((bu0_ref, xs0_ref), (bu1_ref, xs1_ref))
    tb = pl.program_id(1)
    tsub = S5_SUB_ROWS // nb
    tw = 2 * LANES
    pitch = _s5_pitch(tl)

    def step_rows(t):
        return pl.ds(t, nb, stride=pitch) if pitch > 1 else pl.ds(0, nb)

    @pl.when(tb == 0)
    def _():
        for j in range(S5_TILES):
            hst_ref[:, j * tw:j * tw + LANES] = h0r_ref[:, j * LANES:(j + 1) * LANES]
            hst_ref[:, j * tw + LANES:(j + 1) * tw] = h0i_ref[:, j * LANES:(j + 1) * LANES]

    ar = [jnp.broadcast_to(a_ref[:, j * tw:j * tw + LANES], (nb, LANES)) for j in range(S5_TILES)]
    ai = [jnp.broadcast_to(a_ref[:, j * tw + LANES:(j + 1) * tw], (nb, LANES)) for j in range(S5_TILES)]
    dsk = d_ref[...]

    def project_in(t0, bu_ref, xs_ref):
        xs = jnp.concatenate([jnp.concatenate([r[step_rows(t0 + i), :] for r in u_refs], axis=1)
                              for i in range(tsub)], axis=0)
        xs_ref[...] = xs
        lhs = xs.astype(BF16)
        for j in range(S5_TILES):
            bu_ref[:, j * tw:(j + 1) * tw] = _dot(lhs, bm_ref[:, j * tw:(j + 1) * tw])

    def scan_project_out(t0, bu_ref, xs_ref, carry):
        new, hs = [], []
        for j in range(S5_TILES):
            hr, hi = carry[2 * j], carry[2 * j + 1]
            rows = []
            for i in range(tsub):
                br = bu_ref[i * nb:(i + 1) * nb, j * tw:j * tw + LANES]
                bi = bu_ref[i * nb:(i + 1) * nb, j * tw + LANES:(j + 1) * tw]
                hr, hi = ar[j] * hr - ai[j] * hi + br, ar[j] * hi + ai[j] * hr + bi
                rows.append(jnp.concatenate([hr, hi], axis=1))
            new += [hr, hi]
            hs.append(jnp.concatenate(rows, axis=0).astype(BF16))
        y = _dot(jnp.concatenate(hs, axis=1), cm_ref[...]) + dsk * xs_ref[...]
        for i in range(tsub):
            for k, y_ref in enumerate(y_refs):
                y_ref[step_rows(t0 + i), :] = y[i * nb:(i + 1) * nb, k * LANES:(k + 1) * LANES]
        return tuple(new)

    h_in = []
    for j in range(S5_TILES):
        h_in += [hst_ref[:, j * tw:j * tw + LANES], hst_ref[:, j * tw + LANES:(j + 1) * tw]]
    nsub = tl // tsub
    (bu0, xs0), (bu1, xs1) = bufs
    project_in(0, bu0, xs0)
    if nsub == 1:
        h_out = scan_project_out(0, bu0, xs0, tuple(h_in))
    else:
        def pair(p, carry):
            t0 = 2 * p * tsub
            project_in(t0 + tsub, bu1, xs1)
            carry = scan_project_out(t0, bu0, xs0, carry)
            project_in(jnp.minimum(t0 + 2 * tsub, tl - tsub), bu0, xs0)
            return scan_project_out(t0 + tsub, bu1, xs1, carry)
        h_out = lax.fori_loop(0, nsub // 2, pair, tuple(h_in))
    for j in range(S5_TILES):
        hst_ref[:, j * tw:j * tw + LANES] = h_out[2 * j]
        hst_ref[:, j * tw + LANES:(j + 1) * tw] = h_out[2 * j + 1]

    if pitch > tl:
        for y_ref in y_refs:
            for b in range(nb):
                y_ref[b * pitch + tl:(b + 1) * pitch, :] = jnp.zeros((pitch - tl, LANES), F32)

    @pl.when(tb == pl.num_programs(1) - 1)
    def _():
        for j in range(S5_TILES):
            hr_ref[:, j * LANES:(j + 1) * LANES] = h_out[2 * j]
            hi_ref[:, j * LANES:(j + 1) * LANES] = h_out[2 * j + 1]


def _s5(u, h0r, h0i, bm, a, cm, d, *, tl):
    nblk, rows, _ = u.shape
    pitch = _s5_pitch(tl)
    nb = rows // pitch
    nsub = tl * nb // S5_SUB_ROWS
    assert S5_SUB_ROWS % nb == 0 and tl * nb % S5_SUB_ROWS == 0 and (nsub == 1 or nsub % 2 == 0)
    chunk3 = lambda s: pl.BlockSpec((None,) + s, lambda c, t: (c, 0, 0))
    st_spec = pl.BlockSpec((nb, S5_NS), lambda c, t: (0, c))
    nu = S5_PIECES
    u_specs = [pl.BlockSpec((None, rows, LANES), functools.partial(lambda c, t, k: (t, 0, nu * c + k), k=k))
               for k in range(nu)]
    y_spec = pl.BlockSpec((None, rows, LANES), lambda c, t: (t, 0, c))
    y_shape = jax.ShapeDtypeStruct((nblk, rows, D_MODEL // nu), F32)
    res = pl.pallas_call(
        functools.partial(_s5_kernel, nb=nb, tl=tl),
        grid=(S5_CHUNKS, nblk),
        in_specs=u_specs + [st_spec, st_spec, chunk3((S5_CH, 2 * S5_NS)), chunk3((1, 2 * S5_NS)),
                            chunk3((2 * S5_NS, S5_CH)), chunk3((1, S5_CH))],
        out_specs=[y_spec] * nu + [st_spec, st_spec],
        out_shape=[y_shape] * nu + [jax.ShapeDtypeStruct(h0r.shape, F32), jax.ShapeDtypeStruct(h0i.shape, F32)],
        scratch_shapes=[pltpu.VMEM((nb, 2 * S5_NS), F32)]
                       + [pltpu.VMEM((S5_SUB_ROWS, 2 * S5_NS), F32), pltpu.VMEM((S5_SUB_ROWS, S5_CH), F32)] * 2,
        compiler_params=_params(("parallel", "arbitrary")),
        name="s5",
    )(*([u] * nu), h0r, h0i, bm, a, cm, d)
    return res[:nu], res[nu], res[nu + 1]


def _expand_mats():
    r = lax.broadcasted_iota(jnp.int32, (3 * LANES, M_HEADS * M_HEADDIM), 0) % LANES
    c = lax.broadcasted_iota(jnp.int32, (3 * LANES, M_HEADS * M_HEADDIM), 1) // M_HEADDIM
    e64 = (r == c).astype(BF16)
    r = lax.broadcasted_iota(jnp.int32, (3 * LANES, M_HEADS * M_CHUNK), 0) % LANES
    c = lax.broadcasted_iota(jnp.int32, (3 * LANES, M_HEADS * M_CHUNK), 1) // M_CHUNK
    e128 = (r == c).astype(BF16)
    return e64, e128


LOG2E = 1.4426950408889634


def _ssd_kernel(zs_ref, act_ref, dt_ref, alog_ref, dsk_ref, nw_ref, e64_ref, e128_ref,
                y_ref, hout_ref,
                ht_ref, xdt_ref, xdte_ref, eacs_ref, rall_ref, yacc_ref):
    c = pl.program_id(1)
    last = pl.num_programs(1) - 1
    q = M_CHUNK
    cs = 512

    @pl.when(c == 0)
    def _():
        ht_ref[...] = jnp.zeros_like(ht_ref)

    dt = dt_ref[...]
    a2 = dt * (-jnp.exp(alog_ref[...]) * LOG2E)
    ri = lax.broadcasted_iota(jnp.int32, (q, q), 0)
    ci = lax.broadcasted_iota(jnp.int32, (q, q), 1)
    causal = ri >= ci
    acs = jnp.dot(causal.astype(F32), a2, preferred_element_type=F32, precision=lax.Precision.HIGHEST)
    acs_t = acs.T
    dt2 = _split2(dt)
    eacs2 = _split2(jnp.exp2(acs))
    dte2 = _split2(jnp.exp2(acs[q - 1:q, :] - acs))
    for j in range(0, M_INNER, cs):
        sl = slice(j, j + cs)
        e = e64_ref[0:2 * LANES, sl]
        xdt = act_ref[:, sl] * _dot(dt2, e)
        xdt_ref[:, sl] = xdt.astype(BF16)
        xdte_ref[:, sl] = (xdt * _dot(dte2, e)).astype(BF16)
        eacs_ref[:, sl] = _dot(eacs2, e)
    acs3 = _split3(acs)
    for j in range(0, M_HEADS * q, cs):
        sl = slice(j, j + cs)
        rall_ref[:, sl] = _dot(acs3, e128_ref[:, sl])

    hp = M_HPG * M_HEADDIM
    lane = lax.broadcasted_iota(jnp.int32, (q, LANES), 1)
    first_head = lane < M_HEADDIM
    for g in range(M_GROUPS):
        gs = slice(g * hp, (g + 1) * hp)
        bm_t = act_ref[:, M_INNER + g * M_STATE:M_INNER + (g + 1) * M_STATE].T.astype(BF16)
        cm = act_ref[:, M_INNER + (M_GROUPS + g) * M_STATE:M_INNER + (M_GROUPS + g + 1) * M_STATE].astype(BF16)
        cbm = _dot(cm, bm_t)
        y_off = _dot(cm, ht_ref[:, gs].astype(BF16))
        st_t = _dot(bm_t, xdte_ref[:, gs])
        for pr in range(M_HPG // 2):
            h0 = g * M_HPG + 2 * pr
            ps = slice(h0 * M_HEADDIM, (h0 + 2) * M_HEADDIM)
            ms = []
            for h in (h0, h0 + 1):
                d2 = rall_ref[:, h * q:(h + 1) * q] - acs_t[h:h + 1, :]
                ms.append((cbm * jnp.where(causal, jnp.exp2(d2), 0.0)).astype(BF16))
            xp = xdt_ref[:, ps]
            zero = jnp.zeros_like(xp)
            rhs = jnp.concatenate([jnp.where(first_head, xp, zero), jnp.where(first_head, zero, xp)], axis=0)
            yd = _dot(jnp.concatenate(ms, axis=1), rhs)
            yacc_ref[:, ps] = yd + y_off[:, ps.start - gs.start:ps.stop - gs.start] * eacs_ref[:, ps] \
                + dsk_ref[:, ps] * act_ref[:, ps]
        ht_ref[:, gs] = eacs_ref[q - 1:q, gs] * ht_ref[:, gs] + st_t

    ss = jnp.zeros((q, 1), F32)
    for j in range(0, M_INNER, cs):
        sl = slice(j, j + cs)
        gt = yacc_ref[:, sl] * zs_ref[:, sl]
        yacc_ref[:, sl] = gt
        ss = ss + jnp.sum(gt * gt, axis=-1, keepdims=True)
    inv = lax.rsqrt(ss * (1.0 / M_INNER) + EPS)
    for j in range(0, M_INNER, cs):
        sl = slice(j, j + cs)
        y_ref[:, sl] = yacc_ref[:, sl] * inv * nw_ref[:, sl]

    @pl.when(c == last)
    def _():
        for j in range(0, M_INNER, LANES):
            hout_ref[j:j + LANES, :] = ht_ref[:, j:j + LANES].T


def _ssd(zs, act, dt, alog, dsk, nw, e64, e128, *, nbatch, seq):
    nc = seq // M_CHUNK
    q = M_CHUNK
    row = lambda n: pl.BlockSpec((q, n), lambda b, c: (b * nc + c, 0))
    return pl.pallas_call(
        _ssd_kernel,
        grid=(nbatch, nc),
        in_specs=[row(M_INNER), row(M_CONV_DIM), row(LANES), _const_spec(alog.shape),
                  _const_spec(dsk.shape), _const_spec(nw.shape), _const_spec(e64.shape), _const_spec(e128.shape)],
        out_specs=[row(M_INNER), pl.BlockSpec((None, M_INNER, M_STATE), lambda b, c: (b, 0, 0))],
        out_shape=[jax.ShapeDtypeStruct((nbatch * seq, M_INNER), F32),
                   jax.ShapeDtypeStruct((nbatch, M_INNER, M_STATE), F32)],
        scratch_shapes=[pltpu.VMEM((M_STATE, M_INNER), F32), pltpu.VMEM((q, M_INNER), BF16),
                        pltpu.VMEM((q, M_INNER), BF16), pltpu.VMEM((q, M_INNER), F32),
                        pltpu.VMEM((q, M_HEADS * q), F32), pltpu.VMEM((q, M_INNER), F32)],
        compiler_params=_params(("parallel", "arbitrary")),
        name="ssd",
    )(zs, act, dt, alog, dsk, nw, e64, e128)


SSD_STEP_BB = 8


def _ssd_step_kernel(zs_ref, xbc_ref, dt_ref, cbuf_ref, h0_ref, cw_ref, cb_ref, alog_ref, dsk_ref, nw_ref,
                     e64_ref, y_ref, hout_ref, convout_ref):
    bb = SSD_STEP_BB
    x_new = xbc_ref[...]
    conv = cb_ref[...] + x_new * cw_ref[M_CONV - 1:M_CONV, :]
    for k in range(M_CONV - 1):
        conv = conv + cbuf_ref[:, k, :] * cw_ref[k:k + 1, :]
    act = _silu(conv)
    for k in range(M_CONV - 2):
        convout_ref[:, k, :] = cbuf_ref[:, k + 1, :]
    convout_ref[:, M_CONV - 2, :] = x_new

    xs = act[:, :M_INNER]
    bmat = act[:, M_INNER:M_INNER + M_GROUPS * M_STATE]
    cmat = act[:, M_INNER + M_GROUPS * M_STATE:]
    dt = dt_ref[...]
    da = jnp.exp(dt * (-jnp.exp(alog_ref[...])))
    dt_e = _dot(_split3(dt), e64_ref[...])
    da_e = _dot(_split3(da), e64_ref[...])
    xdt = xs * dt_e

    hp = M_HPG * M_HEADDIM
    lane_group = lax.broadcasted_iota(jnp.int32, (M_GROUPS, M_INNER), 1) // hp
    row_group = lax.broadcasted_iota(jnp.int32, (M_GROUPS, M_INNER), 0)
    gmask = lane_group == row_group
    ones_rows = lax.broadcasted_iota(jnp.int32, (SUBLANES, M_STATE), 0) >= M_GROUPS

    cbx = []
    for g in range(M_GROUPS):
        cbg = jnp.sum(cmat[:, g * M_STATE:(g + 1) * M_STATE] * bmat[:, g * M_STATE:(g + 1) * M_STATE],
                      axis=-1, keepdims=True)
        cbx.append(jnp.broadcast_to(cbg, (bb, hp)))
    y = jnp.concatenate(cbx, axis=1) * xdt + dsk_ref[...] * xs

    y_off_rows = []
    for b in range(bb):
        xsel = jnp.where(gmask, jnp.broadcast_to(xdt[b:b + 1, :], (M_GROUPS, M_INNER)), 0.0)
        lhs = jnp.concatenate([xsel, *_pieces3(da_e[b:b + 1, :]), jnp.zeros((1, M_INNER), F32)],
                              axis=0).astype(BF16)
        bm4 = jnp.concatenate([bmat[b:b + 1, g * M_STATE:(g + 1) * M_STATE] for g in range(M_GROUPS)], axis=0)
        cm4 = jnp.concatenate([cmat[b:b + 1, g * M_STATE:(g + 1) * M_STATE] for g in range(M_GROUPS)], axis=0)
        zeros4 = jnp.zeros((SUBLANES - M_GROUPS, M_STATE), F32)
        rhs = jnp.concatenate([jnp.concatenate([bm4, zeros4], axis=0),
                               jnp.where(ones_rows, 1.0, 0.0)], axis=1).astype(BF16)
        res = lax.dot_general(lhs, rhs, _TN, preferred_element_type=F32)
        h0 = h0_ref[b]
        hout_ref[b] = res[:, M_STATE:] * h0 + res[:, :M_STATE]
        cm8 = jnp.concatenate([cm4, zeros4], axis=0).astype(BF16)
        yo = lax.dot_general(cm8, h0.astype(BF16), _NT, preferred_element_type=F32)
        y_off_rows.append(jnp.sum(jnp.where(gmask, yo[:M_GROUPS, :], 0.0), axis=0, keepdims=True))
    y = y + da_e * jnp.concatenate(y_off_rows, axis=0)

    y_ref[...] = _rms(y * zs_ref[...], nw_ref[...])


def _ssd_step(zs, xbc, dt, cbuf, h0, cw, cb, alog, dsk, nw, e64):
    n = zs.shape[0]
    bb = SSD_STEP_BB
    row = lambda w: pl.BlockSpec((bb, w), lambda i: (i, 0))
    blk3 = lambda s: pl.BlockSpec((bb,) + s, lambda i: (i, 0, 0))
    return pl.pallas_call(
        _ssd_step_kernel,
        grid=(n // bb,),
        in_specs=[row(M_INNER), row(M_CONV_DIM), row(LANES), blk3((M_CONV - 1, M_CONV_DIM)),
                  blk3((M_INNER, M_STATE)),
                  _const_spec(cw.shape), _const_spec(cb.shape), _const_spec(alog.shape),
                  _const_spec(dsk.shape), _const_spec(nw.shape), _const_spec(e64.shape)],
        out_specs=[row(M_INNER), blk3((M_INNER, M_STATE)), blk3((M_CONV - 1, M_CONV_DIM))],
        out_shape=[jax.ShapeDtypeStruct((n, M_INNER), F32),
                   jax.ShapeDtypeStruct((n, M_INNER, M_STATE), F32),
                   jax.ShapeDtypeStruct((n, M_CONV - 1, M_CONV_DIM), F32)],
        compiler_params=_params(("parallel",)),
        name="ssd_step",
    )(zs, xbc, dt, cbuf, h0, cw, cb, alog, dsk, nw, e64)


MXU_WIDTH = 256
FFN_CHUNKS = (0, 6 * MXU_WIDTH, D_FF)


def _out_kernel(x_ref, *refs):
    y5_refs = refs[:S5_PIECES]
    (yb_ref, gate_ref, gtm_ref, shf_ref, scf_ref, gtf_ref, wglu_ref, bglu_ref, wb5_ref, wbs_ref, wout_ref,
     npm_ref, npf_ref, npo_ref, wfi_ref, wfo_ref, o_ref) = refs[S5_PIECES:]
    rows = x_ref.shape[0]
    y5 = jnp.concatenate([y5_refs[k][0:rows, c * LANES:(c + 1) * LANES]
                          for c in range(S5_CHUNKS) for k in range(S5_PIECES)], axis=1)
    ya = _gelu_tanh(y5)
    glu = ya * _sigmoid(_dot(ya.astype(BF16), wglu_ref[...]) + bglu_ref[...])
    merged = (gate_ref[:, :D_MODEL] * _dot(glu.astype(BF16), wb5_ref[...])
              + gate_ref[:, D_MODEL:] * _dot(yb_ref[...].astype(BF16), wbs_ref[...]))
    mix = _dot(merged.astype(BF16), wout_ref[...])
    x1 = x_ref[...] + gtm_ref[...] * _rms(mix, npm_ref[...])

    hb = (_rms(x1, npf_ref[...]) * (1.0 + scf_ref[...]) + shf_ref[...]).astype(BF16)
    f = jnp.zeros_like(x1)
    for j, e in zip(FFN_CHUNKS[:-1], FFN_CHUNKS[1:]):
        gg = _dot(hb, wfi_ref[:, j:e])
        uu = _dot(hb, wfi_ref[:, D_FF + j:D_FF + e])
        f = f + _dot((_silu(gg) * uu).astype(BF16), wfo_ref[j:e, :])
    o_ref[...] = x1 + gtf_ref[...] * _rms(f, npo_ref[...])


def _out_stage(x, y5, yb, gates, mod, ws, *, per_row, tm, rows_per_batch):
    t = x.shape[0]
    row = lambda n: pl.BlockSpec((tm, n), lambda i: (i, 0))
    mods = _mod_specs(per_row, tm, rows_per_batch // tm if not per_row else 1, (2, 3, 4, 5))
    _, y5_spec = _s5_layout(per_row, t, tm, rows_per_batch, D_MODEL // S5_PIECES)
    return pl.pallas_call(
        _out_kernel,
        grid=(t // tm,),
        in_specs=[row(D_MODEL)] + [y5_spec] * S5_PIECES + [row(M_INNER), row(2 * D_MODEL)] + mods
                 + [_const_spec(w.shape) for w in ws],
        out_specs=row(D_MODEL),
        out_shape=jax.ShapeDtypeStruct((t, D_MODEL), F32),
        compiler_params=_params(("parallel",)),
        name="out_stage",
    )(x, *y5, yb, gates, mod, mod, mod, mod, *ws)


def _pad_lanes(v):
    return jnp.pad(v.reshape(1, -1), ((0, 0), (0, LANES - v.shape[-1])))


def kernel(x_prompt, x_sample, state_s5_re, state_s5_im, state_ssm, state_conv, c_prompt, c_sample, w_ada, b_ada, norm_pre_mix, norm_post_mix, norm_pre_ffn, norm_post_ffn, w_in, s5_lam_re, s5_lam_im, s5_log_dt, s5_b_re, s5_b_im, s5_c_re, s5_c_im, s5_d, s5_w_glu, s5_b_glu, m_conv_w, m_conv_b, m_dt_bias, m_a_log, m_d, m_norm, w_branch_s5, w_branch_ssd, w_out, w_ffn_in, w_ffn_out):
    bp, seq, _ = x_prompt.shape
    bs = x_sample.shape[0]
    assert x_sample.shape[1] == 1 and w_ada.shape[0] == 1 and seq % M_CHUNK == 0
    row1 = lambda v: v.reshape(1, -1)

    w_in0 = w_in[0]
    ws_in = (w_in0[:, :OFF_DT].astype(BF16),
             jnp.pad(w_in0[:, OFF_DT:OFF_GA], ((0, 0), (0, LANES - M_HEADS))).astype(BF16),
             w_in0[:, OFF_GA:].astype(BF16))
    ws_out = (s5_w_glu[0].astype(BF16), row1(s5_b_glu[0]), w_branch_s5[0].astype(BF16),
              w_branch_ssd[0].astype(BF16), w_out[0].astype(BF16), row1(norm_post_mix[0]),
              row1(norm_pre_ffn[0]), row1(norm_post_ffn[0]), w_ffn_in[0].astype(BF16), w_ffn_out[0].astype(BF16))
    cw, cb = m_conv_w[0], row1(m_conv_b[0])
    dtb, alog = _pad_lanes(m_dt_bias[0]), _pad_lanes(m_a_log[0])
    dsk = row1(jnp.repeat(m_d[0], M_HEADDIM))
    nw = row1(m_norm[0])
    e64, e128 = _expand_mats()

    mod = _ada(jnp.concatenate([c_prompt, c_sample], axis=0), w_ada[0], b_ada[0])
    mod_p = mod[:bp].reshape(bp, 1, 6 * D_MODEL)
    mod_s = mod[bp:]

    ar, ai, bbr, bbi = _s5_disc(s5_lam_re[0], s5_lam_im[0], s5_log_dt[0], s5_b_re[0], s5_b_im[0])
    s5m = _s5_matrices(ar, ai, bbr, bbi, s5_c_re[0], s5_c_im[0], s5_d[0])
    g_pre = row1(norm_pre_mix[0])

    xp = x_prompt.reshape(bp * seq, D_MODEL)
    tm = min(ROW_TILE, seq)
    u, zs, act, dt, gates, tail = _in_proj(xp, mod_p, g_pre, ws_in, dtb, cw, cb,
                                           per_row=False, tm=tm, rows_per_batch=seq)
    p_conv = tail[:, SUBLANES - (M_CONV - 1):, :]
    zeros_s5 = jnp.zeros((bp, S5_GROUPS * S5_STATE), F32)
    y5, p_re, p_im = _s5(u, zeros_s5, zeros_s5, *s5m, tl=tm)
    yb, p_ssm = _ssd(zs, act, dt, alog, dsk, nw, e64, e128, nbatch=bp, seq=seq)
    y_prompt = _out_stage(xp, y5, yb, gates, mod_p, ws_out, per_row=False, tm=tm, rows_per_batch=seq)

    xs = x_sample.reshape(bs, D_MODEL)
    u, zs, xbc, dt, gates = _in_proj(xs, mod_s, g_pre, ws_in, dtb, cw, cb, per_row=True, tm=bs, rows_per_batch=1)
    y5, s_re, s_im = _s5(u, state_s5_re[0].reshape(bs, -1), state_s5_im[0].reshape(bs, -1), *s5m, tl=1)
    yb, s_ssm, s_conv = _ssd_step(zs, xbc, dt, state_conv[0], state_ssm[0].reshape(bs, M_INNER, M_STATE),
                                  cw, cb, alog, dsk, nw, e64)
    y_sample = _out_stage(xs, y5, yb, gates, mod_s, ws_out, per_row=True, tm=bs, rows_per_batch=1)

    s5_shape = (1, -1, S5_GROUPS, S5_STATE)
    ssm_shape = (1, -1, M_HEADS, M_HEADDIM, M_STATE)
    return (y_prompt.reshape(bp, seq, D_MODEL), y_sample.reshape(bs, 1, D_MODEL),
            p_re.reshape(s5_shape), p_im.reshape(s5_shape), p_ssm.reshape(ssm_shape), p_conv[None],
            s_re.reshape(s5_shape), s_im.reshape(s5_shape), s_ssm.reshape(ssm_shape), s_conv[None])
```

```python
import functools

import jax
import jax.numpy as jnp
from jax import lax
from jax.experimental import pallas as pl
from jax.experimental.pallas import tpu as pltpu

F32 = jnp.float32
BF16 = jnp.bfloat16

D_MODEL = 1024
EPS = 1e-6
S5_GROUP = 16
S5_GROUPS = 64
S5_STATE = 64
M_INNER = 2048
M_HEADDIM = 64
M_HEADS = 32
M_GROUPS = 4
M_HPG = 8
M_STATE = 128
M_CONV = 4
M_CONV_DIM = 3072
M_CHUNK = 128
D_FF = 2816
OFF_Z = D_MODEL
OFF_XBC = OFF_Z + M_INNER
OFF_DT = OFF_XBC + M_CONV_DIM
OFF_GA = OFF_DT + M_HEADS
IN_COLS = OFF_GA + 2 * D_MODEL

LANES = 128
SUBLANES = 8
VMEM_LIMIT = 56 * 1024 * 1024

S5_CH = 256
S5_NS = 1024
S5_CHUNKS = D_MODEL // S5_CH
S5_TILES = S5_NS // LANES
S5_PIECES = S5_CH // LANES
S5_SUB_ROWS = 128
ROW_TILE = 256
INPROJ_SLAB = 512
INPROJ_DEPTH = 3

_NT = (((1,), (1,)), ((), ()))
_TN = (((0,), (0,)), ((), ()))


def _const_spec(shape):
    nd = len(shape)
    return pl.BlockSpec(shape, lambda *_: (0,) * nd, pipeline_mode=pl.Buffered(1))


def _params(sem):
    return pltpu.CompilerParams(dimension_semantics=sem, vmem_limit_bytes=VMEM_LIMIT)


def _sigmoid(x):
    return 0.5 * jnp.tanh(0.5 * x) + 0.5


def _silu(x):
    hx = 0.5 * x
    return hx + hx * jnp.tanh(hx)


def _softplus(x):
    return jnp.maximum(x, 0.0) + jnp.log1p(jnp.exp(-jnp.abs(x)))


def _gelu_tanh(x):
    return 0.5 * x * (1.0 + jnp.tanh(0.7978845608028654 * (x + 0.044715 * (x * x * x))))


def _rms(x, g):
    return x * lax.rsqrt(jnp.mean(x * x, axis=-1, keepdims=True) + EPS) * g


def _dot(a, b):
    return jnp.dot(a, b, preferred_element_type=F32)


def _pieces3(x):
    hi = x.astype(BF16).astype(F32)
    r1 = x - hi
    mid = r1.astype(BF16).astype(F32)
    lo = (r1 - mid).astype(BF16).astype(F32)
    return hi, mid, lo


def _split2(x):
    return jnp.concatenate(_pieces3(x)[:2], axis=-1).astype(BF16)


def _split3(x):
    return jnp.concatenate(_pieces3(x), axis=-1).astype(BF16)


def _ada_kernel(c_ref, w_ref, b_ref, o_ref):
    s = _silu(c_ref[...]).astype(BF16)
    o_ref[...] = _dot(s, w_ref[...].astype(BF16)) + b_ref[...]


def _ada(c, w, b):
    n = c.shape[0]
    tn = 512
    return pl.pallas_call(
        _ada_kernel,
        grid=(6 * D_MODEL // tn,),
        in_specs=[_const_spec((n, D_MODEL)),
                  pl.BlockSpec((D_MODEL, tn), lambda j: (0, j)),
                  pl.BlockSpec((1, tn), lambda j: (0, j))],
        out_specs=pl.BlockSpec((n, tn), lambda j: (0, j)),
        out_shape=jax.ShapeDtypeStruct((n, 6 * D_MODEL), F32),
        compiler_params=_params(("arbitrary",)),
        name="ada",
    )(c, w, b.reshape(1, -1))


def _s5_disc_kernel(lr_ref, li_ref, ldt_ref, br_ref, bi_ref, ar_ref, ai_ref, bbr_ref, bbi_ref):
    lr = lr_ref[...]
    li = li_ref[...]
    dt = jnp.exp(ldt_ref[...])
    mag = jnp.exp(lr * dt)
    ar = mag * jnp.cos(li * dt)
    ai = mag * jnp.sin(li * dt)
    den = lr * lr + li * li
    nr = ar - 1.0
    f_re = (nr * lr + ai * li) / den
    f_im = (ai * lr - nr * li) / den
    br = br_ref[...]
    bi = bi_ref[...]
    ar_ref[...] = ar
    ai_ref[...] = ai
    bbr_ref[...] = f_re * br - f_im * bi
    bbi_ref[...] = f_re * bi + f_im * br


def _s5_disc(lam_re, lam_im, log_dt, b_re, b_im):
    g, n = lam_re.shape
    w = S5_GROUP * n
    tile = lambda a: jnp.tile(a, (1, S5_GROUP))
    ldt = jnp.broadcast_to(log_dt[:, None], (g, w))
    brt = jnp.swapaxes(b_re, 1, 2).reshape(g, w)
    bit = jnp.swapaxes(b_im, 1, 2).reshape(g, w)
    shp = jax.ShapeDtypeStruct((g, w), F32)
    ar, ai, bbr, bbi = pl.pallas_call(
        _s5_disc_kernel,
        out_shape=(shp, shp, shp, shp),
        name="s5_disc",
    )(tile(lam_re), tile(lam_im), ldt, brt, bit)
    return (ar[:, :n], ai[:, :n], bbr.reshape(g, S5_GROUP, n), bbi.reshape(g, S5_GROUP, n))


def _s5_matrices(abar_re, abar_im, bbar_re, bbar_im, c_re, c_im, d_skip):
    gl = S5_CH // S5_GROUP
    tw = 2 * LANES
    ns = 2 * S5_NS

    def state_group(i):
        return 2 * (i // tw) + (i % LANES) // S5_STATE

    bre, bim = bbar_re.reshape(D_MODEL, S5_STATE), bbar_im.reshape(D_MODEL, S5_STATE)
    pattern = jnp.tile(jnp.concatenate([bre, bre, bim, bim], axis=1), (1, S5_TILES))
    row_g = (lax.broadcasted_iota(jnp.int32, (D_MODEL, ns), 0) // S5_GROUP) % gl
    col_g = state_group(lax.broadcasted_iota(jnp.int32, (D_MODEL, ns), 1))
    bm = jnp.where(row_g == col_g, pattern, 0.0).astype(BF16).reshape(S5_CHUNKS, S5_CH, ns)

    def by_tile(c):
        return jnp.swapaxes(c, 1, 2).reshape(S5_CHUNKS, S5_TILES, 1, 2, S5_STATE, S5_GROUP)
    cols = jnp.concatenate([by_tile(c_re), by_tile(-c_im)], axis=2).reshape(S5_CHUNKS * ns, S5_GROUP)
    pattern = jnp.tile(cols, (1, gl))
    row_g = state_group(lax.broadcasted_iota(jnp.int32, (S5_CHUNKS * ns, S5_CH), 0) % ns)
    col_g = lax.broadcasted_iota(jnp.int32, (S5_CHUNKS * ns, S5_CH), 1) // S5_GROUP
    cm = jnp.where(row_g == col_g, pattern, 0.0).astype(BF16).reshape(S5_CHUNKS, ns, S5_CH)

    by_tile_a = lambda v: v.reshape(S5_CHUNKS, S5_TILES, 1, LANES)
    a = jnp.concatenate([by_tile_a(abar_re), by_tile_a(abar_im)], axis=2).reshape(S5_CHUNKS, 1, ns)
    d = d_skip.reshape(S5_CHUNKS, 1, S5_CH)
    return bm, a, cm, d


def _inproj_kernel(x_ref, sh_ref, sc_ref, g_ref, wm_ref, wdt_ref, wg_ref, dtb_ref, cw_ref, cb_ref,
                   u_ref, zs_ref, xbc_ref, dt_ref, gate_ref, *rest, tiles_per_seq):
    accs = rest[-INPROJ_DEPTH:]
    h = _rms(x_ref[...], g_ref[...]) * (1.0 + sc_ref[...]) + sh_ref[...]
    hb = h.astype(BF16)
    rows = hb.shape[0]
    step = INPROJ_SLAB

    if u_ref.shape[0] > rows:
        u_ref[rows:, :] = jnp.zeros((u_ref.shape[0] - rows, D_MODEL), F32)

    if tiles_per_seq is None:
        conv_silu = None
    else:
        tail_ref, carry_ref = rest[:2]

        @pl.when(pl.program_id(0) % tiles_per_seq == 0)
        def _():
            carry_ref[...] = jnp.zeros_like(carry_ref)

        def conv_silu(r, sl):
            ext = jnp.concatenate([carry_ref[:, sl], r], axis=0)
            t = ext * cw_ref[0:1, sl]
            for k in range(1, M_CONV):
                t = ext * cw_ref[k:k + 1, sl] + pltpu.roll(t, 1, axis=0)
            carry_ref[:, sl] = r[rows - SUBLANES:, :]
            tail_ref[:, sl] = r[rows - SUBLANES:, :]
            return _silu(t[SUBLANES:, :] + cb_ref[:, sl])

    def slabs(o_ref, w_ref, w_off, fn):
        n = o_ref.shape[-1]
        return [(o_ref, slice(j, j + min(step, n - j)), w_ref, w_off + j, fn) for j in range(0, n, step)]

    su = slabs(u_ref, wm_ref, 0, None)
    sz = slabs(zs_ref, wm_ref, OFF_Z, lambda r, sl: _silu(r))
    sx = slabs(xbc_ref, wm_ref, OFF_XBC, conv_silu)
    sd = slabs(dt_ref, wdt_ref, 0, lambda r, sl: _softplus(r + dtb_ref[:, sl]))
    sg = slabs(gate_ref, wg_ref, 0, lambda r, sl: _sigmoid(r))
    light = su + sz + sg + sd
    order = []
    while sx or light:
        if sx:
            order.append(sx.pop(0))
        if light:
            order.append(light.pop(0))

    def matmul(s):
        o_ref, sl, w_ref, w_col, _ = order[s]
        accs[s % INPROJ_DEPTH][:, 0:sl.stop - sl.start] = _dot(hb, w_ref[:, w_col:w_col + sl.stop - sl.start])

    for s in range(min(INPROJ_DEPTH - 1, len(order))):
        matmul(s)
    for s, (o_ref, sl, _, _, fn) in enumerate(order):
        if s + INPROJ_DEPTH - 1 < len(order):
            matmul(s + INPROJ_DEPTH - 1)
        r = accs[s % INPROJ_DEPTH][:, 0:sl.stop - sl.start]
        o_ref[0:rows, sl] = r if fn is None else fn(r, sl)


def _mod_specs(per_row, tm, rows_per_mod, cols):
    if per_row:
        return [pl.BlockSpec((tm, D_MODEL), functools.partial(lambda i, c: (i, c), c=c)) for c in cols]
    return [pl.BlockSpec((None, 1, D_MODEL), functools.partial(lambda i, c: (i // rows_per_mod, 0, c), c=c))
            for c in cols]


def _s5_layout(per_row, t, tm, rows_per_batch, width):
    if per_row:
        return (1, t, width), pl.BlockSpec((None, tm, width), lambda i: (0, i, 0))
    nblk, pitch = rows_per_batch // tm, _s5_pitch(tm)
    return ((nblk, (t // rows_per_batch) * pitch, width),
            pl.BlockSpec((None, pitch, width), lambda i: (i % nblk, i // nblk, 0)))


def _in_proj(x, mod, g, ws, dtb, cw, cb, *, per_row, tm, rows_per_batch):
    t = x.shape[0]
    row = lambda n: pl.BlockSpec((tm, n), lambda i: (i, 0))
    outs = (M_INNER, M_CONV_DIM, LANES, 2 * D_MODEL)
    u_shape, u_spec = _s5_layout(per_row, t, tm, rows_per_batch, D_MODEL)
    out_specs = [u_spec] + [row(n) for n in outs]
    out_shape = [jax.ShapeDtypeStruct(u_shape, F32)] + [jax.ShapeDtypeStruct((t, n), F32) for n in outs]
    scratch = []
    tiles_per_seq = None
    if not per_row:
        tiles_per_seq = rows_per_batch // tm
        out_specs.append(pl.BlockSpec((None, SUBLANES, M_CONV_DIM), lambda i: (i // tiles_per_seq, 0, 0)))
        out_shape.append(jax.ShapeDtypeStruct((t // rows_per_batch, SUBLANES, M_CONV_DIM), F32))
        scratch.append(pltpu.VMEM((SUBLANES, M_CONV_DIM), F32))
    scratch += [pltpu.VMEM((tm, INPROJ_SLAB), F32)] * INPROJ_DEPTH
    consts = (g,) + tuple(ws) + (dtb, cw, cb)
    return pl.pallas_call(
        functools.partial(_inproj_kernel, tiles_per_seq=tiles_per_seq),
        grid=(t // tm,),
        in_specs=[row(D_MODEL)] + _mod_specs(per_row, tm, rows_per_batch // tm if not per_row else 1, (0, 1))
                 + [_const_spec(w.shape) for w in consts],
        out_specs=out_specs,
        out_shape=out_shape,
        scratch_shapes=scratch,
        compiler_params=_params(("arbitrary",)),
        name="in_proj",
    )(x, mod, mod, *consts)


def _s5_pitch(tl):
    return tl if tl == 1 else tl + SUBLANES


def _s5_kernel(*refs, nb, tl):
    nu = S5_PIECES
    u_refs = refs[:nu]
    h0r_ref, h0i_ref, bm_ref, a_ref, cm_ref, d_ref = refs[nu:nu + 6]
    y_refs = refs[nu + 6:2 * nu + 6]
    hr_ref, hi_ref, hst_ref, bu0_ref, xs0_ref, bu1_ref, xs1_ref = refs[2 * nu + 6:]
    bufs = ((bu0_ref, xs0_ref), (bu1_ref, xs1_ref))
    tb = pl.program_id(1)
    tsub = S5_SUB_ROWS // nb
    tw = 2 * LANES
    pitch = _s5_pitch(tl)

    def step_rows(t):
        return pl.ds(t, nb, stride=pitch) if pitch > 1 else pl.ds(0, nb)

    @pl.when(tb == 0)
    def _():
        for j in range(S5_TILES):
            hst_ref[:, j * tw:j * tw + LANES] = h0r_ref[:, j * LANES:(j + 1) * LANES]
            hst_ref[:, j * tw + LANES:(j + 1) * tw] = h0i_ref[:, j * LANES:(j + 1) * LANES]

    ar = [jnp.broadcast_to(a_ref[:, j * tw:j * tw + LANES], (nb, LANES)) for j in range(S5_TILES)]
    ai = [jnp.broadcast_to(a_ref[:, j * tw + LANES:(j + 1) * tw], (nb, LANES)) for j in range(S5_TILES)]
    dsk = d_ref[...]

    def project_in(t0, bu_ref, xs_ref):
        xs = jnp.concatenate([jnp.concatenate([r[step_rows(t0 + i), :] for r in u_refs], axis=1)
                              for i in range(tsub)], axis=0)
        xs_ref[...] = xs
        lhs = xs.astype(BF16)
        for j in range(S5_TILES):
            bu_ref[:, j * tw:(j + 1) * tw] = _dot(lhs, bm_ref[:, j * tw:(j + 1) * tw])

    def scan_project_out(t0, bu_ref, xs_ref, carry):
        new, hs = [], []
        for j in range(S5_TILES):
            hr, hi = carry[2 * j], carry[2 * j + 1]
            rows = []
            for i in range(tsub):
                br = bu_ref[i * nb:(i + 1) * nb, j * tw:j * tw + LANES]
                bi = bu_ref[i * nb:(i + 1) * nb, j * tw + LANES:(j + 1) * tw]
                hr, hi = ar[j] * hr - ai[j] * hi + br, ar[j] * hi + ai[j] * hr + bi
                rows.append(jnp.concatenate([hr, hi], axis=1))
            new += [hr, hi]
            hs.append(jnp.concatenate(rows, axis=0).astype(BF16))
        y = _dot(jnp.concatenate(hs, axis=1), cm_ref[...]) + dsk * xs_ref[...]
        for i in range(tsub):
            for k, y_ref in enumerate(y_refs):
                y_ref[step_rows(t0 + i), :] = y[i * nb:(i + 1) * nb, k * LANES:(k + 1) * LANES]
        return tuple(new)

    h_in = []
    for j in range(S5_TILES):
        h_in += [hst_ref[:, j * tw:j * tw + LANES], hst_ref[:, j * tw + LANES:(j + 1) * tw]]
    nsub = tl // tsub
    (bu0, xs0), (bu1, xs1) = bufs
    project_in(0, bu0, xs0)
    if nsub == 1:
        h_out = scan_project_out(0, bu0, xs0, tuple(h_in))
    else:
        def pair(p, carry):
            t0 = 2 * p * tsub
            project_in(t0 + tsub, bu1, xs1)
            carry = scan_project_out(t0, bu0, xs0, carry)
            project_in(jnp.minimum(t0 + 2 * tsub, tl - tsub), bu0, xs0)
            return scan_project_out(t0 + tsub, bu1, xs1, carry)
        h_out = lax.fori_loop(0, nsub // 2, pair, tuple(h_in))
    for j in range(S5_TILES):
        hst_ref[:, j * tw:j * tw + LANES] = h_out[2 * j]
        hst_ref[:, j * tw + LANES:(j + 1) * tw] = h_out[2 * j + 1]

    if pitch > tl:
        for y_ref in y_refs:
            for b in range(nb):
                y_ref[b * pitch + tl:(b + 1) * pitch, :] = jnp.zeros((pitch - tl, LANES), F32)

    @pl.when(tb == pl.num_programs(1) - 1)
    def _():
        for j in range(S5_TILES):
            hr_ref[:, j * LANES:(j + 1) * LANES] = h_out[2 * j]
            hi_ref[:, j * LANES:(j + 1) * LANES] = h_out[2 * j + 1]


def _s5(u, h0r, h0i, bm, a, cm, d, *, tl):
    nblk, rows, _ = u.shape
    pitch = _s5_pitch(tl)
    nb = rows // pitch
    nsub = tl * nb // S5_SUB_ROWS
    assert S5_SUB_ROWS % nb == 0 and tl * nb % S5_SUB_ROWS == 0 and (nsub == 1 or nsub % 2 == 0)
    chunk3 = lambda s: pl.BlockSpec((None,) + s, lambda c, t: (c, 0, 0))
    st_spec = pl.BlockSpec((nb, S5_NS), lambda c, t: (0, c))
    nu = S5_PIECES
    u_specs = [pl.BlockSpec((None, rows, LANES), functools.partial(lambda c, t, k: (t, 0, nu * c + k), k=k))
               for k in range(nu)]
    y_spec = pl.BlockSpec((None, rows, LANES), lambda c, t: (t, 0, c))
    y_shape = jax.ShapeDtypeStruct((nblk, rows, D_MODEL // nu), F32)
    res = pl.pallas_call(
        functools.partial(_s5_kernel, nb=nb, tl=tl),
        grid=(S5_CHUNKS, nblk),
        in_specs=u_specs + [st_spec, st_spec, chunk3((S5_CH, 2 * S5_NS)), chunk3((1, 2 * S5_NS)),
                            chunk3((2 * S5_NS, S5_CH)), chunk3((1, S5_CH))],
        out_specs=[y_spec] * nu + [st_spec, st_spec],
        out_shape=[y_shape] * nu + [jax.ShapeDtypeStruct(h0r.shape, F32), jax.ShapeDtypeStruct(h0i.shape, F32)],
        scratch_shapes=[pltpu.VMEM((nb, 2 * S5_NS), F32)]
                       + [pltpu.VMEM((S5_SUB_ROWS, 2 * S5_NS), F32), pltpu.VMEM((S5_SUB_ROWS, S5_CH), F32)] * 2,
        compiler_params=_params(("parallel", "arbitrary")),
        name="s5",
    )(*([u] * nu), h0r, h0i, bm, a, cm, d)
    return res[:nu], res[nu], res[nu + 1]


def _expand_mats():
    r = lax.broadcasted_iota(jnp.int32, (3 * LANES, M_HEADS * M_HEADDIM), 0) % LANES
    c = lax.broadcasted_iota(jnp.int32, (3 * LANES, M_HEADS * M_HEADDIM), 1) // M_HEADDIM
    e64 = (r == c).astype(BF16)
    r = lax.broadcasted_iota(jnp.int32, (3 * LANES, M_HEADS * M_CHUNK), 0) % LANES
    c = lax.broadcasted_iota(jnp.int32, (3 * LANES, M_HEADS * M_CHUNK), 1) // M_CHUNK
    e128 = (r == c).astype(BF16)
    return e64, e128


LOG2E = 1.4426950408889634
SSD_CHUNKS_PER_STEP = 2


def _ssd_kernel(zs_ref, act_ref, dt_ref, alog_ref, dsk_ref, nw_ref, e64_ref, e128_ref,
                y_ref, hout_ref,
                ht_ref, xdt_ref, xdte_ref, eacs_ref, rall_ref, yacc_ref):
    c = pl.program_id(1)

    @pl.when(c == 0)
    def _():
        ht_ref[...] = jnp.zeros_like(ht_ref)

    for k in range(SSD_CHUNKS_PER_STEP):
        rows = pl.ds(k * M_CHUNK, M_CHUNK)
        _ssd_chunk(zs_ref.at[rows], act_ref.at[rows], dt_ref.at[rows], alog_ref, dsk_ref, nw_ref, e64_ref, e128_ref,
                   y_ref.at[rows], ht_ref, xdt_ref.at[k], xdte_ref.at[k], eacs_ref.at[k], rall_ref.at[k],
                   yacc_ref.at[k])

    @pl.when(c == pl.num_programs(1) - 1)
    def _():
        for j in range(0, M_INNER, LANES):
            hout_ref[j:j + LANES, :] = ht_ref[:, j:j + LANES].T


def _ssd_chunk(zs_ref, act_ref, dt_ref, alog_ref, dsk_ref, nw_ref, e64_ref, e128_ref,
               y_ref, ht_ref, xdt_ref, xdte_ref, eacs_ref, rall_ref, yacc_ref):
    q = M_CHUNK
    cs = 512

    dt = dt_ref[...]
    a2 = dt * (-jnp.exp(alog_ref[...]) * LOG2E)
    ri = lax.broadcasted_iota(jnp.int32, (q, q), 0)
    ci = lax.broadcasted_iota(jnp.int32, (q, q), 1)
    causal = ri >= ci
    acs = jnp.dot(causal.astype(F32), a2, preferred_element_type=F32, precision=lax.Precision.HIGHEST)
    acs_t = acs.T
    dt2 = _split2(dt)
    eacs2 = _split2(jnp.exp2(acs))
    dte2 = _split2(jnp.exp2(acs[q - 1:q, :] - acs))
    for j in range(0, M_INNER, cs):
        sl = slice(j, j + cs)
        e = e64_ref[0:2 * LANES, sl]
        xdt = act_ref[:, sl] * _dot(dt2, e)
        xdt_ref[:, sl] = xdt.astype(BF16)
        xdte_ref[:, sl] = (xdt * _dot(dte2, e)).astype(BF16)
        eacs_ref[:, sl] = _dot(eacs2, e)
    acs3 = _split3(acs)
    for j in range(0, M_HEADS * q, cs):
        sl = slice(j, j + cs)
        rall_ref[:, sl] = _dot(acs3, e128_ref[:, sl])

    hp = M_HPG * M_HEADDIM
    lane = lax.broadcasted_iota(jnp.int32, (q, LANES), 1)
    first_head = lane < M_HEADDIM
    for g in range(M_GROUPS):
        gs = slice(g * hp, (g + 1) * hp)
        bm_t = act_ref[:, M_INNER + g * M_STATE:M_INNER + (g + 1) * M_STATE].T.astype(BF16)
        cm = act_ref[:, M_INNER + (M_GROUPS + g) * M_STATE:M_INNER + (M_GROUPS + g + 1) * M_STATE].astype(BF16)
        cbm = _dot(cm, bm_t)
        y_off = _dot(cm, ht_ref[:, gs].astype(BF16))
        st_t = _dot(bm_t, xdte_ref[:, gs])
        for pr in range(M_HPG // 2):
            h0 = g * M_HPG + 2 * pr
            ps = slice(h0 * M_HEADDIM, (h0 + 2) * M_HEADDIM)
            ms = []
            for h in (h0, h0 + 1):
                d2 = rall_ref[:, h * q:(h + 1) * q] - acs_t[h:h + 1, :]
                ms.append((cbm * jnp.where(causal, jnp.exp2(d2), 0.0)).astype(BF16))
            xp = xdt_ref[:, ps]
            zero = jnp.zeros_like(xp)
            rhs = jnp.concatenate([jnp.where(first_head, xp, zero), jnp.where(first_head, zero, xp)], axis=0)
            yd = _dot(jnp.concatenate(ms, axis=1), rhs)
            yacc_ref[:, ps] = yd + y_off[:, ps.start - gs.start:ps.stop - gs.start] * eacs_ref[:, ps] \
                + dsk_ref[:, ps] * act_ref[:, ps]
        ht_ref[:, gs] = eacs_ref[q - 1:q, gs] * ht_ref[:, gs] + st_t

    ss = jnp.zeros((q, 1), F32)
    for j in range(0, M_INNER, cs):
        sl = slice(j, j + cs)
        gt = yacc_ref[:, sl] * zs_ref[:, sl]
        yacc_ref[:, sl] = gt
        ss = ss + jnp.sum(gt * gt, axis=-1, keepdims=True)
    inv = lax.rsqrt(ss * (1.0 / M_INNER) + EPS)
    for j in range(0, M_INNER, cs):
        sl = slice(j, j + cs)
        y_ref[:, sl] = yacc_ref[:, sl] * inv * nw_ref[:, sl]


def _ssd(zs, act, dt, alog, dsk, nw, e64, e128, *, nbatch, seq):
    q = M_CHUNK
    k = SSD_CHUNKS_PER_STEP
    assert seq % (k * q) == 0
    nc = seq // (k * q)
    row = lambda n: pl.BlockSpec((k * q, n), lambda b, c: (b * nc + c, 0))
    return pl.pallas_call(
        _ssd_kernel,
        grid=(nbatch, nc),
        in_specs=[row(M_INNER), row(M_CONV_DIM), row(LANES), _const_spec(alog.shape),
                  _const_spec(dsk.shape), _const_spec(nw.shape), _const_spec(e64.shape), _const_spec(e128.shape)],
        out_specs=[row(M_INNER), pl.BlockSpec((None, M_INNER, M_STATE), lambda b, c: (b, 0, 0))],
        out_shape=[jax.ShapeDtypeStruct((nbatch * seq, M_INNER), F32),
                   jax.ShapeDtypeStruct((nbatch, M_INNER, M_STATE), F32)],
        scratch_shapes=[pltpu.VMEM((M_STATE, M_INNER), F32), pltpu.VMEM((k, q, M_INNER), BF16),
                        pltpu.VMEM((k, q, M_INNER), BF16), pltpu.VMEM((k, q, M_INNER), F32),
                        pltpu.VMEM((k, q, M_HEADS * q), F32), pltpu.VMEM((k, q, M_INNER), F32)],
        compiler_params=_params(("parallel", "arbitrary")),
        name="ssd",
    )(zs, act, dt, alog, dsk, nw, e64, e128)


SSD_STEP_BB = 8


def _ssd_step_kernel(zs_ref, xbc_ref, dt_ref, cbuf_ref, h0_ref, cw_ref, cb_ref, alog_ref, dsk_ref, nw_ref,
                     e64_ref, y_ref, hout_ref, convout_ref):
    bb = SSD_STEP_BB
    x_new = xbc_ref[...]
    conv = cb_ref[...] + x_new * cw_ref[M_CONV - 1:M_CONV, :]
    for k in range(M_CONV - 1):
        conv = conv + cbuf_ref[:, k, :] * cw_ref[k:k + 1, :]
    act = _silu(conv)
    for k in range(M_CONV - 2):
        convout_ref[:, k, :] = cbuf_ref[:, k + 1, :]
    convout_ref[:, M_CONV - 2, :] = x_new

    xs = act[:, :M_INNER]
    bmat = act[:, M_INNER:M_INNER + M_GROUPS * M_STATE]
    cmat = act[:, M_INNER + M_GROUPS * M_STATE:]
    dt = dt_ref[...]
    da = jnp.exp(dt * (-jnp.exp(alog_ref[...])))
    dt_e = _dot(_split3(dt), e64_ref[...])
    da_e = _dot(_split3(da), e64_ref[...])
    xdt = xs * dt_e

    hp = M_HPG * M_HEADDIM
    lane_group = lax.broadcasted_iota(jnp.int32, (M_GROUPS, M_INNER), 1) // hp
    row_group = lax.broadcasted_iota(jnp.int32, (M_GROUPS, M_INNER), 0)
    gmask = lane_group == row_group
    ones_rows = lax.broadcasted_iota(jnp.int32, (SUBLANES, M_STATE), 0) >= M_GROUPS

    cbx = []
    for g in range(M_GROUPS):
        cbg = jnp.sum(cmat[:, g * M_STATE:(g + 1) * M_STATE] * bmat[:, g * M_STATE:(g + 1) * M_STATE],
                      axis=-1, keepdims=True)
        cbx.append(jnp.broadcast_to(cbg, (bb, hp)))
    y = jnp.concatenate(cbx, axis=1) * xdt + dsk_ref[...] * xs

    y_off_rows = []
    for b in range(bb):
        xsel = jnp.where(gmask, jnp.broadcast_to(xdt[b:b + 1, :], (M_GROUPS, M_INNER)), 0.0)
        lhs = jnp.concatenate([xsel, *_pieces3(da_e[b:b + 1, :]), jnp.zeros((1, M_INNER), F32)],
                              axis=0).astype(BF16)
        bm4 = jnp.concatenate([bmat[b:b + 1, g * M_STATE:(g + 1) * M_STATE] for g in range(M_GROUPS)], axis=0)
        cm4 = jnp.concatenate([cmat[b:b + 1, g * M_STATE:(g + 1) * M_STATE] for g in range(M_GROUPS)], axis=0)
        zeros4 = jnp.zeros((SUBLANES - M_GROUPS, M_STATE), F32)
        rhs = jnp.concatenate([jnp.concatenate([bm4, zeros4], axis=0),
                               jnp.where(ones_rows, 1.0, 0.0)], axis=1).astype(BF16)
        res = lax.dot_general(lhs, rhs, _TN, preferred_element_type=F32)
        h0 = h0_ref[b]
        hout_ref[b] = res[:, M_STATE:] * h0 + res[:, :M_STATE]
        cm8 = jnp.concatenate([cm4, zeros4], axis=0).astype(BF16)
        yo = lax.dot_general(cm8, h0.astype(BF16), _NT, preferred_element_type=F32)
        y_off_rows.append(jnp.sum(jnp.where(gmask, yo[:M_GROUPS, :], 0.0), axis=0, keepdims=True))
    y = y + da_e * jnp.concatenate(y_off_rows, axis=0)

    y_ref[...] = _rms(y * zs_ref[...], nw_ref[...])


def _ssd_step(zs, xbc, dt, cbuf, h0, cw, cb, alog, dsk, nw, e64):
    n = zs.shape[0]
    bb = SSD_STEP_BB
    row = lambda w: pl.BlockSpec((bb, w), lambda i: (i, 0))
    blk3 = lambda s: pl.BlockSpec((bb,) + s, lambda i: (i, 0, 0))
    return pl.pallas_call(
        _ssd_step_kernel,
        grid=(n // bb,),
        in_specs=[row(M_INNER), row(M_CONV_DIM), row(LANES), blk3((M_CONV - 1, M_CONV_DIM)),
                  blk3((M_INNER, M_STATE)),
                  _const_spec(cw.shape), _const_spec(cb.shape), _const_spec(alog.shape),
                  _const_spec(dsk.shape), _const_spec(nw.shape), _const_spec(e64.shape)],
        out_specs=[row(M_INNER), blk3((M_INNER, M_STATE)), blk3((M_CONV - 1, M_CONV_DIM))],
        out_shape=[jax.ShapeDtypeStruct((n, M_INNER), F32),
                   jax.ShapeDtypeStruct((n, M_INNER, M_STATE), F32),
                   jax.ShapeDtypeStruct((n, M_CONV - 1, M_CONV_DIM), F32)],
        compiler_params=_params(("parallel",)),
        name="ssd_step",
    )(zs, xbc, dt, cbuf, h0, cw, cb, alog, dsk, nw, e64)


MXU_WIDTH = 256
FFN_CHUNKS = (0, 6 * MXU_WIDTH, D_FF)


def _out_kernel(x_ref, *refs):
    y5_refs = refs[:S5_PIECES]
    (yb_ref, gate_ref, gtm_ref, shf_ref, scf_ref, gtf_ref, wglu_ref, bglu_ref, wb5_ref, wbs_ref, wout_ref,
     npm_ref, npf_ref, npo_ref, wfi_ref, wfo_ref, o_ref) = refs[S5_PIECES:]
    rows = x_ref.shape[0]
    y5 = jnp.concatenate([y5_refs[k][0:rows, c * LANES:(c + 1) * LANES]
                          for c in range(S5_CHUNKS) for k in range(S5_PIECES)], axis=1)
    ya = _gelu_tanh(y5)
    glu = ya * _sigmoid(_dot(ya.astype(BF16), wglu_ref[...]) + bglu_ref[...])
    merged = (gate_ref[:, :D_MODEL] * _dot(glu.astype(BF16), wb5_ref[...])
              + gate_ref[:, D_MODEL:] * _dot(yb_ref[...].astype(BF16), wbs_ref[...]))
    mix = _dot(merged.astype(BF16), wout_ref[...])
    x1 = x_ref[...] + gtm_ref[...] * _rms(mix, npm_ref[...])

    hb = (_rms(x1, npf_ref[...]) * (1.0 + scf_ref[...]) + shf_ref[...]).astype(BF16)
    f = jnp.zeros_like(x1)
    for j, e in zip(FFN_CHUNKS[:-1], FFN_CHUNKS[1:]):
        gg = _dot(hb, wfi_ref[:, j:e])
        uu = _dot(hb, wfi_ref[:, D_FF + j:D_FF + e])
        f = f + _dot((_silu(gg) * uu).astype(BF16), wfo_ref[j:e, :])
    o_ref[...] = x1 + gtf_ref[...] * _rms(f, npo_ref[...])


def _out_stage(x, y5, yb, gates, mod, ws, *, per_row, tm, rows_per_batch):
    t = x.shape[0]
    row = lambda n: pl.BlockSpec((tm, n), lambda i: (i, 0))
    mods = _mod_specs(per_row, tm, rows_per_batch // tm if not per_row else 1, (2, 3, 4, 5))
    _, y5_spec = _s5_layout(per_row, t, tm, rows_per_batch, D_MODEL // S5_PIECES)
    return pl.pallas_call(
        _out_kernel,
        grid=(t // tm,),
        in_specs=[row(D_MODEL)] + [y5_spec] * S5_PIECES + [row(M_INNER), row(2 * D_MODEL)] + mods
                 + [_const_spec(w.shape) for w in ws],
        out_specs=row(D_MODEL),
        out_shape=jax.ShapeDtypeStruct((t, D_MODEL), F32),
        compiler_params=_params(("parallel",)),
        name="out_stage",
    )(x, *y5, yb, gates, mod, mod, mod, mod, *ws)


def _pad_lanes(v):
    return jnp.pad(v.reshape(1, -1), ((0, 0), (0, LANES - v.shape[-1])))


def kernel(x_prompt, x_sample, state_s5_re, state_s5_im, state_ssm, state_conv, c_prompt, c_sample, w_ada, b_ada, norm_pre_mix, norm_post_mix, norm_pre_ffn, norm_post_ffn, w_in, s5_lam_re, s5_lam_im, s5_log_dt, s5_b_re, s5_b_im, s5_c_re, s5_c_im, s5_d, s5_w_glu, s5_b_glu, m_conv_w, m_conv_b, m_dt_bias, m_a_log, m_d, m_norm, w_branch_s5, w_branch_ssd, w_out, w_ffn_in, w_ffn_out):
    bp, seq, _ = x_prompt.shape
    bs = x_sample.shape[0]
    assert x_sample.shape[1] == 1 and w_ada.shape[0] == 1 and seq % M_CHUNK == 0
    row1 = lambda v: v.reshape(1, -1)

    w_in0 = w_in[0]
    ws_in = (w_in0[:, :OFF_DT].astype(BF16),
             jnp.pad(w_in0[:, OFF_DT:OFF_GA], ((0, 0), (0, LANES - M_HEADS))).astype(BF16),
             w_in0[:, OFF_GA:].astype(BF16))
    ws_out = (s5_w_glu[0].astype(BF16), row1(s5_b_glu[0]), w_branch_s5[0].astype(BF16),
              w_branch_ssd[0].astype(BF16), w_out[0].astype(BF16), row1(norm_post_mix[0]),
              row1(norm_pre_ffn[0]), row1(norm_post_ffn[0]), w_ffn_in[0].astype(BF16), w_ffn_out[0].astype(BF16))
    cw, cb = m_conv_w[0], row1(m_conv_b[0])
    dtb, alog = _pad_lanes(m_dt_bias[0]), _pad_lanes(m_a_log[0])
    dsk = row1(jnp.repeat(m_d[0], M_HEADDIM))
    nw = row1(m_norm[0])
    e64, e128 = _expand_mats()

    mod = _ada(jnp.concatenate([c_prompt, c_sample], axis=0), w_ada[0], b_ada[0])
    mod_p = mod[:bp].reshape(bp, 1, 6 * D_MODEL)
    mod_s = mod[bp:]

    ar, ai, bbr, bbi = _s5_disc(s5_lam_re[0], s5_lam_im[0], s5_log_dt[0], s5_b_re[0], s5_b_im[0])
    s5m = _s5_matrices(ar, ai, bbr, bbi, s5_c_re[0], s5_c_im[0], s5_d[0])
    g_pre = row1(norm_pre_mix[0])

    xp = x_prompt.reshape(bp * seq, D_MODEL)
    tm = min(ROW_TILE, seq)
    u, zs, act, dt, gates, tail = _in_proj(xp, mod_p, g_pre, ws_in, dtb, cw, cb,
                                           per_row=False, tm=tm, rows_per_batch=seq)
    p_conv = tail[:, SUBLANES - (M_CONV - 1):, :]
    zeros_s5 = jnp.zeros((bp, S5_GROUPS * S5_STATE), F32)
    y5, p_re, p_im = _s5(u, zeros_s5, zeros_s5, *s5m, tl=tm)
    yb, p_ssm = _ssd(zs, act, dt, alog, dsk, nw, e64, e128, nbatch=bp, seq=seq)
    y_prompt = _out_stage(xp, y5, yb, gates, mod_p, ws_out, per_row=False, tm=tm, rows_per_batch=seq)

    xs = x_sample.reshape(bs, D_MODEL)
    u, zs, xbc, dt, gates = _in_proj(xs, mod_s, g_pre, ws_in, dtb, cw, cb, per_row=True, tm=bs, rows_per_batch=1)
    y5, s_re, s_im = _s5(u, state_s5_re[0].reshape(bs, -1), state_s5_im[0].reshape(bs, -1), *s5m, tl=1)
    yb, s_ssm, s_conv = _ssd_step(zs, xbc, dt, state_conv[0], state_ssm[0].reshape(bs, M_INNER, M_STATE),
                                  cw, cb, alog, dsk, nw, e64)
    y_sample = _out_stage(xs, y5, yb, gates, mod_s, ws_out, per_row=True, tm=bs, rows_per_batch=1)

    s5_shape = (1, -1, S5_GROUPS, S5_STATE)
    ssm_shape = (1, -1, M_HEADS, M_HEADDIM, M_STATE)
    return (y_prompt.reshape(bp, seq, D_MODEL), y_sample.reshape(bs, 1, D_MODEL),
            p_re.reshape(s5_shape), p_im.reshape(s5_shape), p_ssm.reshape(ssm_shape), p_conv[None],
            s_re.reshape(s5_shape), s_im.reshape(s5_shape), s_ssm.reshape(ssm_shape), s_conv[None])
```

```python
import functools

import jax
import jax.numpy as jnp
from jax import lax
from jax.experimental import pallas as pl
from jax.experimental.pallas import tpu as pltpu

F32 = jnp.float32
BF16 = jnp.bfloat16

D_MODEL = 1024
EPS = 1e-6
S5_GROUP = 16
S5_GROUPS = 64
S5_STATE = 64
M_INNER = 2048
M_HEADDIM = 64
M_HEADS = 32
M_GROUPS = 4
M_HPG = 8
M_STATE = 128
M_CONV = 4
M_CONV_DIM = 3072
M_CHUNK = 128
D_FF = 2816
OFF_Z = D_MODEL
OFF_XBC = OFF_Z + M_INNER
OFF_DT = OFF_XBC + M_CONV_DIM
OFF_GA = OFF_DT + M_HEADS
IN_COLS = OFF_GA + 2 * D_MODEL

LANES = 128
SUBLANES = 8
VMEM_LIMIT = 56 * 1024 * 1024

S5_CH = 256
S5_NS = 1024
S5_CHUNKS = D_MODEL // S5_CH
S5_TILES = S5_NS // LANES
S5_PIECES = S5_CH // LANES
S5_SUB_ROWS = 128
ROW_TILE = 256
INPROJ_SLAB = 512
INPROJ_DEPTH = 3

_NT = (((1,), (1,)), ((), ()))
_TN = (((0,), (0,)), ((), ()))


def _const_spec(shape):
    nd = len(shape)
    return pl.BlockSpec(shape, lambda *_: (0,) * nd, pipeline_mode=pl.Buffered(1))


def _params(sem):
    return pltpu.CompilerParams(dimension_semantics=sem, vmem_limit_bytes=VMEM_LIMIT)


def _sigmoid(x):
    return 0.5 * jnp.tanh(0.5 * x) + 0.5


def _silu(x):
    hx = 0.5 * x
    return hx + hx * jnp.tanh(hx)


def _softplus(x):
    return jnp.maximum(x, 0.0) + jnp.log1p(jnp.exp(-jnp.abs(x)))


def _gelu_tanh(x):
    return 0.5 * x * (1.0 + jnp.tanh(0.7978845608028654 * (x + 0.044715 * (x * x * x))))


def _rms(x, g):
    return x * lax.rsqrt(jnp.mean(x * x, axis=-1, keepdims=True) + EPS) * g


def _dot(a, b):
    return jnp.dot(a, b, preferred_element_type=F32)


def _pieces3(x):
    hi = x.astype(BF16).astype(F32)
    r1 = x - hi
    mid = r1.astype(BF16).astype(F32)
    lo = (r1 - mid).astype(BF16).astype(F32)
    return hi, mid, lo


def _split2(x):
    return jnp.concatenate(_pieces3(x)[:2], axis=-1).astype(BF16)


def _split3(x):
    return jnp.concatenate(_pieces3(x), axis=-1).astype(BF16)


def _ada_kernel(c_ref, w_ref, b_ref, o_ref):
    s = _silu(c_ref[...]).astype(BF16)
    o_ref[...] = _dot(s, w_ref[...].astype(BF16)) + b_ref[...]


def _ada(c, w, b):
    n = c.shape[0]
    tn = 512
    return pl.pallas_call(
        _ada_kernel,
        grid=(6 * D_MODEL // tn,),
        in_specs=[_const_spec((n, D_MODEL)),
                  pl.BlockSpec((D_MODEL, tn), lambda j: (0, j)),
                  pl.BlockSpec((1, tn), lambda j: (0, j))],
        out_specs=pl.BlockSpec((n, tn), lambda j: (0, j)),
        out_shape=jax.ShapeDtypeStruct((n, 6 * D_MODEL), F32),
        compiler_params=_params(("arbitrary",)),
        name="ada",
    )(c, w, b.reshape(1, -1))


def _s5_disc_kernel(lr_ref, li_ref, ldt_ref, br_ref, bi_ref, ar_ref, ai_ref, bbr_ref, bbi_ref):
    lr = lr_ref[...]
    li = li_ref[...]
    dt = jnp.exp(ldt_ref[...])
    mag = jnp.exp(lr * dt)
    ar = mag * jnp.cos(li * dt)
    ai = mag * jnp.sin(li * dt)
    den = lr * lr + li * li
    nr = ar - 1.0
    f_re = (nr * lr + ai * li) / den
    f_im = (ai * lr - nr * li) / den
    br = br_ref[...]
    bi = bi_ref[...]
    ar_ref[...] = ar
    ai_ref[...] = ai
    bbr_ref[...] = f_re * br - f_im * bi
    bbi_ref[...] = f_re * bi + f_im * br


def _s5_disc(lam_re, lam_im, log_dt, b_re, b_im):
    g, n = lam_re.shape
    w = S5_GROUP * n
    tile = lambda a: jnp.tile(a, (1, S5_GROUP))
    ldt = jnp.broadcast_to(log_dt[:, None], (g, w))
    brt = jnp.swapaxes(b_re, 1, 2).reshape(g, w)
    bit = jnp.swapaxes(b_im, 1, 2).reshape(g, w)
    shp = jax.ShapeDtypeStruct((g, w), F32)
    ar, ai, bbr, bbi = pl.pallas_call(
        _s5_disc_kernel,
        out_shape=(shp, shp, shp, shp),
        name="s5_disc",
    )(tile(lam_re), tile(lam_im), ldt, brt, bit)
    return (ar[:, :n], ai[:, :n], bbr.reshape(g, S5_GROUP, n), bbi.reshape(g, S5_GROUP, n))


def _s5_matrices(abar_re, abar_im, bbar_re, bbar_im, c_re, c_im, d_skip):
    ns = 2 * S5_NS
    b = jnp.concatenate([bbar_re.reshape(D_MODEL, S5_STATE), bbar_im.reshape(D_MODEL, S5_STATE)],
                        axis=1).reshape(S5_CHUNKS, S5_CH, 2 * S5_STATE)

    def by_tile(c):
        return jnp.swapaxes(c, 1, 2).reshape(S5_CHUNKS, S5_TILES, 1, 2, S5_STATE, S5_GROUP)
    c = jnp.concatenate([by_tile(c_re), by_tile(-c_im)], axis=2).reshape(S5_CHUNKS, ns, S5_GROUP)

    by_tile_a = lambda v: v.reshape(S5_CHUNKS, S5_TILES, 1, LANES)
    a = jnp.concatenate([by_tile_a(abar_re), by_tile_a(abar_im)], axis=2).reshape(S5_CHUNKS, 1, ns)
    d = d_skip.reshape(S5_CHUNKS, 1, S5_CH)
    return b, a, c, d


def _s5_state_group(i):
    return 2 * (i // (2 * LANES)) + (i % LANES) // S5_STATE


def _s5_expand_operands(b_ref, c_ref, bm_ref, cm_ref):
    tw = 2 * LANES
    row_g = lax.broadcasted_iota(jnp.int32, (S5_CH, LANES), 0) // S5_GROUP
    lane_half = lax.broadcasted_iota(jnp.int32, (S5_CH, LANES), 1) // S5_STATE
    for part in range(2):
        bp = b_ref[:, part * S5_STATE:(part + 1) * S5_STATE]
        pattern = jnp.concatenate([bp, bp], axis=1)
        for j in range(S5_TILES):
            tile = jnp.where(row_g == 2 * j + lane_half, pattern, 0.0)
            bm_ref[:, j * tw + part * LANES:j * tw + (part + 1) * LANES] = tile.astype(BF16)
    spread = (lax.broadcasted_iota(jnp.int32, (S5_GROUP, S5_CH), 1) % S5_GROUP
              == lax.broadcasted_iota(jnp.int32, (S5_GROUP, S5_CH), 0)).astype(BF16)
    rows = 512
    for r0 in range(0, 2 * S5_NS, rows):
        tiled = _dot(c_ref[r0:r0 + rows, :].astype(BF16), spread)
        own = (_s5_state_group(lax.broadcasted_iota(jnp.int32, (rows, S5_CH), 0) + r0)
               == lax.broadcasted_iota(jnp.int32, (rows, S5_CH), 1) // S5_GROUP)
        cm_ref[r0:r0 + rows, :] = jnp.where(own, tiled, 0.0).astype(BF16)


def _split_w_in_kernel(w_ref, wm_ref, wdt_ref, wg_ref):
    w = w_ref[...]
    rows = w.shape[0]
    wm_ref[...] = w[:, :OFF_DT].astype(BF16)
    wdt_ref[...] = jnp.concatenate([w[:, OFF_DT:OFF_GA], jnp.zeros((rows, LANES - M_HEADS), F32)],
                                   axis=1).astype(BF16)
    wg_ref[...] = w[:, OFF_GA:].astype(BF16)


def _split_w_in(w):
    k = w.shape[0]
    tk = 256
    outs = (OFF_DT, LANES, 2 * D_MODEL)
    return pl.pallas_call(
        _split_w_in_kernel,
        grid=(k // tk,),
        in_specs=[pl.BlockSpec((tk, IN_COLS), lambda i: (i, 0))],
        out_specs=[pl.BlockSpec((tk, n), lambda i: (i, 0)) for n in outs],
        out_shape=[jax.ShapeDtypeStruct((k, n), BF16) for n in outs],
        compiler_params=_params(("parallel",)),
        name="split_w_in",
    )(w)


def _inproj_kernel(x_ref, sh_ref, sc_ref, g_ref, wm_ref, wdt_ref, wg_ref, dtb_ref, cw_ref, cb_ref,
                   u_ref, z_ref, xbc_ref, dt_ref, gate_ref, *rest, tiles_per_seq):
    accs = rest[-INPROJ_DEPTH:]
    h = _rms(x_ref[...], g_ref[...]) * (1.0 + sc_ref[...]) + sh_ref[...]
    hb = h.astype(BF16)
    rows = hb.shape[0]
    step = INPROJ_SLAB

    if u_ref.shape[0] > rows:
        u_ref[rows:, :] = jnp.zeros((u_ref.shape[0] - rows, D_MODEL), F32)

    if tiles_per_seq is None:
        conv_fn = None
    else:
        tail_ref, carry_ref = rest[:2]

        @pl.when(pl.program_id(0) % tiles_per_seq == 0)
        def _():
            carry_ref[...] = jnp.zeros_like(carry_ref)

        def conv_fn(r, sl):
            ext = jnp.concatenate([carry_ref[:, sl], r], axis=0)
            s1 = pltpu.roll(ext, 1, axis=0)
            near = ext * cw_ref[3:4, sl] + s1 * cw_ref[2:3, sl]
            far = ext * cw_ref[1:2, sl] + s1 * cw_ref[0:1, sl]
            carry_ref[:, sl] = r[rows - SUBLANES:, :]
            tail_ref[:, sl] = r[rows - SUBLANES:, :]
            return (near + pltpu.roll(far, 2, axis=0))[SUBLANES:, :] + cb_ref[:, sl]

    def slabs(o_ref, w_ref, w_off, fn):
        n = o_ref.shape[-1]
        return [(o_ref, slice(j, j + min(step, n - j)), w_ref, w_off + j, fn) for j in range(0, n, step)]

    su = slabs(u_ref, wm_ref, 0, None)
    sz = slabs(z_ref, wm_ref, OFF_Z, None)
    sx = slabs(xbc_ref, wm_ref, OFF_XBC, conv_fn)
    sd = slabs(dt_ref, wdt_ref, 0, lambda r, sl: _softplus(r + dtb_ref[:, sl]))
    sg = slabs(gate_ref, wg_ref, 0, None)
    light = su + sz + sg + sd
    order = []
    while sx or light:
        if sx:
            order.append(sx.pop(0))
        if light:
            order.append(light.pop(0))

    def matmul(s):
        o_ref, sl, w_ref, w_col, _ = order[s]
        accs[s % INPROJ_DEPTH][:, 0:sl.stop - sl.start] = _dot(hb, w_ref[:, w_col:w_col + sl.stop - sl.start])

    for s in range(min(INPROJ_DEPTH - 1, len(order))):
        matmul(s)
    for s, (o_ref, sl, _, _, fn) in enumerate(order):
        if s + INPROJ_DEPTH - 1 < len(order):
            matmul(s + INPROJ_DEPTH - 1)
        r = accs[s % INPROJ_DEPTH][:, 0:sl.stop - sl.start]
        o_ref[0:rows, sl] = r if fn is None else fn(r, sl)


def _mod_specs(per_row, tm, rows_per_mod, cols):
    if per_row:
        return [pl.BlockSpec((tm, D_MODEL), functools.partial(lambda i, c: (i, c), c=c)) for c in cols]
    return [pl.BlockSpec((None, 1, D_MODEL), functools.partial(lambda i, c: (i // rows_per_mod, 0, c), c=c))
            for c in cols]


def _s5_layout(per_row, t, tm, rows_per_batch, width):
    if per_row:
        return (1, t, width), pl.BlockSpec((None, tm, width), lambda i: (0, i, 0))
    nblk, pitch = rows_per_batch // tm, _s5_pitch(tm)
    return ((nblk, (t // rows_per_batch) * pitch, width),
            pl.BlockSpec((None, pitch, width), lambda i: (i % nblk, i // nblk, 0)))


def _in_proj(x, mod, g, ws, dtb, cw, cb, *, per_row, tm, rows_per_batch):
    t = x.shape[0]
    row = lambda n: pl.BlockSpec((tm, n), lambda i: (i, 0))
    outs = (M_INNER, M_CONV_DIM, LANES, 2 * D_MODEL)
    u_shape, u_spec = _s5_layout(per_row, t, tm, rows_per_batch, D_MODEL)
    out_specs = [u_spec] + [row(n) for n in outs]
    out_shape = [jax.ShapeDtypeStruct(u_shape, F32)] + [jax.ShapeDtypeStruct((t, n), F32) for n in outs]
    scratch = []
    tiles_per_seq = None
    if not per_row:
        tiles_per_seq = rows_per_batch // tm
        out_specs.append(pl.BlockSpec((None, SUBLANES, M_CONV_DIM), lambda i: (i // tiles_per_seq, 0, 0)))
        out_shape.append(jax.ShapeDtypeStruct((t // rows_per_batch, SUBLANES, M_CONV_DIM), F32))
        scratch.append(pltpu.VMEM((SUBLANES, M_CONV_DIM), F32))
    scratch += [pltpu.VMEM((tm, INPROJ_SLAB), F32)] * INPROJ_DEPTH
    consts = (g,) + tuple(ws) + (dtb, cw, cb)
    return pl.pallas_call(
        functools.partial(_inproj_kernel, tiles_per_seq=tiles_per_seq),
        grid=(t // tm,),
        in_specs=[row(D_MODEL)] + _mod_specs(per_row, tm, rows_per_batch // tm if not per_row else 1, (0, 1))
                 + [_const_spec(w.shape) for w in consts],
        out_specs=out_specs,
        out_shape=out_shape,
        scratch_shapes=scratch,
        compiler_params=_params(("arbitrary",)),
        name="in_proj",
    )(x, mod, mod, *consts)


def _s5_pitch(tl):
    return tl if tl == 1 else tl + SUBLANES


def _s5_kernel(*refs, nb, tl):
    nu = S5_PIECES
    u_refs = refs[:nu]
    h0r_ref, h0i_ref, b_ref, a_ref, c_ref, d_ref = refs[nu:nu + 6]
    y_refs = refs[nu + 6:2 * nu + 6]
    hr_ref, hi_ref, hst_ref, bu0_ref, xs0_ref, bu1_ref, xs1_ref, bm_ref, cm_ref = refs[2 * nu + 6:]
    bufs = ((bu0_ref, xs0_ref), (bu1_ref, xs1_ref))
    tb = pl.program_id(1)
    tsub = S5_SUB_ROWS // nb
    tw = 2 * LANES
    pitch = _s5_pitch(tl)

    def step_rows(t):
        return pl.ds(t, nb, stride=pitch) if pitch > 1 else pl.ds(0, nb)

    @pl.when(tb == 0)
    def _():
        _s5_expand_operands(b_ref, c_ref, bm_ref, cm_ref)
        for j in range(S5_TILES):
            hst_ref[:, j * tw:j * tw + LANES] = h0r_ref[:, j * LANES:(j + 1) * LANES]
            hst_ref[:, j * tw + LANES:(j + 1) * tw] = h0i_ref[:, j * LANES:(j + 1) * LANES]

    ar = [jnp.broadcast_to(a_ref[:, j * tw:j * tw + LANES], (nb, LANES)) for j in range(S5_TILES)]
    ai = [jnp.broadcast_to(a_ref[:, j * tw + LANES:(j + 1) * tw], (nb, LANES)) for j in range(S5_TILES)]
    dsk = d_ref[...]

    def project_in(t0, bu_ref, xs_ref):
        xs = jnp.concatenate([jnp.concatenate([r[step_rows(t0 + i), :] for r in u_refs], axis=1)
                              for i in range(tsub)], axis=0)
        xs_ref[...] = xs
        lhs = xs.astype(BF16)
        for j in range(S5_TILES):
            bu_ref[:, j * tw:(j + 1) * tw] = _dot(lhs, bm_ref[:, j * tw:(j + 1) * tw])

    def scan_project_out(t0, bu_ref, xs_ref, carry):
        new, hs = [], []
        for j in range(S5_TILES):
            hr, hi = carry[2 * j], carry[2 * j + 1]
            rows = []
            for i in range(tsub):
                br = bu_ref[i * nb:(i + 1) * nb, j * tw:j * tw + LANES]
                bi = bu_ref[i * nb:(i + 1) * nb, j * tw + LANES:(j + 1) * tw]
                hr, hi = ar[j] * hr - ai[j] * hi + br, ar[j] * hi + ai[j] * hr + bi
                rows.append(jnp.concatenate([hr, hi], axis=1))
            new += [hr, hi]
            hs.append(jnp.concatenate(rows, axis=0).astype(BF16))
        y = _dot(jnp.concatenate(hs, axis=1), cm_ref[...]) + dsk * xs_ref[...]
        for i in range(tsub):
            for k, y_ref in enumerate(y_refs):
                y_ref[step_rows(t0 + i), :] = y[i * nb:(i + 1) * nb, k * LANES:(k + 1) * LANES]
        return tuple(new)

    h_in = []
    for j in range(S5_TILES):
        h_in += [hst_ref[:, j * tw:j * tw + LANES], hst_ref[:, j * tw + LANES:(j + 1) * tw]]
    nsub = tl // tsub
    (bu0, xs0), (bu1, xs1) = bufs
    project_in(0, bu0, xs0)
    if nsub == 1:
        h_out = scan_project_out(0, bu0, xs0, tuple(h_in))
    else:
        def pair(p, carry):
            t0 = 2 * p * tsub
            project_in(t0 + tsub, bu1, xs1)
            carry = scan_project_out(t0, bu0, xs0, carry)
            project_in(jnp.minimum(t0 + 2 * tsub, tl - tsub), bu0, xs0)
            return scan_project_out(t0 + tsub, bu1, xs1, carry)
        h_out = lax.fori_loop(0, nsub // 2, pair, tuple(h_in))
    for j in range(S5_TILES):
        hst_ref[:, j * tw:j * tw + LANES] = h_out[2 * j]
        hst_ref[:, j * tw + LANES:(j + 1) * tw] = h_out[2 * j + 1]

    if pitch > tl:
        for y_ref in y_refs:
            for b in range(nb):
                y_ref[b * pitch + tl:(b + 1) * pitch, :] = jnp.zeros((pitch - tl, LANES), F32)

    @pl.when(tb == pl.num_programs(1) - 1)
    def _():
        for j in range(S5_TILES):
            hr_ref[:, j * LANES:(j + 1) * LANES] = h_out[2 * j]
            hi_ref[:, j * LANES:(j + 1) * LANES] = h_out[2 * j + 1]


def _s5(u, h0r, h0i, b, a, c, d, *, tl):
    nblk, rows, _ = u.shape
    pitch = _s5_pitch(tl)
    nb = rows // pitch
    nsub = tl * nb // S5_SUB_ROWS
    assert S5_SUB_ROWS % nb == 0 and tl * nb % S5_SUB_ROWS == 0 and (nsub == 1 or nsub % 2 == 0)
    chunk3 = lambda s: pl.BlockSpec((None,) + s, lambda c, t: (c, 0, 0))
    st_spec = pl.BlockSpec((nb, S5_NS), lambda c, t: (0, c))
    nu = S5_PIECES
    u_specs = [pl.BlockSpec((None, rows, LANES), functools.partial(lambda c, t, k: (t, 0, nu * c + k), k=k))
               for k in range(nu)]
    y_spec = pl.BlockSpec((None, rows, LANES), lambda c, t: (t, 0, c))
    y_shape = jax.ShapeDtypeStruct((nblk, rows, D_MODEL // nu), F32)
    res = pl.pallas_call(
        functools.partial(_s5_kernel, nb=nb, tl=tl),
        grid=(S5_CHUNKS, nblk),
        in_specs=u_specs + [st_spec, st_spec, chunk3((S5_CH, 2 * S5_STATE)), chunk3((1, 2 * S5_NS)),
                            chunk3((2 * S5_NS, S5_GROUP)), chunk3((1, S5_CH))],
        out_specs=[y_spec] * nu + [st_spec, st_spec],
        out_shape=[y_shape] * nu + [jax.ShapeDtypeStruct(h0r.shape, F32), jax.ShapeDtypeStruct(h0i.shape, F32)],
        scratch_shapes=[pltpu.VMEM((nb, 2 * S5_NS), F32)]
                       + [pltpu.VMEM((S5_SUB_ROWS, 2 * S5_NS), F32), pltpu.VMEM((S5_SUB_ROWS, S5_CH), F32)] * 2
                       + [pltpu.VMEM((S5_CH, 2 * S5_NS), BF16), pltpu.VMEM((2 * S5_NS, S5_CH), BF16)],
        compiler_params=_params(("parallel", "arbitrary")),
        name="s5",
    )(*([u] * nu), h0r, h0i, b, a, c, d)
    return res[:nu], res[nu], res[nu + 1]


def _expand_mats():
    r = lax.broadcasted_iota(jnp.int32, (3 * LANES, M_HEADS * M_HEADDIM), 0) % LANES
    c = lax.broadcasted_iota(jnp.int32, (3 * LANES, M_HEADS * M_HEADDIM), 1) // M_HEADDIM
    e64 = (r == c).astype(BF16)
    r = lax.broadcasted_iota(jnp.int32, (3 * LANES, M_HEADS * M_CHUNK), 0) % LANES
    c = lax.broadcasted_iota(jnp.int32, (3 * LANES, M_HEADS * M_CHUNK), 1) // M_CHUNK
    e128 = (r == c).astype(BF16)
    return e64, e128


LOG2E = 1.4426950408889634
SSD_CHUNKS_PER_STEP = 2


def _ssd_kernel(z_ref, conv_ref, dt_ref, alog_ref, dsk_ref, nw_ref, e64_ref, e128_ref,
                y_ref, hout_ref,
                ht_ref, act_ref, xdt_ref, xdte_ref, eacs_ref, rall_ref, yacc_ref):
    c = pl.program_id(1)

    @pl.when(c == 0)
    def _():
        ht_ref[...] = jnp.zeros_like(ht_ref)

    for k in range(SSD_CHUNKS_PER_STEP):
        rows = pl.ds(k * M_CHUNK, M_CHUNK)
        _ssd_chunk(z_ref.at[rows], conv_ref.at[rows], dt_ref.at[rows], alog_ref, dsk_ref, nw_ref, e64_ref, e128_ref,
                   y_ref.at[rows], ht_ref, act_ref.at[k], xdt_ref.at[k], xdte_ref.at[k], eacs_ref.at[k], rall_ref.at[k],
                   yacc_ref.at[k])

    @pl.when(c == pl.num_programs(1) - 1)
    def _():
        for j in range(0, M_INNER, LANES):
            hout_ref[j:j + LANES, :] = ht_ref[:, j:j + LANES].T


def _ssd_chunk(z_ref, conv_ref, dt_ref, alog_ref, dsk_ref, nw_ref, e64_ref, e128_ref,
               y_ref, ht_ref, act_ref, xdt_ref, xdte_ref, eacs_ref, rall_ref, yacc_ref):
    q = M_CHUNK
    cs = 512
    for j in range(0, M_CONV_DIM, cs):
        act_ref[:, j:j + cs] = _silu(conv_ref[:, j:j + cs])

    dt = dt_ref[...]
    a2 = dt * (-jnp.exp(alog_ref[...]) * LOG2E)
    ri = lax.broadcasted_iota(jnp.int32, (q, q), 0)
    ci = lax.broadcasted_iota(jnp.int32, (q, q), 1)
    causal = ri >= ci
    acs = jnp.dot(causal.astype(F32), a2, preferred_element_type=F32, precision=lax.Precision.HIGHEST)
    acs_t = acs.T
    dt2 = _split2(dt)
    eacs2 = _split2(jnp.exp2(acs))
    dte2 = _split2(jnp.exp2(acs[q - 1:q, :] - acs))
    for j in range(0, M_INNER, cs):
        sl = slice(j, j + cs)
        e = e64_ref[0:2 * LANES, sl]
        xdt = act_ref[:, sl] * _dot(dt2, e)
        xdt_ref[:, sl] = xdt.astype(BF16)
        xdte_ref[:, sl] = (xdt * _dot(dte2, e)).astype(BF16)
        eacs_ref[:, sl] = _dot(eacs2, e)
    acs3 = _split3(acs)
    for j in range(0, M_HEADS * q, cs):
        sl = slice(j, j + cs)
        rall_ref[:, sl] = _dot(acs3, e128_ref[:, sl])

    hp = M_HPG * M_HEADDIM
    lane = lax.broadcasted_iota(jnp.int32, (q, LANES), 1)
    first_head = lane < M_HEADDIM
    for g in range(M_GROUPS):
        gs = slice(g * hp, (g + 1) * hp)
        bm_t = act_ref[:, M_INNER + g * M_STATE:M_INNER + (g + 1) * M_STATE].T.astype(BF16)
        cm = act_ref[:, M_INNER + (M_GROUPS + g) * M_STATE:M_INNER + (M_GROUPS + g + 1) * M_STATE].astype(BF16)
        cbm = _dot(cm, bm_t)
        y_off = _dot(cm, ht_ref[:, gs].astype(BF16))
        st_t = _dot(bm_t, xdte_ref[:, gs])
        for pr in range(M_HPG // 2):
            h0 = g * M_HPG + 2 * pr
            ps = slice(h0 * M_HEADDIM, (h0 + 2) * M_HEADDIM)
            ms = []
            for h in (h0, h0 + 1):
                d2 = rall_ref[:, h * q:(h + 1) * q] - acs_t[h:h + 1, :]
                ms.append((cbm * jnp.where(causal, jnp.exp2(d2), 0.0)).astype(BF16))
            xp = xdt_ref[:, ps]
            zero = jnp.zeros_like(xp)
            rhs = jnp.concatenate([jnp.where(first_head, xp, zero), jnp.where(first_head, zero, xp)], axis=0)
            yd = _dot(jnp.concatenate(ms, axis=1), rhs)
            yacc_ref[:, ps] = yd + y_off[:, ps.start - gs.start:ps.stop - gs.start] * eacs_ref[:, ps] \
                + dsk_ref[:, ps] * act_ref[:, ps]
        ht_ref[:, gs] = eacs_ref[q - 1:q, gs] * ht_ref[:, gs] + st_t

    ss = jnp.zeros((q, 1), F32)
    for j in range(0, M_INNER, cs):
        sl = slice(j, j + cs)
        gt = yacc_ref[:, sl] * _silu(z_ref[:, sl])
        yacc_ref[:, sl] = gt
        ss = ss + jnp.sum(gt * gt, axis=-1, keepdims=True)
    inv = lax.rsqrt(ss * (1.0 / M_INNER) + EPS)
    for j in range(0, M_INNER, cs):
        sl = slice(j, j + cs)
        y_ref[:, sl] = yacc_ref[:, sl] * inv * nw_ref[:, sl]


def _ssd(zs, act, dt, alog, dsk, nw, e64, e128, *, nbatch, seq):
    q = M_CHUNK
    k = SSD_CHUNKS_PER_STEP
    assert seq % (k * q) == 0
    nc = seq // (k * q)
    row = lambda n: pl.BlockSpec((k * q, n), lambda b, c: (b * nc + c, 0))
    return pl.pallas_call(
        _ssd_kernel,
        grid=(nbatch, nc),
        in_specs=[row(M_INNER), row(M_CONV_DIM), row(LANES), _const_spec(alog.shape),
                  _const_spec(dsk.shape), _const_spec(nw.shape), _const_spec(e64.shape), _const_spec(e128.shape)],
        out_specs=[row(M_INNER), pl.BlockSpec((None, M_INNER, M_STATE), lambda b, c: (b, 0, 0))],
        out_shape=[jax.ShapeDtypeStruct((nbatch * seq, M_INNER), F32),
                   jax.ShapeDtypeStruct((nbatch, M_INNER, M_STATE), F32)],
        scratch_shapes=[pltpu.VMEM((M_STATE, M_INNER), F32), pltpu.VMEM((k, q, M_CONV_DIM), F32),
                        pltpu.VMEM((k, q, M_INNER), BF16),
                        pltpu.VMEM((k, q, M_INNER), BF16), pltpu.VMEM((k, q, M_INNER), F32),
                        pltpu.VMEM((k, q, M_HEADS * q), F32), pltpu.VMEM((k, q, M_INNER), F32)],
        compiler_params=_params(("parallel", "arbitrary")),
        name="ssd",
    )(zs, act, dt, alog, dsk, nw, e64, e128)


SSD_STEP_BB = 8


def _ssd_step_kernel(z_ref, xbc_ref, dt_ref, cbuf_ref, h0_ref, cw_ref, cb_ref, alog_ref, dsk_ref, nw_ref,
                     e64_ref, y_ref, hout_ref, convout_ref):
    bb = SSD_STEP_BB
    x_new = xbc_ref[...]
    conv = cb_ref[...] + x_new * cw_ref[M_CONV - 1:M_CONV, :]
    for k in range(M_CONV - 1):
        conv = conv + cbuf_ref[:, k, :] * cw_ref[k:k + 1, :]
    act = _silu(conv)
    for k in range(M_CONV - 2):
        convout_ref[:, k, :] = cbuf_ref[:, k + 1, :]
    convout_ref[:, M_CONV - 2, :] = x_new

    xs = act[:, :M_INNER]
    bmat = act[:, M_INNER:M_INNER + M_GROUPS * M_STATE]
    cmat = act[:, M_INNER + M_GROUPS * M_STATE:]
    dt = dt_ref[...]
    da = jnp.exp(dt * (-jnp.exp(alog_ref[...])))
    dt_e = _dot(_split3(dt), e64_ref[...])
    da_e = _dot(_split3(da), e64_ref[...])
    xdt = xs * dt_e

    hp = M_HPG * M_HEADDIM
    lane_group = lax.broadcasted_iota(jnp.int32, (M_GROUPS, M_INNER), 1) // hp
    row_group = lax.broadcasted_iota(jnp.int32, (M_GROUPS, M_INNER), 0)
    gmask = lane_group == row_group
    ones_rows = lax.broadcasted_iota(jnp.int32, (SUBLANES, M_STATE), 0) >= M_GROUPS

    cbx = []
    for g in range(M_GROUPS):
        cbg = jnp.sum(cmat[:, g * M_STATE:(g + 1) * M_STATE] * bmat[:, g * M_STATE:(g + 1) * M_STATE],
                      axis=-1, keepdims=True)
        cbx.append(jnp.broadcast_to(cbg, (bb, hp)))
    y = jnp.concatenate(cbx, axis=1) * xdt + dsk_ref[...] * xs

    y_off_rows = []
    for b in range(bb):
        xsel = jnp.where(gmask, jnp.broadcast_to(xdt[b:b + 1, :], (M_GROUPS, M_INNER)), 0.0)
        lhs = jnp.concatenate([xsel, *_pieces3(da_e[b:b + 1, :]), jnp.zeros((1, M_INNER), F32)],
                              axis=0).astype(BF16)
        bm4 = jnp.concatenate([bmat[b:b + 1, g * M_STATE:(g + 1) * M_STATE] for g in range(M_GROUPS)], axis=0)
        cm4 = jnp.concatenate([cmat[b:b + 1, g * M_STATE:(g + 1) * M_STATE] for g in range(M_GROUPS)], axis=0)
        zeros4 = jnp.zeros((SUBLANES - M_GROUPS, M_STATE), F32)
        rhs = jnp.concatenate([jnp.concatenate([bm4, zeros4], axis=0),
                               jnp.where(ones_rows, 1.0, 0.0)], axis=1).astype(BF16)
        res = lax.dot_general(lhs, rhs, _TN, preferred_element_type=F32)
        h0 = h0_ref[b]
        hout_ref[b] = res[:, M_STATE:] * h0 + res[:, :M_STATE]
        cm8 = jnp.concatenate([cm4, zeros4], axis=0).astype(BF16)
        yo = lax.dot_general(cm8, h0.astype(BF16), _NT, preferred_element_type=F32)
        y_off_rows.append(jnp.sum(jnp.where(gmask, yo[:M_GROUPS, :], 0.0), axis=0, keepdims=True))
    y = y + da_e * jnp.concatenate(y_off_rows, axis=0)

    y_ref[...] = _rms(y * _silu(z_ref[...]), nw_ref[...])


def _ssd_step(zs, xbc, dt, cbuf, h0, cw, cb, alog, dsk, nw, e64):
    n = zs.shape[0]
    bb = SSD_STEP_BB
    row = lambda w: pl.BlockSpec((bb, w), lambda i: (i, 0))
    blk3 = lambda s: pl.BlockSpec((bb,) + s, lambda i: (i, 0, 0))
    return pl.pallas_call(
        _ssd_step_kernel,
        grid=(n // bb,),
        in_specs=[row(M_INNER), row(M_CONV_DIM), row(LANES), blk3((M_CONV - 1, M_CONV_DIM)),
                  blk3((M_INNER, M_STATE)),
                  _const_spec(cw.shape), _const_spec(cb.shape), _const_spec(alog.shape),
                  _const_spec(dsk.shape), _const_spec(nw.shape), _const_spec(e64.shape)],
        out_specs=[row(M_INNER), blk3((M_INNER, M_STATE)), blk3((M_CONV - 1, M_CONV_DIM))],
        out_shape=[jax.ShapeDtypeStruct((n, M_INNER), F32),
                   jax.ShapeDtypeStruct((n, M_INNER, M_STATE), F32),
                   jax.ShapeDtypeStruct((n, M_CONV - 1, M_CONV_DIM), F32)],
        compiler_params=_params(("parallel",)),
        name="ssd_step",
    )(zs, xbc, dt, cbuf, h0, cw, cb, alog, dsk, nw, e64)


MXU_WIDTH = 256
FFN_CHUNKS = (0, 6 * MXU_WIDTH, D_FF)


def _out_kernel(x_ref, *refs):
    y5_refs = refs[:S5_PIECES]
    (yb_ref, gate_ref, gtm_ref, shf_ref, scf_ref, gtf_ref, wglu_ref, bglu_ref, wb5_ref, wbs_ref, wout_ref,
     npm_ref, npf_ref, npo_ref, wfi_ref, wfo_ref, o_ref) = refs[S5_PIECES:]
    rows = x_ref.shape[0]
    y5 = jnp.concatenate([y5_refs[k][0:rows, c * LANES:(c + 1) * LANES]
                          for c in range(S5_CHUNKS) for k in range(S5_PIECES)], axis=1)
    ya = _gelu_tanh(y5)
    glu = ya * _sigmoid(_dot(ya.astype(BF16), wglu_ref[...]) + bglu_ref[...])
    merged = (_sigmoid(gate_ref[:, :D_MODEL]) * _dot(glu.astype(BF16), wb5_ref[...])
              + _sigmoid(gate_ref[:, D_MODEL:]) * _dot(yb_ref[...].astype(BF16), wbs_ref[...]))
    mix = _dot(merged.astype(BF16), wout_ref[...])
    x1 = x_ref[...] + gtm_ref[...] * _rms(mix, npm_ref[...])

    hb = (_rms(x1, npf_ref[...]) * (1.0 + scf_ref[...]) + shf_ref[...]).astype(BF16)
    f = jnp.zeros_like(x1)
    for j, e in zip(FFN_CHUNKS[:-1], FFN_CHUNKS[1:]):
        gg = _dot(hb, wfi_ref[:, j:e])
        uu = _dot(hb, wfi_ref[:, D_FF + j:D_FF + e])
        f = f + _dot((_silu(gg) * uu).astype(BF16), wfo_ref[j:e, :])
    o_ref[...] = x1 + gtf_ref[...] * _rms(f, npo_ref[...])


def _out_stage(x, y5, yb, gates, mod, ws, *, per_row, tm, rows_per_batch):
    t = x.shape[0]
    row = lambda n: pl.BlockSpec((tm, n), lambda i: (i, 0))
    mods = _mod_specs(per_row, tm, rows_per_batch // tm if not per_row else 1, (2, 3, 4, 5))
    _, y5_spec = _s5_layout(per_row, t, tm, rows_per_batch, D_MODEL // S5_PIECES)
    return pl.pallas_call(
        _out_kernel,
        grid=(t // tm,),
        in_specs=[row(D_MODEL)] + [y5_spec] * S5_PIECES + [row(M_INNER), row(2 * D_MODEL)] + mods
                 + [_const_spec(w.shape) for w in ws],
        out_specs=row(D_MODEL),
        out_shape=jax.ShapeDtypeStruct((t, D_MODEL), F32),
        compiler_params=_params(("parallel",)),
        name="out_stage",
    )(x, *y5, yb, gates, mod, mod, mod, mod, *ws)


def _pad_lanes(v):
    return jnp.pad(v.reshape(1, -1), ((0, 0), (0, LANES - v.shape[-1])))


def kernel(x_prompt, x_sample, state_s5_re, state_s5_im, state_ssm, state_conv, c_prompt, c_sample, w_ada, b_ada, norm_pre_mix, norm_post_mix, norm_pre_ffn, norm_post_ffn, w_in, s5_lam_re, s5_lam_im, s5_log_dt, s5_b_re, s5_b_im, s5_c_re, s5_c_im, s5_d, s5_w_glu, s5_b_glu, m_conv_w, m_conv_b, m_dt_bias, m_a_log, m_d, m_norm, w_branch_s5, w_branch_ssd, w_out, w_ffn_in, w_ffn_out):
    bp, seq, _ = x_prompt.shape
    bs = x_sample.shape[0]
    assert x_sample.shape[1] == 1 and w_ada.shape[0] == 1 and seq % M_CHUNK == 0
    row1 = lambda v: v.reshape(1, -1)

    ws_in = _split_w_in(w_in[0])
    ws_out = (s5_w_glu[0].astype(BF16), row1(s5_b_glu[0]), w_branch_s5[0].astype(BF16),
              w_branch_ssd[0].astype(BF16), w_out[0].astype(BF16), row1(norm_post_mix[0]),
              row1(norm_pre_ffn[0]), row1(norm_post_ffn[0]), w_ffn_in[0].astype(BF16), w_ffn_out[0].astype(BF16))
    cw, cb = m_conv_w[0], row1(m_conv_b[0])
    dtb, alog = _pad_lanes(m_dt_bias[0]), _pad_lanes(m_a_log[0])
    dsk = row1(jnp.repeat(m_d[0], M_HEADDIM))
    nw = row1(m_norm[0])
    e64, e128 = _expand_mats()

    mod = _ada(jnp.concatenate([c_prompt, c_sample], axis=0), w_ada[0], b_ada[0])
    mod_p = mod[:bp].reshape(bp, 1, 6 * D_MODEL)
    mod_s = mod[bp:]

    ar, ai, bbr, bbi = _s5_disc(s5_lam_re[0], s5_lam_im[0], s5_log_dt[0], s5_b_re[0], s5_b_im[0])
    s5m = _s5_matrices(ar, ai, bbr, bbi, s5_c_re[0], s5_c_im[0], s5_d[0])
    g_pre = row1(norm_pre_mix[0])

    xp = x_prompt.reshape(bp * seq, D_MODEL)
    tm = min(ROW_TILE, seq)
    u, zs, act, dt, gates, tail = _in_proj(xp, mod_p, g_pre, ws_in, dtb, cw, cb,
                                           per_row=False, tm=tm, rows_per_batch=seq)
    p_conv = tail[:, SUBLANES - (M_CONV - 1):, :]
    zeros_s5 = jnp.zeros((bp, S5_GROUPS * S5_STATE), F32)
    y5, p_re, p_im = _s5(u, zeros_s5, zeros_s5, *s5m, tl=tm)
    yb, p_ssm = _ssd(zs, act, dt, alog, dsk, nw, e64, e128, nbatch=bp, seq=seq)
    y_prompt = _out_stage(xp, y5, yb, gates, mod_p, ws_out, per_row=False, tm=tm, rows_per_batch=seq)

    xs = x_sample.reshape(bs, D_MODEL)
    u, zs, xbc, dt, gates = _in_proj(xs, mod_s, g_pre, ws_in, dtb, cw, cb, per_row=True, tm=bs, rows_per_batch=1)
    y5, s_re, s_im = _s5(u, state_s5_re[0].reshape(bs, -1), state_s5_im[0].reshape(bs, -1), *s5m, tl=1)
    yb, s_ssm, s_conv = _ssd_step(zs, xbc, dt, state_conv[0], state_ssm[0].reshape(bs, M_INNER, M_STATE),
                                  cw, cb, alog, dsk, nw, e64)
    y_sample = _out_stage(xs, y5, yb, gates, mod_s, ws_out, per_row=True, tm=bs, rows_per_batch=1)

    s5_shape = (1, -1, S5_GROUPS, S5_STATE)
    ssm_shape = (1, -1, M_HEADS, M_HEADDIM, M_STATE)
    return (y_prompt.reshape(bp, seq, D_MODEL), y_sample.reshape(bs, 1, D_MODEL),
            p_re.reshape(s5_shape), p_im.reshape(s5_shape), p_ssm.reshape(ssm_shape), p_conv[None],
            s_re.reshape(s5_shape), s_im.reshape(s5_shape), s_ssm.reshape(ssm_shape), s_conv[None])
```

```python
import functools

import jax
import jax.numpy as jnp
from jax import lax
from jax.experimental import pallas as pl
from jax.experimental.pallas import tpu as pltpu

F32 = jnp.float32
BF16 = jnp.bfloat16

D_MODEL = 1024
EPS = 1e-6
S5_GROUP = 16
S5_GROUPS = 64
S5_STATE = 64
M_INNER = 2048
M_HEADDIM = 64
M_HEADS = 32
M_GROUPS = 4
M_HPG = 8
M_STATE = 128
M_CONV = 4
M_CONV_DIM = 3072
M_CHUNK = 128
D_FF = 2816
OFF_Z = D_MODEL
OFF_XBC = OFF_Z + M_INNER
OFF_DT = OFF_XBC + M_CONV_DIM
OFF_GA = OFF_DT + M_HEADS
IN_COLS = OFF_GA + 2 * D_MODEL

LANES = 128
SUBLANES = 8
VMEM_LIMIT = 56 * 1024 * 1024

S5_CH = 256
S5_NS = 1024
S5_CHUNKS = D_MODEL // S5_CH
S5_TILES = S5_NS // LANES
S5_PIECES = S5_CH // LANES
S5_SUB_ROWS = 128
ROW_TILE = 256
INPROJ_SLAB = 512
INPROJ_DEPTH = 3

_NT = (((1,), (1,)), ((), ()))
_TN = (((0,), (0,)), ((), ()))


def _const_spec(shape):
    nd = len(shape)
    return pl.BlockSpec(shape, lambda *_: (0,) * nd, pipeline_mode=pl.Buffered(1))


def _params(sem):
    return pltpu.CompilerParams(dimension_semantics=sem, vmem_limit_bytes=VMEM_LIMIT)


def _sigmoid(x):
    return 0.5 * jnp.tanh(0.5 * x) + 0.5


def _silu(x):
    hx = 0.5 * x
    return hx + hx * jnp.tanh(hx)


def _softplus(x):
    return jnp.maximum(x, 0.0) + jnp.log1p(jnp.exp(-jnp.abs(x)))


def _gelu_tanh(x):
    return 0.5 * x * (1.0 + jnp.tanh(0.7978845608028654 * (x + 0.044715 * (x * x * x))))


def _rms(x, g):
    return x * lax.rsqrt(jnp.mean(x * x, axis=-1, keepdims=True) + EPS) * g


def _dot(a, b):
    return jnp.dot(a, b, preferred_element_type=F32)


def _pieces3(x):
    hi = x.astype(BF16).astype(F32)
    r1 = x - hi
    mid = r1.astype(BF16).astype(F32)
    lo = (r1 - mid).astype(BF16).astype(F32)
    return hi, mid, lo


def _split2(x):
    return jnp.concatenate(_pieces3(x)[:2], axis=-1).astype(BF16)


def _split3(x):
    return jnp.concatenate(_pieces3(x), axis=-1).astype(BF16)


def _ada_kernel(c_ref, w_ref, b_ref, o_ref):
    s = _silu(c_ref[...]).astype(BF16)
    o_ref[...] = _dot(s, w_ref[...].astype(BF16)) + b_ref[...]


def _ada(c, w, b):
    n = c.shape[0]
    tn = 512
    return pl.pallas_call(
        _ada_kernel,
        grid=(6 * D_MODEL // tn,),
        in_specs=[_const_spec((n, D_MODEL)),
                  pl.BlockSpec((D_MODEL, tn), lambda j: (0, j)),
                  pl.BlockSpec((1, tn), lambda j: (0, j))],
        out_specs=pl.BlockSpec((n, tn), lambda j: (0, j)),
        out_shape=jax.ShapeDtypeStruct((n, 6 * D_MODEL), F32),
        compiler_params=_params(("arbitrary",)),
        name="ada",
    )(c, w, b.reshape(1, -1))


def _s5_disc_kernel(lr_ref, li_ref, ldt_ref, br_ref, bi_ref, ar_ref, ai_ref, bbr_ref, bbi_ref):
    lr = lr_ref[...]
    li = li_ref[...]
    dt = jnp.exp(ldt_ref[...])
    mag = jnp.exp(lr * dt)
    ar = mag * jnp.cos(li * dt)
    ai = mag * jnp.sin(li * dt)
    den = lr * lr + li * li
    nr = ar - 1.0
    f_re = (nr * lr + ai * li) / den
    f_im = (ai * lr - nr * li) / den
    br = br_ref[...]
    bi = bi_ref[...]
    ar_ref[...] = ar
    ai_ref[...] = ai
    bbr_ref[...] = f_re * br - f_im * bi
    bbi_ref[...] = f_re * bi + f_im * br


def _s5_disc(lam_re, lam_im, log_dt, b_re, b_im):
    g, n = lam_re.shape
    w = S5_GROUP * n
    tile = lambda a: jnp.tile(a, (1, S5_GROUP))
    ldt = jnp.broadcast_to(log_dt[:, None], (g, w))
    brt = jnp.swapaxes(b_re, 1, 2).reshape(g, w)
    bit = jnp.swapaxes(b_im, 1, 2).reshape(g, w)
    shp = jax.ShapeDtypeStruct((g, w), F32)
    ar, ai, bbr, bbi = pl.pallas_call(
        _s5_disc_kernel,
        out_shape=(shp, shp, shp, shp),
        name="s5_disc",
    )(tile(lam_re), tile(lam_im), ldt, brt, bit)
    return (ar[:, :n], ai[:, :n], bbr.reshape(g, S5_GROUP, n), bbi.reshape(g, S5_GROUP, n))


def _s5_matrices(abar_re, abar_im, bbar_re, bbar_im, c_re, c_im, d_skip):
    ns = 2 * S5_NS
    b = jnp.concatenate([bbar_re.reshape(D_MODEL, S5_STATE), bbar_im.reshape(D_MODEL, S5_STATE)],
                        axis=1).reshape(S5_CHUNKS, S5_CH, 2 * S5_STATE)

    def by_tile(c):
        return jnp.swapaxes(c, 1, 2).reshape(S5_CHUNKS, S5_TILES, 1, 2, S5_STATE, S5_GROUP)
    c = jnp.concatenate([by_tile(c_re), by_tile(-c_im)], axis=2).reshape(S5_CHUNKS, ns, S5_GROUP)

    by_tile_a = lambda v: v.reshape(S5_CHUNKS, S5_TILES, 1, LANES)
    a = jnp.concatenate([by_tile_a(abar_re), by_tile_a(abar_im)], axis=2).reshape(S5_CHUNKS, 1, ns)
    d = d_skip.reshape(S5_CHUNKS, 1, S5_CH)
    return b, a, c, d


def _s5_state_group(i):
    return 2 * (i // (2 * LANES)) + (i % LANES) // S5_STATE


def _s5_expand_operands(b_ref, c_ref, bm_ref, cm_ref):
    tw = 2 * LANES
    row_g = lax.broadcasted_iota(jnp.int32, (S5_CH, LANES), 0) // S5_GROUP
    lane_half = lax.broadcasted_iota(jnp.int32, (S5_CH, LANES), 1) // S5_STATE
    for part in range(2):
        bp = b_ref[:, part * S5_STATE:(part + 1) * S5_STATE]
        pattern = jnp.concatenate([bp, bp], axis=1)
        for j in range(S5_TILES):
            tile = jnp.where(row_g == 2 * j + lane_half, pattern, 0.0)
            bm_ref[:, j * tw + part * LANES:j * tw + (part + 1) * LANES] = tile.astype(BF16)
    spread = (lax.broadcasted_iota(jnp.int32, (S5_GROUP, S5_CH), 1) % S5_GROUP
              == lax.broadcasted_iota(jnp.int32, (S5_GROUP, S5_CH), 0)).astype(BF16)
    rows = 512
    for r0 in range(0, 2 * S5_NS, rows):
        tiled = _dot(c_ref[r0:r0 + rows, :].astype(BF16), spread)
        own = (_s5_state_group(lax.broadcasted_iota(jnp.int32, (rows, S5_CH), 0) + r0)
               == lax.broadcasted_iota(jnp.int32, (rows, S5_CH), 1) // S5_GROUP)
        cm_ref[r0:r0 + rows, :] = jnp.where(own, tiled, 0.0).astype(BF16)


def _split_w_in_kernel(w_ref, wm_ref, wdt_ref, wg_ref):
    w = w_ref[...]
    rows = w.shape[0]
    wm_ref[...] = w[:, :OFF_DT].astype(BF16)
    wdt_ref[...] = jnp.concatenate([w[:, OFF_DT:OFF_GA], jnp.zeros((rows, LANES - M_HEADS), F32)],
                                   axis=1).astype(BF16)
    wg_ref[...] = w[:, OFF_GA:].astype(BF16)


def _split_w_in(w):
    k = w.shape[0]
    tk = 256
    outs = (OFF_DT, LANES, 2 * D_MODEL)
    return pl.pallas_call(
        _split_w_in_kernel,
        grid=(k // tk,),
        in_specs=[pl.BlockSpec((tk, IN_COLS), lambda i: (i, 0))],
        out_specs=[pl.BlockSpec((tk, n), lambda i: (i, 0)) for n in outs],
        out_shape=[jax.ShapeDtypeStruct((k, n), BF16) for n in outs],
        compiler_params=_params(("parallel",)),
        name="split_w_in",
    )(w)


def _inproj_kernel(x_ref, sh_ref, sc_ref, g_ref, wm_ref, wdt_ref, wg_ref, dtb_ref, cw_ref, cb_ref,
                   u_ref, z_ref, xbc_ref, dt_ref, gate_ref, *rest, tiles_per_seq):
    accs = rest[-INPROJ_DEPTH:]
    h = _rms(x_ref[...], g_ref[...]) * (1.0 + sc_ref[...]) + sh_ref[...]
    hb = h.astype(BF16)
    rows = hb.shape[0]
    step = INPROJ_SLAB

    if u_ref.shape[0] > rows:
        u_ref[rows:, :] = jnp.zeros((u_ref.shape[0] - rows, D_MODEL), F32)

    if tiles_per_seq is None:
        conv_fn = None
    else:
        tail_ref, carry_ref = rest[:2]

        @pl.when(pl.program_id(0) % tiles_per_seq == 0)
        def _():
            carry_ref[...] = jnp.zeros_like(carry_ref)

        def conv_fn(r, sl):
            ext = jnp.concatenate([carry_ref[:, sl], r], axis=0)
            s1 = pltpu.roll(ext, 1, axis=0)
            near = ext * cw_ref[3:4, sl] + s1 * cw_ref[2:3, sl]
            far = ext * cw_ref[1:2, sl] + s1 * cw_ref[0:1, sl]
            carry_ref[:, sl] = r[rows - SUBLANES:, :]
            tail_ref[:, sl] = r[rows - SUBLANES:, :]
            return (near + pltpu.roll(far, 2, axis=0))[SUBLANES:, :] + cb_ref[:, sl]

    def slabs(o_ref, w_ref, w_off, fn):
        n = o_ref.shape[-1]
        return [(o_ref, slice(j, j + min(step, n - j)), w_ref, w_off + j, fn) for j in range(0, n, step)]

    su = slabs(u_ref, wm_ref, 0, None)
    sz = slabs(z_ref, wm_ref, OFF_Z, None)
    sx = slabs(xbc_ref, wm_ref, OFF_XBC, conv_fn)
    sd = slabs(dt_ref, wdt_ref, 0, lambda r, sl: _softplus(r + dtb_ref[:, sl]))
    sg = slabs(gate_ref, wg_ref, 0, None)
    light = su + sz + sg + sd
    order = []
    while sx or light:
        if sx:
            order.append(sx.pop(0))
        if light:
            order.append(light.pop(0))

    def matmul(s):
        o_ref, sl, w_ref, w_col, _ = order[s]
        accs[s % INPROJ_DEPTH][:, 0:sl.stop - sl.start] = _dot(hb, w_ref[:, w_col:w_col + sl.stop - sl.start])

    for s in range(min(INPROJ_DEPTH - 1, len(order))):
        matmul(s)
    for s, (o_ref, sl, _, _, fn) in enumerate(order):
        if s + INPROJ_DEPTH - 1 < len(order):
            matmul(s + INPROJ_DEPTH - 1)
        r = accs[s % INPROJ_DEPTH][:, 0:sl.stop - sl.start]
        o_ref[0:rows, sl] = r if fn is None else fn(r, sl)


def _mod_specs(per_row, tm, rows_per_mod, cols):
    if per_row:
        return [pl.BlockSpec((tm, D_MODEL), functools.partial(lambda i, c: (i, c), c=c)) for c in cols]
    return [pl.BlockSpec((None, 1, D_MODEL), functools.partial(lambda i, c: (i // rows_per_mod, 0, c), c=c))
            for c in cols]


def _s5_layout(per_row, t, tm, rows_per_batch, width):
    if per_row:
        return (1, t, width), pl.BlockSpec((None, tm, width), lambda i: (0, i, 0))
    nblk, pitch = rows_per_batch // tm, _s5_pitch(tm)
    return ((nblk, (t // rows_per_batch) * pitch, width),
            pl.BlockSpec((None, pitch, width), lambda i: (i % nblk, i // nblk, 0)))


def _in_proj(x, mod, g, ws, dtb, cw, cb, *, per_row, tm, rows_per_batch):
    t = x.shape[0]
    row = lambda n: pl.BlockSpec((tm, n), lambda i: (i, 0))
    outs = (M_INNER, M_CONV_DIM, LANES, 2 * D_MODEL)
    u_shape, u_spec = _s5_layout(per_row, t, tm, rows_per_batch, D_MODEL)
    out_specs = [u_spec] + [row(n) for n in outs]
    out_shape = [jax.ShapeDtypeStruct(u_shape, F32)] + [jax.ShapeDtypeStruct((t, n), F32) for n in outs]
    scratch = []
    tiles_per_seq = None
    if not per_row:
        tiles_per_seq = rows_per_batch // tm
        out_specs.append(pl.BlockSpec((None, SUBLANES, M_CONV_DIM), lambda i: (i // tiles_per_seq, 0, 0)))
        out_shape.append(jax.ShapeDtypeStruct((t // rows_per_batch, SUBLANES, M_CONV_DIM), F32))
        scratch.append(pltpu.VMEM((SUBLANES, M_CONV_DIM), F32))
    scratch += [pltpu.VMEM((tm, INPROJ_SLAB), F32)] * INPROJ_DEPTH
    consts = (g,) + tuple(ws) + (dtb, cw, cb)
    return pl.pallas_call(
        functools.partial(_inproj_kernel, tiles_per_seq=tiles_per_seq),
        grid=(t // tm,),
        in_specs=[row(D_MODEL)] + _mod_specs(per_row, tm, rows_per_batch // tm if not per_row else 1, (0, 1))
                 + [_const_spec(w.shape) for w in consts],
        out_specs=out_specs,
        out_shape=out_shape,
        scratch_shapes=scratch,
        compiler_params=_params(("arbitrary",)),
        name="in_proj",
    )(x, mod, mod, *consts)


def _s5_pitch(tl):
    return tl if tl == 1 else tl + SUBLANES


def _s5_kernel(*refs, nb, tl):
    nu = S5_PIECES
    u_refs = refs[:nu]
    h0r_ref, h0i_ref, b_ref, a_ref, c_ref, d_ref = refs[nu:nu + 6]
    y_refs = refs[nu + 6:2 * nu + 6]
    hr_ref, hi_ref, hst_ref, bu0_ref, xs0_ref, bu1_ref, xs1_ref, bm_ref, cm_ref = refs[2 * nu + 6:]
    bufs = ((bu0_ref, xs0_ref), (bu1_ref, xs1_ref))
    tb = pl.program_id(1)
    tsub = S5_SUB_ROWS // nb
    tw = 2 * LANES
    pitch = _s5_pitch(tl)

    def step_rows(t):
        return pl.ds(t, nb, stride=pitch) if pitch > 1 else pl.ds(0, nb)

    @pl.when(tb == 0)
    def _():
        _s5_expand_operands(b_ref, c_ref, bm_ref, cm_ref)
        for j in range(S5_TILES):
            hst_ref[:, j * tw:j * tw + LANES] = h0r_ref[:, j * LANES:(j + 1) * LANES]
            hst_ref[:, j * tw + LANES:(j + 1) * tw] = h0i_ref[:, j * LANES:(j + 1) * LANES]

    ar = [jnp.broadcast_to(a_ref[:, j * tw:j * tw + LANES], (nb, LANES)) for j in range(S5_TILES)]
    ai = [jnp.broadcast_to(a_ref[:, j * tw + LANES:(j + 1) * tw], (nb, LANES)) for j in range(S5_TILES)]
    dsk = d_ref[...]

    def project_in(t0, bu_ref, xs_ref):
        xs = jnp.concatenate([jnp.concatenate([r[step_rows(t0 + i), :] for r in u_refs], axis=1)
                              for i in range(tsub)], axis=0)
        xs_ref[...] = xs
        lhs = xs.astype(BF16)
        for j in range(S5_TILES):
            bu_ref[:, j * tw:(j + 1) * tw] = _dot(lhs, bm_ref[:, j * tw:(j + 1) * tw])

    def scan_project_out(t0, bu_ref, xs_ref, carry):
        new, hs = [], []
        for j in range(S5_TILES):
            hr, hi = carry[2 * j], carry[2 * j + 1]
            rows = []
            for i in range(tsub):
                br = bu_ref[i * nb:(i + 1) * nb, j * tw:j * tw + LANES]
                bi = bu_ref[i * nb:(i + 1) * nb, j * tw + LANES:(j + 1) * tw]
                hr, hi = ar[j] * hr - ai[j] * hi + br, ar[j] * hi + ai[j] * hr + bi
                rows.append(jnp.concatenate([hr, hi], axis=1))
            new += [hr, hi]
            hs.append(jnp.concatenate(rows, axis=0).astype(BF16))
        y = _dot(jnp.concatenate(hs, axis=1), cm_ref[...]) + dsk * xs_ref[...]
        for i in range(tsub):
            for k, y_ref in enumerate(y_refs):
                y_ref[step_rows(t0 + i), :] = y[i * nb:(i + 1) * nb, k * LANES:(k + 1) * LANES]
        return tuple(new)

    h_in = []
    for j in range(S5_TILES):
        h_in += [hst_ref[:, j * tw:j * tw + LANES], hst_ref[:, j * tw + LANES:(j + 1) * tw]]
    nsub = tl // tsub
    (bu0, xs0), (bu1, xs1) = bufs
    project_in(0, bu0, xs0)
    if nsub == 1:
        h_out = scan_project_out(0, bu0, xs0, tuple(h_in))
    else:
        def pair(p, carry):
            t0 = 2 * p * tsub
            project_in(t0 + tsub, bu1, xs1)
            carry = scan_project_out(t0, bu0, xs0, carry)
            project_in(jnp.minimum(t0 + 2 * tsub, tl - tsub), bu0, xs0)
            return scan_project_out(t0 + tsub, bu1, xs1, carry)
        h_out = lax.fori_loop(0, nsub // 2, pair, tuple(h_in))
    for j in range(S5_TILES):
        hst_ref[:, j * tw:j * tw + LANES] = h_out[2 * j]
        hst_ref[:, j * tw + LANES:(j + 1) * tw] = h_out[2 * j + 1]

    if pitch > tl:
        for y_ref in y_refs:
            for b in range(nb):
                y_ref[b * pitch + tl:(b + 1) * pitch, :] = jnp.zeros((pitch - tl, LANES), F32)

    @pl.when(tb == pl.num_programs(1) - 1)
    def _():
        for j in range(S5_TILES):
            hr_ref[:, j * LANES:(j + 1) * LANES] = h_out[2 * j]
            hi_ref[:, j * LANES:(j + 1) * LANES] = h_out[2 * j + 1]


def _s5(u, h0r, h0i, b, a, c, d, *, tl):
    nblk, rows, _ = u.shape
    pitch = _s5_pitch(tl)
    nb = rows // pitch
    nsub = tl * nb // S5_SUB_ROWS
    assert S5_SUB_ROWS % nb == 0 and tl * nb % S5_SUB_ROWS == 0 and (nsub == 1 or nsub % 2 == 0)
    chunk3 = lambda s: pl.BlockSpec((None,) + s, lambda c, t: (c, 0, 0))
    st_spec = pl.BlockSpec((nb, S5_NS), lambda c, t: (0, c))
    nu = S5_PIECES
    u_specs = [pl.BlockSpec((None, rows, LANES), functools.partial(lambda c, t, k: (t, 0, nu * c + k), k=k))
               for k in range(nu)]
    y_spec = pl.BlockSpec((None, rows, LANES), lambda c, t: (t, 0, c))
    y_shape = jax.ShapeDtypeStruct((nblk, rows, D_MODEL // nu), F32)
    res = pl.pallas_call(
        functools.partial(_s5_kernel, nb=nb, tl=tl),
        grid=(S5_CHUNKS, nblk),
        in_specs=u_specs + [st_spec, st_spec, chunk3((S5_CH, 2 * S5_STATE)), chunk3((1, 2 * S5_NS)),
                            chunk3((2 * S5_NS, S5_GROUP)), chunk3((1, S5_CH))],
        out_specs=[y_spec] * nu + [st_spec, st_spec],
        out_shape=[y_shape] * nu + [jax.ShapeDtypeStruct(h0r.shape, F32), jax.ShapeDtypeStruct(h0i.shape, F32)],
        scratch_shapes=[pltpu.VMEM((nb, 2 * S5_NS), F32)]
                       + [pltpu.VMEM((S5_SUB_ROWS, 2 * S5_NS), F32), pltpu.VMEM((S5_SUB_ROWS, S5_CH), F32)] * 2
                       + [pltpu.VMEM((S5_CH, 2 * S5_NS), BF16), pltpu.VMEM((2 * S5_NS, S5_CH), BF16)],
        compiler_params=_params(("parallel", "arbitrary")),
        name="s5",
    )(*([u] * nu), h0r, h0i, b, a, c, d)
    return res[:nu], res[nu], res[nu + 1]


def _expand_mat():
    r = lax.broadcasted_iota(jnp.int32, (3 * LANES, M_HEADS * M_HEADDIM), 0) % LANES
    c = lax.broadcasted_iota(jnp.int32, (3 * LANES, M_HEADS * M_HEADDIM), 1) // M_HEADDIM
    return (r == c).astype(BF16)


LOG2E = 1.4426950408889634
SSD_CHUNKS_PER_STEP = 2


def _ssd_kernel(z_ref, conv_ref, dt_ref, alog_ref, dsk_ref, nw_ref, e64_ref,
                y_ref, hout_ref,
                ht_ref, act_ref, xs_ref, xdte_ref, eacs_ref, yacc_ref):
    c = pl.program_id(1)

    @pl.when(c == 0)
    def _():
        ht_ref[...] = jnp.zeros_like(ht_ref)

    for k in range(SSD_CHUNKS_PER_STEP):
        rows = pl.ds(k * M_CHUNK, M_CHUNK)
        _ssd_chunk(z_ref.at[rows], conv_ref.at[rows], dt_ref.at[rows], alog_ref, dsk_ref, nw_ref, e64_ref,
                   y_ref.at[rows], ht_ref, act_ref.at[k], xs_ref.at[k], xdte_ref.at[k], eacs_ref.at[k], yacc_ref.at[k])

    @pl.when(c == pl.num_programs(1) - 1)
    def _():
        for j in range(0, M_INNER, LANES):
            hout_ref[j:j + LANES, :] = ht_ref[:, j:j + LANES].T


def _ssd_chunk(z_ref, conv_ref, dt_ref, alog_ref, dsk_ref, nw_ref, e64_ref,
               y_ref, ht_ref, act_ref, xs_ref, xdte_ref, eacs_ref, yacc_ref):
    q = M_CHUNK
    cs = 512
    for j in range(0, M_CONV_DIM, cs):
        act_ref[:, j:j + cs] = _silu(conv_ref[:, j:j + cs])

    dt = dt_ref[...]
    a2 = dt * (-jnp.exp(alog_ref[...]) * LOG2E)
    ri = lax.broadcasted_iota(jnp.int32, (q, q), 0)
    ci = lax.broadcasted_iota(jnp.int32, (q, q), 1)
    causal = ri >= ci
    acs = jnp.dot(causal.astype(F32), a2, preferred_element_type=F32, precision=lax.Precision.HIGHEST)
    src_t = (acs - jnp.log2(dt)).T
    eacs2 = _split2(jnp.exp2(acs))
    dte2 = _split2(dt * jnp.exp2(acs[q - 1:q, :] - acs))
    for j in range(0, M_INNER, cs):
        sl = slice(j, j + cs)
        e = e64_ref[0:2 * LANES, sl]
        xs = act_ref[:, sl]
        xs_ref[:, sl] = xs.astype(BF16)
        xdte_ref[:, sl] = (xs * _dot(dte2, e)).astype(BF16)
        eacs_ref[:, sl] = _dot(eacs2, e)

    hp = M_HPG * M_HEADDIM
    lane = lax.broadcasted_iota(jnp.int32, (q, LANES), 1)
    first_head = lane < M_HEADDIM
    for g in range(M_GROUPS):
        gs = slice(g * hp, (g + 1) * hp)
        bm_t = act_ref[:, M_INNER + g * M_STATE:M_INNER + (g + 1) * M_STATE].T.astype(BF16)
        cm = act_ref[:, M_INNER + (M_GROUPS + g) * M_STATE:M_INNER + (M_GROUPS + g + 1) * M_STATE].astype(BF16)
        cbm = _dot(cm, bm_t)
        y_off = _dot(cm, ht_ref[:, gs].astype(BF16))
        st_t = _dot(bm_t, xdte_ref[:, gs])
        for pr in range(M_HPG // 2):
            h0 = g * M_HPG + 2 * pr
            ps = slice(h0 * M_HEADDIM, (h0 + 2) * M_HEADDIM)
            ms = []
            for h in (h0, h0 + 1):
                d2 = jnp.broadcast_to(acs[:, h:h + 1], (q, q)) - src_t[h:h + 1, :]
                ms.append((cbm * jnp.where(causal, jnp.exp2(d2), 0.0)).astype(BF16))
            xp = xs_ref[:, ps]
            zero = jnp.zeros_like(xp)
            rhs = jnp.concatenate([jnp.where(first_head, xp, zero), jnp.where(first_head, zero, xp)], axis=0)
            yd = _dot(jnp.concatenate(ms, axis=1), rhs)
            yacc_ref[:, ps] = yd + y_off[:, ps.start - gs.start:ps.stop - gs.start] * eacs_ref[:, ps] \
                + dsk_ref[:, ps] * act_ref[:, ps]
        ht_ref[:, gs] = eacs_ref[q - 1:q, gs] * ht_ref[:, gs] + st_t

    ss = jnp.zeros((q, 1), F32)
    for j in range(0, M_INNER, cs):
        sl = slice(j, j + cs)
        gt = yacc_ref[:, sl] * _silu(z_ref[:, sl])
        yacc_ref[:, sl] = gt
        ss = ss + jnp.sum(gt * gt, axis=-1, keepdims=True)
    inv = lax.rsqrt(ss * (1.0 / M_INNER) + EPS)
    for j in range(0, M_INNER, cs):
        sl = slice(j, j + cs)
        y_ref[:, sl] = yacc_ref[:, sl] * inv * nw_ref[:, sl]


def _ssd(z, conv, dt, alog, dsk, nw, e64, *, nbatch, seq):
    q = M_CHUNK
    k = SSD_CHUNKS_PER_STEP
    assert seq % (k * q) == 0
    nc = seq // (k * q)
    row = lambda n: pl.BlockSpec((k * q, n), lambda b, c: (b * nc + c, 0))
    return pl.pallas_call(
        _ssd_kernel,
        grid=(nbatch, nc),
        in_specs=[row(M_INNER), row(M_CONV_DIM), row(LANES), _const_spec(alog.shape),
                  _const_spec(dsk.shape), _const_spec(nw.shape), _const_spec(e64.shape)],
        out_specs=[row(M_INNER), pl.BlockSpec((None, M_INNER, M_STATE), lambda b, c: (b, 0, 0))],
        out_shape=[jax.ShapeDtypeStruct((nbatch * seq, M_INNER), F32),
                   jax.ShapeDtypeStruct((nbatch, M_INNER, M_STATE), F32)],
        scratch_shapes=[pltpu.VMEM((M_STATE, M_INNER), F32), pltpu.VMEM((k, q, M_CONV_DIM), F32),
                        pltpu.VMEM((k, q, M_INNER), BF16),
                        pltpu.VMEM((k, q, M_INNER), BF16), pltpu.VMEM((k, q, M_INNER), F32),
                        pltpu.VMEM((k, q, M_INNER), F32)],
        compiler_params=_params(("parallel", "arbitrary")),
        name="ssd",
    )(z, conv, dt, alog, dsk, nw, e64)


SSD_STEP_BB = 8


def _ssd_step_kernel(z_ref, xbc_ref, dt_ref, cbuf_ref, h0_ref, cw_ref, cb_ref, alog_ref, dsk_ref, nw_ref,
                     e64_ref, y_ref, hout_ref, convout_ref):
    bb = SSD_STEP_BB
    x_new = xbc_ref[...]
    conv = cb_ref[...] + x_new * cw_ref[M_CONV - 1:M_CONV, :]
    for k in range(M_CONV - 1):
        conv = conv + cbuf_ref[:, k, :] * cw_ref[k:k + 1, :]
    act = _silu(conv)
    for k in range(M_CONV - 2):
        convout_ref[:, k, :] = cbuf_ref[:, k + 1, :]
    convout_ref[:, M_CONV - 2, :] = x_new

    xs = act[:, :M_INNER]
    bmat = act[:, M_INNER:M_INNER + M_GROUPS * M_STATE]
    cmat = act[:, M_INNER + M_GROUPS * M_STATE:]
    dt = dt_ref[...]
    da = jnp.exp(dt * (-jnp.exp(alog_ref[...])))
    dt_e = _dot(_split3(dt), e64_ref[...])
    da_e = _dot(_split3(da), e64_ref[...])
    xdt = xs * dt_e

    hp = M_HPG * M_HEADDIM
    lane_group = lax.broadcasted_iota(jnp.int32, (M_GROUPS, M_INNER), 1) // hp
    row_group = lax.broadcasted_iota(jnp.int32, (M_GROUPS, M_INNER), 0)
    gmask = lane_group == row_group
    ones_rows = lax.broadcasted_iota(jnp.int32, (SUBLANES, M_STATE), 0) >= M_GROUPS

    cbx = []
    for g in range(M_GROUPS):
        cbg = jnp.sum(cmat[:, g * M_STATE:(g + 1) * M_STATE] * bmat[:, g * M_STATE:(g + 1) * M_STATE],
                      axis=-1, keepdims=True)
        cbx.append(jnp.broadcast_to(cbg, (bb, hp)))
    y = jnp.concatenate(cbx, axis=1) * xdt + dsk_ref[...] * xs

    y_off_rows = []
    for b in range(bb):
        xsel = jnp.where(gmask, jnp.broadcast_to(xdt[b:b + 1, :], (M_GROUPS, M_INNER)), 0.0)
        lhs = jnp.concatenate([xsel, *_pieces3(da_e[b:b + 1, :]), jnp.zeros((1, M_INNER), F32)],
                              axis=0).astype(BF16)
        bm4 = jnp.concatenate([bmat[b:b + 1, g * M_STATE:(g + 1) * M_STATE] for g in range(M_GROUPS)], axis=0)
        cm4 = jnp.concatenate([cmat[b:b + 1, g * M_STATE:(g + 1) * M_STATE] for g in range(M_GROUPS)], axis=0)
        zeros4 = jnp.zeros((SUBLANES - M_GROUPS, M_STATE), F32)
        rhs = jnp.concatenate([jnp.concatenate([bm4, zeros4], axis=0),
                               jnp.where(ones_rows, 1.0, 0.0)], axis=1).astype(BF16)
        res = lax.dot_general(lhs, rhs, _TN, preferred_element_type=F32)
        h0 = h0_ref[b]
        hout_ref[b] = res[:, M_STATE:] * h0 + res[:, :M_STATE]
        cm8 = jnp.concatenate([cm4, zeros4], axis=0).astype(BF16)
        yo = lax.dot_general(cm8, h0.astype(BF16), _NT, preferred_element_type=F32)
        y_off_rows.append(jnp.sum(jnp.where(gmask, yo[:M_GROUPS, :], 0.0), axis=0, keepdims=True))
    y = y + da_e * jnp.concatenate(y_off_rows, axis=0)

    y_ref[...] = _rms(y * _silu(z_ref[...]), nw_ref[...])


def _ssd_step(zs, xbc, dt, cbuf, h0, cw, cb, alog, dsk, nw, e64):
    n = zs.shape[0]
    bb = SSD_STEP_BB
    row = lambda w: pl.BlockSpec((bb, w), lambda i: (i, 0))
    blk3 = lambda s: pl.BlockSpec((bb,) + s, lambda i: (i, 0, 0))
    return pl.pallas_call(
        _ssd_step_kernel,
        grid=(n // bb,),
        in_specs=[row(M_INNER), row(M_CONV_DIM), row(LANES), blk3((M_CONV - 1, M_CONV_DIM)),
                  blk3((M_INNER, M_STATE)),
                  _const_spec(cw.shape), _const_spec(cb.shape), _const_spec(alog.shape),
                  _const_spec(dsk.shape), _const_spec(nw.shape), _const_spec(e64.shape)],
        out_specs=[row(M_INNER), blk3((M_INNER, M_STATE)), blk3((M_CONV - 1, M_CONV_DIM))],
        out_shape=[jax.ShapeDtypeStruct((n, M_INNER), F32),
                   jax.ShapeDtypeStruct((n, M_INNER, M_STATE), F32),
                   jax.ShapeDtypeStruct((n, M_CONV - 1, M_CONV_DIM), F32)],
        compiler_params=_params(("parallel",)),
        name="ssd_step",
    )(zs, xbc, dt, cbuf, h0, cw, cb, alog, dsk, nw, e64)


MXU_WIDTH = 256
FFN_CHUNKS = (0, 6 * MXU_WIDTH, D_FF)


def _out_kernel(x_ref, *refs):
    y5_refs = refs[:S5_PIECES]
    (yb_ref, gate_ref, gtm_ref, shf_ref, scf_ref, gtf_ref, wglu_ref, bglu_ref, wb5_ref, wbs_ref, wout_ref,
     npm_ref, npf_ref, npo_ref, wfi_ref, wfo_ref, o_ref) = refs[S5_PIECES:]
    rows = x_ref.shape[0]
    y5 = jnp.concatenate([y5_refs[k][0:rows, c * LANES:(c + 1) * LANES]
                          for c in range(S5_CHUNKS) for k in range(S5_PIECES)], axis=1)
    ya = _gelu_tanh(y5)
    glu = ya * _sigmoid(_dot(ya.astype(BF16), wglu_ref[...]) + bglu_ref[...])
    merged = (_sigmoid(gate_ref[:, :D_MODEL]) * _dot(glu.astype(BF16), wb5_ref[...])
              + _sigmoid(gate_ref[:, D_MODEL:]) * _dot(yb_ref[...].astype(BF16), wbs_ref[...]))
    mix = _dot(merged.astype(BF16), wout_ref[...])
    x1 = x_ref[...] + gtm_ref[...] * _rms(mix, npm_ref[...])

    hb = (_rms(x1, npf_ref[...]) * (1.0 + scf_ref[...]) + shf_ref[...]).astype(BF16)
    f = jnp.zeros_like(x1)
    for j, e in zip(FFN_CHUNKS[:-1], FFN_CHUNKS[1:]):
        gg = _dot(hb, wfi_ref[:, j:e])
        uu = _dot(hb, wfi_ref[:, D_FF + j:D_FF + e])
        f = f + _dot((_silu(gg) * uu).astype(BF16), wfo_ref[j:e, :])
    o_ref[...] = x1 + gtf_ref[...] * _rms(f, npo_ref[...])


def _out_stage(x, y5, yb, gates, mod, ws, *, per_row, tm, rows_per_batch):
    t = x.shape[0]
    row = lambda n: pl.BlockSpec((tm, n), lambda i: (i, 0))
    mods = _mod_specs(per_row, tm, rows_per_batch // tm if not per_row else 1, (2, 3, 4, 5))
    _, y5_spec = _s5_layout(per_row, t, tm, rows_per_batch, D_MODEL // S5_PIECES)
    return pl.pallas_call(
        _out_kernel,
        grid=(t // tm,),
        in_specs=[row(D_MODEL)] + [y5_spec] * S5_PIECES + [row(M_INNER), row(2 * D_MODEL)] + mods
                 + [_const_spec(w.shape) for w in ws],
        out_specs=row(D_MODEL),
        out_shape=jax.ShapeDtypeStruct((t, D_MODEL), F32),
        compiler_params=_params(("parallel",)),
        name="out_stage",
    )(x, *y5, yb, gates, mod, mod, mod, mod, *ws)


def _pad_lanes(v):
    return jnp.pad(v.reshape(1, -1), ((0, 0), (0, LANES - v.shape[-1])))


def kernel(x_prompt, x_sample, state_s5_re, state_s5_im, state_ssm, state_conv, c_prompt, c_sample, w_ada, b_ada, norm_pre_mix, norm_post_mix, norm_pre_ffn, norm_post_ffn, w_in, s5_lam_re, s5_lam_im, s5_log_dt, s5_b_re, s5_b_im, s5_c_re, s5_c_im, s5_d, s5_w_glu, s5_b_glu, m_conv_w, m_conv_b, m_dt_bias, m_a_log, m_d, m_norm, w_branch_s5, w_branch_ssd, w_out, w_ffn_in, w_ffn_out):
    bp, seq, _ = x_prompt.shape
    bs = x_sample.shape[0]
    assert x_sample.shape[1] == 1 and w_ada.shape[0] == 1 and seq % M_CHUNK == 0
    row1 = lambda v: v.reshape(1, -1)

    ws_in = _split_w_in(w_in[0])
    ws_out = (s5_w_glu[0].astype(BF16), row1(s5_b_glu[0]), w_branch_s5[0].astype(BF16),
              w_branch_ssd[0].astype(BF16), w_out[0].astype(BF16), row1(norm_post_mix[0]),
              row1(norm_pre_ffn[0]), row1(norm_post_ffn[0]), w_ffn_in[0].astype(BF16), w_ffn_out[0].astype(BF16))
    cw, cb = m_conv_w[0], row1(m_conv_b[0])
    dtb, alog = _pad_lanes(m_dt_bias[0]), _pad_lanes(m_a_log[0])
    dsk = row1(jnp.repeat(m_d[0], M_HEADDIM))
    nw = row1(m_norm[0])
    e64 = _expand_mat()

    mod = _ada(jnp.concatenate([c_prompt, c_sample], axis=0), w_ada[0], b_ada[0])
    mod_p = mod[:bp].reshape(bp, 1, 6 * D_MODEL)
    mod_s = mod[bp:]

    ar, ai, bbr, bbi = _s5_disc(s5_lam_re[0], s5_lam_im[0], s5_log_dt[0], s5_b_re[0], s5_b_im[0])
    s5m = _s5_matrices(ar, ai, bbr, bbi, s5_c_re[0], s5_c_im[0], s5_d[0])
    g_pre = row1(norm_pre_mix[0])

    xp = x_prompt.reshape(bp * seq, D_MODEL)
    tm = min(ROW_TILE, seq)
    u, zs, act, dt, gates, tail = _in_proj(xp, mod_p, g_pre, ws_in, dtb, cw, cb,
                                           per_row=False, tm=tm, rows_per_batch=seq)
    p_conv = tail[:, SUBLANES - (M_CONV - 1):, :]
    zeros_s5 = jnp.zeros((bp, S5_GROUPS * S5_STATE), F32)
    y5, p_re, p_im = _s5(u, zeros_s5, zeros_s5, *s5m, tl=tm)
    yb, p_ssm = _ssd(zs, act, dt, alog, dsk, nw, e64, nbatch=bp, seq=seq)
    y_prompt = _out_stage(xp, y5, yb, gates, mod_p, ws_out, per_row=False, tm=tm, rows_per_batch=seq)

    xs = x_sample.reshape(bs, D_MODEL)
    u, zs, xbc, dt, gates = _in_proj(xs, mod_s, g_pre, ws_in, dtb, cw, cb, per_row=True, tm=bs, rows_per_batch=1)
    y5, s_re, s_im = _s5(u, state_s5_re[0].reshape(bs, -1), state_s5_im[0].reshape(bs, -1), *s5m, tl=1)
    yb, s_ssm, s_conv = _ssd_step(zs, xbc, dt, state_conv[0], state_ssm[0].reshape(bs, M_INNER, M_STATE),
                                  cw, cb, alog, dsk, nw, e64)
    y_sample = _out_stage(xs, y5, yb, gates, mod_s, ws_out, per_row=True, tm=bs, rows_per_batch=1)

    s5_shape = (1, -1, S5_GROUPS, S5_STATE)
    ssm_shape = (1, -1, M_HEADS, M_HEADDIM, M_STATE)
    return (y_prompt.reshape(bp, seq, D_MODEL), y_sample.reshape(bs, 1, D_MODEL),
            p_re.reshape(s5_shape), p_im.reshape(s5_shape), p_ssm.reshape(ssm_shape), p_conv[None],
            s_re.reshape(s5_shape), s_im.reshape(s5_shape), s_ssm.reshape(ssm_shape), s_conv[None])
```

```python
import functools

import jax
import jax.numpy as jnp
from jax import lax
from jax.experimental import pallas as pl
from jax.experimental.pallas import tpu as pltpu

F32 = jnp.float32
BF16 = jnp.bfloat16

D_MODEL = 1024
EPS = 1e-6
S5_GROUP = 16
S5_GROUPS = 64
S5_STATE = 64
M_INNER = 2048
M_HEADDIM = 64
M_HEADS = 32
M_GROUPS = 4
M_HPG = 8
M_STATE = 128
M_CONV = 4
M_CONV_DIM = 3072
M_CHUNK = 128
D_FF = 2816
OFF_Z = D_MODEL
OFF_XBC = OFF_Z + M_INNER
OFF_DT = OFF_XBC + M_CONV_DIM
OFF_GA = OFF_DT + M_HEADS
IN_COLS = OFF_GA + 2 * D_MODEL

LANES = 128
SUBLANES = 8
VMEM_LIMIT = 56 * 1024 * 1024

S5_CH = 256
S5_NS = 1024
S5_CHUNKS = D_MODEL // S5_CH
S5_TILES = S5_NS // LANES
S5_PIECES = S5_CH // LANES
S5_SUB_ROWS = 128
ROW_TILE = 256
INPROJ_SLAB = 512
INPROJ_DEPTH = 3

_NT = (((1,), (1,)), ((), ()))
_TN = (((0,), (0,)), ((), ()))


def _const_spec(shape):
    nd = len(shape)
    return pl.BlockSpec(shape, lambda *_: (0,) * nd, pipeline_mode=pl.Buffered(1))


def _params(sem):
    return pltpu.CompilerParams(dimension_semantics=sem, vmem_limit_bytes=VMEM_LIMIT)


def _sigmoid(x):
    return 0.5 * jnp.tanh(0.5 * x) + 0.5


def _silu(x):
    hx = 0.5 * x
    return hx + hx * jnp.tanh(hx)


def _softplus(x):
    return jnp.maximum(x, 0.0) + jnp.log1p(jnp.exp(-jnp.abs(x)))


def _gelu_tanh(x):
    return 0.5 * x * (1.0 + jnp.tanh(0.7978845608028654 * (x + 0.044715 * (x * x * x))))


def _rms(x, g):
    return x * lax.rsqrt(jnp.mean(x * x, axis=-1, keepdims=True) + EPS) * g


def _dot(a, b):
    return jnp.dot(a, b, preferred_element_type=F32)


def _pieces3(x):
    hi = x.astype(BF16).astype(F32)
    r1 = x - hi
    mid = r1.astype(BF16).astype(F32)
    lo = (r1 - mid).astype(BF16).astype(F32)
    return hi, mid, lo


def _split2(x):
    return jnp.concatenate(_pieces3(x)[:2], axis=-1).astype(BF16)


def _split3(x):
    return jnp.concatenate(_pieces3(x), axis=-1).astype(BF16)


def _ada_kernel(c_ref, w_ref, b_ref, o_ref):
    s = _silu(c_ref[...]).astype(BF16)
    o_ref[...] = _dot(s, w_ref[...].astype(BF16)) + b_ref[...]


def _ada(c, w, b):
    n = c.shape[0]
    tn = 1024
    return pl.pallas_call(
        _ada_kernel,
        grid=(6 * D_MODEL // tn,),
        in_specs=[_const_spec((n, D_MODEL)),
                  pl.BlockSpec((D_MODEL, tn), lambda j: (0, j)),
                  pl.BlockSpec((1, tn), lambda j: (0, j))],
        out_specs=pl.BlockSpec((n, tn), lambda j: (0, j)),
        out_shape=jax.ShapeDtypeStruct((n, 6 * D_MODEL), F32),
        compiler_params=_params(("arbitrary",)),
        name="ada",
    )(c, w, b.reshape(1, -1))


def _s5_disc_kernel(lr_ref, li_ref, ldt_ref, br_ref, bi_ref, ar_ref, ai_ref, bbr_ref, bbi_ref):
    lr = lr_ref[...]
    li = li_ref[...]
    dt = jnp.exp(ldt_ref[...])
    mag = jnp.exp(lr * dt)
    ar = mag * jnp.cos(li * dt)
    ai = mag * jnp.sin(li * dt)
    den = lr * lr + li * li
    nr = ar - 1.0
    f_re = (nr * lr + ai * li) / den
    f_im = (ai * lr - nr * li) / den
    br = br_ref[...]
    bi = bi_ref[...]
    ar_ref[...] = ar
    ai_ref[...] = ai
    bbr_ref[...] = f_re * br - f_im * bi
    bbi_ref[...] = f_re * bi + f_im * br


def _s5_disc(lam_re, lam_im, log_dt, b_re, b_im):
    g, n = lam_re.shape
    w = S5_GROUP * n
    tile = lambda a: jnp.tile(a, (1, S5_GROUP))
    ldt = jnp.broadcast_to(log_dt[:, None], (g, w))
    brt = jnp.swapaxes(b_re, 1, 2).reshape(g, w)
    bit = jnp.swapaxes(b_im, 1, 2).reshape(g, w)
    shp = jax.ShapeDtypeStruct((g, w), F32)
    ar, ai, bbr, bbi = pl.pallas_call(
        _s5_disc_kernel,
        out_shape=(shp, shp, shp, shp),
        name="s5_disc",
    )(tile(lam_re), tile(lam_im), ldt, brt, bit)
    return (ar[:, :n], ai[:, :n], bbr.reshape(g, S5_GROUP, n), bbi.reshape(g, S5_GROUP, n))


def _s5_matrices(abar_re, abar_im, bbar_re, bbar_im, c_re, c_im, d_skip):
    ns = 2 * S5_NS
    b = jnp.concatenate([bbar_re.reshape(D_MODEL, S5_STATE), bbar_im.reshape(D_MODEL, S5_STATE)],
                        axis=1).reshape(S5_CHUNKS, S5_CH, 2 * S5_STATE)

    def by_tile(c):
        return jnp.swapaxes(c, 1, 2).reshape(S5_CHUNKS, S5_TILES, 1, 2, S5_STATE, S5_GROUP)
    c = jnp.concatenate([by_tile(c_re), by_tile(-c_im)], axis=2).reshape(S5_CHUNKS, ns, S5_GROUP)

    by_tile_a = lambda v: v.reshape(S5_CHUNKS, S5_TILES, 1, LANES)
    a = jnp.concatenate([by_tile_a(abar_re), by_tile_a(abar_im)], axis=2).reshape(S5_CHUNKS, 1, ns)
    d = d_skip.reshape(S5_CHUNKS, 1, S5_CH)
    return b, a, c, d


def _s5_state_group(i):
    return 2 * (i // (2 * LANES)) + (i % LANES) // S5_STATE


def _s5_expand_operands(b_ref, c_ref, bm_ref, cm_ref):
    tw = 2 * LANES
    row_g = lax.broadcasted_iota(jnp.int32, (S5_CH, LANES), 0) // S5_GROUP
    lane_half = lax.broadcasted_iota(jnp.int32, (S5_CH, LANES), 1) // S5_STATE
    for part in range(2):
        bp = b_ref[:, part * S5_STATE:(part + 1) * S5_STATE]
        pattern = jnp.concatenate([bp, bp], axis=1)
        for j in range(S5_TILES):
            tile = jnp.where(row_g == 2 * j + lane_half, pattern, 0.0)
            bm_ref[:, j * tw + part * LANES:j * tw + (part + 1) * LANES] = tile.astype(BF16)
    spread = (lax.broadcasted_iota(jnp.int32, (S5_GROUP, S5_CH), 1) % S5_GROUP
              == lax.broadcasted_iota(jnp.int32, (S5_GROUP, S5_CH), 0)).astype(BF16)
    rows = 512
    for r0 in range(0, 2 * S5_NS, rows):
        tiled = _dot(c_ref[r0:r0 + rows, :].astype(BF16), spread)
        own = (_s5_state_group(lax.broadcasted_iota(jnp.int32, (rows, S5_CH), 0) + r0)
               == lax.broadcasted_iota(jnp.int32, (rows, S5_CH), 1) // S5_GROUP)
        cm_ref[r0:r0 + rows, :] = jnp.where(own, tiled, 0.0).astype(BF16)


def _split_w_in_kernel(wt_ref, wm_ref, wdt_ref, wg_ref):
    step = 1024
    for j in range(0, OFF_DT, step):
        wm_ref[:, j:j + step] = wt_ref[j:j + step, :].T.astype(BF16)
    head = lax.broadcasted_iota(jnp.int32, (LANES, LANES), 1) < M_HEADS
    wdt_ref[...] = jnp.where(head, wt_ref[OFF_DT:OFF_DT + LANES, :].T, 0.0).astype(BF16)
    for j in range(0, 2 * D_MODEL, step):
        wg_ref[:, j:j + step] = wt_ref[OFF_GA + j:OFF_GA + j + step, :].T.astype(BF16)


def _split_w_in(w):
    k = w.shape[0]
    outs = (OFF_DT, LANES, 2 * D_MODEL)
    return pl.pallas_call(
        _split_w_in_kernel,
        grid=(k // LANES,),
        in_specs=[pl.BlockSpec((IN_COLS, LANES), lambda i: (0, i))],
        out_specs=[pl.BlockSpec((LANES, n), lambda i: (i, 0)) for n in outs],
        out_shape=[jax.ShapeDtypeStruct((k, n), BF16) for n in outs],
        compiler_params=_params(("parallel",)),
        name="split_w_in",
    )(jnp.swapaxes(w, 0, 1))


def _inproj_kernel(x_ref, sh_ref, sc_ref, g_ref, wm_ref, wdt_ref, wg_ref, dtb_ref, cw_ref, cb_ref,
                   u_ref, z_ref, xbc_ref, dt_ref, gate_ref, *rest, tiles_per_seq):
    accs = rest[-INPROJ_DEPTH:]
    h = _rms(x_ref[...], g_ref[...]) * (1.0 + sc_ref[...]) + sh_ref[...]
    hb = h.astype(BF16)
    rows = hb.shape[0]
    step = INPROJ_SLAB

    if u_ref.shape[0] > rows:
        u_ref[rows:, :] = jnp.zeros((u_ref.shape[0] - rows, D_MODEL), F32)

    if tiles_per_seq is None:
        conv_fn = None
    else:
        tail_ref, carry_ref = rest[:2]

        @pl.when(pl.program_id(0) % tiles_per_seq == 0)
        def _():
            carry_ref[...] = jnp.zeros_like(carry_ref)

        def conv_fn(r, sl):
            ext = jnp.concatenate([carry_ref[:, sl], r], axis=0)
            s1 = pltpu.roll(ext, 1, axis=0)
            near = ext * cw_ref[3:4, sl] + s1 * cw_ref[2:3, sl]
            far = ext * cw_ref[1:2, sl] + s1 * cw_ref[0:1, sl]
            carry_ref[:, sl] = r[rows - SUBLANES:, :]
            tail_ref[:, sl] = r[rows - SUBLANES:, :]
            return (near + pltpu.roll(far, 2, axis=0))[SUBLANES:, :] + cb_ref[:, sl]

    def slabs(o_ref, w_ref, w_off, fn):
        n = o_ref.shape[-1]
        return [(o_ref, slice(j, j + min(step, n - j)), w_ref, w_off + j, fn) for j in range(0, n, step)]

    su = slabs(u_ref, wm_ref, 0, None)
    sz = slabs(z_ref, wm_ref, OFF_Z, None)
    sx = slabs(xbc_ref, wm_ref, OFF_XBC, conv_fn)
    sd = slabs(dt_ref, wdt_ref, 0, lambda r, sl: _softplus(r + dtb_ref[:, sl]))
    sg = slabs(gate_ref, wg_ref, 0, None)
    light = su + sz + sg + sd
    order = []
    while sx or light:
        if sx:
            order.append(sx.pop(0))
        if light:
            order.append(light.pop(0))

    def matmul(s):
        o_ref, sl, w_ref, w_col, _ = order[s]
        accs[s % INPROJ_DEPTH][:, 0:sl.stop - sl.start] = _dot(hb, w_ref[:, w_col:w_col + sl.stop - sl.start])

    for s in range(min(INPROJ_DEPTH - 1, len(order))):
        matmul(s)
    for s, (o_ref, sl, _, _, fn) in enumerate(order):
        if s + INPROJ_DEPTH - 1 < len(order):
            matmul(s + INPROJ_DEPTH - 1)
        r = accs[s % INPROJ_DEPTH][:, 0:sl.stop - sl.start]
        o_ref[0:rows, sl] = r if fn is None else fn(r, sl)


def _mod_specs(per_row, tm, rows_per_mod, cols):
    if per_row:
        return [pl.BlockSpec((tm, D_MODEL), functools.partial(lambda i, c: (i, c), c=c)) for c in cols]
    return [pl.BlockSpec((None, 1, D_MODEL), functools.partial(lambda i, c: (i // rows_per_mod, 0, c), c=c))
            for c in cols]


def _s5_layout(per_row, t, tm, rows_per_batch, width):
    if per_row:
        return (1, t, width), pl.BlockSpec((None, tm, width), lambda i: (0, i, 0))
    nblk, pitch = rows_per_batch // tm, _s5_pitch(tm)
    return ((nblk, (t // rows_per_batch) * pitch, width),
            pl.BlockSpec((None, pitch, width), lambda i: (i % nblk, i // nblk, 0)))


def _in_proj(x, mod, g, ws, dtb, cw, cb, *, per_row, tm, rows_per_batch):
    t = x.shape[0]
    row = lambda n: pl.BlockSpec((tm, n), lambda i: (i, 0))
    outs = (M_INNER, M_CONV_DIM, LANES, 2 * D_MODEL)
    u_shape, u_spec = _s5_layout(per_row, t, tm, rows_per_batch, D_MODEL)
    out_specs = [u_spec] + [row(n) for n in outs]
    out_shape = [jax.ShapeDtypeStruct(u_shape, F32)] + [jax.ShapeDtypeStruct((t, n), F32) for n in outs]
    scratch = []
    tiles_per_seq = None
    if not per_row:
        tiles_per_seq = rows_per_batch // tm
        out_specs.append(pl.BlockSpec((None, SUBLANES, M_CONV_DIM), lambda i: (i // tiles_per_seq, 0, 0)))
        out_shape.append(jax.ShapeDtypeStruct((t // rows_per_batch, SUBLANES, M_CONV_DIM), F32))
        scratch.append(pltpu.VMEM((SUBLANES, M_CONV_DIM), F32))
    scratch += [pltpu.VMEM((tm, INPROJ_SLAB), F32)] * INPROJ_DEPTH
    consts = (g,) + tuple(ws) + (dtb, cw, cb)
    return pl.pallas_call(
        functools.partial(_inproj_kernel, tiles_per_seq=tiles_per_seq),
        grid=(t // tm,),
        in_specs=[row(D_MODEL)] + _mod_specs(per_row, tm, rows_per_batch // tm if not per_row else 1, (0, 1))
                 + [_const_spec(w.shape) for w in consts],
        out_specs=out_specs,
        out_shape=out_shape,
        scratch_shapes=scratch,
        compiler_params=_params(("arbitrary",)),
        name="in_proj",
    )(x, mod, mod, *consts)


def _s5_pitch(tl):
    return tl if tl == 1 else tl + SUBLANES


def _s5_kernel(*refs, nb, tl):
    nu = S5_PIECES
    u_refs = refs[:nu]
    h0r_ref, h0i_ref, b_ref, a_ref, c_ref, d_ref = refs[nu:nu + 6]
    y_refs = refs[nu + 6:2 * nu + 6]
    hr_ref, hi_ref, hst_ref, bu0_ref, xs0_ref, bu1_ref, xs1_ref, bm_ref, cm_ref = refs[2 * nu + 6:]
    bufs = ((bu0_ref, xs0_ref), (bu1_ref, xs1_ref))
    tb = pl.program_id(1)
    tsub = S5_SUB_ROWS // nb
    tw = 2 * LANES
    pitch = _s5_pitch(tl)

    def step_rows(t):
        return pl.ds(t, nb, stride=pitch) if pitch > 1 else pl.ds(0, nb)

    @pl.when(tb == 0)
    def _():
        _s5_expand_operands(b_ref, c_ref, bm_ref, cm_ref)
        for j in range(S5_TILES):
            hst_ref[:, j * tw:j * tw + LANES] = h0r_ref[:, j * LANES:(j + 1) * LANES]
            hst_ref[:, j * tw + LANES:(j + 1) * tw] = h0i_ref[:, j * LANES:(j + 1) * LANES]

    ar = [jnp.broadcast_to(a_ref[:, j * tw:j * tw + LANES], (nb, LANES)) for j in range(S5_TILES)]
    ai = [jnp.broadcast_to(a_ref[:, j * tw + LANES:(j + 1) * tw], (nb, LANES)) for j in range(S5_TILES)]
    dsk = d_ref[...]

    def project_in(t0, bu_ref, xs_ref):
        xs = jnp.concatenate([jnp.concatenate([r[step_rows(t0 + i), :] for r in u_refs], axis=1)
                              for i in range(tsub)], axis=0)
        xs_ref[...] = xs
        lhs = xs.astype(BF16)
        for j in range(S5_TILES):
            bu_ref[:, j * tw:(j + 1) * tw] = _dot(lhs, bm_ref[:, j * tw:(j + 1) * tw])

    def scan_project_out(t0, bu_ref, xs_ref, carry):
        new, hs = [], []
        for j in range(S5_TILES):
            hr, hi = carry[2 * j], carry[2 * j + 1]
            rows = []
            for i in range(tsub):
                br = bu_ref[i * nb:(i + 1) * nb, j * tw:j * tw + LANES]
                bi = bu_ref[i * nb:(i + 1) * nb, j * tw + LANES:(j + 1) * tw]
                hr, hi = ar[j] * hr - ai[j] * hi + br, ar[j] * hi + ai[j] * hr + bi
                rows.append(jnp.concatenate([hr, hi], axis=1))
            new += [hr, hi]
            hs.append(jnp.concatenate(rows, axis=0).astype(BF16))
        y = _dot(jnp.concatenate(hs, axis=1), cm_ref[...]) + dsk * xs_ref[...]
        for i in range(tsub):
            for k, y_ref in enumerate(y_refs):
                y_ref[step_rows(t0 + i), :] = y[i * nb:(i + 1) * nb, k * LANES:(k + 1) * LANES]
        return tuple(new)

    h_in = []
    for j in range(S5_TILES):
        h_in += [hst_ref[:, j * tw:j * tw + LANES], hst_ref[:, j * tw + LANES:(j + 1) * tw]]
    nsub = tl // tsub
    (bu0, xs0), (bu1, xs1) = bufs
    project_in(0, bu0, xs0)
    if nsub == 1:
        h_out = scan_project_out(0, bu0, xs0, tuple(h_in))
    else:
        def pair(p, carry):
            t0 = 2 * p * tsub
            project_in(t0 + tsub, bu1, xs1)
            carry = scan_project_out(t0, bu0, xs0, carry)
            project_in(jnp.minimum(t0 + 2 * tsub, tl - tsub), bu0, xs0)
            return scan_project_out(t0 + tsub, bu1, xs1, carry)
        h_out = lax.fori_loop(0, nsub // 2, pair, tuple(h_in))
    for j in range(S5_TILES):
        hst_ref[:, j * tw:j * tw + LANES] = h_out[2 * j]
        hst_ref[:, j * tw + LANES:(j + 1) * tw] = h_out[2 * j + 1]

    if pitch > tl:
        for y_ref in y_refs:
            for b in range(nb):
                y_ref[b * pitch + tl:(b + 1) * pitch, :] = jnp.zeros((pitch - tl, LANES), F32)

    @pl.when(tb == pl.num_programs(1) - 1)
    def _():
        for j in range(S5_TILES):
            hr_ref[:, j * LANES:(j + 1) * LANES] = h_out[2 * j]
            hi_ref[:, j * LANES:(j + 1) * LANES] = h_out[2 * j + 1]


def _s5(u, h0r, h0i, b, a, c, d, *, tl):
    nblk, rows, _ = u.shape
    pitch = _s5_pitch(tl)
    nb = rows // pitch
    nsub = tl * nb // S5_SUB_ROWS
    assert S5_SUB_ROWS % nb == 0 and tl * nb % S5_SUB_ROWS == 0 and (nsub == 1 or nsub % 2 == 0)
    chunk3 = lambda s: pl.BlockSpec((None,) + s, lambda c, t: (c, 0, 0))
    st_spec = pl.BlockSpec((nb, S5_NS), lambda c, t: (0, c))
    nu = S5_PIECES
    u_specs = [pl.BlockSpec((None, rows, LANES), functools.partial(lambda c, t, k: (t, 0, nu * c + k), k=k))
               for k in range(nu)]
    y_spec = pl.BlockSpec((None, rows, LANES), lambda c, t: (t, 0, c))
    y_shape = jax.ShapeDtypeStruct((nblk, rows, D_MODEL // nu), F32)
    res = pl.pallas_call(
        functools.partial(_s5_kernel, nb=nb, tl=tl),
        grid=(S5_CHUNKS, nblk),
        in_specs=u_specs + [st_spec, st_spec, chunk3((S5_CH, 2 * S5_STATE)), chunk3((1, 2 * S5_NS)),
                            chunk3((2 * S5_NS, S5_GROUP)), chunk3((1, S5_CH))],
        out_specs=[y_spec] * nu + [st_spec, st_spec],
        out_shape=[y_shape] * nu + [jax.ShapeDtypeStruct(h0r.shape, F32), jax.ShapeDtypeStruct(h0i.shape, F32)],
        scratch_shapes=[pltpu.VMEM((nb, 2 * S5_NS), F32)]
                       + [pltpu.VMEM((S5_SUB_ROWS, 2 * S5_NS), F32), pltpu.VMEM((S5_SUB_ROWS, S5_CH), F32)] * 2
                       + [pltpu.VMEM((S5_CH, 2 * S5_NS), BF16), pltpu.VMEM((2 * S5_NS, S5_CH), BF16)],
        compiler_params=_params(("parallel", "arbitrary")),
        name="s5",
    )(*([u] * nu), h0r, h0i, b, a, c, d)
    return res[:nu], res[nu], res[nu + 1]


def _expand_mat():
    r = lax.broadcasted_iota(jnp.int32, (3 * LANES, M_HEADS * M_HEADDIM), 0) % LANES
    c = lax.broadcasted_iota(jnp.int32, (3 * LANES, M_HEADS * M_HEADDIM), 1) // M_HEADDIM
    return (r == c).astype(BF16)


LOG2E = 1.4426950408889634
SSD_CHUNKS_PER_STEP = 2


def _ssd_kernel(z_ref, conv_ref, dt_ref, alog_ref, dsk_ref, nw_ref, e64_ref,
                y_ref, hout_ref,
                ht_ref, act_ref, xs_ref, xdte_ref, eacs_ref, yacc_ref):
    c = pl.program_id(1)

    @pl.when(c == 0)
    def _():
        ht_ref[...] = jnp.zeros_like(ht_ref)

    for k in range(SSD_CHUNKS_PER_STEP):
        rows = pl.ds(k * M_CHUNK, M_CHUNK)
        _ssd_chunk(z_ref.at[rows], conv_ref.at[rows], dt_ref.at[rows], alog_ref, dsk_ref, nw_ref, e64_ref,
                   y_ref.at[rows], ht_ref, act_ref.at[k], xs_ref.at[k], xdte_ref.at[k], eacs_ref.at[k], yacc_ref.at[k])

    @pl.when(c == pl.num_programs(1) - 1)
    def _():
        for j in range(0, M_INNER, LANES):
            hout_ref[j:j + LANES, :] = ht_ref[:, j:j + LANES].T


def _ssd_chunk(z_ref, conv_ref, dt_ref, alog_ref, dsk_ref, nw_ref, e64_ref,
               y_ref, ht_ref, act_ref, xs_ref, xdte_ref, eacs_ref, yacc_ref):
    q = M_CHUNK
    cs = 512
    for j in range(0, M_CONV_DIM, cs):
        act_ref[:, j:j + cs] = _silu(conv_ref[:, j:j + cs])

    dt = dt_ref[...]
    a2 = dt * (-jnp.exp(alog_ref[...]) * LOG2E)
    ri = lax.broadcasted_iota(jnp.int32, (q, q), 0)
    ci = lax.broadcasted_iota(jnp.int32, (q, q), 1)
    causal = ri >= ci
    acs = jnp.dot(causal.astype(F32), a2, preferred_element_type=F32, precision=lax.Precision.HIGHEST)
    src_t = (acs - jnp.log2(dt)).T
    eacs2 = _split2(jnp.exp2(acs))
    dte2 = _split2(dt * jnp.exp2(acs[q - 1:q, :] - acs))
    for j in range(0, M_INNER, cs):
        sl = slice(j, j + cs)
        e = e64_ref[0:2 * LANES, sl]
        xs = act_ref[:, sl]
        xs_ref[:, sl] = xs.astype(BF16)
        xdte_ref[:, sl] = (xs * _dot(dte2, e)).astype(BF16)
        eacs_ref[:, sl] = _dot(eacs2, e)

    hp = M_HPG * M_HEADDIM
    lane = lax.broadcasted_iota(jnp.int32, (q, LANES), 1)
    first_head = lane < M_HEADDIM
    for g in range(M_GROUPS):
        gs = slice(g * hp, (g + 1) * hp)
        bm_t = act_ref[:, M_INNER + g * M_STATE:M_INNER + (g + 1) * M_STATE].T.astype(BF16)
        cm = act_ref[:, M_INNER + (M_GROUPS + g) * M_STATE:M_INNER + (M_GROUPS + g + 1) * M_STATE].astype(BF16)
        cbm = _dot(cm, bm_t)
        y_off = _dot(cm, ht_ref[:, gs].astype(BF16))
        st_t = _dot(bm_t, xdte_ref[:, gs])
        for pr in range(M_HPG // 2):
            h0 = g * M_HPG + 2 * pr
            ps = slice(h0 * M_HEADDIM, (h0 + 2) * M_HEADDIM)
            ms = []
            for h in (h0, h0 + 1):
                d2 = jnp.broadcast_to(acs[:, h:h + 1], (q, q)) - src_t[h:h + 1, :]
                ms.append((cbm * jnp.where(causal, jnp.exp2(d2), 0.0)).astype(BF16))
            xp = xs_ref[:, ps]
            zero = jnp.zeros_like(xp)
            rhs = jnp.concatenate([jnp.where(first_head, xp, zero), jnp.where(first_head, zero, xp)], axis=0)
            yd = _dot(jnp.concatenate(ms, axis=1), rhs)
            yacc_ref[:, ps] = yd + y_off[:, ps.start - gs.start:ps.stop - gs.start] * eacs_ref[:, ps] \
                + dsk_ref[:, ps] * act_ref[:, ps]
        ht_ref[:, gs] = eacs_ref[q - 1:q, gs] * ht_ref[:, gs] + st_t

    ss = jnp.zeros((q, 1), F32)
    for j in range(0, M_INNER, cs):
        sl = slice(j, j + cs)
        gt = yacc_ref[:, sl] * _silu(z_ref[:, sl])
        yacc_ref[:, sl] = gt
        ss = ss + jnp.sum(gt * gt, axis=-1, keepdims=True)
    inv = lax.rsqrt(ss * (1.0 / M_INNER) + EPS)
    for j in range(0, M_INNER, cs):
        sl = slice(j, j + cs)
        y_ref[:, sl] = yacc_ref[:, sl] * inv * nw_ref[:, sl]


def _ssd(z, conv, dt, alog, dsk, nw, e64, *, nbatch, seq):
    q = M_CHUNK
    k = SSD_CHUNKS_PER_STEP
    assert seq % (k * q) == 0
    nc = seq // (k * q)
    row = lambda n: pl.BlockSpec((k * q, n), lambda b, c: (b * nc + c, 0))
    return pl.pallas_call(
        _ssd_kernel,
        grid=(nbatch, nc),
        in_specs=[row(M_INNER), row(M_CONV_DIM), row(LANES), _const_spec(alog.shape),
                  _const_spec(dsk.shape), _const_spec(nw.shape), _const_spec(e64.shape)],
        out_specs=[row(M_INNER), pl.BlockSpec((None, M_INNER, M_STATE), lambda b, c: (b, 0, 0))],
        out_shape=[jax.ShapeDtypeStruct((nbatch * seq, M_INNER), F32),
                   jax.ShapeDtypeStruct((nbatch, M_INNER, M_STATE), F32)],
        scratch_shapes=[pltpu.VMEM((M_STATE, M_INNER), F32), pltpu.VMEM((k, q, M_CONV_DIM), F32),
                        pltpu.VMEM((k, q, M_INNER), BF16),
                        pltpu.VMEM((k, q, M_INNER), BF16), pltpu.VMEM((k, q, M_INNER), F32),
                        pltpu.VMEM((k, q, M_INNER), F32)],
        compiler_params=_params(("parallel", "arbitrary")),
        name="ssd",
    )(z, conv, dt, alog, dsk, nw, e64)


SSD_STEP_BB = 8


def _ssd_step_kernel(z_ref, xbc_ref, dt_ref, cbuf_ref, h0_ref, cw_ref, cb_ref, alog_ref, dsk_ref, nw_ref,
                     e64_ref, y_ref, hout_ref, convout_ref):
    bb = SSD_STEP_BB
    x_new = xbc_ref[...]
    conv = cb_ref[...] + x_new * cw_ref[M_CONV - 1:M_CONV, :]
    for k in range(M_CONV - 1):
        conv = conv + cbuf_ref[k] * cw_ref[k:k + 1, :]
    act = _silu(conv)
    for k in range(M_CONV - 2):
        convout_ref[k] = cbuf_ref[k + 1]
    convout_ref[M_CONV - 2] = x_new

    xs = act[:, :M_INNER]
    bmat = act[:, M_INNER:M_INNER + M_GROUPS * M_STATE]
    cmat = act[:, M_INNER + M_GROUPS * M_STATE:]
    dt = dt_ref[...]
    da = jnp.exp(dt * (-jnp.exp(alog_ref[...])))
    dt_e = _dot(_split3(dt), e64_ref[...])
    da_e = _dot(_split3(da), e64_ref[...])
    xdt = xs * dt_e

    hp = M_HPG * M_HEADDIM
    lane_group = lax.broadcasted_iota(jnp.int32, (M_GROUPS, M_INNER), 1) // hp
    row_group = lax.broadcasted_iota(jnp.int32, (M_GROUPS, M_INNER), 0)
    gmask = lane_group == row_group
    ones_rows = lax.broadcasted_iota(jnp.int32, (SUBLANES, M_STATE), 0) >= M_GROUPS

    cbx = []
    for g in range(M_GROUPS):
        cbg = jnp.sum(cmat[:, g * M_STATE:(g + 1) * M_STATE] * bmat[:, g * M_STATE:(g + 1) * M_STATE],
                      axis=-1, keepdims=True)
        cbx.append(jnp.broadcast_to(cbg, (bb, hp)))
    y = jnp.concatenate(cbx, axis=1) * xdt + dsk_ref[...] * xs

    y_off_rows = []
    for b in range(bb):
        xsel = jnp.where(gmask, jnp.broadcast_to(xdt[b:b + 1, :], (M_GROUPS, M_INNER)), 0.0)
        lhs = jnp.concatenate([xsel, *_pieces3(da_e[b:b + 1, :]), jnp.zeros((1, M_INNER), F32)],
                              axis=0).astype(BF16)
        bm4 = jnp.concatenate([bmat[b:b + 1, g * M_STATE:(g + 1) * M_STATE] for g in range(M_GROUPS)], axis=0)
        cm4 = jnp.concatenate([cmat[b:b + 1, g * M_STATE:(g + 1) * M_STATE] for g in range(M_GROUPS)], axis=0)
        zeros4 = jnp.zeros((SUBLANES - M_GROUPS, M_STATE), F32)
        rhs = jnp.concatenate([jnp.concatenate([bm4, zeros4], axis=0),
                               jnp.where(ones_rows, 1.0, 0.0)], axis=1).astype(BF16)
        res = lax.dot_general(lhs, rhs, _TN, preferred_element_type=F32)
        h0 = h0_ref[b]
        hout_ref[b] = res[:, M_STATE:] * h0 + res[:, :M_STATE]
        cm8 = jnp.concatenate([cm4, zeros4], axis=0).astype(BF16)
        yo = lax.dot_general(cm8, h0.astype(BF16), _NT, preferred_element_type=F32)
        y_off_rows.append(jnp.sum(jnp.where(gmask, yo[:M_GROUPS, :], 0.0), axis=0, keepdims=True))
    y = y + da_e * jnp.concatenate(y_off_rows, axis=0)

    y_ref[...] = _rms(y * _silu(z_ref[...]), nw_ref[...])


def _ssd_step(zs, xbc, dt, cbuf, h0, cw, cb, alog, dsk, nw, e64):
    n = zs.shape[0]
    bb = SSD_STEP_BB
    row = lambda w: pl.BlockSpec((bb, w), lambda i: (i, 0))
    blk3 = lambda s: pl.BlockSpec((bb,) + s, lambda i: (i, 0, 0))
    conv_spec = pl.BlockSpec((M_CONV - 1, bb, M_CONV_DIM), lambda i: (0, i, 0))
    return pl.pallas_call(
        _ssd_step_kernel,
        grid=(n // bb,),
        in_specs=[row(M_INNER), row(M_CONV_DIM), row(LANES), conv_spec, blk3((M_INNER, M_STATE)),
                  _const_spec(cw.shape), _const_spec(cb.shape), _const_spec(alog.shape),
                  _const_spec(dsk.shape), _const_spec(nw.shape), _const_spec(e64.shape)],
        out_specs=[row(M_INNER), blk3((M_INNER, M_STATE)), conv_spec],
        out_shape=[jax.ShapeDtypeStruct((n, M_INNER), F32),
                   jax.ShapeDtypeStruct((n, M_INNER, M_STATE), F32),
                   jax.ShapeDtypeStruct((M_CONV - 1, n, M_CONV_DIM), F32)],
        compiler_params=_params(("parallel",)),
        name="ssd_step",
    )(zs, xbc, dt, cbuf, h0, cw, cb, alog, dsk, nw, e64)


MXU_WIDTH = 256
FFN_CHUNKS = (0, 6 * MXU_WIDTH, D_FF)


def _out_kernel(x_ref, *refs):
    y5_refs = refs[:S5_PIECES]
    (yb_ref, gate_ref, gtm_ref, shf_ref, scf_ref, gtf_ref, wglu_ref, bglu_ref, wb5_ref, wbs_ref, wout_ref,
     npm_ref, npf_ref, npo_ref, wfi_ref, wfo_ref, o_ref) = refs[S5_PIECES:]
    rows = x_ref.shape[0]
    y5 = jnp.concatenate([y5_refs[k][0:rows, c * LANES:(c + 1) * LANES]
                          for c in range(S5_CHUNKS) for k in range(S5_PIECES)], axis=1)
    ya = _gelu_tanh(y5)
    glu = ya * _sigmoid(_dot(ya.astype(BF16), wglu_ref[...]) + bglu_ref[...])
    merged = (_sigmoid(gate_ref[:, :D_MODEL]) * _dot(glu.astype(BF16), wb5_ref[...])
              + _sigmoid(gate_ref[:, D_MODEL:]) * _dot(yb_ref[...].astype(BF16), wbs_ref[...]))
    mix = _dot(merged.astype(BF16), wout_ref[...])
    x1 = x_ref[...] + gtm_ref[...] * _rms(mix, npm_ref[...])

    hb = (_rms(x1, npf_ref[...]) * (1.0 + scf_ref[...]) + shf_ref[...]).astype(BF16)
    f = jnp.zeros_like(x1)
    for j, e in zip(FFN_CHUNKS[:-1], FFN_CHUNKS[1:]):
        gg = _dot(hb, wfi_ref[:, j:e])
        uu = _dot(hb, wfi_ref[:, D_FF + j:D_FF + e])
        f = f + _dot((_silu(gg) * uu).astype(BF16), wfo_ref[j:e, :])
    o_ref[...] = x1 + gtf_ref[...] * _rms(f, npo_ref[...])


def _out_stage(x, y5, yb, gates, mod, ws, *, per_row, tm, rows_per_batch):
    t = x.shape[0]
    row = lambda n: pl.BlockSpec((tm, n), lambda i: (i, 0))
    mods = _mod_specs(per_row, tm, rows_per_batch // tm if not per_row else 1, (2, 3, 4, 5))
    _, y5_spec = _s5_layout(per_row, t, tm, rows_per_batch, D_MODEL // S5_PIECES)
    return pl.pallas_call(
        _out_kernel,
        grid=(t // tm,),
        in_specs=[row(D_MODEL)] + [y5_spec] * S5_PIECES + [row(M_INNER), row(2 * D_MODEL)] + mods
                 + [_const_spec(w.shape) for w in ws],
        out_specs=row(D_MODEL),
        out_shape=jax.ShapeDtypeStruct((t, D_MODEL), F32),
        compiler_params=_params(("parallel",)),
        name="out_stage",
    )(x, *y5, yb, gates, mod, mod, mod, mod, *ws)


def _pad_lanes(v):
    return jnp.pad(v.reshape(1, -1), ((0, 0), (0, LANES - v.shape[-1])))


def kernel(x_prompt, x_sample, state_s5_re, state_s5_im, state_ssm, state_conv, c_prompt, c_sample, w_ada, b_ada, norm_pre_mix, norm_post_mix, norm_pre_ffn, norm_post_ffn, w_in, s5_lam_re, s5_lam_im, s5_log_dt, s5_b_re, s5_b_im, s5_c_re, s5_c_im, s5_d, s5_w_glu, s5_b_glu, m_conv_w, m_conv_b, m_dt_bias, m_a_log, m_d, m_norm, w_branch_s5, w_branch_ssd, w_out, w_ffn_in, w_ffn_out):
    bp, seq, _ = x_prompt.shape
    bs = x_sample.shape[0]
    assert x_sample.shape[1] == 1 and w_ada.shape[0] == 1 and seq % M_CHUNK == 0
    row1 = lambda v: v.reshape(1, -1)

    ws_in = _split_w_in(w_in[0])
    ws_out = (s5_w_glu[0].astype(BF16), row1(s5_b_glu[0]), w_branch_s5[0].astype(BF16),
              w_branch_ssd[0].astype(BF16), w_out[0].astype(BF16), row1(norm_post_mix[0]),
              row1(norm_pre_ffn[0]), row1(norm_post_ffn[0]), w_ffn_in[0].astype(BF16), w_ffn_out[0].astype(BF16))
    cw, cb = m_conv_w[0], row1(m_conv_b[0])
    dtb, alog = _pad_lanes(m_dt_bias[0]), _pad_lanes(m_a_log[0])
    dsk = row1(jnp.repeat(m_d[0], M_HEADDIM))
    nw = row1(m_norm[0])
    e64 = _expand_mat()

    mod = _ada(jnp.concatenate([c_prompt, c_sample], axis=0), w_ada[0], b_ada[0])
    mod_p = mod[:bp].reshape(bp, 1, 6 * D_MODEL)
    mod_s = mod[bp:]

    ar, ai, bbr, bbi = _s5_disc(s5_lam_re[0], s5_lam_im[0], s5_log_dt[0], s5_b_re[0], s5_b_im[0])
    s5m = _s5_matrices(ar, ai, bbr, bbi, s5_c_re[0], s5_c_im[0], s5_d[0])
    g_pre = row1(norm_pre_mix[0])

    xp = x_prompt.reshape(bp * seq, D_MODEL)
    tm = min(ROW_TILE, seq)
    u, zs, act, dt, gates, tail = _in_proj(xp, mod_p, g_pre, ws_in, dtb, cw, cb,
                                           per_row=False, tm=tm, rows_per_batch=seq)
    p_conv = tail[:, SUBLANES - (M_CONV - 1):, :]
    zeros_s5 = jnp.zeros((bp, S5_GROUPS * S5_STATE), F32)
    y5, p_re, p_im = _s5(u, zeros_s5, zeros_s5, *s5m, tl=tm)
    yb, p_ssm = _ssd(zs, act, dt, alog, dsk, nw, e64, nbatch=bp, seq=seq)
    y_prompt = _out_stage(xp, y5, yb, gates, mod_p, ws_out, per_row=False, tm=tm, rows_per_batch=seq)

    xs = x_sample.reshape(bs, D_MODEL)
    u, zs, xbc, dt, gates = _in_proj(xs, mod_s, g_pre, ws_in, dtb, cw, cb, per_row=True, tm=bs, rows_per_batch=1)
    y5, s_re, s_im = _s5(u, state_s5_re[0].reshape(bs, -1), state_s5_im[0].reshape(bs, -1), *s5m, tl=1)
    yb, s_ssm, s_conv = _ssd_step(zs, xbc, dt, jnp.swapaxes(state_conv[0], 0, 1),
                                  state_ssm[0].reshape(bs, M_INNER, M_STATE),
                                  cw, cb, alog, dsk, nw, e64)
    y_sample = _out_stage(xs, y5, yb, gates, mod_s, ws_out, per_row=True, tm=bs, rows_per_batch=1)

    s5_shape = (1, -1, S5_GROUPS, S5_STATE)
    ssm_shape = (1, -1, M_HEADS, M_HEADDIM, M_STATE)
    return (y_prompt.reshape(bp, seq, D_MODEL), y_sample.reshape(bs, 1, D_MODEL),
            p_re.reshape(s5_shape), p_im.reshape(s5_shape), p_ssm.reshape(ssm_shape), p_conv[None],
            s_re.reshape(s5_shape), s_im.reshape(s5_shape), s_ssm.reshape(ssm_shape), jnp.swapaxes(s_conv, 0, 1)[None])
```

```python
import functools

import jax
import jax.numpy as jnp
from jax import lax
from jax.experimental import pallas as pl
from jax.experimental.pallas import tpu as pltpu

F32 = jnp.float32
BF16 = jnp.bfloat16

D_MODEL = 1024
EPS = 1e-6
S5_GROUP = 16
S5_GROUPS = 64
S5_STATE = 64
M_INNER = 2048
M_HEADDIM = 64
M_HEADS = 32
M_GROUPS = 4
M_HPG = 8
M_STATE = 128
M_CONV = 4
M_CONV_DIM = 3072
M_CHUNK = 128
D_FF = 2816
OFF_Z = D_MODEL
OFF_XBC = OFF_Z + M_INNER
OFF_DT = OFF_XBC + M_CONV_DIM
OFF_GA = OFF_DT + M_HEADS
IN_COLS = OFF_GA + 2 * D_MODEL

LANES = 128
SUBLANES = 8
VMEM_LIMIT = 56 * 1024 * 1024

S5_CH = 256
S5_NS = 1024
S5_CHUNKS = D_MODEL // S5_CH
S5_TILES = S5_NS // LANES
S5_PIECES = S5_CH // LANES
S5_SUB_ROWS = 128
ROW_TILE = 256
INPROJ_SLAB = 512
INPROJ_DEPTH = 3

_NT = (((1,), (1,)), ((), ()))
_TN = (((0,), (0,)), ((), ()))


def _const_spec(shape):
    nd = len(shape)
    return pl.BlockSpec(shape, lambda *_: (0,) * nd, pipeline_mode=pl.Buffered(1))


def _params(sem):
    return pltpu.CompilerParams(dimension_semantics=sem, vmem_limit_bytes=VMEM_LIMIT)


def _sigmoid(x):
    return 0.5 * jnp.tanh(0.5 * x) + 0.5


def _silu(x):
    hx = 0.5 * x
    return hx + hx * jnp.tanh(hx)


def _softplus(x):
    return jnp.maximum(x, 0.0) + jnp.log1p(jnp.exp(-jnp.abs(x)))


def _gelu_tanh(x):
    return 0.5 * x * (1.0 + jnp.tanh(0.7978845608028654 * (x + 0.044715 * (x * x * x))))


def _rms(x, g):
    return x * lax.rsqrt(jnp.mean(x * x, axis=-1, keepdims=True) + EPS) * g


def _dot(a, b):
    return jnp.dot(a, b, preferred_element_type=F32)


def _pieces3(x):
    hi = x.astype(BF16).astype(F32)
    r1 = x - hi
    mid = r1.astype(BF16).astype(F32)
    lo = (r1 - mid).astype(BF16).astype(F32)
    return hi, mid, lo


def _ordered_after(x, dep):
    zero = jnp.where(dep[:, 0:LANES] > jnp.inf, 1.0, 0.0)
    return x + jnp.concatenate([zero] * (x.shape[1] // LANES), axis=1)


def _split2(x):
    return jnp.concatenate(_pieces3(x)[:2], axis=-1).astype(BF16)


def _split3(x):
    return jnp.concatenate(_pieces3(x), axis=-1).astype(BF16)


def _ada_kernel(c_ref, w_ref, b_ref, o_ref):
    s = _silu(c_ref[...]).astype(BF16)
    o_ref[...] = _dot(s, w_ref[...].astype(BF16)) + b_ref[...]


def _ada(c, w, b):
    n = c.shape[0]
    tn = 1024
    return pl.pallas_call(
        _ada_kernel,
        grid=(6 * D_MODEL // tn,),
        in_specs=[_const_spec((n, D_MODEL)),
                  pl.BlockSpec((D_MODEL, tn), lambda j: (0, j)),
                  pl.BlockSpec((1, tn), lambda j: (0, j))],
        out_specs=pl.BlockSpec((n, tn), lambda j: (0, j)),
        out_shape=jax.ShapeDtypeStruct((n, 6 * D_MODEL), F32),
        compiler_params=_params(("arbitrary",)),
        name="ada",
    )(c, w, b.reshape(1, -1))


def _s5_disc_kernel(lr_ref, li_ref, ldt_ref, br_ref, bi_ref, ar_ref, ai_ref, bbr_ref, bbi_ref):
    lr = lr_ref[...]
    li = li_ref[...]
    dt = jnp.exp(ldt_ref[...])
    mag = jnp.exp(lr * dt)
    ar = mag * jnp.cos(li * dt)
    ai = mag * jnp.sin(li * dt)
    den = lr * lr + li * li
    nr = ar - 1.0
    f_re = (nr * lr + ai * li) / den
    f_im = (ai * lr - nr * li) / den
    br = br_ref[...]
    bi = bi_ref[...]
    ar_ref[...] = ar
    ai_ref[...] = ai
    bbr_ref[...] = f_re * br - f_im * bi
    bbi_ref[...] = f_re * bi + f_im * br


def _s5_disc(lam_re, lam_im, log_dt, b_re, b_im):
    g, n = lam_re.shape
    w = S5_GROUP * n
    tile = lambda a: jnp.tile(a, (1, S5_GROUP))
    ldt = jnp.broadcast_to(log_dt[:, None], (g, w))
    brt = jnp.swapaxes(b_re, 1, 2).reshape(g, w)
    bit = jnp.swapaxes(b_im, 1, 2).reshape(g, w)
    shp = jax.ShapeDtypeStruct((g, w), F32)
    ar, ai, bbr, bbi = pl.pallas_call(
        _s5_disc_kernel,
        out_shape=(shp, shp, shp, shp),
        name="s5_disc",
    )(tile(lam_re), tile(lam_im), ldt, brt, bit)
    return (ar[:, :n], ai[:, :n], bbr.reshape(g, S5_GROUP, n), bbi.reshape(g, S5_GROUP, n))


def _s5_matrices(abar_re, abar_im, bbar_re, bbar_im, c_re, c_im, d_skip):
    ns = 2 * S5_NS
    b = jnp.concatenate([bbar_re.reshape(D_MODEL, S5_STATE), bbar_im.reshape(D_MODEL, S5_STATE)],
                        axis=1).reshape(S5_CHUNKS, S5_CH, 2 * S5_STATE)

    def by_tile(c):
        return jnp.swapaxes(c, 1, 2).reshape(S5_CHUNKS, S5_TILES, 1, 2, S5_STATE, S5_GROUP)
    c = jnp.concatenate([by_tile(c_re), by_tile(-c_im)], axis=2).reshape(S5_CHUNKS, ns, S5_GROUP)

    by_tile_a = lambda v: v.reshape(S5_CHUNKS, S5_TILES, 1, LANES)
    a = jnp.concatenate([by_tile_a(abar_re), by_tile_a(abar_im)], axis=2).reshape(S5_CHUNKS, 1, ns)
    d = d_skip.reshape(S5_CHUNKS, 1, S5_CH)
    return b, a, c, d


def _s5_state_group(i):
    return 2 * (i // (2 * LANES)) + (i % LANES) // S5_STATE


def _s5_expand_operands(b_ref, c_ref, bm_ref, cm_ref):
    tw = 2 * LANES
    row_g = lax.broadcasted_iota(jnp.int32, (S5_CH, LANES), 0) // S5_GROUP
    lane_half = lax.broadcasted_iota(jnp.int32, (S5_CH, LANES), 1) // S5_STATE
    for part in range(2):
        bp = b_ref[:, part * S5_STATE:(part + 1) * S5_STATE]
        pattern = jnp.concatenate([bp, bp], axis=1)
        for j in range(S5_TILES):
            tile = jnp.where(row_g == 2 * j + lane_half, pattern, 0.0)
            bm_ref[:, j * tw + part * LANES:j * tw + (part + 1) * LANES] = tile.astype(BF16)
    spread = (lax.broadcasted_iota(jnp.int32, (S5_GROUP, S5_CH), 1) % S5_GROUP
              == lax.broadcasted_iota(jnp.int32, (S5_GROUP, S5_CH), 0)).astype(BF16)
    rows = 512
    for r0 in range(0, 2 * S5_NS, rows):
        tiled = _dot(c_ref[r0:r0 + rows, :].astype(BF16), spread)
        own = (_s5_state_group(lax.broadcasted_iota(jnp.int32, (rows, S5_CH), 0) + r0)
               == lax.broadcasted_iota(jnp.int32, (rows, S5_CH), 1) // S5_GROUP)
        cm_ref[r0:r0 + rows, :] = jnp.where(own, tiled, 0.0).astype(BF16)


def _split_w_in_kernel(wt_ref, wm_ref, wdt_ref, wg_ref):
    step = 1024
    for j in range(0, OFF_DT, step):
        wm_ref[:, j:j + step] = wt_ref[j:j + step, :].T.astype(BF16)
    head = lax.broadcasted_iota(jnp.int32, (LANES, LANES), 1) < M_HEADS
    wdt_ref[...] = jnp.where(head, wt_ref[OFF_DT:OFF_DT + LANES, :].T, 0.0).astype(BF16)
    for j in range(0, 2 * D_MODEL, step):
        wg_ref[:, j:j + step] = wt_ref[OFF_GA + j:OFF_GA + j + step, :].T.astype(BF16)


def _split_w_in(w):
    k = w.shape[0]
    outs = (OFF_DT, LANES, 2 * D_MODEL)
    return pl.pallas_call(
        _split_w_in_kernel,
        grid=(k // LANES,),
        in_specs=[pl.BlockSpec((IN_COLS, LANES), lambda i: (0, i))],
        out_specs=[pl.BlockSpec((LANES, n), lambda i: (i, 0)) for n in outs],
        out_shape=[jax.ShapeDtypeStruct((k, n), BF16) for n in outs],
        compiler_params=_params(("parallel",)),
        name="split_w_in",
    )(jnp.swapaxes(w, 0, 1))


def _inproj_kernel(x_ref, sh_ref, sc_ref, g_ref, wm_ref, wdt_ref, wg_ref, dtb_ref, cw_ref, cb_ref,
                   u_ref, z_ref, xbc_ref, dt_ref, gate_ref, *rest, tiles_per_seq):
    accs = rest[-INPROJ_DEPTH:]
    h = _rms(x_ref[...], g_ref[...]) * (1.0 + sc_ref[...]) + sh_ref[...]
    hb = h.astype(BF16)
    rows = hb.shape[0]
    step = INPROJ_SLAB

    if u_ref.shape[0] > rows:
        u_ref[rows:, :] = jnp.zeros((u_ref.shape[0] - rows, D_MODEL), F32)

    if tiles_per_seq is None:
        conv_fn = None
    else:
        tail_ref, carry_ref = rest[:2]

        @pl.when(pl.program_id(0) % tiles_per_seq == 0)
        def _():
            carry_ref[...] = jnp.zeros_like(carry_ref)

        def conv_fn(r, sl):
            width = sl.stop - sl.start
            w = [cw_ref[k:k + 1, sl] for k in range(M_CONV)]
            bias = cb_ref[:, sl]
            row = lax.broadcasted_iota(jnp.int32, (SUBLANES, width), 0)
            xp = carry_ref[:, sl]
            farp = xp * w[1] + pltpu.roll(xp, 1, axis=0) * w[0]
            outs = []
            for i in range(rows // SUBLANES):
                xi = r[i * SUBLANES:(i + 1) * SUBLANES, :]
                if outs:
                    xi = _ordered_after(xi, outs[-1])
                s1 = jnp.where(row == 0, pltpu.roll(xp, 1, axis=0), pltpu.roll(xi, 1, axis=0))
                far = xi * w[1] + s1 * w[0]
                s2 = jnp.where(row < 2, pltpu.roll(farp, 2, axis=0), pltpu.roll(far, 2, axis=0))
                outs.append(xi * w[3] + s1 * w[2] + s2 + bias)
                xp, farp = xi, far
            carry_ref[:, sl] = r[rows - SUBLANES:, :]
            tail_ref[:, sl] = r[rows - SUBLANES:, :]
            return jnp.concatenate(outs, axis=0)

    def slabs(o_ref, w_ref, w_off, fn):
        n = o_ref.shape[-1]
        return [(o_ref, slice(j, j + min(step, n - j)), w_ref, w_off + j, fn) for j in range(0, n, step)]

    su = slabs(u_ref, wm_ref, 0, None)
    sz = slabs(z_ref, wm_ref, OFF_Z, None)
    sx = slabs(xbc_ref, wm_ref, OFF_XBC, conv_fn)
    sd = slabs(dt_ref, wdt_ref, 0, lambda r, sl: _softplus(r + dtb_ref[:, sl]))
    sg = slabs(gate_ref, wg_ref, 0, None)
    light = su + sz + sg + sd
    order = []
    while sx or light:
        if sx:
            order.append(sx.pop(0))
        if light:
            order.append(light.pop(0))

    def matmul(s):
        o_ref, sl, w_ref, w_col, _ = order[s]
        accs[s % INPROJ_DEPTH][:, 0:sl.stop - sl.start] = _dot(hb, w_ref[:, w_col:w_col + sl.stop - sl.start])

    for s in range(min(INPROJ_DEPTH - 1, len(order))):
        matmul(s)
    for s, (o_ref, sl, _, _, fn) in enumerate(order):
        if s + INPROJ_DEPTH - 1 < len(order):
            matmul(s + INPROJ_DEPTH - 1)
        r = accs[s % INPROJ_DEPTH][:, 0:sl.stop - sl.start]
        o_ref[0:rows, sl] = r if fn is None else fn(r, sl)


def _mod_specs(per_row, tm, rows_per_mod, cols):
    if per_row:
        return [pl.BlockSpec((tm, D_MODEL), functools.partial(lambda i, c: (i, c), c=c)) for c in cols]
    return [pl.BlockSpec((None, 1, D_MODEL), functools.partial(lambda i, c: (i // rows_per_mod, 0, c), c=c))
            for c in cols]


def _s5_layout(per_row, t, tm, rows_per_batch, width):
    if per_row:
        return (1, t, width), pl.BlockSpec((None, tm, width), lambda i: (0, i, 0))
    nblk, pitch = rows_per_batch // tm, _s5_pitch(tm)
    return ((nblk, (t // rows_per_batch) * pitch, width),
            pl.BlockSpec((None, pitch, width), lambda i: (i % nblk, i // nblk, 0)))


def _in_proj(x, mod, g, ws, dtb, cw, cb, *, per_row, tm, rows_per_batch):
    t = x.shape[0]
    row = lambda n: pl.BlockSpec((tm, n), lambda i: (i, 0))
    outs = (M_INNER, M_CONV_DIM, LANES, 2 * D_MODEL)
    u_shape, u_spec = _s5_layout(per_row, t, tm, rows_per_batch, D_MODEL)
    out_specs = [u_spec] + [row(n) for n in outs]
    out_shape = [jax.ShapeDtypeStruct(u_shape, F32)] + [jax.ShapeDtypeStruct((t, n), F32) for n in outs]
    scratch = []
    tiles_per_seq = None
    if not per_row:
        tiles_per_seq = rows_per_batch // tm
        out_specs.append(pl.BlockSpec((None, SUBLANES, M_CONV_DIM), lambda i: (i // tiles_per_seq, 0, 0)))
        out_shape.append(jax.ShapeDtypeStruct((t // rows_per_batch, SUBLANES, M_CONV_DIM), F32))
        scratch.append(pltpu.VMEM((SUBLANES, M_CONV_DIM), F32))
    scratch += [pltpu.VMEM((tm, INPROJ_SLAB), F32)] * INPROJ_DEPTH
    consts = (g,) + tuple(ws) + (dtb, cw, cb)
    return pl.pallas_call(
        functools.partial(_inproj_kernel, tiles_per_seq=tiles_per_seq),
        grid=(t // tm,),
        in_specs=[row(D_MODEL)] + _mod_specs(per_row, tm, rows_per_batch // tm if not per_row else 1, (0, 1))
                 + [_const_spec(w.shape) for w in consts],
        out_specs=out_specs,
        out_shape=out_shape,
        scratch_shapes=scratch,
        compiler_params=_params(("arbitrary",)),
        name="in_proj",
    )(x, mod, mod, *consts)


def _s5_pitch(tl):
    return tl if tl == 1 else tl + SUBLANES


def _s5_kernel(*refs, nb, tl):
    nu = S5_PIECES
    u_refs = refs[:nu]
    h0r_ref, h0i_ref, b_ref, a_ref, c_ref, d_ref = refs[nu:nu + 6]
    y_refs = refs[nu + 6:2 * nu + 6]
    hr_ref, hi_ref, hst_ref, bu0_ref, xs0_ref, bu1_ref, xs1_ref, bm_ref, cm_ref = refs[2 * nu + 6:]
    bufs = ((bu0_ref, xs0_ref), (bu1_ref, xs1_ref))
    tb = pl.program_id(1)
    tsub = S5_SUB_ROWS // nb
    tw = 2 * LANES
    pitch = _s5_pitch(tl)

    def step_rows(t):
        return pl.ds(t, nb, stride=pitch) if pitch > 1 else pl.ds(0, nb)

    @pl.when(tb == 0)
    def _():
        _s5_expand_operands(b_ref, c_ref, bm_ref, cm_ref)
        for j in range(S5_TILES):
            hst_ref[:, j * tw:j * tw + LANES] = h0r_ref[:, j * LANES:(j + 1) * LANES]
            hst_ref[:, j * tw + LANES:(j + 1) * tw] = h0i_ref[:, j * LANES:(j + 1) * LANES]

    ar = [jnp.broadcast_to(a_ref[:, j * tw:j * tw + LANES], (nb, LANES)) for j in range(S5_TILES)]
    ai = [jnp.broadcast_to(a_ref[:, j * tw + LANES:(j + 1) * tw], (nb, LANES)) for j in range(S5_TILES)]
    dsk = d_ref[...]

    def project_in(t0, bu_ref, xs_ref):
        xs = jnp.concatenate([jnp.concatenate([r[step_rows(t0 + i), :] for r in u_refs], axis=1)
                              for i in range(tsub)], axis=0)
        xs_ref[...] = xs
        lhs = xs.astype(BF16)
        for j in range(S5_TILES):
            bu_ref[:, j * tw:(j + 1) * tw] = _dot(lhs, bm_ref[:, j * tw:(j + 1) * tw])

    def scan_project_out(t0, bu_ref, xs_ref, carry):
        new, hs = [], []
        for j in range(S5_TILES):
            hr, hi = carry[2 * j], carry[2 * j + 1]
            rows = []
            for i in range(tsub):
                br = bu_ref[i * nb:(i + 1) * nb, j * tw:j * tw + LANES]
                bi = bu_ref[i * nb:(i + 1) * nb, j * tw + LANES:(j + 1) * tw]
                hr, hi = ar[j] * hr - ai[j] * hi + br, ar[j] * hi + ai[j] * hr + bi
                rows.append(jnp.concatenate([hr, hi], axis=1))
            new += [hr, hi]
            hs.append(jnp.concatenate(rows, axis=0).astype(BF16))
        y = _dot(jnp.concatenate(hs, axis=1), cm_ref[...]) + dsk * xs_ref[...]
        for i in range(tsub):
            for k, y_ref in enumerate(y_refs):
                y_ref[step_rows(t0 + i), :] = y[i * nb:(i + 1) * nb, k * LANES:(k + 1) * LANES]
        return tuple(new)

    h_in = []
    for j in range(S5_TILES):
        h_in += [hst_ref[:, j * tw:j * tw + LANES], hst_ref[:, j * tw + LANES:(j + 1) * tw]]
    nsub = tl // tsub
    (bu0, xs0), (bu1, xs1) = bufs
    project_in(0, bu0, xs0)
    if nsub == 1:
        h_out = scan_project_out(0, bu0, xs0, tuple(h_in))
    else:
        def pair(p, carry):
            t0 = 2 * p * tsub
            project_in(t0 + tsub, bu1, xs1)
            carry = scan_project_out(t0, bu0, xs0, carry)
            project_in(jnp.minimum(t0 + 2 * tsub, tl - tsub), bu0, xs0)
            return scan_project_out(t0 + tsub, bu1, xs1, carry)
        h_out = lax.fori_loop(0, nsub // 2, pair, tuple(h_in))
    for j in range(S5_TILES):
        hst_ref[:, j * tw:j * tw + LANES] = h_out[2 * j]
        hst_ref[:, j * tw + LANES:(j + 1) * tw] = h_out[2 * j + 1]

    if pitch > tl:
        for y_ref in y_refs:
            for b in range(nb):
                y_ref[b * pitch + tl:(b + 1) * pitch, :] = jnp.zeros((pitch - tl, LANES), F32)

    @pl.when(tb == pl.num_programs(1) - 1)
    def _():
        for j in range(S5_TILES):
            hr_ref[:, j * LANES:(j + 1) * LANES] = h_out[2 * j]
            hi_ref[:, j * LANES:(j + 1) * LANES] = h_out[2 * j + 1]


def _s5(u, h0r, h0i, b, a, c, d, *, tl):
    nblk, rows, _ = u.shape
    pitch = _s5_pitch(tl)
    nb = rows // pitch
    nsub = tl * nb // S5_SUB_ROWS
    assert S5_SUB_ROWS % nb == 0 and tl * nb % S5_SUB_ROWS == 0 and (nsub == 1 or nsub % 2 == 0)
    chunk3 = lambda s: pl.BlockSpec((None,) + s, lambda c, t: (c, 0, 0))
    st_spec = pl.BlockSpec((nb, S5_NS), lambda c, t: (0, c))
    nu = S5_PIECES
    u_specs = [pl.BlockSpec((None, rows, LANES), functools.partial(lambda c, t, k: (t, 0, nu * c + k), k=k))
               for k in range(nu)]
    y_spec = pl.BlockSpec((None, rows, LANES), lambda c, t: (t, 0, c))
    y_shape = jax.ShapeDtypeStruct((nblk, rows, D_MODEL // nu), F32)
    res = pl.pallas_call(
        functools.partial(_s5_kernel, nb=nb, tl=tl),
        grid=(S5_CHUNKS, nblk),
        in_specs=u_specs + [st_spec, st_spec, chunk3((S5_CH, 2 * S5_STATE)), chunk3((1, 2 * S5_NS)),
                            chunk3((2 * S5_NS, S5_GROUP)), chunk3((1, S5_CH))],
        out_specs=[y_spec] * nu + [st_spec, st_spec],
        out_shape=[y_shape] * nu + [jax.ShapeDtypeStruct(h0r.shape, F32), jax.ShapeDtypeStruct(h0i.shape, F32)],
        scratch_shapes=[pltpu.VMEM((nb, 2 * S5_NS), F32)]
                       + [pltpu.VMEM((S5_SUB_ROWS, 2 * S5_NS), F32), pltpu.VMEM((S5_SUB_ROWS, S5_CH), F32)] * 2
                       + [pltpu.VMEM((S5_CH, 2 * S5_NS), BF16), pltpu.VMEM((2 * S5_NS, S5_CH), BF16)],
        compiler_params=_params(("parallel", "arbitrary")),
        name="s5",
    )(*([u] * nu), h0r, h0i, b, a, c, d)
    return res[:nu], res[nu], res[nu + 1]


def _expand_mat():
    r = lax.broadcasted_iota(jnp.int32, (3 * LANES, M_HEADS * M_HEADDIM), 0) % LANES
    c = lax.broadcasted_iota(jnp.int32, (3 * LANES, M_HEADS * M_HEADDIM), 1) // M_HEADDIM
    return (r == c).astype(BF16)


LOG2E = 1.4426950408889634
SSD_CHUNKS_PER_STEP = 2


def _ssd_kernel(z_ref, conv_ref, dt_ref, alog_ref, dsk_ref, nw_ref, e64_ref,
                y_ref, hout_ref,
                ht_ref, act_ref, xs_ref, xdte_ref, eacs_ref, yacc_ref):
    c = pl.program_id(1)

    @pl.when(c == 0)
    def _():
        ht_ref[...] = jnp.zeros_like(ht_ref)

    for k in range(SSD_CHUNKS_PER_STEP):
        rows = pl.ds(k * M_CHUNK, M_CHUNK)
        _ssd_chunk(z_ref.at[rows], conv_ref.at[rows], dt_ref.at[rows], alog_ref, dsk_ref, nw_ref, e64_ref,
                   y_ref.at[rows], ht_ref, act_ref.at[k], xs_ref.at[k], xdte_ref.at[k], eacs_ref.at[k], yacc_ref.at[k])

    @pl.when(c == pl.num_programs(1) - 1)
    def _():
        for j in range(0, M_INNER, LANES):
            hout_ref[j:j + LANES, :] = ht_ref[:, j:j + LANES].T


def _ssd_chunk(z_ref, conv_ref, dt_ref, alog_ref, dsk_ref, nw_ref, e64_ref,
               y_ref, ht_ref, act_ref, xs_ref, xdte_ref, eacs_ref, yacc_ref):
    q = M_CHUNK
    cs = 512
    for j in range(0, M_CONV_DIM, cs):
        act_ref[:, j:j + cs] = _silu(conv_ref[:, j:j + cs])

    dt = dt_ref[...]
    a2 = dt * (-jnp.exp(alog_ref[...]) * LOG2E)
    ri = lax.broadcasted_iota(jnp.int32, (q, q), 0)
    ci = lax.broadcasted_iota(jnp.int32, (q, q), 1)
    causal = ri >= ci
    acs = jnp.dot(causal.astype(F32), a2, preferred_element_type=F32, precision=lax.Precision.HIGHEST)
    src_t = (acs - jnp.log2(dt)).T
    eacs2 = _split2(jnp.exp2(acs))
    dte2 = _split2(dt * jnp.exp2(acs[q - 1:q, :] - acs))
    for j in range(0, M_INNER, cs):
        sl = slice(j, j + cs)
        e = e64_ref[0:2 * LANES, sl]
        xs = act_ref[:, sl]
        xs_ref[:, sl] = xs.astype(BF16)
        xdte_ref[:, sl] = (xs * _dot(dte2, e)).astype(BF16)
        eacs_ref[:, sl] = _dot(eacs2, e)

    hp = M_HPG * M_HEADDIM
    lane = lax.broadcasted_iota(jnp.int32, (q, LANES), 1)
    first_head = lane < M_HEADDIM
    for g in range(M_GROUPS):
        gs = slice(g * hp, (g + 1) * hp)
        bm_t = act_ref[:, M_INNER + g * M_STATE:M_INNER + (g + 1) * M_STATE].T.astype(BF16)
        cm = act_ref[:, M_INNER + (M_GROUPS + g) * M_STATE:M_INNER + (M_GROUPS + g + 1) * M_STATE].astype(BF16)
        cbm = _dot(cm, bm_t)
        y_off = _dot(cm, ht_ref[:, gs].astype(BF16))
        st_t = _dot(bm_t, xdte_ref[:, gs])
        for pr in range(M_HPG // 2):
            h0 = g * M_HPG + 2 * pr
            ps = slice(h0 * M_HEADDIM, (h0 + 2) * M_HEADDIM)
            ms = []
            for h in (h0, h0 + 1):
                d2 = jnp.broadcast_to(acs[:, h:h + 1], (q, q)) - src_t[h:h + 1, :]
                ms.append((cbm * jnp.where(causal, jnp.exp2(d2), 0.0)).astype(BF16))
            xp = xs_ref[:, ps]
            zero = jnp.zeros_like(xp)
            rhs = jnp.concatenate([jnp.where(first_head, xp, zero), jnp.where(first_head, zero, xp)], axis=0)
            yd = _dot(jnp.concatenate(ms, axis=1), rhs)
            yacc_ref[:, ps] = yd + y_off[:, ps.start - gs.start:ps.stop - gs.start] * eacs_ref[:, ps] \
                + dsk_ref[:, ps] * act_ref[:, ps]
        ht_ref[:, gs] = eacs_ref[q - 1:q, gs] * ht_ref[:, gs] + st_t

    ss = jnp.zeros((q, 1), F32)
    for j in range(0, M_INNER, cs):
        sl = slice(j, j + cs)
        gt = yacc_ref[:, sl] * _silu(z_ref[:, sl])
        yacc_ref[:, sl] = gt
        ss = ss + jnp.sum(gt * gt, axis=-1, keepdims=True)
    inv = lax.rsqrt(ss * (1.0 / M_INNER) + EPS)
    for j in range(0, M_INNER, cs):
        sl = slice(j, j + cs)
        y_ref[:, sl] = yacc_ref[:, sl] * inv * nw_ref[:, sl]


def _ssd(z, conv, dt, alog, dsk, nw, e64, *, nbatch, seq):
    q = M_CHUNK
    k = SSD_CHUNKS_PER_STEP
    assert seq % (k * q) == 0
    nc = seq // (k * q)
    row = lambda n: pl.BlockSpec((k * q, n), lambda b, c: (b * nc + c, 0))
    return pl.pallas_call(
        _ssd_kernel,
        grid=(nbatch, nc),
        in_specs=[row(M_INNER), row(M_CONV_DIM), row(LANES), _const_spec(alog.shape),
                  _const_spec(dsk.shape), _const_spec(nw.shape), _const_spec(e64.shape)],
        out_specs=[row(M_INNER), pl.BlockSpec((None, M_INNER, M_STATE), lambda b, c: (b, 0, 0))],
        out_shape=[jax.ShapeDtypeStruct((nbatch * seq, M_INNER), F32),
                   jax.ShapeDtypeStruct((nbatch, M_INNER, M_STATE), F32)],
        scratch_shapes=[pltpu.VMEM((M_STATE, M_INNER), F32), pltpu.VMEM((k, q, M_CONV_DIM), F32),
                        pltpu.VMEM((k, q, M_INNER), BF16),
                        pltpu.VMEM((k, q, M_INNER), BF16), pltpu.VMEM((k, q, M_INNER), F32),
                        pltpu.VMEM((k, q, M_INNER), F32)],
        compiler_params=_params(("parallel", "arbitrary")),
        name="ssd",
    )(z, conv, dt, alog, dsk, nw, e64)


SSD_STEP_BB = 8


def _ssd_step_kernel(z_ref, xbc_ref, dt_ref, cbuf_ref, h0_ref, cw_ref, cb_ref, alog_ref, dsk_ref, nw_ref,
                     e64_ref, y_ref, hout_ref, convout_ref):
    bb = SSD_STEP_BB
    x_new = xbc_ref[...]
    conv = cb_ref[...] + x_new * cw_ref[M_CONV - 1:M_CONV, :]
    for k in range(M_CONV - 1):
        conv = conv + cbuf_ref[k] * cw_ref[k:k + 1, :]
    act = _silu(conv)
    for k in range(M_CONV - 2):
        convout_ref[k] = cbuf_ref[k + 1]
    convout_ref[M_CONV - 2] = x_new

    xs = act[:, :M_INNER]
    bmat = act[:, M_INNER:M_INNER + M_GROUPS * M_STATE]
    cmat = act[:, M_INNER + M_GROUPS * M_STATE:]
    dt = dt_ref[...]
    da = jnp.exp(dt * (-jnp.exp(alog_ref[...])))
    dt_e = _dot(_split3(dt), e64_ref[...])
    da_e = _dot(_split3(da), e64_ref[...])
    xdt = xs * dt_e

    hp = M_HPG * M_HEADDIM
    lane_group = lax.broadcasted_iota(jnp.int32, (M_GROUPS, M_INNER), 1) // hp
    row_group = lax.broadcasted_iota(jnp.int32, (M_GROUPS, M_INNER), 0)
    gmask = lane_group == row_group
    ones_rows = lax.broadcasted_iota(jnp.int32, (SUBLANES, M_STATE), 0) >= M_GROUPS

    cbx = []
    for g in range(M_GROUPS):
        cbg = jnp.sum(cmat[:, g * M_STATE:(g + 1) * M_STATE] * bmat[:, g * M_STATE:(g + 1) * M_STATE],
                      axis=-1, keepdims=True)
        cbx.append(jnp.broadcast_to(cbg, (bb, hp)))
    y = jnp.concatenate(cbx, axis=1) * xdt + dsk_ref[...] * xs

    y_off_rows = []
    for b in range(bb):
        xsel = jnp.where(gmask, jnp.broadcast_to(xdt[b:b + 1, :], (M_GROUPS, M_INNER)), 0.0)
        lhs = jnp.concatenate([xsel, *_pieces3(da_e[b:b + 1, :]), jnp.zeros((1, M_INNER), F32)],
                              axis=0).astype(BF16)
        bm4 = jnp.concatenate([bmat[b:b + 1, g * M_STATE:(g + 1) * M_STATE] for g in range(M_GROUPS)], axis=0)
        cm4 = jnp.concatenate([cmat[b:b + 1, g * M_STATE:(g + 1) * M_STATE] for g in range(M_GROUPS)], axis=0)
        zeros4 = jnp.zeros((SUBLANES - M_GROUPS, M_STATE), F32)
        rhs = jnp.concatenate([jnp.concatenate([bm4, zeros4], axis=0),
                               jnp.where(ones_rows, 1.0, 0.0)], axis=1).astype(BF16)
        res = lax.dot_general(lhs, rhs, _TN, preferred_element_type=F32)
        h0 = h0_ref[b]
        hout_ref[b] = res[:, M_STATE:] * h0 + res[:, :M_STATE]
        cm8 = jnp.concatenate([cm4, zeros4], axis=0).astype(BF16)
        yo = lax.dot_general(cm8, h0.astype(BF16), _NT, preferred_element_type=F32)
        y_off_rows.append(jnp.sum(jnp.where(gmask, yo[:M_GROUPS, :], 0.0), axis=0, keepdims=True))
    y = y + da_e * jnp.concatenate(y_off_rows, axis=0)

    y_ref[...] = _rms(y * _silu(z_ref[...]), nw_ref[...])


def _ssd_step(zs, xbc, dt, cbuf, h0, cw, cb, alog, dsk, nw, e64):
    n = zs.shape[0]
    bb = SSD_STEP_BB
    row = lambda w: pl.BlockSpec((bb, w), lambda i: (i, 0))
    blk3 = lambda s: pl.BlockSpec((bb,) + s, lambda i: (i, 0, 0))
    conv_spec = pl.BlockSpec((M_CONV - 1, bb, M_CONV_DIM), lambda i: (0, i, 0))
    return pl.pallas_call(
        _ssd_step_kernel,
        grid=(n // bb,),
        in_specs=[row(M_INNER), row(M_CONV_DIM), row(LANES), conv_spec, blk3((M_INNER, M_STATE)),
                  _const_spec(cw.shape), _const_spec(cb.shape), _const_spec(alog.shape),
                  _const_spec(dsk.shape), _const_spec(nw.shape), _const_spec(e64.shape)],
        out_specs=[row(M_INNER), blk3((M_INNER, M_STATE)), conv_spec],
        out_shape=[jax.ShapeDtypeStruct((n, M_INNER), F32),
                   jax.ShapeDtypeStruct((n, M_INNER, M_STATE), F32),
                   jax.ShapeDtypeStruct((M_CONV - 1, n, M_CONV_DIM), F32)],
        compiler_params=_params(("parallel",)),
        name="ssd_step",
    )(zs, xbc, dt, cbuf, h0, cw, cb, alog, dsk, nw, e64)


MXU_WIDTH = 256
FFN_CHUNKS = (0, 6 * MXU_WIDTH, D_FF)


def _out_kernel(x_ref, *refs):
    y5_refs = refs[:S5_PIECES]
    (yb_ref, gate_ref, gtm_ref, shf_ref, scf_ref, gtf_ref, wglu_ref, bglu_ref, wb5_ref, wbs_ref, wout_ref,
     npm_ref, npf_ref, npo_ref, wfi_ref, wfo_ref, o_ref) = refs[S5_PIECES:]
    rows = x_ref.shape[0]
    y5 = jnp.concatenate([y5_refs[k][0:rows, c * LANES:(c + 1) * LANES]
                          for c in range(S5_CHUNKS) for k in range(S5_PIECES)], axis=1)
    ya = _gelu_tanh(y5)
    glu = ya * _sigmoid(_dot(ya.astype(BF16), wglu_ref[...]) + bglu_ref[...])
    merged = (_sigmoid(gate_ref[:, :D_MODEL]) * _dot(glu.astype(BF16), wb5_ref[...])
              + _sigmoid(gate_ref[:, D_MODEL:]) * _dot(yb_ref[...].astype(BF16), wbs_ref[...]))
    mix = _dot(merged.astype(BF16), wout_ref[...])
    x1 = x_ref[...] + gtm_ref[...] * _rms(mix, npm_ref[...])

    hb = (_rms(x1, npf_ref[...]) * (1.0 + scf_ref[...]) + shf_ref[...]).astype(BF16)
    f = jnp.zeros_like(x1)
    for j, e in zip(FFN_CHUNKS[:-1], FFN_CHUNKS[1:]):
        gg = _dot(hb, wfi_ref[:, j:e])
        uu = _dot(hb, wfi_ref[:, D_FF + j:D_FF + e])
        f = f + _dot((_silu(gg) * uu).astype(BF16), wfo_ref[j:e, :])
    o_ref[...] = x1 + gtf_ref[...] * _rms(f, npo_ref[...])


def _out_stage(x, y5, yb, gates, mod, ws, *, per_row, tm, rows_per_batch):
    t = x.shape[0]
    row = lambda n: pl.BlockSpec((tm, n), lambda i: (i, 0))
    mods = _mod_specs(per_row, tm, rows_per_batch // tm if not per_row else 1, (2, 3, 4, 5))
    _, y5_spec = _s5_layout(per_row, t, tm, rows_per_batch, D_MODEL // S5_PIECES)
    return pl.pallas_call(
        _out_kernel,
        grid=(t // tm,),
        in_specs=[row(D_MODEL)] + [y5_spec] * S5_PIECES + [row(M_INNER), row(2 * D_MODEL)] + mods
                 + [_const_spec(w.shape) for w in ws],
        out_specs=row(D_MODEL),
        out_shape=jax.ShapeDtypeStruct((t, D_MODEL), F32),
        compiler_params=_params(("parallel",)),
        name="out_stage",
    )(x, *y5, yb, gates, mod, mod, mod, mod, *ws)


def _pad_lanes(v):
    return jnp.pad(v.reshape(1, -1), ((0, 0), (0, LANES - v.shape[-1])))


def kernel(x_prompt, x_sample, state_s5_re, state_s5_im, state_ssm, state_conv, c_prompt, c_sample, w_ada, b_ada, norm_pre_mix, norm_post_mix, norm_pre_ffn, norm_post_ffn, w_in, s5_lam_re, s5_lam_im, s5_log_dt, s5_b_re, s5_b_im, s5_c_re, s5_c_im, s5_d, s5_w_glu, s5_b_glu, m_conv_w, m_conv_b, m_dt_bias, m_a_log, m_d, m_norm, w_branch_s5, w_branch_ssd, w_out, w_ffn_in, w_ffn_out):
    bp, seq, _ = x_prompt.shape
    bs = x_sample.shape[0]
    assert x_sample.shape[1] == 1 and w_ada.shape[0] == 1 and seq % M_CHUNK == 0
    row1 = lambda v: v.reshape(1, -1)

    ws_in = _split_w_in(w_in[0])
    ws_out = (s5_w_glu[0].astype(BF16), row1(s5_b_glu[0]), w_branch_s5[0].astype(BF16),
              w_branch_ssd[0].astype(BF16), w_out[0].astype(BF16), row1(norm_post_mix[0]),
              row1(norm_pre_ffn[0]), row1(norm_post_ffn[0]), w_ffn_in[0].astype(BF16), w_ffn_out[0].astype(BF16))
    cw, cb = m_conv_w[0], row1(m_conv_b[0])
    dtb, alog = _pad_lanes(m_dt_bias[0]), _pad_lanes(m_a_log[0])
    dsk = row1(jnp.repeat(m_d[0], M_HEADDIM))
    nw = row1(m_norm[0])
    e64 = _expand_mat()

    mod = _ada(jnp.concatenate([c_prompt, c_sample], axis=0), w_ada[0], b_ada[0])
    mod_p = mod[:bp].reshape(bp, 1, 6 * D_MODEL)
    mod_s = mod[bp:]

    ar, ai, bbr, bbi = _s5_disc(s5_lam_re[0], s5_lam_im[0], s5_log_dt[0], s5_b_re[0], s5_b_im[0])
    s5m = _s5_matrices(ar, ai, bbr, bbi, s5_c_re[0], s5_c_im[0], s5_d[0])
    g_pre = row1(norm_pre_mix[0])

    xp = x_prompt.reshape(bp * seq, D_MODEL)
    tm = min(ROW_TILE, seq)
    u, zs, act, dt, gates, tail = _in_proj(xp, mod_p, g_pre, ws_in, dtb, cw, cb,
                                           per_row=False, tm=tm, rows_per_batch=seq)
    p_conv = tail[:, SUBLANES - (M_CONV - 1):, :]
    zeros_s5 = jnp.zeros((bp, S5_GROUPS * S5_STATE), F32)
    y5, p_re, p_im = _s5(u, zeros_s5, zeros_s5, *s5m, tl=tm)
    yb, p_ssm = _ssd(zs, act, dt, alog, dsk, nw, e64, nbatch=bp, seq=seq)
    y_prompt = _out_stage(xp, y5, yb, gates, mod_p, ws_out, per_row=False, tm=tm, rows_per_batch=seq)

    xs = x_sample.reshape(bs, D_MODEL)
    u, zs, xbc, dt, gates = _in_proj(xs, mod_s, g_pre, ws_in, dtb, cw, cb, per_row=True, tm=bs, rows_per_batch=1)
    y5, s_re, s_im = _s5(u, state_s5_re[0].reshape(bs, -1), state_s5_im[0].reshape(bs, -1), *s5m, tl=1)
    yb, s_ssm, s_conv = _ssd_step(zs, xbc, dt, jnp.swapaxes(state_conv[0], 0, 1),
                                  state_ssm[0].reshape(bs, M_INNER, M_STATE),
                                  cw, cb, alog, dsk, nw, e64)
    y_sample = _out_stage(xs, y5, yb, gates, mod_s, ws_out, per_row=True, tm=bs, rows_per_batch=1)

    s5_shape = (1, -1, S5_GROUPS, S5_STATE)
    ssm_shape = (1, -1, M_HEADS, M_HEADDIM, M_STATE)
    return (y_prompt.reshape(bp, seq, D_MODEL), y_sample.reshape(bs, 1, D_MODEL),
            p_re.reshape(s5_shape), p_im.reshape(s5_shape), p_ssm.reshape(ssm_shape), p_conv[None],
            s_re.reshape(s5_shape), s_im.reshape(s5_shape), s_ssm.reshape(ssm_shape), jnp.swapaxes(s_conv, 0, 1)[None])
```

```python
import functools

import jax
import jax.numpy as jnp
from jax import lax
from jax.experimental import pallas as pl
from jax.experimental.pallas import tpu as pltpu

F32 = jnp.float32
BF16 = jnp.bfloat16

D_MODEL = 1024
EPS = 1e-6
S5_GROUP = 16
S5_GROUPS = 64
S5_STATE = 64
M_INNER = 2048
M_HEADDIM = 64
M_HEADS = 32
M_GROUPS = 4
M_HPG = 8
M_STATE = 128
M_CONV = 4
M_CONV_DIM = 3072
M_CHUNK = 128
D_FF = 2816
OFF_Z = D_MODEL
OFF_XBC = OFF_Z + M_INNER
OFF_DT = OFF_XBC + M_CONV_DIM
OFF_GA = OFF_DT + M_HEADS
IN_COLS = OFF_GA + 2 * D_MODEL

LANES = 128
SUBLANES = 8
VMEM_LIMIT = 56 * 1024 * 1024

S5_CH = 256
S5_NS = 1024
S5_CHUNKS = D_MODEL // S5_CH
S5_TILES = S5_NS // LANES
S5_PIECES = S5_CH // LANES
S5_SUB_ROWS = 128
ROW_TILE = 256
INPROJ_SLAB = 512
INPROJ_DEPTH = 3

_NT = (((1,), (1,)), ((), ()))
_TN = (((0,), (0,)), ((), ()))


def _const_spec(shape):
    nd = len(shape)
    return pl.BlockSpec(shape, lambda *_: (0,) * nd, pipeline_mode=pl.Buffered(1))


def _params(sem):
    return pltpu.CompilerParams(dimension_semantics=sem, vmem_limit_bytes=VMEM_LIMIT)


def _sigmoid(x):
    return 0.5 * jnp.tanh(0.5 * x) + 0.5


def _silu(x):
    hx = 0.5 * x
    return hx + hx * jnp.tanh(hx)


def _softplus(x):
    return jnp.maximum(x, 0.0) + jnp.log1p(jnp.exp(-jnp.abs(x)))


def _gelu_tanh(x):
    return 0.5 * x * (1.0 + jnp.tanh(0.7978845608028654 * (x + 0.044715 * (x * x * x))))


def _rms(x, g):
    return x * lax.rsqrt(jnp.mean(x * x, axis=-1, keepdims=True) + EPS) * g


def _dot(a, b):
    return jnp.dot(a, b, preferred_element_type=F32)


def _pieces3(x):
    hi = x.astype(BF16).astype(F32)
    r1 = x - hi
    mid = r1.astype(BF16).astype(F32)
    lo = (r1 - mid).astype(BF16).astype(F32)
    return hi, mid, lo


def _ordered_after(x, dep):
    zero = jnp.where(dep[:, 0:LANES] > jnp.inf, 1.0, 0.0)
    return x + jnp.concatenate([zero] * (x.shape[1] // LANES), axis=1)


def _split2(x):
    return jnp.concatenate(_pieces3(x)[:2], axis=-1).astype(BF16)


def _split3(x):
    return jnp.concatenate(_pieces3(x), axis=-1).astype(BF16)


def _ada_kernel(c_ref, w_ref, b_ref, o_ref):
    s = _silu(c_ref[...]).astype(BF16)
    o_ref[...] = _dot(s, w_ref[...].astype(BF16)) + b_ref[...]


def _ada(c, w, b):
    n = c.shape[0]
    tn = 1024
    return pl.pallas_call(
        _ada_kernel,
        grid=(6 * D_MODEL // tn,),
        in_specs=[_const_spec((n, D_MODEL)),
                  pl.BlockSpec((D_MODEL, tn), lambda j: (0, j)),
                  pl.BlockSpec((1, tn), lambda j: (0, j))],
        out_specs=pl.BlockSpec((n, tn), lambda j: (0, j)),
        out_shape=jax.ShapeDtypeStruct((n, 6 * D_MODEL), F32),
        compiler_params=_params(("arbitrary",)),
        name="ada",
    )(c, w, b.reshape(1, -1))


def _s5_disc_kernel(lr_ref, li_ref, ldt_ref, br_ref, bi_ref, ar_ref, ai_ref, bbr_ref, bbi_ref):
    lr = lr_ref[...]
    li = li_ref[...]
    dt = jnp.exp(ldt_ref[...])
    mag = jnp.exp(lr * dt)
    ar = mag * jnp.cos(li * dt)
    ai = mag * jnp.sin(li * dt)
    den = lr * lr + li * li
    nr = ar - 1.0
    f_re = (nr * lr + ai * li) / den
    f_im = (ai * lr - nr * li) / den
    br = br_ref[...]
    bi = bi_ref[...]
    ar_ref[...] = ar
    ai_ref[...] = ai
    bbr_ref[...] = f_re * br - f_im * bi
    bbi_ref[...] = f_re * bi + f_im * br


def _s5_disc(lam_re, lam_im, log_dt, b_re, b_im):
    g, n = lam_re.shape
    w = S5_GROUP * n
    tile = lambda a: jnp.tile(a, (1, S5_GROUP))
    ldt = jnp.broadcast_to(log_dt[:, None], (g, w))
    brt = jnp.swapaxes(b_re, 1, 2).reshape(g, w)
    bit = jnp.swapaxes(b_im, 1, 2).reshape(g, w)
    shp = jax.ShapeDtypeStruct((g, w), F32)
    ar, ai, bbr, bbi = pl.pallas_call(
        _s5_disc_kernel,
        out_shape=(shp, shp, shp, shp),
        name="s5_disc",
    )(tile(lam_re), tile(lam_im), ldt, brt, bit)
    return (ar[:, :n], ai[:, :n], bbr.reshape(g, S5_GROUP, n), bbi.reshape(g, S5_GROUP, n))


def _s5_matrices(abar_re, abar_im, bbar_re, bbar_im, c_re, c_im, d_skip):
    ns = 2 * S5_NS
    b = jnp.concatenate([bbar_re.reshape(D_MODEL, S5_STATE), bbar_im.reshape(D_MODEL, S5_STATE)],
                        axis=1).reshape(S5_CHUNKS, S5_CH, 2 * S5_STATE)

    def by_tile(c):
        return jnp.swapaxes(c, 1, 2).reshape(S5_CHUNKS, S5_TILES, 1, 2, S5_STATE, S5_GROUP)
    c = jnp.concatenate([by_tile(c_re), by_tile(-c_im)], axis=2).reshape(S5_CHUNKS, ns, S5_GROUP)

    by_tile_a = lambda v: v.reshape(S5_CHUNKS, S5_TILES, 1, LANES)
    a = jnp.concatenate([by_tile_a(abar_re), by_tile_a(abar_im)], axis=2).reshape(S5_CHUNKS, 1, ns)
    d = d_skip.reshape(S5_CHUNKS, 1, S5_CH)
    return b, a, c, d


def _s5_state_group(i):
    return 2 * (i // (2 * LANES)) + (i % LANES) // S5_STATE


def _s5_expand_operands(b_ref, c_ref, bm_ref, cm_ref):
    tw = 2 * LANES
    row_g = lax.broadcasted_iota(jnp.int32, (S5_CH, LANES), 0) // S5_GROUP
    lane_half = lax.broadcasted_iota(jnp.int32, (S5_CH, LANES), 1) // S5_STATE
    for part in range(2):
        bp = b_ref[:, part * S5_STATE:(part + 1) * S5_STATE]
        pattern = jnp.concatenate([bp, bp], axis=1)
        for j in range(S5_TILES):
            tile = jnp.where(row_g == 2 * j + lane_half, pattern, 0.0)
            bm_ref[:, j * tw + part * LANES:j * tw + (part + 1) * LANES] = tile.astype(BF16)
    spread = (lax.broadcasted_iota(jnp.int32, (S5_GROUP, S5_CH), 1) % S5_GROUP
              == lax.broadcasted_iota(jnp.int32, (S5_GROUP, S5_CH), 0)).astype(BF16)
    rows = 512
    for r0 in range(0, 2 * S5_NS, rows):
        tiled = _dot(c_ref[r0:r0 + rows, :].astype(BF16), spread)
        own = (_s5_state_group(lax.broadcasted_iota(jnp.int32, (rows, S5_CH), 0) + r0)
               == lax.broadcasted_iota(jnp.int32, (rows, S5_CH), 1) // S5_GROUP)
        cm_ref[r0:r0 + rows, :] = jnp.where(own, tiled, 0.0).astype(BF16)


def _split_w_in_kernel(wt_ref, wm_ref, wdt_ref, wg_ref):
    tk = wt_ref.shape[1]
    step = 1024
    for j in range(0, OFF_DT, step):
        wm_ref[:, j:j + step] = wt_ref[j:j + step, :].T.astype(BF16)
    head = lax.broadcasted_iota(jnp.int32, (tk, LANES), 1) < M_HEADS
    wdt_ref[...] = jnp.where(head, wt_ref[OFF_DT:OFF_DT + LANES, :].T, 0.0).astype(BF16)
    for j in range(0, 2 * D_MODEL, step):
        wg_ref[:, j:j + step] = wt_ref[OFF_GA + j:OFF_GA + j + step, :].T.astype(BF16)


def _split_w_in(w):
    k = w.shape[0]
    tk = 256
    outs = (OFF_DT, LANES, 2 * D_MODEL)
    return pl.pallas_call(
        _split_w_in_kernel,
        grid=(k // tk,),
        in_specs=[pl.BlockSpec((IN_COLS, tk), lambda i: (0, i))],
        out_specs=[pl.BlockSpec((tk, n), lambda i: (i, 0)) for n in outs],
        out_shape=[jax.ShapeDtypeStruct((k, n), BF16) for n in outs],
        compiler_params=_params(("parallel",)),
        name="split_w_in",
    )(jnp.swapaxes(w, 0, 1))


def _inproj_kernel(x_ref, sh_ref, sc_ref, g_ref, wm_ref, wdt_ref, wg_ref, dtb_ref, cw_ref, cb_ref,
                   u_ref, z_ref, xbc_ref, dt_ref, gate_ref, *rest, tiles_per_seq):
    accs = rest[-INPROJ_DEPTH:]
    h = _rms(x_ref[...], g_ref[...]) * (1.0 + sc_ref[...]) + sh_ref[...]
    hb = h.astype(BF16)
    rows = hb.shape[0]
    step = INPROJ_SLAB

    if u_ref.shape[0] > rows:
        u_ref[rows:, :] = jnp.zeros((u_ref.shape[0] - rows, D_MODEL), F32)

    if tiles_per_seq is None:
        conv_fn = None
    else:
        tail_ref, carry_ref = rest[:2]

        @pl.when(pl.program_id(0) % tiles_per_seq == 0)
        def _():
            carry_ref[...] = jnp.zeros_like(carry_ref)

        def conv_fn(r, sl):
            width = sl.stop - sl.start
            w = [cw_ref[k:k + 1, sl] for k in range(M_CONV)]
            bias = cb_ref[:, sl]
            row = lax.broadcasted_iota(jnp.int32, (SUBLANES, width), 0)
            xp = carry_ref[:, sl]
            farp = xp * w[1] + pltpu.roll(xp, 1, axis=0) * w[0]
            outs = []
            for i in range(rows // SUBLANES):
                xi = r[i * SUBLANES:(i + 1) * SUBLANES, :]
                if outs:
                    xi = _ordered_after(xi, outs[-1])
                s1 = jnp.where(row == 0, pltpu.roll(xp, 1, axis=0), pltpu.roll(xi, 1, axis=0))
                far = xi * w[1] + s1 * w[0]
                s2 = jnp.where(row < 2, pltpu.roll(farp, 2, axis=0), pltpu.roll(far, 2, axis=0))
                outs.append(_silu(xi * w[3] + s1 * w[2] + s2 + bias))
                xp, farp = xi, far
            carry_ref[:, sl] = r[rows - SUBLANES:, :]
            tail_ref[:, sl] = r[rows - SUBLANES:, :]
            return jnp.concatenate(outs, axis=0)

    def slabs(o_ref, w_ref, w_off, fn):
        n = o_ref.shape[-1]
        return [(o_ref, slice(j, j + min(step, n - j)), w_ref, w_off + j, fn) for j in range(0, n, step)]

    su = slabs(u_ref, wm_ref, 0, None)
    sz = slabs(z_ref, wm_ref, OFF_Z, None)
    sx = slabs(xbc_ref, wm_ref, OFF_XBC, conv_fn)
    sd = slabs(dt_ref, wdt_ref, 0, lambda r, sl: _softplus(r + dtb_ref[:, sl]))
    sg = slabs(gate_ref, wg_ref, 0, None)
    light = su + sz + sg + sd
    order = []
    while sx or light:
        if sx:
            order.append(sx.pop(0))
        if light:
            order.append(light.pop(0))

    def matmul(s):
        o_ref, sl, w_ref, w_col, _ = order[s]
        accs[s % INPROJ_DEPTH][:, 0:sl.stop - sl.start] = _dot(hb, w_ref[:, w_col:w_col + sl.stop - sl.start])

    for s in range(min(INPROJ_DEPTH - 1, len(order))):
        matmul(s)
    for s, (o_ref, sl, _, _, fn) in enumerate(order):
        if s + INPROJ_DEPTH - 1 < len(order):
            matmul(s + INPROJ_DEPTH - 1)
        r = accs[s % INPROJ_DEPTH][:, 0:sl.stop - sl.start]
        o_ref[0:rows, sl] = r if fn is None else fn(r, sl)


def _mod_specs(per_row, tm, rows_per_mod, cols):
    if per_row:
        return [pl.BlockSpec((tm, D_MODEL), functools.partial(lambda i, c: (i, c), c=c)) for c in cols]
    return [pl.BlockSpec((None, 1, D_MODEL), functools.partial(lambda i, c: (i // rows_per_mod, 0, c), c=c))
            for c in cols]


def _s5_layout(per_row, t, tm, rows_per_batch, width):
    if per_row:
        return (1, t, width), pl.BlockSpec((None, tm, width), lambda i: (0, i, 0))
    nblk, pitch = rows_per_batch // tm, _s5_pitch(tm)
    return ((nblk, (t // rows_per_batch) * pitch, width),
            pl.BlockSpec((None, pitch, width), lambda i: (i % nblk, i // nblk, 0)))


def _in_proj(x, mod, g, ws, dtb, cw, cb, *, per_row, tm, rows_per_batch):
    t = x.shape[0]
    row = lambda n: pl.BlockSpec((tm, n), lambda i: (i, 0))
    outs = (M_INNER, M_CONV_DIM, LANES, 2 * D_MODEL)
    u_shape, u_spec = _s5_layout(per_row, t, tm, rows_per_batch, D_MODEL)
    out_specs = [u_spec] + [row(n) for n in outs]
    out_shape = [jax.ShapeDtypeStruct(u_shape, F32)] + [jax.ShapeDtypeStruct((t, n), F32) for n in outs]
    scratch = []
    tiles_per_seq = None
    if not per_row:
        tiles_per_seq = rows_per_batch // tm
        out_specs.append(pl.BlockSpec((None, SUBLANES, M_CONV_DIM), lambda i: (i // tiles_per_seq, 0, 0)))
        out_shape.append(jax.ShapeDtypeStruct((t // rows_per_batch, SUBLANES, M_CONV_DIM), F32))
        scratch.append(pltpu.VMEM((SUBLANES, M_CONV_DIM), F32))
    scratch += [pltpu.VMEM((tm, INPROJ_SLAB), F32)] * INPROJ_DEPTH
    consts = (g,) + tuple(ws) + (dtb, cw, cb)
    return pl.pallas_call(
        functools.partial(_inproj_kernel, tiles_per_seq=tiles_per_seq),
        grid=(t // tm,),
        in_specs=[row(D_MODEL)] + _mod_specs(per_row, tm, rows_per_batch // tm if not per_row else 1, (0, 1))
                 + [_const_spec(w.shape) for w in consts],
        out_specs=out_specs,
        out_shape=out_shape,
        scratch_shapes=scratch,
        compiler_params=_params(("arbitrary",)),
        name="in_proj",
    )(x, mod, mod, *consts)


def _s5_pitch(tl):
    return tl if tl == 1 else tl + SUBLANES


def _s5_kernel(*refs, nb, tl):
    nu = S5_PIECES
    u_refs = refs[:nu]
    h0r_ref, h0i_ref, b_ref, a_ref, c_ref, d_ref = refs[nu:nu + 6]
    y_refs = refs[nu + 6:2 * nu + 6]
    hr_ref, hi_ref, hst_ref, bu0_ref, xs0_ref, bu1_ref, xs1_ref, bm_ref, cm_ref = refs[2 * nu + 6:]
    bufs = ((bu0_ref, xs0_ref), (bu1_ref, xs1_ref))
    tb = pl.program_id(1)
    tsub = S5_SUB_ROWS // nb
    tw = 2 * LANES
    pitch = _s5_pitch(tl)

    def step_rows(t):
        return pl.ds(t, nb, stride=pitch) if pitch > 1 else pl.ds(0, nb)

    @pl.when(tb == 0)
    def _():
        _s5_expand_operands(b_ref, c_ref, bm_ref, cm_ref)
        for j in range(S5_TILES):
            hst_ref[:, j * tw:j * tw + LANES] = h0r_ref[:, j * LANES:(j + 1) * LANES]
            hst_ref[:, j * tw + LANES:(j + 1) * tw] = h0i_ref[:, j * LANES:(j + 1) * LANES]

    ar = [jnp.broadcast_to(a_ref[:, j * tw:j * tw + LANES], (nb, LANES)) for j in range(S5_TILES)]
    ai = [jnp.broadcast_to(a_ref[:, j * tw + LANES:(j + 1) * tw], (nb, LANES)) for j in range(S5_TILES)]
    dsk = d_ref[...]

    def project_in(t0, bu_ref, xs_ref):
        xs = jnp.concatenate([jnp.concatenate([r[step_rows(t0 + i), :] for r in u_refs], axis=1)
                              for i in range(tsub)], axis=0)
        xs_ref[...] = xs
        lhs = xs.astype(BF16)
        for j in range(S5_TILES):
            bu_ref[:, j * tw:(j + 1) * tw] = _dot(lhs, bm_ref[:, j * tw:(j + 1) * tw])

    def scan_project_out(t0, bu_ref, xs_ref, carry):
        new, hs = [], []
        for j in range(S5_TILES):
            hr, hi = carry[2 * j], carry[2 * j + 1]
            rows = []
            for i in range(tsub):
                br = bu_ref[i * nb:(i + 1) * nb, j * tw:j * tw + LANES]
                bi = bu_ref[i * nb:(i + 1) * nb, j * tw + LANES:(j + 1) * tw]
                hr, hi = ar[j] * hr - ai[j] * hi + br, ar[j] * hi + ai[j] * hr + bi
                rows.append(jnp.concatenate([hr, hi], axis=1))
            new += [hr, hi]
            hs.append(jnp.concatenate(rows, axis=0).astype(BF16))
        y = _dot(jnp.concatenate(hs, axis=1), cm_ref[...]) + dsk * xs_ref[...]
        for i in range(tsub):
            for k, y_ref in enumerate(y_refs):
                y_ref[step_rows(t0 + i), :] = y[i * nb:(i + 1) * nb, k * LANES:(k + 1) * LANES]
        return tuple(new)

    h_in = []
    for j in range(S5_TILES):
        h_in += [hst_ref[:, j * tw:j * tw + LANES], hst_ref[:, j * tw + LANES:(j + 1) * tw]]
    nsub = tl // tsub
    (bu0, xs0), (bu1, xs1) = bufs
    project_in(0, bu0, xs0)
    if nsub == 1:
        h_out = scan_project_out(0, bu0, xs0, tuple(h_in))
    else:
        def pair(p, carry):
            t0 = 2 * p * tsub
            project_in(t0 + tsub, bu1, xs1)
            carry = scan_project_out(t0, bu0, xs0, carry)
            project_in(jnp.minimum(t0 + 2 * tsub, tl - tsub), bu0, xs0)
            return scan_project_out(t0 + tsub, bu1, xs1, carry)
        h_out = lax.fori_loop(0, nsub // 2, pair, tuple(h_in))
    for j in range(S5_TILES):
        hst_ref[:, j * tw:j * tw + LANES] = h_out[2 * j]
        hst_ref[:, j * tw + LANES:(j + 1) * tw] = h_out[2 * j + 1]

    if pitch > tl:
        for y_ref in y_refs:
            for b in range(nb):
                y_ref[b * pitch + tl:(b + 1) * pitch, :] = jnp.zeros((pitch - tl, LANES), F32)

    @pl.when(tb == pl.num_programs(1) - 1)
    def _():
        for j in range(S5_TILES):
            hr_ref[:, j * LANES:(j + 1) * LANES] = h_out[2 * j]
            hi_ref[:, j * LANES:(j + 1) * LANES] = h_out[2 * j + 1]


def _s5(u, h0r, h0i, b, a, c, d, *, tl):
    nblk, rows, _ = u.shape
    pitch = _s5_pitch(tl)
    nb = rows // pitch
    nsub = tl * nb // S5_SUB_ROWS
    assert S5_SUB_ROWS % nb == 0 and tl * nb % S5_SUB_ROWS == 0 and (nsub == 1 or nsub % 2 == 0)
    chunk3 = lambda s: pl.BlockSpec((None,) + s, lambda c, t: (c, 0, 0))
    st_spec = pl.BlockSpec((nb, S5_NS), lambda c, t: (0, c))
    nu = S5_PIECES
    u_specs = [pl.BlockSpec((None, rows, LANES), functools.partial(lambda c, t, k: (t, 0, nu * c + k), k=k))
               for k in range(nu)]
    y_spec = pl.BlockSpec((None, rows, LANES), lambda c, t: (t, 0, c))
    y_shape = jax.ShapeDtypeStruct((nblk, rows, D_MODEL // nu), F32)
    res = pl.pallas_call(
        functools.partial(_s5_kernel, nb=nb, tl=tl),
        grid=(S5_CHUNKS, nblk),
        in_specs=u_specs + [st_spec, st_spec, chunk3((S5_CH, 2 * S5_STATE)), chunk3((1, 2 * S5_NS)),
                            chunk3((2 * S5_NS, S5_GROUP)), chunk3((1, S5_CH))],
        out_specs=[y_spec] * nu + [st_spec, st_spec],
        out_shape=[y_shape] * nu + [jax.ShapeDtypeStruct(h0r.shape, F32), jax.ShapeDtypeStruct(h0i.shape, F32)],
        scratch_shapes=[pltpu.VMEM((nb, 2 * S5_NS), F32)]
                       + [pltpu.VMEM((S5_SUB_ROWS, 2 * S5_NS), F32), pltpu.VMEM((S5_SUB_ROWS, S5_CH), F32)] * 2
                       + [pltpu.VMEM((S5_CH, 2 * S5_NS), BF16), pltpu.VMEM((2 * S5_NS, S5_CH), BF16)],
        compiler_params=_params(("parallel", "arbitrary")),
        name="s5",
    )(*([u] * nu), h0r, h0i, b, a, c, d)
    return res[:nu], res[nu], res[nu + 1]


def _expand_mat():
    r = lax.broadcasted_iota(jnp.int32, (3 * LANES, M_HEADS * M_HEADDIM), 0) % LANES
    c = lax.broadcasted_iota(jnp.int32, (3 * LANES, M_HEADS * M_HEADDIM), 1) // M_HEADDIM
    return (r == c).astype(BF16)


LOG2E = 1.4426950408889634
SSD_CHUNKS_PER_STEP = 4


def _ssd_kernel(z_ref, act_ref, dt_ref, alog_ref, dsk_ref, nw_ref, e64_ref,
                y_ref, hout_ref,
                ht_ref, xs_ref, xdte_ref, eacs_ref, yacc_ref):
    c = pl.program_id(1)

    @pl.when(c == 0)
    def _():
        ht_ref[...] = jnp.zeros_like(ht_ref)

    for k in range(SSD_CHUNKS_PER_STEP):
        rows = pl.ds(k * M_CHUNK, M_CHUNK)
        _ssd_chunk(z_ref.at[rows], act_ref.at[rows], dt_ref.at[rows], alog_ref, dsk_ref, nw_ref, e64_ref,
                   y_ref.at[rows], ht_ref, xs_ref.at[k], xdte_ref.at[k], eacs_ref.at[k], yacc_ref.at[k])

    @pl.when(c == pl.num_programs(1) - 1)
    def _():
        for j in range(0, M_INNER, LANES):
            hout_ref[j:j + LANES, :] = ht_ref[:, j:j + LANES].T


def _ssd_chunk(z_ref, act_ref, dt_ref, alog_ref, dsk_ref, nw_ref, e64_ref,
               y_ref, ht_ref, xs_ref, xdte_ref, eacs_ref, yacc_ref):
    q = M_CHUNK
    cs = 512

    dt = dt_ref[...]
    a2 = dt * (-jnp.exp(alog_ref[...]) * LOG2E)
    ri = lax.broadcasted_iota(jnp.int32, (q, q), 0)
    ci = lax.broadcasted_iota(jnp.int32, (q, q), 1)
    causal = ri >= ci
    acs = jnp.dot(causal.astype(F32), a2, preferred_element_type=F32, precision=lax.Precision.HIGHEST)
    src_t = (acs - jnp.log2(dt)).T
    eacs2 = _split2(jnp.exp2(acs))
    dte2 = _split2(dt * jnp.exp2(acs[q - 1:q, :] - acs))
    for j in range(0, M_INNER, cs):
        sl = slice(j, j + cs)
        e = e64_ref[0:2 * LANES, sl]
        xs = act_ref[:, sl]
        xs_ref[:, sl] = xs.astype(BF16)
        xdte_ref[:, sl] = (xs * _dot(dte2, e)).astype(BF16)
        eacs_ref[:, sl] = _dot(eacs2, e)

    hp = M_HPG * M_HEADDIM
    lane = lax.broadcasted_iota(jnp.int32, (q, LANES), 1)
    first_head = lane < M_HEADDIM
    for g in range(M_GROUPS):
        gs = slice(g * hp, (g + 1) * hp)
        bm_t = act_ref[:, M_INNER + g * M_STATE:M_INNER + (g + 1) * M_STATE].T.astype(BF16)
        cm = act_ref[:, M_INNER + (M_GROUPS + g) * M_STATE:M_INNER + (M_GROUPS + g + 1) * M_STATE].astype(BF16)
        cbm = _dot(cm, bm_t)
        y_off = _dot(cm, ht_ref[:, gs].astype(BF16))
        st_t = _dot(bm_t, xdte_ref[:, gs])
        for pr in range(M_HPG // 2):
            h0 = g * M_HPG + 2 * pr
            ps = slice(h0 * M_HEADDIM, (h0 + 2) * M_HEADDIM)
            ms = []
            for h in (h0, h0 + 1):
                d2 = jnp.broadcast_to(acs[:, h:h + 1], (q, q)) - src_t[h:h + 1, :]
                ms.append((cbm * jnp.where(causal, jnp.exp2(d2), 0.0)).astype(BF16))
            xp = xs_ref[:, ps]
            zero = jnp.zeros_like(xp)
            rhs = jnp.concatenate([jnp.where(first_head, xp, zero), jnp.where(first_head, zero, xp)], axis=0)
            yd = _dot(jnp.concatenate(ms, axis=1), rhs)
            yacc_ref[:, ps] = yd + y_off[:, ps.start - gs.start:ps.stop - gs.start] * eacs_ref[:, ps] \
                + dsk_ref[:, ps] * act_ref[:, ps]
        ht_ref[:, gs] = eacs_ref[q - 1:q, gs] * ht_ref[:, gs] + st_t

    ss = jnp.zeros((q, 1), F32)
    for j in range(0, M_INNER, cs):
        sl = slice(j, j + cs)
        gt = yacc_ref[:, sl] * _silu(z_ref[:, sl])
        yacc_ref[:, sl] = gt
        ss = ss + jnp.sum(gt * gt, axis=-1, keepdims=True)
    inv = lax.rsqrt(ss * (1.0 / M_INNER) + EPS)
    for j in range(0, M_INNER, cs):
        sl = slice(j, j + cs)
        y_ref[:, sl] = yacc_ref[:, sl] * inv * nw_ref[:, sl]


def _ssd(z, act, dt, alog, dsk, nw, e64, *, nbatch, seq):
    q = M_CHUNK
    k = SSD_CHUNKS_PER_STEP
    assert seq % (k * q) == 0
    nc = seq // (k * q)
    row = lambda n: pl.BlockSpec((k * q, n), lambda b, c: (b * nc + c, 0))
    return pl.pallas_call(
        _ssd_kernel,
        grid=(nbatch, nc),
        in_specs=[row(M_INNER), row(M_CONV_DIM), row(LANES), _const_spec(alog.shape),
                  _const_spec(dsk.shape), _const_spec(nw.shape), _const_spec(e64.shape)],
        out_specs=[row(M_INNER), pl.BlockSpec((None, M_INNER, M_STATE), lambda b, c: (b, 0, 0))],
        out_shape=[jax.ShapeDtypeStruct((nbatch * seq, M_INNER), F32),
                   jax.ShapeDtypeStruct((nbatch, M_INNER, M_STATE), F32)],
        scratch_shapes=[pltpu.VMEM((M_STATE, M_INNER), F32), pltpu.VMEM((k, q, M_INNER), BF16),
                        pltpu.VMEM((k, q, M_INNER), BF16), pltpu.VMEM((k, q, M_INNER), F32),
                        pltpu.VMEM((k, q, M_INNER), F32)],
        compiler_params=_params(("parallel", "arbitrary")),
        name="ssd",
    )(z, act, dt, alog, dsk, nw, e64)


SSD_STEP_BB = 8


def _ssd_step_kernel(z_ref, xbc_ref, dt_ref, cbuf_ref, h0_ref, cw_ref, cb_ref, alog_ref, dsk_ref, nw_ref,
                     e64_ref, y_ref, hout_ref, convout_ref):
    bb = SSD_STEP_BB
    x_new = xbc_ref[...]
    conv = cb_ref[...] + x_new * cw_ref[M_CONV - 1:M_CONV, :]
    for k in range(M_CONV - 1):
        conv = conv + cbuf_ref[k] * cw_ref[k:k + 1, :]
    act = _silu(conv)
    for k in range(M_CONV - 2):
        convout_ref[k] = cbuf_ref[k + 1]
    convout_ref[M_CONV - 2] = x_new

    xs = act[:, :M_INNER]
    bmat = act[:, M_INNER:M_INNER + M_GROUPS * M_STATE]
    cmat = act[:, M_INNER + M_GROUPS * M_STATE:]
    dt = dt_ref[...]
    da = jnp.exp(dt * (-jnp.exp(alog_ref[...])))
    dt_e = _dot(_split3(dt), e64_ref[...])
    da_e = _dot(_split3(da), e64_ref[...])
    xdt = xs * dt_e

    hp = M_HPG * M_HEADDIM
    lane_group = lax.broadcasted_iota(jnp.int32, (M_GROUPS, M_INNER), 1) // hp
    row_group = lax.broadcasted_iota(jnp.int32, (M_GROUPS, M_INNER), 0)
    gmask = lane_group == row_group
    ones_rows = lax.broadcasted_iota(jnp.int32, (SUBLANES, M_STATE), 0) >= M_GROUPS

    cbx = []
    for g in range(M_GROUPS):
        cbg = jnp.sum(cmat[:, g * M_STATE:(g + 1) * M_STATE] * bmat[:, g * M_STATE:(g + 1) * M_STATE],
                      axis=-1, keepdims=True)
        cbx.append(jnp.broadcast_to(cbg, (bb, hp)))
    y = jnp.concatenate(cbx, axis=1) * xdt + dsk_ref[...] * xs

    y_off_rows = []
    for b in range(bb):
        xsel = jnp.where(gmask, jnp.broadcast_to(xdt[b:b + 1, :], (M_GROUPS, M_INNER)), 0.0)
        lhs = jnp.concatenate([xsel, *_pieces3(da_e[b:b + 1, :]), jnp.zeros((1, M_INNER), F32)],
                              axis=0).astype(BF16)
        bm4 = jnp.concatenate([bmat[b:b + 1, g * M_STATE:(g + 1) * M_STATE] for g in range(M_GROUPS)], axis=0)
        cm4 = jnp.concatenate([cmat[b:b + 1, g * M_STATE:(g + 1) * M_STATE] for g in range(M_GROUPS)], axis=0)
        zeros4 = jnp.zeros((SUBLANES - M_GROUPS, M_STATE), F32)
        rhs = jnp.concatenate([jnp.concatenate([bm4, zeros4], axis=0),
                               jnp.where(ones_rows, 1.0, 0.0)], axis=1).astype(BF16)
        res = lax.dot_general(lhs, rhs, _TN, preferred_element_type=F32)
        h0 = h0_ref[b]
        hout_ref[b] = res[:, M_STATE:] * h0 + res[:, :M_STATE]
        cm8 = jnp.concatenate([cm4, zeros4], axis=0).astype(BF16)
        yo = lax.dot_general(cm8, h0.astype(BF16), _NT, preferred_element_type=F32)
        y_off_rows.append(jnp.sum(jnp.where(gmask, yo[:M_GROUPS, :], 0.0), axis=0, keepdims=True))
    y = y + da_e * jnp.concatenate(y_off_rows, axis=0)

    y_ref[...] = _rms(y * _silu(z_ref[...]), nw_ref[...])


def _ssd_step(zs, xbc, dt, cbuf, h0, cw, cb, alog, dsk, nw, e64):
    n = zs.shape[0]
    bb = SSD_STEP_BB
    row = lambda w: pl.BlockSpec((bb, w), lambda i: (i, 0))
    blk3 = lambda s: pl.BlockSpec((bb,) + s, lambda i: (i, 0, 0))
    conv_spec = pl.BlockSpec((M_CONV - 1, bb, M_CONV_DIM), lambda i: (0, i, 0))
    return pl.pallas_call(
        _ssd_step_kernel,
        grid=(n // bb,),
        in_specs=[row(M_INNER), row(M_CONV_DIM), row(LANES), conv_spec, blk3((M_INNER, M_STATE)),
                  _const_spec(cw.shape), _const_spec(cb.shape), _const_spec(alog.shape),
                  _const_spec(dsk.shape), _const_spec(nw.shape), _const_spec(e64.shape)],
        out_specs=[row(M_INNER), blk3((M_INNER, M_STATE)), conv_spec],
        out_shape=[jax.ShapeDtypeStruct((n, M_INNER), F32),
                   jax.ShapeDtypeStruct((n, M_INNER, M_STATE), F32),
                   jax.ShapeDtypeStruct((M_CONV - 1, n, M_CONV_DIM), F32)],
        compiler_params=_params(("parallel",)),
        name="ssd_step",
    )(zs, xbc, dt, cbuf, h0, cw, cb, alog, dsk, nw, e64)


MXU_WIDTH = 256
FFN_CHUNKS = (0, 6 * MXU_WIDTH, D_FF)


def _out_kernel(x_ref, *refs):
    y5_refs = refs[:S5_PIECES]
    (yb_ref, gate_ref, gtm_ref, shf_ref, scf_ref, gtf_ref, wglu_ref, bglu_ref, wb5_ref, wbs_ref, wout_ref,
     npm_ref, npf_ref, npo_ref, wfi_ref, wfo_ref, o_ref) = refs[S5_PIECES:]
    rows = x_ref.shape[0]
    y5 = jnp.concatenate([y5_refs[k][0:rows, c * LANES:(c + 1) * LANES]
                          for c in range(S5_CHUNKS) for k in range(S5_PIECES)], axis=1)
    ya = _gelu_tanh(y5)
    glu = ya * _sigmoid(_dot(ya.astype(BF16), wglu_ref[...]) + bglu_ref[...])
    merged = (_sigmoid(gate_ref[:, :D_MODEL]) * _dot(glu.astype(BF16), wb5_ref[...])
              + _sigmoid(gate_ref[:, D_MODEL:]) * _dot(yb_ref[...].astype(BF16), wbs_ref[...]))
    mix = _dot(merged.astype(BF16), wout_ref[...])
    x1 = x_ref[...] + gtm_ref[...] * _rms(mix, npm_ref[...])

    hb = (_rms(x1, npf_ref[...]) * (1.0 + scf_ref[...]) + shf_ref[...]).astype(BF16)
    f = jnp.zeros_like(x1)
    for j, e in zip(FFN_CHUNKS[:-1], FFN_CHUNKS[1:]):
        gg = _dot(hb, wfi_ref[:, j:e])
        uu = _dot(hb, wfi_ref[:, D_FF + j:D_FF + e])
        f = f + _dot((_silu(gg) * uu).astype(BF16), wfo_ref[j:e, :])
    o_ref[...] = x1 + gtf_ref[...] * _rms(f, npo_ref[...])


def _out_stage(x, y5, yb, gates, mod, ws, *, per_row, tm, rows_per_batch):
    t = x.shape[0]
    row = lambda n: pl.BlockSpec((tm, n), lambda i: (i, 0))
    mods = _mod_specs(per_row, tm, rows_per_batch // tm if not per_row else 1, (2, 3, 4, 5))
    _, y5_spec = _s5_layout(per_row, t, tm, rows_per_batch, D_MODEL // S5_PIECES)
    return pl.pallas_call(
        _out_kernel,
        grid=(t // tm,),
        in_specs=[row(D_MODEL)] + [y5_spec] * S5_PIECES + [row(M_INNER), row(2 * D_MODEL)] + mods
                 + [_const_spec(w.shape) for w in ws],
        out_specs=row(D_MODEL),
        out_shape=jax.ShapeDtypeStruct((t, D_MODEL), F32),
        compiler_params=_params(("parallel",)),
        name="out_stage",
    )(x, *y5, yb, gates, mod, mod, mod, mod, *ws)


def _pad_lanes(v):
    return jnp.pad(v.reshape(1, -1), ((0, 0), (0, LANES - v.shape[-1])))


def kernel(x_prompt, x_sample, state_s5_re, state_s5_im, state_ssm, state_conv, c_prompt, c_sample, w_ada, b_ada, norm_pre_mix, norm_post_mix, norm_pre_ffn, norm_post_ffn, w_in, s5_lam_re, s5_lam_im, s5_log_dt, s5_b_re, s5_b_im, s5_c_re, s5_c_im, s5_d, s5_w_glu, s5_b_glu, m_conv_w, m_conv_b, m_dt_bias, m_a_log, m_d, m_norm, w_branch_s5, w_branch_ssd, w_out, w_ffn_in, w_ffn_out):
    bp, seq, _ = x_prompt.shape
    bs = x_sample.shape[0]
    assert x_sample.shape[1] == 1 and w_ada.shape[0] == 1 and seq % M_CHUNK == 0
    row1 = lambda v: v.reshape(1, -1)

    ws_in = _split_w_in(w_in[0])
    ws_out = (s5_w_glu[0].astype(BF16), row1(s5_b_glu[0]), w_branch_s5[0].astype(BF16),
              w_branch_ssd[0].astype(BF16), w_out[0].astype(BF16), row1(norm_post_mix[0]),
              row1(norm_pre_ffn[0]), row1(norm_post_ffn[0]), w_ffn_in[0].astype(BF16), w_ffn_out[0].astype(BF16))
    cw, cb = m_conv_w[0], row1(m_conv_b[0])
    dtb, alog = _pad_lanes(m_dt_bias[0]), _pad_lanes(m_a_log[0])
    dsk = row1(jnp.repeat(m_d[0], M_HEADDIM))
    nw = row1(m_norm[0])
    e64 = _expand_mat()

    mod = _ada(jnp.concatenate([c_prompt, c_sample], axis=0), w_ada[0], b_ada[0])
    mod_p = mod[:bp].reshape(bp, 1, 6 * D_MODEL)
    mod_s = mod[bp:]

    ar, ai, bbr, bbi = _s5_disc(s5_lam_re[0], s5_lam_im[0], s5_log_dt[0], s5_b_re[0], s5_b_im[0])
    s5m = _s5_matrices(ar, ai, bbr, bbi, s5_c_re[0], s5_c_im[0], s5_d[0])
    g_pre = row1(norm_pre_mix[0])

    xp = x_prompt.reshape(bp * seq, D_MODEL)
    tm = min(ROW_TILE, seq)
    u, zs, act, dt, gates, tail = _in_proj(xp, mod_p, g_pre, ws_in, dtb, cw, cb,
                                           per_row=False, tm=tm, rows_per_batch=seq)
    p_conv = tail[:, SUBLANES - (M_CONV - 1):, :]
    zeros_s5 = jnp.zeros((bp, S5_GROUPS * S5_STATE), F32)
    y5, p_re, p_im = _s5(u, zeros_s5, zeros_s5, *s5m, tl=tm)
    yb, p_ssm = _ssd(zs, act, dt, alog, dsk, nw, e64, nbatch=bp, seq=seq)
    y_prompt = _out_stage(xp, y5, yb, gates, mod_p, ws_out, per_row=False, tm=tm, rows_per_batch=seq)

    xs = x_sample.reshape(bs, D_MODEL)
    u, zs, xbc, dt, gates = _in_proj(xs, mod_s, g_pre, ws_in, dtb, cw, cb, per_row=True, tm=bs, rows_per_batch=1)
    y5, s_re, s_im = _s5(u, state_s5_re[0].reshape(bs, -1), state_s5_im[0].reshape(bs, -1), *s5m, tl=1)
    yb, s_ssm, s_conv = _ssd_step(zs, xbc, dt, jnp.swapaxes(state_conv[0], 0, 1),
                                  state_ssm[0].reshape(bs, M_INNER, M_STATE),
                                  cw, cb, alog, dsk, nw, e64)
    y_sample = _out_stage(xs, y5, yb, gates, mod_s, ws_out, per_row=True, tm=bs, rows_per_batch=1)

    s5_shape = (1, -1, S5_GROUPS, S5_STATE)
    ssm_shape = (1, -1, M_HEADS, M_HEADDIM, M_STATE)
    return (y_prompt.reshape(bp, seq, D_MODEL), y_sample.reshape(bs, 1, D_MODEL),
            p_re.reshape(s5_shape), p_im.reshape(s5_shape), p_ssm.reshape(ssm_shape), p_conv[None],
            s_re.reshape(s5_shape), s_im.reshape(s5_shape), s_ssm.reshape(ssm_shape), jnp.swapaxes(s_conv, 0, 1)[None])
```

```python
import functools

import jax
import jax.numpy as jnp
from jax import lax
from jax.experimental import pallas as pl
from jax.experimental.pallas import tpu as pltpu

F32 = jnp.float32
BF16 = jnp.bfloat16

D_MODEL = 1024
EPS = 1e-6
S5_GROUP = 16
S5_GROUPS = 64
S5_STATE = 64
M_INNER = 2048
M_HEADDIM = 64
M_HEADS = 32
M_GROUPS = 4
M_HPG = 8
M_STATE = 128
M_CONV = 4
M_CONV_DIM = 3072
M_CHUNK = 128
D_FF = 2816
OFF_Z = D_MODEL
OFF_XBC = OFF_Z + M_INNER
OFF_DT = OFF_XBC + M_CONV_DIM
OFF_GA = OFF_DT + M_HEADS
IN_COLS = OFF_GA + 2 * D_MODEL

LANES = 128
SUBLANES = 8
VMEM_LIMIT = 56 * 1024 * 1024

S5_CH = 256
S5_NS = 1024
S5_CHUNKS = D_MODEL // S5_CH
S5_TILES = S5_NS // LANES
S5_PIECES = S5_CH // LANES
S5_SUB_ROWS = 256
ROW_TILE = 256
INPROJ_SLAB = 512
INPROJ_DEPTH = 3

_NT = (((1,), (1,)), ((), ()))
_TN = (((0,), (0,)), ((), ()))


def _const_spec(shape):
    nd = len(shape)
    return pl.BlockSpec(shape, lambda *_: (0,) * nd, pipeline_mode=pl.Buffered(1))


def _params(sem):
    return pltpu.CompilerParams(dimension_semantics=sem, vmem_limit_bytes=VMEM_LIMIT)


def _sigmoid(x):
    return 0.5 * jnp.tanh(0.5 * x) + 0.5


def _silu(x):
    hx = 0.5 * x
    return hx + hx * jnp.tanh(hx)


def _softplus(x):
    return jnp.maximum(x, 0.0) + jnp.log1p(jnp.exp(-jnp.abs(x)))


def _gelu_tanh(x):
    return 0.5 * x * (1.0 + jnp.tanh(0.7978845608028654 * (x + 0.044715 * (x * x * x))))


def _rms(x, g):
    return x * lax.rsqrt(jnp.mean(x * x, axis=-1, keepdims=True) + EPS) * g


def _dot(a, b):
    return jnp.dot(a, b, preferred_element_type=F32)


def _pieces3(x):
    hi = x.astype(BF16).astype(F32)
    r1 = x - hi
    mid = r1.astype(BF16).astype(F32)
    lo = (r1 - mid).astype(BF16).astype(F32)
    return hi, mid, lo


def _ordered_after(x, dep):
    zero = jnp.where(dep[:, 0:LANES] > jnp.inf, 1.0, 0.0)
    return x + jnp.concatenate([zero] * (x.shape[1] // LANES), axis=1)


def _split2(x):
    return jnp.concatenate(_pieces3(x)[:2], axis=-1).astype(BF16)


def _split3(x):
    return jnp.concatenate(_pieces3(x), axis=-1).astype(BF16)


def _ada_kernel(c_ref, w_ref, b_ref, o_ref):
    s = _silu(c_ref[...]).astype(BF16)
    o_ref[...] = _dot(s, w_ref[...].astype(BF16)) + b_ref[...]


def _ada(c, w, b):
    n = c.shape[0]
    tn = 1024
    return pl.pallas_call(
        _ada_kernel,
        grid=(6 * D_MODEL // tn,),
        in_specs=[_const_spec((n, D_MODEL)),
                  pl.BlockSpec((D_MODEL, tn), lambda j: (0, j)),
                  pl.BlockSpec((1, tn), lambda j: (0, j))],
        out_specs=pl.BlockSpec((n, tn), lambda j: (0, j)),
        out_shape=jax.ShapeDtypeStruct((n, 6 * D_MODEL), F32),
        compiler_params=_params(("arbitrary",)),
        name="ada",
    )(c, w, b.reshape(1, -1))


def _s5_disc_kernel(lr_ref, li_ref, ldt_ref, br_ref, bi_ref, ar_ref, ai_ref, bbr_ref, bbi_ref):
    lr = lr_ref[...]
    li = li_ref[...]
    dt = jnp.exp(ldt_ref[...])
    mag = jnp.exp(lr * dt)
    ar = mag * jnp.cos(li * dt)
    ai = mag * jnp.sin(li * dt)
    den = lr * lr + li * li
    nr = ar - 1.0
    f_re = (nr * lr + ai * li) / den
    f_im = (ai * lr - nr * li) / den
    br = br_ref[...]
    bi = bi_ref[...]
    ar_ref[...] = ar
    ai_ref[...] = ai
    bbr_ref[...] = f_re * br - f_im * bi
    bbi_ref[...] = f_re * bi + f_im * br


def _s5_disc(lam_re, lam_im, log_dt, b_re, b_im):
    g, n = lam_re.shape
    w = S5_GROUP * n
    tile = lambda a: jnp.tile(a, (1, S5_GROUP))
    ldt = jnp.broadcast_to(log_dt[:, None], (g, w))
    brt = jnp.swapaxes(b_re, 1, 2).reshape(g, w)
    bit = jnp.swapaxes(b_im, 1, 2).reshape(g, w)
    shp = jax.ShapeDtypeStruct((g, w), F32)
    ar, ai, bbr, bbi = pl.pallas_call(
        _s5_disc_kernel,
        out_shape=(shp, shp, shp, shp),
        name="s5_disc",
    )(tile(lam_re), tile(lam_im), ldt, brt, bit)
    return (ar[:, :n], ai[:, :n], bbr.reshape(g, S5_GROUP, n), bbi.reshape(g, S5_GROUP, n))


def _s5_matrices(abar_re, abar_im, bbar_re, bbar_im, c_re, c_im, d_skip):
    ns = 2 * S5_NS
    b = jnp.concatenate([bbar_re.reshape(D_MODEL, S5_STATE), bbar_im.reshape(D_MODEL, S5_STATE)],
                        axis=1).reshape(S5_CHUNKS, S5_CH, 2 * S5_STATE)

    def by_tile(c):
        return jnp.swapaxes(c, 1, 2).reshape(S5_CHUNKS, S5_TILES, 1, 2, S5_STATE, S5_GROUP)
    c = jnp.concatenate([by_tile(c_re), by_tile(-c_im)], axis=2).reshape(S5_CHUNKS, ns, S5_GROUP)

    by_tile_a = lambda v: v.reshape(S5_CHUNKS, S5_TILES, 1, LANES)
    a = jnp.concatenate([by_tile_a(abar_re), by_tile_a(abar_im)], axis=2).reshape(S5_CHUNKS, 1, ns)
    d = d_skip.reshape(S5_CHUNKS, 1, S5_CH)
    return b, a, c, d


def _s5_state_group(i):
    return 2 * (i // (2 * LANES)) + (i % LANES) // S5_STATE


def _s5_expand_operands(b_ref, c_ref, bm_ref, cm_ref):
    tw = 2 * LANES
    row_g = lax.broadcasted_iota(jnp.int32, (S5_CH, LANES), 0) // S5_GROUP
    lane_half = lax.broadcasted_iota(jnp.int32, (S5_CH, LANES), 1) // S5_STATE
    for part in range(2):
        bp = b_ref[:, part * S5_STATE:(part + 1) * S5_STATE]
        pattern = jnp.concatenate([bp, bp], axis=1)
        for j in range(S5_TILES):
            tile = jnp.where(row_g == 2 * j + lane_half, pattern, 0.0)
            bm_ref[:, j * tw + part * LANES:j * tw + (part + 1) * LANES] = tile.astype(BF16)
    spread = (lax.broadcasted_iota(jnp.int32, (S5_GROUP, S5_CH), 1) % S5_GROUP
              == lax.broadcasted_iota(jnp.int32, (S5_GROUP, S5_CH), 0)).astype(BF16)
    rows = 512
    for r0 in range(0, 2 * S5_NS, rows):
        tiled = _dot(c_ref[r0:r0 + rows, :].astype(BF16), spread)
        own = (_s5_state_group(lax.broadcasted_iota(jnp.int32, (rows, S5_CH), 0) + r0)
               == lax.broadcasted_iota(jnp.int32, (rows, S5_CH), 1) // S5_GROUP)
        cm_ref[r0:r0 + rows, :] = jnp.where(own, tiled, 0.0).astype(BF16)


def _split_w_in_kernel(wt_ref, wm_ref, wdt_ref, wg_ref):
    tk = wt_ref.shape[1]
    step = 1024
    for j in range(0, OFF_DT, step):
        wm_ref[:, j:j + step] = wt_ref[j:j + step, :].T.astype(BF16)
    head = lax.broadcasted_iota(jnp.int32, (tk, LANES), 1) < M_HEADS
    wdt_ref[...] = jnp.where(head, wt_ref[OFF_DT:OFF_DT + LANES, :].T, 0.0).astype(BF16)
    for j in range(0, 2 * D_MODEL, step):
        wg_ref[:, j:j + step] = wt_ref[OFF_GA + j:OFF_GA + j + step, :].T.astype(BF16)


def _split_w_in(w):
    k = w.shape[0]
    tk = 256
    outs = (OFF_DT, LANES, 2 * D_MODEL)
    return pl.pallas_call(
        _split_w_in_kernel,
        grid=(k // tk,),
        in_specs=[pl.BlockSpec((IN_COLS, tk), lambda i: (0, i))],
        out_specs=[pl.BlockSpec((tk, n), lambda i: (i, 0)) for n in outs],
        out_shape=[jax.ShapeDtypeStruct((k, n), BF16) for n in outs],
        compiler_params=_params(("parallel",)),
        name="split_w_in",
    )(jnp.swapaxes(w, 0, 1))


def _inproj_kernel(x_ref, sh_ref, sc_ref, g_ref, wm_ref, wdt_ref, wg_ref, dtb_ref, cw_ref, cb_ref,
                   u_ref, z_ref, xbc_ref, dt_ref, gate_ref, *rest, tiles_per_seq):
    accs = rest[-INPROJ_DEPTH:]
    h = _rms(x_ref[...], g_ref[...]) * (1.0 + sc_ref[...]) + sh_ref[...]
    hb = h.astype(BF16)
    rows = hb.shape[0]
    step = INPROJ_SLAB

    if u_ref.shape[0] > rows:
        u_ref[rows:, :] = jnp.zeros((u_ref.shape[0] - rows, D_MODEL), F32)

    if tiles_per_seq is None:
        conv_fn = None
    else:
        tail_ref, carry_ref = rest[:2]

        @pl.when(pl.program_id(0) % tiles_per_seq == 0)
        def _():
            carry_ref[...] = jnp.zeros_like(carry_ref)

        def conv_fn(r, sl):
            width = sl.stop - sl.start
            w = [cw_ref[k:k + 1, sl] for k in range(M_CONV)]
            bias = cb_ref[:, sl]
            row = lax.broadcasted_iota(jnp.int32, (SUBLANES, width), 0)
            xp = carry_ref[:, sl]
            farp = xp * w[1] + pltpu.roll(xp, 1, axis=0) * w[0]
            outs = []
            for i in range(rows // SUBLANES):
                xi = r[i * SUBLANES:(i + 1) * SUBLANES, :]
                if outs:
                    xi = _ordered_after(xi, outs[-1])
                s1 = jnp.where(row == 0, pltpu.roll(xp, 1, axis=0), pltpu.roll(xi, 1, axis=0))
                far = xi * w[1] + s1 * w[0]
                s2 = jnp.where(row < 2, pltpu.roll(farp, 2, axis=0), pltpu.roll(far, 2, axis=0))
                outs.append(_silu(xi * w[3] + s1 * w[2] + s2 + bias))
                xp, farp = xi, far
            carry_ref[:, sl] = r[rows - SUBLANES:, :]
            tail_ref[:, sl] = r[rows - SUBLANES:, :]
            return jnp.concatenate(outs, axis=0)

    def slabs(o_ref, w_ref, w_off, fn):
        n = o_ref.shape[-1]
        return [(o_ref, slice(j, j + min(step, n - j)), w_ref, w_off + j, fn) for j in range(0, n, step)]

    su = slabs(u_ref, wm_ref, 0, None)
    sz = slabs(z_ref, wm_ref, OFF_Z, None)
    sx = slabs(xbc_ref, wm_ref, OFF_XBC, conv_fn)
    sd = slabs(dt_ref, wdt_ref, 0, lambda r, sl: _softplus(r + dtb_ref[:, sl]))
    sg = slabs(gate_ref, wg_ref, 0, None)
    light = su + sz + sg + sd
    order = []
    while sx or light:
        if sx:
            order.append(sx.pop(0))
        if light:
            order.append(light.pop(0))

    def matmul(s):
        o_ref, sl, w_ref, w_col, _ = order[s]
        accs[s % INPROJ_DEPTH][:, 0:sl.stop - sl.start] = _dot(hb, w_ref[:, w_col:w_col + sl.stop - sl.start])

    for s in range(min(INPROJ_DEPTH - 1, len(order))):
        matmul(s)
    for s, (o_ref, sl, _, _, fn) in enumerate(order):
        if s + INPROJ_DEPTH - 1 < len(order):
            matmul(s + INPROJ_DEPTH - 1)
        r = accs[s % INPROJ_DEPTH][:, 0:sl.stop - sl.start]
        o_ref[0:rows, sl] = r if fn is None else fn(r, sl)


def _mod_specs(per_row, tm, rows_per_mod, cols):
    if per_row:
        return [pl.BlockSpec((tm, D_MODEL), functools.partial(lambda i, c: (i, c), c=c)) for c in cols]
    return [pl.BlockSpec((None, 1, D_MODEL), functools.partial(lambda i, c: (i // rows_per_mod, 0, c), c=c))
            for c in cols]


def _s5_layout(per_row, t, tm, rows_per_batch, width):
    if per_row:
        return (1, t, width), pl.BlockSpec((None, tm, width), lambda i: (0, i, 0))
    nblk, pitch = rows_per_batch // tm, _s5_pitch(tm)
    return ((nblk, (t // rows_per_batch) * pitch, width),
            pl.BlockSpec((None, pitch, width), lambda i: (i % nblk, i // nblk, 0)))


def _in_proj(x, mod, g, ws, dtb, cw, cb, *, per_row, tm, rows_per_batch):
    t = x.shape[0]
    row = lambda n: pl.BlockSpec((tm, n), lambda i: (i, 0))
    outs = (M_INNER, M_CONV_DIM, LANES, 2 * D_MODEL)
    u_shape, u_spec = _s5_layout(per_row, t, tm, rows_per_batch, D_MODEL)
    out_specs = [u_spec] + [row(n) for n in outs]
    out_shape = [jax.ShapeDtypeStruct(u_shape, F32)] + [jax.ShapeDtypeStruct((t, n), F32) for n in outs]
    scratch = []
    tiles_per_seq = None
    if not per_row:
        tiles_per_seq = rows_per_batch // tm
        out_specs.append(pl.BlockSpec((None, SUBLANES, M_CONV_DIM), lambda i: (i // tiles_per_seq, 0, 0)))
        out_shape.append(jax.ShapeDtypeStruct((t // rows_per_batch, SUBLANES, M_CONV_DIM), F32))
        scratch.append(pltpu.VMEM((SUBLANES, M_CONV_DIM), F32))
    scratch += [pltpu.VMEM((tm, INPROJ_SLAB), F32)] * INPROJ_DEPTH
    consts = (g,) + tuple(ws) + (dtb, cw, cb)
    return pl.pallas_call(
        functools.partial(_inproj_kernel, tiles_per_seq=tiles_per_seq),
        grid=(t // tm,),
        in_specs=[row(D_MODEL)] + _mod_specs(per_row, tm, rows_per_batch // tm if not per_row else 1, (0, 1))
                 + [_const_spec(w.shape) for w in consts],
        out_specs=out_specs,
        out_shape=out_shape,
        scratch_shapes=scratch,
        compiler_params=_params(("arbitrary",)),
        name="in_proj",
    )(x, mod, mod, *consts)


def _s5_pitch(tl):
    return tl if tl == 1 else tl + SUBLANES


def _s5_kernel(*refs, nb, tl, sub_rows):
    nu = S5_PIECES
    u_refs = refs[:nu]
    h0r_ref, h0i_ref, b_ref, a_ref, c_ref, d_ref = refs[nu:nu + 6]
    y_refs = refs[nu + 6:2 * nu + 6]
    hr_ref, hi_ref, hst_ref, bu0_ref, xs0_ref, bu1_ref, xs1_ref, bm_ref, cm_ref = refs[2 * nu + 6:]
    bufs = ((bu0_ref, xs0_ref), (bu1_ref, xs1_ref))
    tb = pl.program_id(1)
    tsub = sub_rows // nb
    tw = 2 * LANES
    pitch = _s5_pitch(tl)

    def step_rows(t):
        return pl.ds(t, nb, stride=pitch) if pitch > 1 else pl.ds(0, nb)

    @pl.when(tb == 0)
    def _():
        _s5_expand_operands(b_ref, c_ref, bm_ref, cm_ref)
        for j in range(S5_TILES):
            hst_ref[:, j * tw:j * tw + LANES] = h0r_ref[:, j * LANES:(j + 1) * LANES]
            hst_ref[:, j * tw + LANES:(j + 1) * tw] = h0i_ref[:, j * LANES:(j + 1) * LANES]

    ar = [jnp.broadcast_to(a_ref[:, j * tw:j * tw + LANES], (nb, LANES)) for j in range(S5_TILES)]
    ai = [jnp.broadcast_to(a_ref[:, j * tw + LANES:(j + 1) * tw], (nb, LANES)) for j in range(S5_TILES)]
    dsk = d_ref[...]

    def project_in(t0, bu_ref, xs_ref):
        xs = jnp.concatenate([jnp.concatenate([r[step_rows(t0 + i), :] for r in u_refs], axis=1)
                              for i in range(tsub)], axis=0)
        xs_ref[...] = xs
        lhs = xs.astype(BF16)
        for j in range(S5_TILES):
            bu_ref[:, j * tw:(j + 1) * tw] = _dot(lhs, bm_ref[:, j * tw:(j + 1) * tw])

    def scan_project_out(t0, bu_ref, xs_ref, carry):
        new, hs = [], []
        for j in range(S5_TILES):
            hr, hi = carry[2 * j], carry[2 * j + 1]
            rows = []
            for i in range(tsub):
                br = bu_ref[i * nb:(i + 1) * nb, j * tw:j * tw + LANES]
                bi = bu_ref[i * nb:(i + 1) * nb, j * tw + LANES:(j + 1) * tw]
                hr, hi = ar[j] * hr - ai[j] * hi + br, ar[j] * hi + ai[j] * hr + bi
                rows.append(jnp.concatenate([hr, hi], axis=1))
            new += [hr, hi]
            hs.append(jnp.concatenate(rows, axis=0).astype(BF16))
        y = _dot(jnp.concatenate(hs, axis=1), cm_ref[...]) + dsk * xs_ref[...]
        for i in range(tsub):
            for k, y_ref in enumerate(y_refs):
                y_ref[step_rows(t0 + i), :] = y[i * nb:(i + 1) * nb, k * LANES:(k + 1) * LANES]
        return tuple(new)

    h_in = []
    for j in range(S5_TILES):
        h_in += [hst_ref[:, j * tw:j * tw + LANES], hst_ref[:, j * tw + LANES:(j + 1) * tw]]
    nsub = tl // tsub
    (bu0, xs0), (bu1, xs1) = bufs
    project_in(0, bu0, xs0)
    if nsub == 1:
        h_out = scan_project_out(0, bu0, xs0, tuple(h_in))
    else:
        def pair(p, carry):
            t0 = 2 * p * tsub
            project_in(t0 + tsub, bu1, xs1)
            carry = scan_project_out(t0, bu0, xs0, carry)
            project_in(jnp.minimum(t0 + 2 * tsub, tl - tsub), bu0, xs0)
            return scan_project_out(t0 + tsub, bu1, xs1, carry)
        h_out = lax.fori_loop(0, nsub // 2, pair, tuple(h_in))
    for j in range(S5_TILES):
        hst_ref[:, j * tw:j * tw + LANES] = h_out[2 * j]
        hst_ref[:, j * tw + LANES:(j + 1) * tw] = h_out[2 * j + 1]

    if pitch > tl:
        for y_ref in y_refs:
            for b in range(nb):
                y_ref[b * pitch + tl:(b + 1) * pitch, :] = jnp.zeros((pitch - tl, LANES), F32)

    @pl.when(tb == pl.num_programs(1) - 1)
    def _():
        for j in range(S5_TILES):
            hr_ref[:, j * LANES:(j + 1) * LANES] = h_out[2 * j]
            hi_ref[:, j * LANES:(j + 1) * LANES] = h_out[2 * j + 1]


def _s5(u, h0r, h0i, b, a, c, d, *, tl):
    nblk, rows, _ = u.shape
    pitch = _s5_pitch(tl)
    nb = rows // pitch
    sub_rows = min(S5_SUB_ROWS, tl * nb)
    nsub = tl * nb // sub_rows
    assert sub_rows % nb == 0 and tl * nb % sub_rows == 0 and (nsub == 1 or nsub % 2 == 0)
    chunk3 = lambda s: pl.BlockSpec((None,) + s, lambda c, t: (c, 0, 0))
    st_spec = pl.BlockSpec((nb, S5_NS), lambda c, t: (0, c))
    nu = S5_PIECES
    u_specs = [pl.BlockSpec((None, rows, LANES), functools.partial(lambda c, t, k: (t, 0, nu * c + k), k=k))
               for k in range(nu)]
    y_spec = pl.BlockSpec((None, rows, LANES), lambda c, t: (t, 0, c))
    y_shape = jax.ShapeDtypeStruct((nblk, rows, D_MODEL // nu), F32)
    res = pl.pallas_call(
        functools.partial(_s5_kernel, nb=nb, tl=tl, sub_rows=sub_rows),
        grid=(S5_CHUNKS, nblk),
        in_specs=u_specs + [st_spec, st_spec, chunk3((S5_CH, 2 * S5_STATE)), chunk3((1, 2 * S5_NS)),
                            chunk3((2 * S5_NS, S5_GROUP)), chunk3((1, S5_CH))],
        out_specs=[y_spec] * nu + [st_spec, st_spec],
        out_shape=[y_shape] * nu + [jax.ShapeDtypeStruct(h0r.shape, F32), jax.ShapeDtypeStruct(h0i.shape, F32)],
        scratch_shapes=[pltpu.VMEM((nb, 2 * S5_NS), F32)]
                       + [pltpu.VMEM((sub_rows, 2 * S5_NS), F32), pltpu.VMEM((sub_rows, S5_CH), F32)] * 2
                       + [pltpu.VMEM((S5_CH, 2 * S5_NS), BF16), pltpu.VMEM((2 * S5_NS, S5_CH), BF16)],
        compiler_params=_params(("parallel", "arbitrary")),
        name="s5",
    )(*([u] * nu), h0r, h0i, b, a, c, d)
    return res[:nu], res[nu], res[nu + 1]


def _expand_mat():
    r = lax.broadcasted_iota(jnp.int32, (3 * LANES, M_HEADS * M_HEADDIM), 0) % LANES
    c = lax.broadcasted_iota(jnp.int32, (3 * LANES, M_HEADS * M_HEADDIM), 1) // M_HEADDIM
    return (r == c).astype(BF16)


LOG2E = 1.4426950408889634
SSD_CHUNKS_PER_STEP = 4


def _ssd_kernel(z_ref, act_ref, dt_ref, alog_ref, dsk_ref, nw_ref, e64_ref,
                y_ref, hout_ref,
                ht_ref, xs_ref, xdte_ref, eacs_ref, yacc_ref):
    c = pl.program_id(1)

    @pl.when(c == 0)
    def _():
        ht_ref[...] = jnp.zeros_like(ht_ref)

    for k in range(SSD_CHUNKS_PER_STEP):
        rows = pl.ds(k * M_CHUNK, M_CHUNK)
        _ssd_chunk(z_ref.at[rows], act_ref.at[rows], dt_ref.at[rows], alog_ref, dsk_ref, nw_ref, e64_ref,
                   y_ref.at[rows], ht_ref, xs_ref.at[k], xdte_ref.at[k], eacs_ref.at[k], yacc_ref.at[k])

    @pl.when(c == pl.num_programs(1) - 1)
    def _():
        for j in range(0, M_INNER, LANES):
            hout_ref[j:j + LANES, :] = ht_ref[:, j:j + LANES].T


def _ssd_chunk(z_ref, act_ref, dt_ref, alog_ref, dsk_ref, nw_ref, e64_ref,
               y_ref, ht_ref, xs_ref, xdte_ref, eacs_ref, yacc_ref):
    q = M_CHUNK
    cs = 512

    dt = dt_ref[...]
    a2 = dt * (-jnp.exp(alog_ref[...]) * LOG2E)
    ri = lax.broadcasted_iota(jnp.int32, (q, q), 0)
    ci = lax.broadcasted_iota(jnp.int32, (q, q), 1)
    causal = ri >= ci
    acs = jnp.dot(causal.astype(F32), a2, preferred_element_type=F32, precision=lax.Precision.HIGHEST)
    src_t = (acs - jnp.log2(dt)).T
    eacs2 = _split2(jnp.exp2(acs))
    dte2 = _split2(dt * jnp.exp2(acs[q - 1:q, :] - acs))
    for j in range(0, M_INNER, cs):
        sl = slice(j, j + cs)
        e = e64_ref[0:2 * LANES, sl]
        xs = act_ref[:, sl]
        xs_ref[:, sl] = xs.astype(BF16)
        xdte_ref[:, sl] = (xs * _dot(dte2, e)).astype(BF16)
        eacs_ref[:, sl] = _dot(eacs2, e)

    hp = M_HPG * M_HEADDIM
    lane = lax.broadcasted_iota(jnp.int32, (q, LANES), 1)
    first_head = lane < M_HEADDIM
    for g in range(M_GROUPS):
        gs = slice(g * hp, (g + 1) * hp)
        bm_t = act_ref[:, M_INNER + g * M_STATE:M_INNER + (g + 1) * M_STATE].T.astype(BF16)
        cm = act_ref[:, M_INNER + (M_GROUPS + g) * M_STATE:M_INNER + (M_GROUPS + g + 1) * M_STATE].astype(BF16)
        cbm = _dot(cm, bm_t)
        y_off = _dot(cm, ht_ref[:, gs].astype(BF16))
        st_t = _dot(bm_t, xdte_ref[:, gs])
        for pr in range(M_HPG // 2):
            h0 = g * M_HPG + 2 * pr
            ps = slice(h0 * M_HEADDIM, (h0 + 2) * M_HEADDIM)
            ms = []
            for h in (h0, h0 + 1):
                d2 = jnp.broadcast_to(acs[:, h:h + 1], (q, q)) - src_t[h:h + 1, :]
                ms.append((cbm * jnp.where(causal, jnp.exp2(d2), 0.0)).astype(BF16))
            xp = xs_ref[:, ps]
            zero = jnp.zeros_like(xp)
            rhs = jnp.concatenate([jnp.where(first_head, xp, zero), jnp.where(first_head, zero, xp)], axis=0)
            yd = _dot(jnp.concatenate(ms, axis=1), rhs)
            yacc_ref[:, ps] = yd + y_off[:, ps.start - gs.start:ps.stop - gs.start] * eacs_ref[:, ps] \
                + dsk_ref[:, ps] * act_ref[:, ps]
        ht_ref[:, gs] = eacs_ref[q - 1:q, gs] * ht_ref[:, gs] + st_t

    ss = jnp.zeros((q, 1), F32)
    for j in range(0, M_INNER, cs):
        sl = slice(j, j + cs)
        gt = yacc_ref[:, sl] * _silu(z_ref[:, sl])
        yacc_ref[:, sl] = gt
        ss = ss + jnp.sum(gt * gt, axis=-1, keepdims=True)
    inv = lax.rsqrt(ss * (1.0 / M_INNER) + EPS)
    for j in range(0, M_INNER, cs):
        sl = slice(j, j + cs)
        y_ref[:, sl] = yacc_ref[:, sl] * inv * nw_ref[:, sl]


def _ssd(z, act, dt, alog, dsk, nw, e64, *, nbatch, seq):
    q = M_CHUNK
    k = SSD_CHUNKS_PER_STEP
    assert seq % (k * q) == 0
    nc = seq // (k * q)
    row = lambda n: pl.BlockSpec((k * q, n), lambda b, c: (b * nc + c, 0))
    return pl.pallas_call(
        _ssd_kernel,
        grid=(nbatch, nc),
        in_specs=[row(M_INNER), row(M_CONV_DIM), row(LANES), _const_spec(alog.shape),
                  _const_spec(dsk.shape), _const_spec(nw.shape), _const_spec(e64.shape)],
        out_specs=[row(M_INNER), pl.BlockSpec((None, M_INNER, M_STATE), lambda b, c: (b, 0, 0))],
        out_shape=[jax.ShapeDtypeStruct((nbatch * seq, M_INNER), F32),
                   jax.ShapeDtypeStruct((nbatch, M_INNER, M_STATE), F32)],
        scratch_shapes=[pltpu.VMEM((M_STATE, M_INNER), F32), pltpu.VMEM((k, q, M_INNER), BF16),
                        pltpu.VMEM((k, q, M_INNER), BF16), pltpu.VMEM((k, q, M_INNER), F32),
                        pltpu.VMEM((k, q, M_INNER), F32)],
        compiler_params=_params(("parallel", "arbitrary")),
        name="ssd",
    )(z, act, dt, alog, dsk, nw, e64)


SSD_STEP_BB = 8


def _ssd_step_kernel(z_ref, xbc_ref, dt_ref, cbuf_ref, h0_ref, cw_ref, cb_ref, alog_ref, dsk_ref, nw_ref,
                     e64_ref, y_ref, hout_ref, convout_ref):
    bb = SSD_STEP_BB
    x_new = xbc_ref[...]
    conv = cb_ref[...] + x_new * cw_ref[M_CONV - 1:M_CONV, :]
    for k in range(M_CONV - 1):
        conv = conv + cbuf_ref[k] * cw_ref[k:k + 1, :]
    act = _silu(conv)
    for k in range(M_CONV - 2):
        convout_ref[k] = cbuf_ref[k + 1]
    convout_ref[M_CONV - 2] = x_new

    xs = act[:, :M_INNER]
    bmat = act[:, M_INNER:M_INNER + M_GROUPS * M_STATE]
    cmat = act[:, M_INNER + M_GROUPS * M_STATE:]
    dt = dt_ref[...]
    da = jnp.exp(dt * (-jnp.exp(alog_ref[...])))
    dt_e = _dot(_split3(dt), e64_ref[...])
    da_e = _dot(_split3(da), e64_ref[...])
    xdt = xs * dt_e

    hp = M_HPG * M_HEADDIM
    lane_group = lax.broadcasted_iota(jnp.int32, (M_GROUPS, M_INNER), 1) // hp
    row_group = lax.broadcasted_iota(jnp.int32, (M_GROUPS, M_INNER), 0)
    gmask = lane_group == row_group
    ones_rows = lax.broadcasted_iota(jnp.int32, (SUBLANES, M_STATE), 0) >= M_GROUPS

    cbx = []
    for g in range(M_GROUPS):
        cbg = jnp.sum(cmat[:, g * M_STATE:(g + 1) * M_STATE] * bmat[:, g * M_STATE:(g + 1) * M_STATE],
                      axis=-1, keepdims=True)
        cbx.append(jnp.broadcast_to(cbg, (bb, hp)))
    y = jnp.concatenate(cbx, axis=1) * xdt + dsk_ref[...] * xs

    y_off_rows = []
    for b in range(bb):
        xsel = jnp.where(gmask, jnp.broadcast_to(xdt[b:b + 1, :], (M_GROUPS, M_INNER)), 0.0)
        lhs = jnp.concatenate([xsel, *_pieces3(da_e[b:b + 1, :]), jnp.zeros((1, M_INNER), F32)],
                              axis=0).astype(BF16)
        bm4 = jnp.concatenate([bmat[b:b + 1, g * M_STATE:(g + 1) * M_STATE] for g in range(M_GROUPS)], axis=0)
        cm4 = jnp.concatenate([cmat[b:b + 1, g * M_STATE:(g + 1) * M_STATE] for g in range(M_GROUPS)], axis=0)
        zeros4 = jnp.zeros((SUBLANES - M_GROUPS, M_STATE), F32)
        rhs = jnp.concatenate([jnp.concatenate([bm4, zeros4], axis=0),
                               jnp.where(ones_rows, 1.0, 0.0)], axis=1).astype(BF16)
        res = lax.dot_general(lhs, rhs, _TN, preferred_element_type=F32)
        h0 = h0_ref[b]
        hout_ref[b] = res[:, M_STATE:] * h0 + res[:, :M_STATE]
        cm8 = jnp.concatenate([cm4, zeros4], axis=0).astype(BF16)
        yo = lax.dot_general(cm8, h0.astype(BF16), _NT, preferred_element_type=F32)
        y_off_rows.append(jnp.sum(jnp.where(gmask, yo[:M_GROUPS, :], 0.0), axis=0, keepdims=True))
    y = y + da_e * jnp.concatenate(y_off_rows, axis=0)

    y_ref[...] = _rms(y * _silu(z_ref[...]), nw_ref[...])


def _ssd_step(zs, xbc, dt, cbuf, h0, cw, cb, alog, dsk, nw, e64):
    n = zs.shape[0]
    bb = SSD_STEP_BB
    row = lambda w: pl.BlockSpec((bb, w), lambda i: (i, 0))
    blk3 = lambda s: pl.BlockSpec((bb,) + s, lambda i: (i, 0, 0))
    conv_spec = pl.BlockSpec((M_CONV - 1, bb, M_CONV_DIM), lambda i: (0, i, 0))
    return pl.pallas_call(
        _ssd_step_kernel,
        grid=(n // bb,),
        in_specs=[row(M_INNER), row(M_CONV_DIM), row(LANES), conv_spec, blk3((M_INNER, M_STATE)),
                  _const_spec(cw.shape), _const_spec(cb.shape), _const_spec(alog.shape),
                  _const_spec(dsk.shape), _const_spec(nw.shape), _const_spec(e64.shape)],
        out_specs=[row(M_INNER), blk3((M_INNER, M_STATE)), conv_spec],
        out_shape=[jax.ShapeDtypeStruct((n, M_INNER), F32),
                   jax.ShapeDtypeStruct((n, M_INNER, M_STATE), F32),
                   jax.ShapeDtypeStruct((M_CONV - 1, n, M_CONV_DIM), F32)],
        compiler_params=_params(("parallel",)),
        name="ssd_step",
    )(zs, xbc, dt, cbuf, h0, cw, cb, alog, dsk, nw, e64)


MXU_WIDTH = 256
FFN_CHUNKS = (0, 6 * MXU_WIDTH, D_FF)


def _out_kernel(x_ref, *refs):
    y5_refs = refs[:S5_PIECES]
    (yb_ref, gate_ref, gtm_ref, shf_ref, scf_ref, gtf_ref, wglu_ref, bglu_ref, wb5_ref, wbs_ref, wout_ref,
     npm_ref, npf_ref, npo_ref, wfi_ref, wfo_ref, o_ref) = refs[S5_PIECES:]
    rows = x_ref.shape[0]
    y5 = jnp.concatenate([y5_refs[k][0:rows, c * LANES:(c + 1) * LANES]
                          for c in range(S5_CHUNKS) for k in range(S5_PIECES)], axis=1)
    ya = _gelu_tanh(y5)
    glu = ya * _sigmoid(_dot(ya.astype(BF16), wglu_ref[...]) + bglu_ref[...])
    merged = (_sigmoid(gate_ref[:, :D_MODEL]) * _dot(glu.astype(BF16), wb5_ref[...])
              + _sigmoid(gate_ref[:, D_MODEL:]) * _dot(yb_ref[...].astype(BF16), wbs_ref[...]))
    mix = _dot(merged.astype(BF16), wout_ref[...])
    x1 = x_ref[...] + gtm_ref[...] * _rms(mix, npm_ref[...])

    hb = (_rms(x1, npf_ref[...]) * (1.0 + scf_ref[...]) + shf_ref[...]).astype(BF16)
    f = jnp.zeros_like(x1)
    for j, e in zip(FFN_CHUNKS[:-1], FFN_CHUNKS[1:]):
        gg = _dot(hb, wfi_ref[:, j:e])
        uu = _dot(hb, wfi_ref[:, D_FF + j:D_FF + e])
        f = f + _dot((_silu(gg) * uu).astype(BF16), wfo_ref[j:e, :])
    o_ref[...] = x1 + gtf_ref[...] * _rms(f, npo_ref[...])


def _out_stage(x, y5, yb, gates, mod, ws, *, per_row, tm, rows_per_batch):
    t = x.shape[0]
    row = lambda n: pl.BlockSpec((tm, n), lambda i: (i, 0))
    mods = _mod_specs(per_row, tm, rows_per_batch // tm if not per_row else 1, (2, 3, 4, 5))
    _, y5_spec = _s5_layout(per_row, t, tm, rows_per_batch, D_MODEL // S5_PIECES)
    return pl.pallas_call(
        _out_kernel,
        grid=(t // tm,),
        in_specs=[row(D_MODEL)] + [y5_spec] * S5_PIECES + [row(M_INNER), row(2 * D_MODEL)] + mods
                 + [_const_spec(w.shape) for w in ws],
        out_specs=row(D_MODEL),
        out_shape=jax.ShapeDtypeStruct((t, D_MODEL), F32),
        compiler_params=_params(("parallel",)),
        name="out_stage",
    )(x, *y5, yb, gates, mod, mod, mod, mod, *ws)


def _pad_lanes(v):
    return jnp.pad(v.reshape(1, -1), ((0, 0), (0, LANES - v.shape[-1])))


def kernel(x_prompt, x_sample, state_s5_re, state_s5_im, state_ssm, state_conv, c_prompt, c_sample, w_ada, b_ada, norm_pre_mix, norm_post_mix, norm_pre_ffn, norm_post_ffn, w_in, s5_lam_re, s5_lam_im, s5_log_dt, s5_b_re, s5_b_im, s5_c_re, s5_c_im, s5_d, s5_w_glu, s5_b_glu, m_conv_w, m_conv_b, m_dt_bias, m_a_log, m_d, m_norm, w_branch_s5, w_branch_ssd, w_out, w_ffn_in, w_ffn_out):
    bp, seq, _ = x_prompt.shape
    bs = x_sample.shape[0]
    assert x_sample.shape[1] == 1 and w_ada.shape[0] == 1 and seq % M_CHUNK == 0
    row1 = lambda v: v.reshape(1, -1)

    ws_in = _split_w_in(w_in[0])
    ws_out = (s5_w_glu[0].astype(BF16), row1(s5_b_glu[0]), w_branch_s5[0].astype(BF16),
              w_branch_ssd[0].astype(BF16), w_out[0].astype(BF16), row1(norm_post_mix[0]),
              row1(norm_pre_ffn[0]), row1(norm_post_ffn[0]), w_ffn_in[0].astype(BF16), w_ffn_out[0].astype(BF16))
    cw, cb = m_conv_w[0], row1(m_conv_b[0])
    dtb, alog = _pad_lanes(m_dt_bias[0]), _pad_lanes(m_a_log[0])
    dsk = row1(jnp.repeat(m_d[0], M_HEADDIM))
    nw = row1(m_norm[0])
    e64 = _expand_mat()

    mod = _ada(jnp.concatenate([c_prompt, c_sample], axis=0), w_ada[0], b_ada[0])
    mod_p = mod[:bp].reshape(bp, 1, 6 * D_MODEL)
    mod_s = mod[bp:]

    ar, ai, bbr, bbi = _s5_disc(s5_lam_re[0], s5_lam_im[0], s5_log_dt[0], s5_b_re[0], s5_b_im[0])
    s5m = _s5_matrices(ar, ai, bbr, bbi, s5_c_re[0], s5_c_im[0], s5_d[0])
    g_pre = row1(norm_pre_mix[0])

    xp = x_prompt.reshape(bp * seq, D_MODEL)
    tm = min(ROW_TILE, seq)
    u, zs, act, dt, gates, tail = _in_proj(xp, mod_p, g_pre, ws_in, dtb, cw, cb,
                                           per_row=False, tm=tm, rows_per_batch=seq)
    p_conv = tail[:, SUBLANES - (M_CONV - 1):, :]
    zeros_s5 = jnp.zeros((bp, S5_GROUPS * S5_STATE), F32)
    y5, p_re, p_im = _s5(u, zeros_s5, zeros_s5, *s5m, tl=tm)
    yb, p_ssm = _ssd(zs, act, dt, alog, dsk, nw, e64, nbatch=bp, seq=seq)
    y_prompt = _out_stage(xp, y5, yb, gates, mod_p, ws_out, per_row=False, tm=tm, rows_per_batch=seq)

    xs = x_sample.reshape(bs, D_MODEL)
    u, zs, xbc, dt, gates = _in_proj(xs, mod_s, g_pre, ws_in, dtb, cw, cb, per_row=True, tm=bs, rows_per_batch=1)
    y5, s_re, s_im = _s5(u, state_s5_re[0].reshape(bs, -1), state_s5_im[0].reshape(bs, -1), *s5m, tl=1)
    yb, s_ssm, s_conv = _ssd_step(zs, xbc, dt, jnp.swapaxes(state_conv[0], 0, 1),
                                  state_ssm[0].reshape(bs, M_INNER, M_STATE),
                                  cw, cb, alog, dsk, nw, e64)
    y_sample = _out_stage(xs, y5, yb, gates, mod_s, ws_out, per_row=True, tm=bs, rows_per_batch=1)

    s5_shape = (1, -1, S5_GROUPS, S5_STATE)
    ssm_shape = (1, -1, M_HEADS, M_HEADDIM, M_STATE)
    return (y_prompt.reshape(bp, seq, D_MODEL), y_sample.reshape(bs, 1, D_MODEL),
            p_re.reshape(s5_shape), p_im.reshape(s5_shape), p_ssm.reshape(ssm_shape), p_conv[None],
            s_re.reshape(s5_shape), s_im.reshape(s5_shape), s_ssm.reshape(ssm_shape), jnp.swapaxes(s_conv, 0, 1)[None])
```

```python
import functools

import jax
import jax.numpy as jnp
from jax import lax
from jax.experimental import pallas as pl
from jax.experimental.pallas import tpu as pltpu

F32 = jnp.float32
BF16 = jnp.bfloat16

D_MODEL = 1024
EPS = 1e-6
S5_GROUP = 16
S5_GROUPS = 64
S5_STATE = 64
M_INNER = 2048
M_HEADDIM = 64
M_HEADS = 32
M_GROUPS = 4
M_HPG = 8
M_STATE = 128
M_CONV = 4
M_CONV_DIM = 3072
M_CHUNK = 128
D_FF = 2816
OFF_Z = D_MODEL
OFF_XBC = OFF_Z + M_INNER
OFF_DT = OFF_XBC + M_CONV_DIM
OFF_GA = OFF_DT + M_HEADS
IN_COLS = OFF_GA + 2 * D_MODEL

LANES = 128
SUBLANES = 8
VMEM_LIMIT = 56 * 1024 * 1024

S5_CH = 256
S5_NS = 1024
S5_CHUNKS = D_MODEL // S5_CH
S5_TILES = S5_NS // LANES
S5_PIECES = S5_CH // LANES
S5_SUB_ROWS = 256
ROW_TILE = 256
INPROJ_SLAB = 512
INPROJ_DEPTH = 3

_NT = (((1,), (1,)), ((), ()))
_TN = (((0,), (0,)), ((), ()))


def _const_spec(shape):
    nd = len(shape)
    return pl.BlockSpec(shape, lambda *_: (0,) * nd, pipeline_mode=pl.Buffered(1))


def _params(sem):
    return pltpu.CompilerParams(dimension_semantics=sem, vmem_limit_bytes=VMEM_LIMIT)


def _sigmoid(x):
    return 0.5 * jnp.tanh(0.5 * x) + 0.5


def _silu(x):
    hx = 0.5 * x
    return hx + hx * jnp.tanh(hx)


def _softplus(x):
    return jnp.maximum(x, 0.0) + jnp.log1p(jnp.exp(-jnp.abs(x)))


def _gelu_tanh(x):
    return 0.5 * x * (1.0 + jnp.tanh(0.7978845608028654 * (x + 0.044715 * (x * x * x))))


def _rms(x, g):
    return x * lax.rsqrt(jnp.mean(x * x, axis=-1, keepdims=True) + EPS) * g


def _dot(a, b):
    return jnp.dot(a, b, preferred_element_type=F32)


def _pieces3(x):
    hi = x.astype(BF16).astype(F32)
    r1 = x - hi
    mid = r1.astype(BF16).astype(F32)
    lo = (r1 - mid).astype(BF16).astype(F32)
    return hi, mid, lo


def _ordered_after(x, dep):
    zero = jnp.where(dep[:, 0:LANES] > jnp.inf, 1.0, 0.0)
    return x + jnp.concatenate([zero] * (x.shape[1] // LANES), axis=1)


def _split2(x):
    return jnp.concatenate(_pieces3(x)[:2], axis=-1).astype(BF16)


def _split3(x):
    return jnp.concatenate(_pieces3(x), axis=-1).astype(BF16)


def _ada_kernel(c_ref, w_ref, b_ref, o_ref):
    s = _silu(c_ref[...]).astype(BF16)
    o_ref[...] = _dot(s, w_ref[...].astype(BF16)) + b_ref[...]


def _ada(c, w, b):
    n = c.shape[0]
    tn = 1024
    return pl.pallas_call(
        _ada_kernel,
        grid=(6 * D_MODEL // tn,),
        in_specs=[_const_spec((n, D_MODEL)),
                  pl.BlockSpec((D_MODEL, tn), lambda j: (0, j)),
                  pl.BlockSpec((1, tn), lambda j: (0, j))],
        out_specs=pl.BlockSpec((n, tn), lambda j: (0, j)),
        out_shape=jax.ShapeDtypeStruct((n, 6 * D_MODEL), F32),
        compiler_params=_params(("arbitrary",)),
        name="ada",
    )(c, w, b.reshape(1, -1))


def _s5_disc_kernel(lr_ref, li_ref, ldt_ref, br_ref, bi_ref, ar_ref, ai_ref, bbr_ref, bbi_ref):
    lr = lr_ref[...]
    li = li_ref[...]
    dt = jnp.exp(ldt_ref[...])
    mag = jnp.exp(lr * dt)
    ar = mag * jnp.cos(li * dt)
    ai = mag * jnp.sin(li * dt)
    den = lr * lr + li * li
    nr = ar - 1.0
    f_re = (nr * lr + ai * li) / den
    f_im = (ai * lr - nr * li) / den
    br = br_ref[...]
    bi = bi_ref[...]
    ar_ref[...] = ar
    ai_ref[...] = ai
    bbr_ref[...] = f_re * br - f_im * bi
    bbi_ref[...] = f_re * bi + f_im * br


def _s5_disc(lam_re, lam_im, log_dt, b_re, b_im):
    g, n = lam_re.shape
    w = S5_GROUP * n
    tile = lambda a: jnp.tile(a, (1, S5_GROUP))
    ldt = jnp.broadcast_to(log_dt[:, None], (g, w))
    brt = jnp.swapaxes(b_re, 1, 2).reshape(g, w)
    bit = jnp.swapaxes(b_im, 1, 2).reshape(g, w)
    shp = jax.ShapeDtypeStruct((g, w), F32)
    ar, ai, bbr, bbi = pl.pallas_call(
        _s5_disc_kernel,
        out_shape=(shp, shp, shp, shp),
        name="s5_disc",
    )(tile(lam_re), tile(lam_im), ldt, brt, bit)
    return (ar[:, :n], ai[:, :n], bbr.reshape(g, S5_GROUP, n), bbi.reshape(g, S5_GROUP, n))


def _s5_matrices(abar_re, abar_im, bbar_re, bbar_im, c_re, c_im, d_skip):
    ns = 2 * S5_NS
    b = jnp.concatenate([bbar_re.reshape(D_MODEL, S5_STATE), bbar_im.reshape(D_MODEL, S5_STATE)],
                        axis=1).reshape(S5_CHUNKS, S5_CH, 2 * S5_STATE)

    def by_tile(c):
        return jnp.swapaxes(c, 1, 2).reshape(S5_CHUNKS, S5_TILES, 1, 2, S5_STATE, S5_GROUP)
    c = jnp.concatenate([by_tile(c_re), by_tile(-c_im)], axis=2).reshape(S5_CHUNKS, ns, S5_GROUP)

    by_tile_a = lambda v: v.reshape(S5_CHUNKS, S5_TILES, 1, LANES)
    a = jnp.concatenate([by_tile_a(abar_re), by_tile_a(abar_im)], axis=2).reshape(S5_CHUNKS, 1, ns)
    d = d_skip.reshape(S5_CHUNKS, 1, S5_CH)
    return b, a, c, d


def _s5_state_group(i):
    return 2 * (i // (2 * LANES)) + (i % LANES) // S5_STATE


def _s5_expand_operands(b_ref, c_ref, bm_ref, cm_ref):
    tw = 2 * LANES
    row_g = lax.broadcasted_iota(jnp.int32, (S5_CH, LANES), 0) // S5_GROUP
    lane_half = lax.broadcasted_iota(jnp.int32, (S5_CH, LANES), 1) // S5_STATE
    for part in range(2):
        bp = b_ref[:, part * S5_STATE:(part + 1) * S5_STATE]
        pattern = jnp.concatenate([bp, bp], axis=1)
        for j in range(S5_TILES):
            tile = jnp.where(row_g == 2 * j + lane_half, pattern, 0.0)
            bm_ref[:, j * tw + part * LANES:j * tw + (part + 1) * LANES] = tile.astype(BF16)
    spread = (lax.broadcasted_iota(jnp.int32, (S5_GROUP, S5_CH), 1) % S5_GROUP
              == lax.broadcasted_iota(jnp.int32, (S5_GROUP, S5_CH), 0)).astype(BF16)
    rows = 512
    for r0 in range(0, 2 * S5_NS, rows):
        tiled = _dot(c_ref[r0:r0 + rows, :].astype(BF16), spread)
        own = (_s5_state_group(lax.broadcasted_iota(jnp.int32, (rows, S5_CH), 0) + r0)
               == lax.broadcasted_iota(jnp.int32, (rows, S5_CH), 1) // S5_GROUP)
        cm_ref[r0:r0 + rows, :] = jnp.where(own, tiled, 0.0).astype(BF16)


def _split_w_in_kernel(wt_ref, wm_ref, wdt_ref, wg_ref):
    tk = wt_ref.shape[1]
    step = 1024
    for j in range(0, OFF_DT, step):
        wm_ref[:, j:j + step] = wt_ref[j:j + step, :].T.astype(BF16)
    head = lax.broadcasted_iota(jnp.int32, (tk, LANES), 1) < M_HEADS
    wdt_ref[...] = jnp.where(head, wt_ref[OFF_DT:OFF_DT + LANES, :].T, 0.0).astype(BF16)
    for j in range(0, 2 * D_MODEL, step):
        wg_ref[:, j:j + step] = wt_ref[OFF_GA + j:OFF_GA + j + step, :].T.astype(BF16)


def _split_w_in(w):
    k = w.shape[0]
    tk = 256
    outs = (OFF_DT, LANES, 2 * D_MODEL)
    return pl.pallas_call(
        _split_w_in_kernel,
        grid=(k // tk,),
        in_specs=[pl.BlockSpec((IN_COLS, tk), lambda i: (0, i))],
        out_specs=[pl.BlockSpec((tk, n), lambda i: (i, 0)) for n in outs],
        out_shape=[jax.ShapeDtypeStruct((k, n), BF16) for n in outs],
        compiler_params=_params(("parallel",)),
        name="split_w_in",
    )(jnp.swapaxes(w, 0, 1))


def _inproj_kernel(x_ref, sh_ref, sc_ref, g_ref, wm_ref, wdt_ref, wg_ref, dtb_ref, cw_ref, cb_ref,
                   u_ref, z_ref, xbc_ref, dt_ref, gate_ref, *rest, tiles_per_seq):
    accs = rest[-INPROJ_DEPTH:]
    h = _rms(x_ref[...], g_ref[...]) * (1.0 + sc_ref[...]) + sh_ref[...]
    hb = h.astype(BF16)
    rows = hb.shape[0]
    step = INPROJ_SLAB

    if u_ref.shape[0] > rows:
        u_ref[rows:, :] = jnp.zeros((u_ref.shape[0] - rows, D_MODEL), F32)

    if tiles_per_seq is None:
        conv_fn = None
    else:
        tail_ref, carry_ref = rest[:2]

        @pl.when(pl.program_id(0) % tiles_per_seq == 0)
        def _():
            carry_ref[...] = jnp.zeros_like(carry_ref)

        def conv_fn(r, sl):
            width = sl.stop - sl.start
            w = [cw_ref[k:k + 1, sl] for k in range(M_CONV)]
            bias = cb_ref[:, sl]
            row = lax.broadcasted_iota(jnp.int32, (SUBLANES, width), 0)
            xp = carry_ref[:, sl]
            farp = xp * w[1] + pltpu.roll(xp, 1, axis=0) * w[0]
            outs = []
            for i in range(rows // SUBLANES):
                xi = r[i * SUBLANES:(i + 1) * SUBLANES, :]
                if outs:
                    xi = _ordered_after(xi, outs[-1])
                s1 = jnp.where(row == 0, pltpu.roll(xp, 1, axis=0), pltpu.roll(xi, 1, axis=0))
                far = xi * w[1] + s1 * w[0]
                s2 = jnp.where(row < 2, pltpu.roll(farp, 2, axis=0), pltpu.roll(far, 2, axis=0))
                outs.append(_silu(xi * w[3] + s1 * w[2] + s2 + bias))
                xp, farp = xi, far
            carry_ref[:, sl] = r[rows - SUBLANES:, :]
            tail_ref[:, sl] = r[rows - SUBLANES:, :]
            return jnp.concatenate(outs, axis=0)

    def slabs(o_ref, w_ref, w_off, fn):
        n = o_ref.shape[-1]
        return [(o_ref, slice(j, j + min(step, n - j)), w_ref, w_off + j, fn) for j in range(0, n, step)]

    su = slabs(u_ref, wm_ref, 0, None)
    sz = slabs(z_ref, wm_ref, OFF_Z, None)
    sx = slabs(xbc_ref, wm_ref, OFF_XBC, conv_fn)
    sd = slabs(dt_ref, wdt_ref, 0, lambda r, sl: _softplus(r + dtb_ref[:, sl]))
    sg = slabs(gate_ref, wg_ref, 0, None)
    light = su + sz + sg + sd
    order = []
    while sx or light:
        if sx:
            order.append(sx.pop(0))
        if light:
            order.append(light.pop(0))

    def matmul(s):
        o_ref, sl, w_ref, w_col, _ = order[s]
        accs[s % INPROJ_DEPTH][:, 0:sl.stop - sl.start] = _dot(hb, w_ref[:, w_col:w_col + sl.stop - sl.start])

    for s in range(min(INPROJ_DEPTH - 1, len(order))):
        matmul(s)
    for s, (o_ref, sl, _, _, fn) in enumerate(order):
        if s + INPROJ_DEPTH - 1 < len(order):
            matmul(s + INPROJ_DEPTH - 1)
        r = accs[s % INPROJ_DEPTH][:, 0:sl.stop - sl.start]
        o_ref[0:rows, sl] = r if fn is None else fn(r, sl)


def _mod_specs(per_row, tm, rows_per_mod, cols):
    if per_row:
        return [pl.BlockSpec((tm, D_MODEL), functools.partial(lambda i, c: (i, c), c=c)) for c in cols]
    return [pl.BlockSpec((None, 1, D_MODEL), functools.partial(lambda i, c: (i // rows_per_mod, 0, c), c=c))
            for c in cols]


def _s5_layout(per_row, t, tm, rows_per_batch, width):
    if per_row:
        return (1, t, width), pl.BlockSpec((None, tm, width), lambda i: (0, i, 0))
    nblk, pitch = rows_per_batch // tm, _s5_pitch(tm)
    return ((nblk, (t // rows_per_batch) * pitch, width),
            pl.BlockSpec((None, pitch, width), lambda i: (i % nblk, i // nblk, 0)))


def _in_proj(x, mod, g, ws, dtb, cw, cb, *, per_row, tm, rows_per_batch):
    t = x.shape[0]
    row = lambda n: pl.BlockSpec((tm, n), lambda i: (i, 0))
    outs = (M_INNER, M_CONV_DIM, LANES, 2 * D_MODEL)
    u_shape, u_spec = _s5_layout(per_row, t, tm, rows_per_batch, D_MODEL)
    out_specs = [u_spec] + [row(n) for n in outs]
    out_shape = [jax.ShapeDtypeStruct(u_shape, F32)] + [jax.ShapeDtypeStruct((t, n), F32) for n in outs]
    scratch = []
    tiles_per_seq = None
    if not per_row:
        tiles_per_seq = rows_per_batch // tm
        out_specs.append(pl.BlockSpec((None, SUBLANES, M_CONV_DIM), lambda i: (i // tiles_per_seq, 0, 0)))
        out_shape.append(jax.ShapeDtypeStruct((t // rows_per_batch, SUBLANES, M_CONV_DIM), F32))
        scratch.append(pltpu.VMEM((SUBLANES, M_CONV_DIM), F32))
    scratch += [pltpu.VMEM((tm, INPROJ_SLAB), F32)] * INPROJ_DEPTH
    consts = (g,) + tuple(ws) + (dtb, cw, cb)
    return pl.pallas_call(
        functools.partial(_inproj_kernel, tiles_per_seq=tiles_per_seq),
        grid=(t // tm,),
        in_specs=[row(D_MODEL)] + _mod_specs(per_row, tm, rows_per_batch // tm if not per_row else 1, (0, 1))
                 + [_const_spec(w.shape) for w in consts],
        out_specs=out_specs,
        out_shape=out_shape,
        scratch_shapes=scratch,
        compiler_params=_params(("arbitrary",)),
        name="in_proj",
    )(x, mod, mod, *consts)


def _s5_pitch(tl):
    return tl if tl == 1 else tl + SUBLANES


def _s5_kernel(*refs, nb, tl, sub_rows, side=None):
    nu = S5_PIECES
    u_refs = refs[:nu]
    h0r_ref, h0i_ref, b_ref, a_ref, c_ref, d_ref = refs[nu:nu + 6]
    y_refs = refs[nu + 6:2 * nu + 6]
    hr_ref, hi_ref, hst_ref, bu0_ref, xs0_ref, bu1_ref, xs1_ref, bm_ref, cm_ref = refs[2 * nu + 6:]
    bufs = ((bu0_ref, xs0_ref), (bu1_ref, xs1_ref))
    tb = pl.program_id(1)
    tsub = sub_rows // nb
    tw = 2 * LANES
    pitch = _s5_pitch(tl)

    def step_rows(t):
        return pl.ds(t, nb, stride=pitch) if pitch > 1 else pl.ds(0, nb)

    @pl.when(tb == 0)
    def _():
        _s5_expand_operands(b_ref, c_ref, bm_ref, cm_ref)
        for j in range(S5_TILES):
            hst_ref[:, j * tw:j * tw + LANES] = h0r_ref[:, j * LANES:(j + 1) * LANES]
            hst_ref[:, j * tw + LANES:(j + 1) * tw] = h0i_ref[:, j * LANES:(j + 1) * LANES]

    ar = [jnp.broadcast_to(a_ref[:, j * tw:j * tw + LANES], (nb, LANES)) for j in range(S5_TILES)]
    ai = [jnp.broadcast_to(a_ref[:, j * tw + LANES:(j + 1) * tw], (nb, LANES)) for j in range(S5_TILES)]
    dsk = d_ref[...]

    def project_in(t0, bu_ref, xs_ref):
        xs = jnp.concatenate([jnp.concatenate([r[step_rows(t0 + i), :] for r in u_refs], axis=1)
                              for i in range(tsub)], axis=0)
        xs_ref[...] = xs
        lhs = xs.astype(BF16)
        for j in range(S5_TILES):
            bu_ref[:, j * tw:(j + 1) * tw] = _dot(lhs, bm_ref[:, j * tw:(j + 1) * tw])

    def scan_project_out(t0, bu_ref, xs_ref, carry):
        new, hs = [], []
        for j in range(S5_TILES):
            hr, hi = carry[2 * j], carry[2 * j + 1]
            rows = []
            for i in range(tsub):
                br = bu_ref[i * nb:(i + 1) * nb, j * tw:j * tw + LANES]
                bi = bu_ref[i * nb:(i + 1) * nb, j * tw + LANES:(j + 1) * tw]
                hr, hi = ar[j] * hr - ai[j] * hi + br, ar[j] * hi + ai[j] * hr + bi
                rows.append(jnp.concatenate([hr, hi], axis=1))
            new += [hr, hi]
            hs.append(jnp.concatenate(rows, axis=0).astype(BF16))
        y = _dot(jnp.concatenate(hs, axis=1), cm_ref[...]) + dsk * xs_ref[...]
        for i in range(tsub):
            for k, y_ref in enumerate(y_refs):
                y_ref[step_rows(t0 + i), :] = y[i * nb:(i + 1) * nb, k * LANES:(k + 1) * LANES]
        return tuple(new)

    h_in = []
    for j in range(S5_TILES):
        h_in += [hst_ref[:, j * tw:j * tw + LANES], hst_ref[:, j * tw + LANES:(j + 1) * tw]]
    nsub = tl // tsub
    (bu0, xs0), (bu1, xs1) = bufs
    project_in(0, bu0, xs0)
    if nsub == 1:
        h_out = scan_project_out(0, bu0, xs0, tuple(h_in))
    else:
        def pair(p, carry):
            t0 = 2 * p * tsub
            project_in(t0 + tsub, bu1, xs1)
            carry = scan_project_out(t0, bu0, xs0, carry)
            project_in(jnp.minimum(t0 + 2 * tsub, tl - tsub), bu0, xs0)
            return scan_project_out(t0 + tsub, bu1, xs1, carry)
        h_out = lax.fori_loop(0, nsub // 2, pair, tuple(h_in))
    for j in range(S5_TILES):
        hst_ref[:, j * tw:j * tw + LANES] = h_out[2 * j]
        hst_ref[:, j * tw + LANES:(j + 1) * tw] = h_out[2 * j + 1]

    if pitch > tl:
        for y_ref in y_refs:
            for b in range(nb):
                y_ref[b * pitch + tl:(b + 1) * pitch, :] = jnp.zeros((pitch - tl, LANES), F32)

    @pl.when(tb == pl.num_programs(1) - 1)
    def _():
        for j in range(S5_TILES):
            hr_ref[:, j * LANES:(j + 1) * LANES] = h_out[2 * j]
            hi_ref[:, j * LANES:(j + 1) * LANES] = h_out[2 * j + 1]

    if side is not None:
        side()


def _s5(u, h0r, h0i, b, a, c, d, *, tl):
    nblk, rows, _ = u.shape
    pitch = _s5_pitch(tl)
    nb = rows // pitch
    sub_rows = min(S5_SUB_ROWS, tl * nb)
    nsub = tl * nb // sub_rows
    assert sub_rows % nb == 0 and tl * nb % sub_rows == 0 and (nsub == 1 or nsub % 2 == 0)
    chunk3 = lambda s: pl.BlockSpec((None,) + s, lambda c, t: (c, 0, 0))
    st_spec = pl.BlockSpec((nb, S5_NS), lambda c, t: (0, c))
    nu = S5_PIECES
    u_specs = [pl.BlockSpec((None, rows, LANES), functools.partial(lambda c, t, k: (t, 0, nu * c + k), k=k))
               for k in range(nu)]
    y_spec = pl.BlockSpec((None, rows, LANES), lambda c, t: (t, 0, c))
    y_shape = jax.ShapeDtypeStruct((nblk, rows, D_MODEL // nu), F32)
    res = pl.pallas_call(
        functools.partial(_s5_kernel, nb=nb, tl=tl, sub_rows=sub_rows),
        grid=(S5_CHUNKS, nblk),
        in_specs=u_specs + [st_spec, st_spec, chunk3((S5_CH, 2 * S5_STATE)), chunk3((1, 2 * S5_NS)),
                            chunk3((2 * S5_NS, S5_GROUP)), chunk3((1, S5_CH))],
        out_specs=[y_spec] * nu + [st_spec, st_spec],
        out_shape=[y_shape] * nu + [jax.ShapeDtypeStruct(h0r.shape, F32), jax.ShapeDtypeStruct(h0i.shape, F32)],
        scratch_shapes=[pltpu.VMEM((nb, 2 * S5_NS), F32)]
                       + [pltpu.VMEM((sub_rows, 2 * S5_NS), F32), pltpu.VMEM((sub_rows, S5_CH), F32)] * 2
                       + [pltpu.VMEM((S5_CH, 2 * S5_NS), BF16), pltpu.VMEM((2 * S5_NS, S5_CH), BF16)],
        compiler_params=_params(("parallel", "arbitrary")),
        name="s5",
    )(*([u] * nu), h0r, h0i, b, a, c, d)
    return res[:nu], res[nu], res[nu + 1]


def _expand_mat():
    r = lax.broadcasted_iota(jnp.int32, (3 * LANES, M_HEADS * M_HEADDIM), 0) % LANES
    c = lax.broadcasted_iota(jnp.int32, (3 * LANES, M_HEADS * M_HEADDIM), 1) // M_HEADDIM
    return (r == c).astype(BF16)


LOG2E = 1.4426950408889634
SSD_CHUNKS_PER_STEP = 4


def _ssd_kernel(z_ref, act_ref, dt_ref, alog_ref, dsk_ref, nw_ref, e64_ref,
                y_ref, hout_ref,
                ht_ref, xs_ref, xdte_ref, eacs_ref, yacc_ref):
    c = pl.program_id(1)

    @pl.when(c == 0)
    def _():
        ht_ref[...] = jnp.zeros_like(ht_ref)

    for k in range(SSD_CHUNKS_PER_STEP):
        rows = pl.ds(k * M_CHUNK, M_CHUNK)
        _ssd_chunk(z_ref.at[rows], act_ref.at[rows], dt_ref.at[rows], alog_ref, dsk_ref, nw_ref, e64_ref,
                   y_ref.at[rows], ht_ref, xs_ref.at[k], xdte_ref.at[k], eacs_ref.at[k], yacc_ref.at[k])

    @pl.when(c == pl.num_programs(1) - 1)
    def _():
        for j in range(0, M_INNER, LANES):
            hout_ref[j:j + LANES, :] = ht_ref[:, j:j + LANES].T


def _ssd_chunk(z_ref, act_ref, dt_ref, alog_ref, dsk_ref, nw_ref, e64_ref,
               y_ref, ht_ref, xs_ref, xdte_ref, eacs_ref, yacc_ref):
    q = M_CHUNK
    cs = 512

    dt = dt_ref[...]
    a2 = dt * (-jnp.exp(alog_ref[...]) * LOG2E)
    ri = lax.broadcasted_iota(jnp.int32, (q, q), 0)
    ci = lax.broadcasted_iota(jnp.int32, (q, q), 1)
    causal = ri >= ci
    acs = jnp.dot(causal.astype(F32), a2, preferred_element_type=F32, precision=lax.Precision.HIGHEST)
    src_t = (acs - jnp.log2(dt)).T
    eacs2 = _split2(jnp.exp2(acs))
    dte2 = _split2(dt * jnp.exp2(acs[q - 1:q, :] - acs))
    for j in range(0, M_INNER, cs):
        sl = slice(j, j + cs)
        e = e64_ref[0:2 * LANES, sl]
        xs = act_ref[:, sl]
        xs_ref[:, sl] = xs.astype(BF16)
        xdte_ref[:, sl] = (xs * _dot(dte2, e)).astype(BF16)
        eacs_ref[:, sl] = _dot(eacs2, e)

    hp = M_HPG * M_HEADDIM
    lane = lax.broadcasted_iota(jnp.int32, (q, LANES), 1)
    first_head = lane < M_HEADDIM
    for g in range(M_GROUPS):
        gs = slice(g * hp, (g + 1) * hp)
        bm_t = act_ref[:, M_INNER + g * M_STATE:M_INNER + (g + 1) * M_STATE].T.astype(BF16)
        cm = act_ref[:, M_INNER + (M_GROUPS + g) * M_STATE:M_INNER + (M_GROUPS + g + 1) * M_STATE].astype(BF16)
        cbm = _dot(cm, bm_t)
        y_off = _dot(cm, ht_ref[:, gs].astype(BF16))
        st_t = _dot(bm_t, xdte_ref[:, gs])
        for pr in range(M_HPG // 2):
            h0 = g * M_HPG + 2 * pr
            ps = slice(h0 * M_HEADDIM, (h0 + 2) * M_HEADDIM)
            ms = []
            for h in (h0, h0 + 1):
                d2 = jnp.broadcast_to(acs[:, h:h + 1], (q, q)) - src_t[h:h + 1, :]
                ms.append((cbm * jnp.where(causal, jnp.exp2(d2), 0.0)).astype(BF16))
            xp = xs_ref[:, ps]
            zero = jnp.zeros_like(xp)
            rhs = jnp.concatenate([jnp.where(first_head, xp, zero), jnp.where(first_head, zero, xp)], axis=0)
            yd = _dot(jnp.concatenate(ms, axis=1), rhs)
            yacc_ref[:, ps] = yd + y_off[:, ps.start - gs.start:ps.stop - gs.start] * eacs_ref[:, ps] \
                + dsk_ref[:, ps] * act_ref[:, ps]
        ht_ref[:, gs] = eacs_ref[q - 1:q, gs] * ht_ref[:, gs] + st_t

    ss = jnp.zeros((q, 1), F32)
    for j in range(0, M_INNER, cs):
        sl = slice(j, j + cs)
        gt = yacc_ref[:, sl] * _silu(z_ref[:, sl])
        yacc_ref[:, sl] = gt
        ss = ss + jnp.sum(gt * gt, axis=-1, keepdims=True)
    inv = lax.rsqrt(ss * (1.0 / M_INNER) + EPS)
    for j in range(0, M_INNER, cs):
        sl = slice(j, j + cs)
        y_ref[:, sl] = yacc_ref[:, sl] * inv * nw_ref[:, sl]


def _ssd(z, act, dt, alog, dsk, nw, e64, *, nbatch, seq):
    q = M_CHUNK
    k = SSD_CHUNKS_PER_STEP
    assert seq % (k * q) == 0
    nc = seq // (k * q)
    row = lambda n: pl.BlockSpec((k * q, n), lambda b, c: (b * nc + c, 0))
    return pl.pallas_call(
        _ssd_kernel,
        grid=(nbatch, nc),
        in_specs=[row(M_INNER), row(M_CONV_DIM), row(LANES), _const_spec(alog.shape),
                  _const_spec(dsk.shape), _const_spec(nw.shape), _const_spec(e64.shape)],
        out_specs=[row(M_INNER), pl.BlockSpec((None, M_INNER, M_STATE), lambda b, c: (b, 0, 0))],
        out_shape=[jax.ShapeDtypeStruct((nbatch * seq, M_INNER), F32),
                   jax.ShapeDtypeStruct((nbatch, M_INNER, M_STATE), F32)],
        scratch_shapes=[pltpu.VMEM((M_STATE, M_INNER), F32), pltpu.VMEM((k, q, M_INNER), BF16),
                        pltpu.VMEM((k, q, M_INNER), BF16), pltpu.VMEM((k, q, M_INNER), F32),
                        pltpu.VMEM((k, q, M_INNER), F32)],
        compiler_params=_params(("parallel", "arbitrary")),
        name="ssd",
    )(z, act, dt, alog, dsk, nw, e64)


SSD_STEP_BB = 8


def _ssd_step_kernel(z_ref, xbc_ref, dt_ref, cbuf_ref, h0_ref, cw_ref, cb_ref, alog_ref, dsk_ref, nw_ref,
                     e64_ref, y_ref, hout_ref, convout_ref):
    bb = SSD_STEP_BB
    x_new = xbc_ref[...]
    conv = cb_ref[...] + x_new * cw_ref[M_CONV - 1:M_CONV, :]
    for k in range(M_CONV - 1):
        conv = conv + cbuf_ref[k] * cw_ref[k:k + 1, :]
    act = _silu(conv)
    for k in range(M_CONV - 2):
        convout_ref[k] = cbuf_ref[k + 1]
    convout_ref[M_CONV - 2] = x_new

    xs = act[:, :M_INNER]
    bmat = act[:, M_INNER:M_INNER + M_GROUPS * M_STATE]
    cmat = act[:, M_INNER + M_GROUPS * M_STATE:]
    dt = dt_ref[...]
    da = jnp.exp(dt * (-jnp.exp(alog_ref[...])))
    dt_e = _dot(_split3(dt), e64_ref[...])
    da_e = _dot(_split3(da), e64_ref[...])
    xdt = xs * dt_e

    hp = M_HPG * M_HEADDIM
    lane_group = lax.broadcasted_iota(jnp.int32, (M_GROUPS, M_INNER), 1) // hp
    row_group = lax.broadcasted_iota(jnp.int32, (M_GROUPS, M_INNER), 0)
    gmask = lane_group == row_group
    ones_rows = lax.broadcasted_iota(jnp.int32, (SUBLANES, M_STATE), 0) >= M_GROUPS

    cbx = []
    for g in range(M_GROUPS):
        cbg = jnp.sum(cmat[:, g * M_STATE:(g + 1) * M_STATE] * bmat[:, g * M_STATE:(g + 1) * M_STATE],
                      axis=-1, keepdims=True)
        cbx.append(jnp.broadcast_to(cbg, (bb, hp)))
    y = jnp.concatenate(cbx, axis=1) * xdt + dsk_ref[...] * xs

    y_off_rows = []
    for b in range(bb):
        xsel = jnp.where(gmask, jnp.broadcast_to(xdt[b:b + 1, :], (M_GROUPS, M_INNER)), 0.0)
        lhs = jnp.concatenate([xsel, *_pieces3(da_e[b:b + 1, :]), jnp.zeros((1, M_INNER), F32)],
                              axis=0).astype(BF16)
        bm4 = jnp.concatenate([bmat[b:b + 1, g * M_STATE:(g + 1) * M_STATE] for g in range(M_GROUPS)], axis=0)
        cm4 = jnp.concatenate([cmat[b:b + 1, g * M_STATE:(g + 1) * M_STATE] for g in range(M_GROUPS)], axis=0)
        zeros4 = jnp.zeros((SUBLANES - M_GROUPS, M_STATE), F32)
        rhs = jnp.concatenate([jnp.concatenate([bm4, zeros4], axis=0),
                               jnp.where(ones_rows, 1.0, 0.0)], axis=1).astype(BF16)
        res = lax.dot_general(lhs, rhs, _TN, preferred_element_type=F32)
        h0 = h0_ref[b]
        hout_ref[b] = res[:, M_STATE:] * h0 + res[:, :M_STATE]
        cm8 = jnp.concatenate([cm4, zeros4], axis=0).astype(BF16)
        yo = lax.dot_general(cm8, h0.astype(BF16), _NT, preferred_element_type=F32)
        y_off_rows.append(jnp.sum(jnp.where(gmask, yo[:M_GROUPS, :], 0.0), axis=0, keepdims=True))
    y = y + da_e * jnp.concatenate(y_off_rows, axis=0)

    y_ref[...] = _rms(y * _silu(z_ref[...]), nw_ref[...])


SSD_STEP_EVERY = 2


def _s5_ssd_step_kernel(*refs, nb, tl, sub_rows):
    n5i, nsi = S5_PIECES + 6, 11
    n5o, nso = S5_PIECES + 2, 3
    ins5, inss = refs[:n5i], refs[n5i:n5i + nsi]
    o = n5i + nsi
    outs5, outss = refs[o:o + n5o], refs[o + n5o:o + n5o + nso]
    sc5 = refs[o + n5o + nso:]
    s = pl.program_id(0) * pl.num_programs(1) + pl.program_id(1)

    def side():
        @pl.when(s % SSD_STEP_EVERY == 0)
        def _():
            _ssd_step_kernel(*inss, *outss)

    _s5_kernel(*ins5, *outs5, *sc5, nb=nb, tl=tl, sub_rows=sub_rows, side=side)


def _s5_ssd_step(u, h0r, h0i, b, a, c, d, zs, xbc, dt, cbuf, h0, cw, cb, alog, dsk, nw, e64, *, tl):
    nblk, rows, _ = u.shape
    pitch = _s5_pitch(tl)
    nb = rows // pitch
    sub_rows = min(S5_SUB_ROWS, tl * nb)
    nsub = tl * nb // sub_rows
    ns = zs.shape[0]
    bb, per = SSD_STEP_BB, SSD_STEP_EVERY
    assert sub_rows % nb == 0 and tl * nb % sub_rows == 0 and nsub % 2 == 0 and ns * per == bb * S5_CHUNKS * nblk
    chunk3 = lambda sh: pl.BlockSpec((None,) + sh, lambda c, t: (c, 0, 0))
    st_spec = pl.BlockSpec((nb, S5_NS), lambda c, t: (0, c))
    nu = S5_PIECES
    u_specs = [pl.BlockSpec((None, rows, LANES), functools.partial(lambda c, t, k: (t, 0, nu * c + k), k=k))
               for k in range(nu)]
    y5_spec = pl.BlockSpec((None, rows, LANES), lambda c, t: (t, 0, c))
    y5_shape = jax.ShapeDtypeStruct((nblk, rows, D_MODEL // nu), F32)
    step = lambda c, t: c * nblk + t
    row = lambda w: pl.BlockSpec((bb, w), lambda c, t: (step(c, t) // per, 0))
    conv_spec = pl.BlockSpec((M_CONV - 1, bb, M_CONV_DIM), lambda c, t: (0, step(c, t) // per, 0))
    st3_spec = pl.BlockSpec((bb, M_INNER, M_STATE), lambda c, t: (step(c, t) // per, 0, 0))
    res = pl.pallas_call(
        functools.partial(_s5_ssd_step_kernel, nb=nb, tl=tl, sub_rows=sub_rows),
        grid=(S5_CHUNKS, nblk),
        in_specs=u_specs + [st_spec, st_spec, chunk3((S5_CH, 2 * S5_STATE)), chunk3((1, 2 * S5_NS)),
                            chunk3((2 * S5_NS, S5_GROUP)), chunk3((1, S5_CH))]
                 + [row(M_INNER), row(M_CONV_DIM), row(LANES), conv_spec, st3_spec,
                    _const_spec(cw.shape), _const_spec(cb.shape), _const_spec(alog.shape),
                    _const_spec(dsk.shape), _const_spec(nw.shape), _const_spec(e64.shape)],
        out_specs=[y5_spec] * nu + [st_spec, st_spec] + [row(M_INNER), st3_spec, conv_spec],
        out_shape=[y5_shape] * nu + [jax.ShapeDtypeStruct(h0r.shape, F32), jax.ShapeDtypeStruct(h0i.shape, F32)]
                  + [jax.ShapeDtypeStruct((ns, M_INNER), F32), jax.ShapeDtypeStruct((ns, M_INNER, M_STATE), F32),
                     jax.ShapeDtypeStruct((M_CONV - 1, ns, M_CONV_DIM), F32)],
        scratch_shapes=[pltpu.VMEM((nb, 2 * S5_NS), F32)]
                       + [pltpu.VMEM((sub_rows, 2 * S5_NS), F32), pltpu.VMEM((sub_rows, S5_CH), F32)] * 2
                       + [pltpu.VMEM((S5_CH, 2 * S5_NS), BF16), pltpu.VMEM((2 * S5_NS, S5_CH), BF16)],
        compiler_params=_params(("arbitrary", "arbitrary")),
        name="s5_ssd_step",
    )(*([u] * nu), h0r, h0i, b, a, c, d, zs, xbc, dt, cbuf, h0, cw, cb, alog, dsk, nw, e64)
    return (res[:nu], res[nu], res[nu + 1]), (res[nu + 2], res[nu + 3], res[nu + 4])


MXU_WIDTH = 256
FFN_CHUNKS = (0, 6 * MXU_WIDTH, D_FF)


def _out_kernel(x_ref, *refs):
    y5_refs = refs[:S5_PIECES]
    (yb_ref, gate_ref, gtm_ref, shf_ref, scf_ref, gtf_ref, wglu_ref, bglu_ref, wb5_ref, wbs_ref, wout_ref,
     npm_ref, npf_ref, npo_ref, wfi_ref, wfo_ref, o_ref) = refs[S5_PIECES:]
    rows = x_ref.shape[0]
    y5 = jnp.concatenate([y5_refs[k][0:rows, c * LANES:(c + 1) * LANES]
                          for c in range(S5_CHUNKS) for k in range(S5_PIECES)], axis=1)
    ya = _gelu_tanh(y5)
    glu = ya * _sigmoid(_dot(ya.astype(BF16), wglu_ref[...]) + bglu_ref[...])
    merged = (_sigmoid(gate_ref[:, :D_MODEL]) * _dot(glu.astype(BF16), wb5_ref[...])
              + _sigmoid(gate_ref[:, D_MODEL:]) * _dot(yb_ref[...].astype(BF16), wbs_ref[...]))
    mix = _dot(merged.astype(BF16), wout_ref[...])
    x1 = x_ref[...] + gtm_ref[...] * _rms(mix, npm_ref[...])

    hb = (_rms(x1, npf_ref[...]) * (1.0 + scf_ref[...]) + shf_ref[...]).astype(BF16)
    f = jnp.zeros_like(x1)
    for j, e in zip(FFN_CHUNKS[:-1], FFN_CHUNKS[1:]):
        gg = _dot(hb, wfi_ref[:, j:e])
        uu = _dot(hb, wfi_ref[:, D_FF + j:D_FF + e])
        f = f + _dot((_silu(gg) * uu).astype(BF16), wfo_ref[j:e, :])
    o_ref[...] = x1 + gtf_ref[...] * _rms(f, npo_ref[...])


def _out_stage(x, y5, yb, gates, mod, ws, *, per_row, tm, rows_per_batch):
    t = x.shape[0]
    row = lambda n: pl.BlockSpec((tm, n), lambda i: (i, 0))
    mods = _mod_specs(per_row, tm, rows_per_batch // tm if not per_row else 1, (2, 3, 4, 5))
    _, y5_spec = _s5_layout(per_row, t, tm, rows_per_batch, D_MODEL // S5_PIECES)
    return pl.pallas_call(
        _out_kernel,
        grid=(t // tm,),
        in_specs=[row(D_MODEL)] + [y5_spec] * S5_PIECES + [row(M_INNER), row(2 * D_MODEL)] + mods
                 + [_const_spec(w.shape) for w in ws],
        out_specs=row(D_MODEL),
        out_shape=jax.ShapeDtypeStruct((t, D_MODEL), F32),
        compiler_params=_params(("parallel",)),
        name="out_stage",
    )(x, *y5, yb, gates, mod, mod, mod, mod, *ws)


def _pad_lanes(v):
    return jnp.pad(v.reshape(1, -1), ((0, 0), (0, LANES - v.shape[-1])))


def kernel(x_prompt, x_sample, state_s5_re, state_s5_im, state_ssm, state_conv, c_prompt, c_sample, w_ada, b_ada, norm_pre_mix, norm_post_mix, norm_pre_ffn, norm_post_ffn, w_in, s5_lam_re, s5_lam_im, s5_log_dt, s5_b_re, s5_b_im, s5_c_re, s5_c_im, s5_d, s5_w_glu, s5_b_glu, m_conv_w, m_conv_b, m_dt_bias, m_a_log, m_d, m_norm, w_branch_s5, w_branch_ssd, w_out, w_ffn_in, w_ffn_out):
    bp, seq, _ = x_prompt.shape
    bs = x_sample.shape[0]
    assert x_sample.shape[1] == 1 and w_ada.shape[0] == 1 and seq % M_CHUNK == 0
    row1 = lambda v: v.reshape(1, -1)

    ws_in = _split_w_in(w_in[0])
    ws_out = (s5_w_glu[0].astype(BF16), row1(s5_b_glu[0]), w_branch_s5[0].astype(BF16),
              w_branch_ssd[0].astype(BF16), w_out[0].astype(BF16), row1(norm_post_mix[0]),
              row1(norm_pre_ffn[0]), row1(norm_post_ffn[0]), w_ffn_in[0].astype(BF16), w_ffn_out[0].astype(BF16))
    cw, cb = m_conv_w[0], row1(m_conv_b[0])
    dtb, alog = _pad_lanes(m_dt_bias[0]), _pad_lanes(m_a_log[0])
    dsk = row1(jnp.repeat(m_d[0], M_HEADDIM))
    nw = row1(m_norm[0])
    e64 = _expand_mat()

    mod = _ada(jnp.concatenate([c_prompt, c_sample], axis=0), w_ada[0], b_ada[0])
    mod_p = mod[:bp].reshape(bp, 1, 6 * D_MODEL)
    mod_s = mod[bp:]

    ar, ai, bbr, bbi = _s5_disc(s5_lam_re[0], s5_lam_im[0], s5_log_dt[0], s5_b_re[0], s5_b_im[0])
    s5m = _s5_matrices(ar, ai, bbr, bbi, s5_c_re[0], s5_c_im[0], s5_d[0])
    g_pre = row1(norm_pre_mix[0])

    xp = x_prompt.reshape(bp * seq, D_MODEL)
    xs = x_sample.reshape(bs, D_MODEL)
    tm = min(ROW_TILE, seq)
    u, zs, act, dt, gates, tail = _in_proj(xp, mod_p, g_pre, ws_in, dtb, cw, cb,
                                           per_row=False, tm=tm, rows_per_batch=seq)
    u_s, zs_s, xbc_s, dt_s, gates_s = _in_proj(xs, mod_s, g_pre, ws_in, dtb, cw, cb,
                                               per_row=True, tm=bs, rows_per_batch=1)

    p_conv = tail[:, SUBLANES - (M_CONV - 1):, :]
    zeros_s5 = jnp.zeros((bp, S5_GROUPS * S5_STATE), F32)
    (y5, p_re, p_im), (yb_s, s_ssm, s_conv) = _s5_ssd_step(
        u, zeros_s5, zeros_s5, *s5m, zs_s, xbc_s, dt_s, jnp.swapaxes(state_conv[0], 0, 1),
        state_ssm[0].reshape(bs, M_INNER, M_STATE), cw, cb, alog, dsk, nw, e64, tl=tm)
    yb, p_ssm = _ssd(zs, act, dt, alog, dsk, nw, e64, nbatch=bp, seq=seq)
    y_prompt = _out_stage(xp, y5, yb, gates, mod_p, ws_out, per_row=False, tm=tm, rows_per_batch=seq)

    y5_s, s_re, s_im = _s5(u_s, state_s5_re[0].reshape(bs, -1), state_s5_im[0].reshape(bs, -1), *s5m, tl=1)
    y_sample = _out_stage(xs, y5_s, yb_s, gates_s, mod_s, ws_out, per_row=True, tm=bs, rows_per_batch=1)

    s5_shape = (1, -1, S5_GROUPS, S5_STATE)
    ssm_shape = (1, -1, M_HEADS, M_HEADDIM, M_STATE)
    return (y_prompt.reshape(bp, seq, D_MODEL), y_sample.reshape(bs, 1, D_MODEL),
            p_re.reshape(s5_shape), p_im.reshape(s5_shape), p_ssm.reshape(ssm_shape), p_conv[None],
            s_re.reshape(s5_shape), s_im.reshape(s5_shape), s_ssm.reshape(ssm_shape), jnp.swapaxes(s_conv, 0, 1)[None])
```

```python
import functools

import jax
import jax.numpy as jnp
from jax import lax
from jax.experimental import pallas as pl
from jax.experimental.pallas import tpu as pltpu

F32 = jnp.float32
BF16 = jnp.bfloat16

D_MODEL = 1024
EPS = 1e-6
S5_GROUP = 16
S5_GROUPS = 64
S5_STATE = 64
M_INNER = 2048
M_HEADDIM = 64
M_HEADS = 32
M_GROUPS = 4
M_HPG = 8
M_STATE = 128
M_CONV = 4
M_CONV_DIM = 3072
M_CHUNK = 128
D_FF = 2816
OFF_Z = D_MODEL
OFF_XBC = OFF_Z + M_INNER
OFF_DT = OFF_XBC + M_CONV_DIM
OFF_GA = OFF_DT + M_HEADS
IN_COLS = OFF_GA + 2 * D_MODEL

LANES = 128
SUBLANES = 8
VMEM_LIMIT = 56 * 1024 * 1024

S5_CH = 256
S5_NS = 1024
S5_CHUNKS = D_MODEL // S5_CH
S5_TILES = S5_NS // LANES
S5_PIECES = S5_CH // LANES
S5_SUB_ROWS = 256
ROW_TILE = 256
INPROJ_SLAB = 512
INPROJ_DEPTH = 3

_NT = (((1,), (1,)), ((), ()))
_TN = (((0,), (0,)), ((), ()))


def _const_spec(shape):
    nd = len(shape)
    return pl.BlockSpec(shape, lambda *_: (0,) * nd, pipeline_mode=pl.Buffered(1))


def _params(sem):
    return pltpu.CompilerParams(dimension_semantics=sem, vmem_limit_bytes=VMEM_LIMIT)


def _sigmoid(x):
    return 0.5 * jnp.tanh(0.5 * x) + 0.5


def _silu(x):
    hx = 0.5 * x
    return hx + hx * jnp.tanh(hx)


def _softplus(x):
    return jnp.maximum(x, 0.0) + jnp.log1p(jnp.exp(-jnp.abs(x)))


def _gelu_tanh(x):
    return 0.5 * x * (1.0 + jnp.tanh(0.7978845608028654 * (x + 0.044715 * (x * x * x))))


def _rms(x, g):
    return x * lax.rsqrt(jnp.mean(x * x, axis=-1, keepdims=True) + EPS) * g


def _dot(a, b):
    return jnp.dot(a, b, preferred_element_type=F32)


def _pieces3(x):
    hi = x.astype(BF16).astype(F32)
    r1 = x - hi
    mid = r1.astype(BF16).astype(F32)
    lo = (r1 - mid).astype(BF16).astype(F32)
    return hi, mid, lo


def _ordered_after(x, dep):
    zero = jnp.where(dep[:, 0:LANES] > jnp.inf, 1.0, 0.0)
    return x + jnp.concatenate([zero] * (x.shape[1] // LANES), axis=1)


def _split2(x):
    return jnp.concatenate(_pieces3(x)[:2], axis=-1).astype(BF16)


def _split3(x):
    return jnp.concatenate(_pieces3(x), axis=-1).astype(BF16)


def _ada_kernel(c_ref, w_ref, b_ref, o_ref):
    s = _silu(c_ref[...]).astype(BF16)
    o_ref[...] = _dot(s, w_ref[...].astype(BF16)) + b_ref[...]


def _ada(c, w, b):
    n = c.shape[0]
    tn = 1024
    return pl.pallas_call(
        _ada_kernel,
        grid=(6 * D_MODEL // tn,),
        in_specs=[_const_spec((n, D_MODEL)),
                  pl.BlockSpec((D_MODEL, tn), lambda j: (0, j)),
                  pl.BlockSpec((1, tn), lambda j: (0, j))],
        out_specs=pl.BlockSpec((n, tn), lambda j: (0, j)),
        out_shape=jax.ShapeDtypeStruct((n, 6 * D_MODEL), F32),
        compiler_params=_params(("arbitrary",)),
        name="ada",
    )(c, w, b.reshape(1, -1))


def _s5_disc_kernel(lr_ref, li_ref, ldt_ref, br_ref, bi_ref, ar_ref, ai_ref, bbr_ref, bbi_ref):
    lr = lr_ref[...]
    li = li_ref[...]
    dt = jnp.exp(ldt_ref[...])
    mag = jnp.exp(lr * dt)
    ar = mag * jnp.cos(li * dt)
    ai = mag * jnp.sin(li * dt)
    den = lr * lr + li * li
    nr = ar - 1.0
    f_re = (nr * lr + ai * li) / den
    f_im = (ai * lr - nr * li) / den
    br = br_ref[...]
    bi = bi_ref[...]
    ar_ref[...] = ar
    ai_ref[...] = ai
    bbr_ref[...] = f_re * br - f_im * bi
    bbi_ref[...] = f_re * bi + f_im * br


def _s5_disc(lam_re, lam_im, log_dt, b_re, b_im):
    g, n = lam_re.shape
    w = S5_GROUP * n
    tile = lambda a: jnp.tile(a, (1, S5_GROUP))
    ldt = jnp.broadcast_to(log_dt[:, None], (g, w))
    brt = jnp.swapaxes(b_re, 1, 2).reshape(g, w)
    bit = jnp.swapaxes(b_im, 1, 2).reshape(g, w)
    shp = jax.ShapeDtypeStruct((g, w), F32)
    ar, ai, bbr, bbi = pl.pallas_call(
        _s5_disc_kernel,
        out_shape=(shp, shp, shp, shp),
        name="s5_disc",
    )(tile(lam_re), tile(lam_im), ldt, brt, bit)
    return (ar[:, :n], ai[:, :n], bbr.reshape(g, S5_GROUP, n), bbi.reshape(g, S5_GROUP, n))


def _s5_matrices(abar_re, abar_im, bbar_re, bbar_im, c_re, c_im, d_skip):
    ns = 2 * S5_NS
    b = jnp.concatenate([bbar_re.reshape(D_MODEL, S5_STATE), bbar_im.reshape(D_MODEL, S5_STATE)],
                        axis=1).reshape(S5_CHUNKS, S5_CH, 2 * S5_STATE)

    def by_tile(c):
        return jnp.swapaxes(c, 1, 2).reshape(S5_CHUNKS, S5_TILES, 1, 2, S5_STATE, S5_GROUP)
    c = jnp.concatenate([by_tile(c_re), by_tile(-c_im)], axis=2).reshape(S5_CHUNKS, ns, S5_GROUP)

    by_tile_a = lambda v: v.reshape(S5_CHUNKS, S5_TILES, 1, LANES)
    a = jnp.concatenate([by_tile_a(abar_re), by_tile_a(abar_im)], axis=2).reshape(S5_CHUNKS, 1, ns)
    d = d_skip.reshape(S5_CHUNKS, 1, S5_CH)
    return b, a, c, d


def _s5_state_group(i):
    return 2 * (i // (2 * LANES)) + (i % LANES) // S5_STATE


def _s5_expand_operands(b_ref, c_ref, bm_ref, cm_ref):
    tw = 2 * LANES
    row_g = lax.broadcasted_iota(jnp.int32, (S5_CH, LANES), 0) // S5_GROUP
    lane_half = lax.broadcasted_iota(jnp.int32, (S5_CH, LANES), 1) // S5_STATE
    for part in range(2):
        bp = b_ref[:, part * S5_STATE:(part + 1) * S5_STATE]
        pattern = jnp.concatenate([bp, bp], axis=1)
        for j in range(S5_TILES):
            tile = jnp.where(row_g == 2 * j + lane_half, pattern, 0.0)
            bm_ref[:, j * tw + part * LANES:j * tw + (part + 1) * LANES] = tile.astype(BF16)
    spread = (lax.broadcasted_iota(jnp.int32, (S5_GROUP, S5_CH), 1) % S5_GROUP
              == lax.broadcasted_iota(jnp.int32, (S5_GROUP, S5_CH), 0)).astype(BF16)
    rows = 512
    for r0 in range(0, 2 * S5_NS, rows):
        tiled = _dot(c_ref[r0:r0 + rows, :].astype(BF16), spread)
        own = (_s5_state_group(lax.broadcasted_iota(jnp.int32, (rows, S5_CH), 0) + r0)
               == lax.broadcasted_iota(jnp.int32, (rows, S5_CH), 1) // S5_GROUP)
        cm_ref[r0:r0 + rows, :] = jnp.where(own, tiled, 0.0).astype(BF16)


def _split_w_in_kernel(wt_ref, wm_ref, wdt_ref, wg_ref):
    tk = wt_ref.shape[1]
    step = 1024
    for j in range(0, OFF_DT, step):
        wm_ref[:, j:j + step] = wt_ref[j:j + step, :].T.astype(BF16)
    head = lax.broadcasted_iota(jnp.int32, (tk, LANES), 1) < M_HEADS
    wdt_ref[...] = jnp.where(head, wt_ref[OFF_DT:OFF_DT + LANES, :].T, 0.0).astype(BF16)
    for j in range(0, 2 * D_MODEL, step):
        wg_ref[:, j:j + step] = wt_ref[OFF_GA + j:OFF_GA + j + step, :].T.astype(BF16)


def _split_w_in(w):
    k = w.shape[0]
    tk = 256
    outs = (OFF_DT, LANES, 2 * D_MODEL)
    return pl.pallas_call(
        _split_w_in_kernel,
        grid=(k // tk,),
        in_specs=[pl.BlockSpec((IN_COLS, tk), lambda i: (0, i))],
        out_specs=[pl.BlockSpec((tk, n), lambda i: (i, 0)) for n in outs],
        out_shape=[jax.ShapeDtypeStruct((k, n), BF16) for n in outs],
        compiler_params=_params(("parallel",)),
        name="split_w_in",
    )(jnp.swapaxes(w, 0, 1))


def _inproj_kernel(x_ref, sh_ref, sc_ref, g_ref, wm_ref, wdt_ref, wg_ref, dtb_ref, cw_ref, cb_ref,
                   u_ref, z_ref, xbc_ref, dt_ref, gate_ref, *rest, tiles_per_seq):
    accs = rest[-INPROJ_DEPTH:]
    h = _rms(x_ref[...], g_ref[...]) * (1.0 + sc_ref[...]) + sh_ref[...]
    hb = h.astype(BF16)
    rows = hb.shape[0]
    step = INPROJ_SLAB

    if u_ref.shape[0] > rows:
        u_ref[rows:, :] = jnp.zeros((u_ref.shape[0] - rows, D_MODEL), F32)

    if tiles_per_seq is None:
        conv_fn = None
    else:
        tail_ref, carry_ref = rest[:2]

        @pl.when(pl.program_id(0) % tiles_per_seq == 0)
        def _():
            carry_ref[...] = jnp.zeros_like(carry_ref)

        def conv_fn(r, sl):
            width = sl.stop - sl.start
            w = [cw_ref[k:k + 1, sl] for k in range(M_CONV)]
            bias = cb_ref[:, sl]
            row = lax.broadcasted_iota(jnp.int32, (SUBLANES, width), 0)
            xp = carry_ref[:, sl]
            farp = xp * w[1] + pltpu.roll(xp, 1, axis=0) * w[0]
            outs = []
            for i in range(rows // SUBLANES):
                xi = r[i * SUBLANES:(i + 1) * SUBLANES, :]
                if outs:
                    xi = _ordered_after(xi, outs[-1])
                s1 = jnp.where(row == 0, pltpu.roll(xp, 1, axis=0), pltpu.roll(xi, 1, axis=0))
                far = xi * w[1] + s1 * w[0]
                s2 = jnp.where(row < 2, pltpu.roll(farp, 2, axis=0), pltpu.roll(far, 2, axis=0))
                outs.append(_silu(xi * w[3] + s1 * w[2] + s2 + bias))
                xp, farp = xi, far
            carry_ref[:, sl] = r[rows - SUBLANES:, :]
            tail_ref[:, sl] = r[rows - SUBLANES:, :]
            return jnp.concatenate(outs, axis=0)

    def slabs(o_ref, w_ref, w_off, fn):
        n = o_ref.shape[-1]
        return [(o_ref, slice(j, j + min(step, n - j)), w_ref, w_off + j, fn) for j in range(0, n, step)]

    su = slabs(u_ref, wm_ref, 0, None)
    sz = slabs(z_ref, wm_ref, OFF_Z, None)
    sx = slabs(xbc_ref, wm_ref, OFF_XBC, conv_fn)
    sd = slabs(dt_ref, wdt_ref, 0, lambda r, sl: _softplus(r + dtb_ref[:, sl]))
    sg = slabs(gate_ref, wg_ref, 0, None)
    light = su + sz + sg + sd
    order = []
    while sx or light:
        if sx:
            order.append(sx.pop(0))
        if light:
            order.append(light.pop(0))

    def matmul(s):
        o_ref, sl, w_ref, w_col, _ = order[s]
        accs[s % INPROJ_DEPTH][:, 0:sl.stop - sl.start] = _dot(hb, w_ref[:, w_col:w_col + sl.stop - sl.start])

    for s in range(min(INPROJ_DEPTH - 1, len(order))):
        matmul(s)
    for s, (o_ref, sl, _, _, fn) in enumerate(order):
        if s + INPROJ_DEPTH - 1 < len(order):
            matmul(s + INPROJ_DEPTH - 1)
        r = accs[s % INPROJ_DEPTH][:, 0:sl.stop - sl.start]
        o_ref[0:rows, sl] = r if fn is None else fn(r, sl)


def _mod_specs(per_row, tm, rows_per_mod, cols):
    if per_row:
        return [pl.BlockSpec((tm, D_MODEL), functools.partial(lambda i, c: (i, c), c=c)) for c in cols]
    return [pl.BlockSpec((None, 1, D_MODEL), functools.partial(lambda i, c: (i // rows_per_mod, 0, c), c=c))
            for c in cols]


def _s5_layout(per_row, t, tm, rows_per_batch, width):
    if per_row:
        return (1, t, width), pl.BlockSpec((None, tm, width), lambda i: (0, i, 0))
    nblk, pitch = rows_per_batch // tm, _s5_pitch(tm)
    return ((nblk, (t // rows_per_batch) * pitch, width),
            pl.BlockSpec((None, pitch, width), lambda i: (i % nblk, i // nblk, 0)))


def _in_proj(x, mod, g, ws, dtb, cw, cb, *, per_row, tm, rows_per_batch):
    t = x.shape[0]
    row = lambda n: pl.BlockSpec((tm, n), lambda i: (i, 0))
    outs = (M_INNER, M_CONV_DIM, LANES, 2 * D_MODEL)
    u_shape, u_spec = _s5_layout(per_row, t, tm, rows_per_batch, D_MODEL)
    out_specs = [u_spec] + [row(n) for n in outs]
    out_shape = [jax.ShapeDtypeStruct(u_shape, F32)] + [jax.ShapeDtypeStruct((t, n), F32) for n in outs]
    scratch = []
    tiles_per_seq = None
    if not per_row:
        tiles_per_seq = rows_per_batch // tm
        out_specs.append(pl.BlockSpec((None, SUBLANES, M_CONV_DIM), lambda i: (i // tiles_per_seq, 0, 0)))
        out_shape.append(jax.ShapeDtypeStruct((t // rows_per_batch, SUBLANES, M_CONV_DIM), F32))
        scratch.append(pltpu.VMEM((SUBLANES, M_CONV_DIM), F32))
    scratch += [pltpu.VMEM((tm, INPROJ_SLAB), F32)] * INPROJ_DEPTH
    consts = (g,) + tuple(ws) + (dtb, cw, cb)
    return pl.pallas_call(
        functools.partial(_inproj_kernel, tiles_per_seq=tiles_per_seq),
        grid=(t // tm,),
        in_specs=[row(D_MODEL)] + _mod_specs(per_row, tm, rows_per_batch // tm if not per_row else 1, (0, 1))
                 + [_const_spec(w.shape) for w in consts],
        out_specs=out_specs,
        out_shape=out_shape,
        scratch_shapes=scratch,
        compiler_params=_params(("arbitrary",)),
        name="in_proj",
    )(x, mod, mod, *consts)


def _s5_pitch(tl):
    return tl if tl == 1 else tl + SUBLANES


def _s5_kernel(*refs, nb, tl, sub_rows, side=None):
    nu = S5_PIECES
    u_refs = refs[:nu]
    h0r_ref, h0i_ref, b_ref, a_ref, c_ref, d_ref = refs[nu:nu + 6]
    y_refs = refs[nu + 6:2 * nu + 6]
    hr_ref, hi_ref, hst_ref, bu0_ref, xs0_ref, bu1_ref, xs1_ref, bm_ref, cm_ref = refs[2 * nu + 6:]
    bufs = ((bu0_ref, xs0_ref), (bu1_ref, xs1_ref))
    tb = pl.program_id(1)
    tsub = sub_rows // nb
    tw = 2 * LANES
    pitch = _s5_pitch(tl)

    def step_rows(t):
        return pl.ds(t, nb, stride=pitch) if pitch > 1 else pl.ds(0, nb)

    @pl.when(tb == 0)
    def _():
        _s5_expand_operands(b_ref, c_ref, bm_ref, cm_ref)
        for j in range(S5_TILES):
            hst_ref[:, j * tw:j * tw + LANES] = h0r_ref[:, j * LANES:(j + 1) * LANES]
            hst_ref[:, j * tw + LANES:(j + 1) * tw] = h0i_ref[:, j * LANES:(j + 1) * LANES]

    ar = [jnp.broadcast_to(a_ref[:, j * tw:j * tw + LANES], (nb, LANES)) for j in range(S5_TILES)]
    ai = [jnp.broadcast_to(a_ref[:, j * tw + LANES:(j + 1) * tw], (nb, LANES)) for j in range(S5_TILES)]
    dsk = d_ref[...]

    def project_in(t0, bu_ref, xs_ref):
        xs = jnp.concatenate([jnp.concatenate([r[step_rows(t0 + i), :] for r in u_refs], axis=1)
                              for i in range(tsub)], axis=0)
        xs_ref[...] = xs
        lhs = xs.astype(BF16)
        for j in range(S5_TILES):
            bu_ref[:, j * tw:(j + 1) * tw] = _dot(lhs, bm_ref[:, j * tw:(j + 1) * tw])

    def scan_project_out(t0, bu_ref, xs_ref, carry):
        new, hs = [], []
        for j in range(S5_TILES):
            hr, hi = carry[2 * j], carry[2 * j + 1]
            rows = []
            for i in range(tsub):
                br = bu_ref[i * nb:(i + 1) * nb, j * tw:j * tw + LANES]
                bi = bu_ref[i * nb:(i + 1) * nb, j * tw + LANES:(j + 1) * tw]
                hr, hi = ar[j] * hr - ai[j] * hi + br, ar[j] * hi + ai[j] * hr + bi
                rows.append(jnp.concatenate([hr, hi], axis=1))
            new += [hr, hi]
            hs.append(jnp.concatenate(rows, axis=0).astype(BF16))
        y = _dot(jnp.concatenate(hs, axis=1), cm_ref[...]) + dsk * xs_ref[...]
        for i in range(tsub):
            for k, y_ref in enumerate(y_refs):
                y_ref[step_rows(t0 + i), :] = y[i * nb:(i + 1) * nb, k * LANES:(k + 1) * LANES]
        return tuple(new)

    h_in = []
    for j in range(S5_TILES):
        h_in += [hst_ref[:, j * tw:j * tw + LANES], hst_ref[:, j * tw + LANES:(j + 1) * tw]]
    nsub = tl // tsub
    (bu0, xs0), (bu1, xs1) = bufs
    project_in(0, bu0, xs0)
    if nsub == 1:
        h_out = scan_project_out(0, bu0, xs0, tuple(h_in))
    else:
        def pair(p, carry):
            t0 = 2 * p * tsub
            project_in(t0 + tsub, bu1, xs1)
            carry = scan_project_out(t0, bu0, xs0, carry)
            project_in(jnp.minimum(t0 + 2 * tsub, tl - tsub), bu0, xs0)
            return scan_project_out(t0 + tsub, bu1, xs1, carry)
        h_out = lax.fori_loop(0, nsub // 2, pair, tuple(h_in))
    for j in range(S5_TILES):
        hst_ref[:, j * tw:j * tw + LANES] = h_out[2 * j]
        hst_ref[:, j * tw + LANES:(j + 1) * tw] = h_out[2 * j + 1]

    if pitch > tl:
        for y_ref in y_refs:
            for b in range(nb):
                y_ref[b * pitch + tl:(b + 1) * pitch, :] = jnp.zeros((pitch - tl, LANES), F32)

    @pl.when(tb == pl.num_programs(1) - 1)
    def _():
        for j in range(S5_TILES):
            hr_ref[:, j * LANES:(j + 1) * LANES] = h_out[2 * j]
            hi_ref[:, j * LANES:(j + 1) * LANES] = h_out[2 * j + 1]

    if side is not None:
        side()


def _s5(u, h0r, h0i, b, a, c, d, *, tl):
    nblk, rows, _ = u.shape
    pitch = _s5_pitch(tl)
    nb = rows // pitch
    sub_rows = min(S5_SUB_ROWS, tl * nb)
    nsub = tl * nb // sub_rows
    assert sub_rows % nb == 0 and tl * nb % sub_rows == 0 and (nsub == 1 or nsub % 2 == 0)
    chunk3 = lambda s: pl.BlockSpec((None,) + s, lambda c, t: (c, 0, 0))
    st_spec = pl.BlockSpec((nb, S5_NS), lambda c, t: (0, c))
    nu = S5_PIECES
    u_specs = [pl.BlockSpec((None, rows, LANES), functools.partial(lambda c, t, k: (t, 0, nu * c + k), k=k))
               for k in range(nu)]
    y_spec = pl.BlockSpec((None, rows, LANES), lambda c, t: (t, 0, c))
    y_shape = jax.ShapeDtypeStruct((nblk, rows, D_MODEL // nu), F32)
    res = pl.pallas_call(
        functools.partial(_s5_kernel, nb=nb, tl=tl, sub_rows=sub_rows),
        grid=(S5_CHUNKS, nblk),
        in_specs=u_specs + [st_spec, st_spec, chunk3((S5_CH, 2 * S5_STATE)), chunk3((1, 2 * S5_NS)),
                            chunk3((2 * S5_NS, S5_GROUP)), chunk3((1, S5_CH))],
        out_specs=[y_spec] * nu + [st_spec, st_spec],
        out_shape=[y_shape] * nu + [jax.ShapeDtypeStruct(h0r.shape, F32), jax.ShapeDtypeStruct(h0i.shape, F32)],
        scratch_shapes=[pltpu.VMEM((nb, 2 * S5_NS), F32)]
                       + [pltpu.VMEM((sub_rows, 2 * S5_NS), F32), pltpu.VMEM((sub_rows, S5_CH), F32)] * 2
                       + [pltpu.VMEM((S5_CH, 2 * S5_NS), BF16), pltpu.VMEM((2 * S5_NS, S5_CH), BF16)],
        compiler_params=_params(("parallel", "arbitrary")),
        name="s5",
    )(*([u] * nu), h0r, h0i, b, a, c, d)
    return res[:nu], res[nu], res[nu + 1]


def _expand_mat():
    r = lax.broadcasted_iota(jnp.int32, (3 * LANES, M_HEADS * M_HEADDIM), 0) % LANES
    c = lax.broadcasted_iota(jnp.int32, (3 * LANES, M_HEADS * M_HEADDIM), 1) // M_HEADDIM
    return (r == c).astype(BF16)


LOG2E = 1.4426950408889634
SSD_CHUNKS_PER_STEP = 4


def _ssd_kernel(*refs, n_cast):
    z_ref, act_ref, dt_ref, alog_ref, dsk_ref, nw_ref, e64_ref = refs[:7]
    w_refs = refs[7:7 + n_cast]
    y_ref, hout_ref = refs[7 + n_cast:9 + n_cast]
    wb_refs = refs[9 + n_cast:9 + 2 * n_cast]
    ht_ref, xs_ref, xdte_ref, eacs_ref, yacc_ref = refs[9 + 2 * n_cast:]
    c = pl.program_id(1)
    for w_ref, wb_ref in zip(w_refs, wb_refs):
        wb_ref[...] = w_ref[...].astype(BF16)

    @pl.when(c == 0)
    def _():
        ht_ref[...] = jnp.zeros_like(ht_ref)

    for k in range(SSD_CHUNKS_PER_STEP):
        rows = pl.ds(k * M_CHUNK, M_CHUNK)
        _ssd_chunk(z_ref.at[rows], act_ref.at[rows], dt_ref.at[rows], alog_ref, dsk_ref, nw_ref, e64_ref,
                   y_ref.at[rows], ht_ref, xs_ref.at[k], xdte_ref.at[k], eacs_ref.at[k], yacc_ref.at[k])

    @pl.when(c == pl.num_programs(1) - 1)
    def _():
        for j in range(0, M_INNER, LANES):
            hout_ref[j:j + LANES, :] = ht_ref[:, j:j + LANES].T


def _ssd_chunk(z_ref, act_ref, dt_ref, alog_ref, dsk_ref, nw_ref, e64_ref,
               y_ref, ht_ref, xs_ref, xdte_ref, eacs_ref, yacc_ref):
    q = M_CHUNK
    cs = 512

    dt = dt_ref[...]
    a2 = dt * (-jnp.exp(alog_ref[...]) * LOG2E)
    ri = lax.broadcasted_iota(jnp.int32, (q, q), 0)
    ci = lax.broadcasted_iota(jnp.int32, (q, q), 1)
    causal = ri >= ci
    acs = jnp.dot(causal.astype(F32), a2, preferred_element_type=F32, precision=lax.Precision.HIGHEST)
    src_t = (acs - jnp.log2(dt)).T
    eacs2 = _split2(jnp.exp2(acs))
    dte2 = _split2(dt * jnp.exp2(acs[q - 1:q, :] - acs))
    for j in range(0, M_INNER, cs):
        sl = slice(j, j + cs)
        e = e64_ref[0:2 * LANES, sl]
        xs = act_ref[:, sl]
        xs_ref[:, sl] = xs.astype(BF16)
        xdte_ref[:, sl] = (xs * _dot(dte2, e)).astype(BF16)
        eacs_ref[:, sl] = _dot(eacs2, e)

    hp = M_HPG * M_HEADDIM
    lane = lax.broadcasted_iota(jnp.int32, (q, LANES), 1)
    first_head = lane < M_HEADDIM
    for g in range(M_GROUPS):
        gs = slice(g * hp, (g + 1) * hp)
        bm_t = act_ref[:, M_INNER + g * M_STATE:M_INNER + (g + 1) * M_STATE].T.astype(BF16)
        cm = act_ref[:, M_INNER + (M_GROUPS + g) * M_STATE:M_INNER + (M_GROUPS + g + 1) * M_STATE].astype(BF16)
        cbm = _dot(cm, bm_t)
        y_off = _dot(cm, ht_ref[:, gs].astype(BF16))
        st_t = _dot(bm_t, xdte_ref[:, gs])
        for pr in range(M_HPG // 2):
            h0 = g * M_HPG + 2 * pr
            ps = slice(h0 * M_HEADDIM, (h0 + 2) * M_HEADDIM)
            ms = []
            for h in (h0, h0 + 1):
                d2 = jnp.broadcast_to(acs[:, h:h + 1], (q, q)) - src_t[h:h + 1, :]
                ms.append((cbm * jnp.where(causal, jnp.exp2(d2), 0.0)).astype(BF16))
            xp = xs_ref[:, ps]
            zero = jnp.zeros_like(xp)
            rhs = jnp.concatenate([jnp.where(first_head, xp, zero), jnp.where(first_head, zero, xp)], axis=0)
            yd = _dot(jnp.concatenate(ms, axis=1), rhs)
            yacc_ref[:, ps] = yd + y_off[:, ps.start - gs.start:ps.stop - gs.start] * eacs_ref[:, ps] \
                + dsk_ref[:, ps] * act_ref[:, ps]
        ht_ref[:, gs] = eacs_ref[q - 1:q, gs] * ht_ref[:, gs] + st_t

    ss = jnp.zeros((q, 1), F32)
    for j in range(0, M_INNER, cs):
        sl = slice(j, j + cs)
        gt = yacc_ref[:, sl] * _silu(z_ref[:, sl])
        yacc_ref[:, sl] = gt
        ss = ss + jnp.sum(gt * gt, axis=-1, keepdims=True)
    inv = lax.rsqrt(ss * (1.0 / M_INNER) + EPS)
    for j in range(0, M_INNER, cs):
        sl = slice(j, j + cs)
        y_ref[:, sl] = yacc_ref[:, sl] * inv * nw_ref[:, sl]


def _cast_spec(w, nbatch, nc):
    rows, steps = w.shape[0], nbatch * nc
    n = next(n for n in range(steps, 0, -1) if rows % n == 0 and (rows // n) % (2 * SUBLANES) == 0)
    return pl.BlockSpec((rows // n, w.shape[1]), lambda b, c: (jnp.minimum(b * nc + c, n - 1), 0))


def _ssd(z, act, dt, alog, dsk, nw, e64, cast_ws, *, nbatch, seq):
    q = M_CHUNK
    k = SSD_CHUNKS_PER_STEP
    assert seq % (k * q) == 0
    nc = seq // (k * q)
    row = lambda n: pl.BlockSpec((k * q, n), lambda b, c: (b * nc + c, 0))
    cast_specs = [_cast_spec(w, nbatch, nc) for w in cast_ws]
    res = pl.pallas_call(
        functools.partial(_ssd_kernel, n_cast=len(cast_ws)),
        grid=(nbatch, nc),
        in_specs=[row(M_INNER), row(M_CONV_DIM), row(LANES), _const_spec(alog.shape),
                  _const_spec(dsk.shape), _const_spec(nw.shape), _const_spec(e64.shape)] + cast_specs,
        out_specs=[row(M_INNER), pl.BlockSpec((None, M_INNER, M_STATE), lambda b, c: (b, 0, 0))] + cast_specs,
        out_shape=[jax.ShapeDtypeStruct((nbatch * seq, M_INNER), F32),
                   jax.ShapeDtypeStruct((nbatch, M_INNER, M_STATE), F32)]
                  + [jax.ShapeDtypeStruct(w.shape, BF16) for w in cast_ws],
        scratch_shapes=[pltpu.VMEM((M_STATE, M_INNER), F32), pltpu.VMEM((k, q, M_INNER), BF16),
                        pltpu.VMEM((k, q, M_INNER), BF16), pltpu.VMEM((k, q, M_INNER), F32),
                        pltpu.VMEM((k, q, M_INNER), F32)],
        compiler_params=_params(("arbitrary", "arbitrary")),
        name="ssd",
    )(z, act, dt, alog, dsk, nw, e64, *cast_ws)
    return res[0], res[1], res[2:]


SSD_STEP_BB = 8


def _ssd_step_kernel(z_ref, xbc_ref, dt_ref, cbuf_ref, h0_ref, cw_ref, cb_ref, alog_ref, dsk_ref, nw_ref,
                     e64_ref, y_ref, hout_ref, convout_ref):
    bb = SSD_STEP_BB
    x_new = xbc_ref[...]
    conv = cb_ref[...] + x_new * cw_ref[M_CONV - 1:M_CONV, :]
    for k in range(M_CONV - 1):
        conv = conv + cbuf_ref[k] * cw_ref[k:k + 1, :]
    act = _silu(conv)
    for k in range(M_CONV - 2):
        convout_ref[k] = cbuf_ref[k + 1]
    convout_ref[M_CONV - 2] = x_new

    xs = act[:, :M_INNER]
    bmat = act[:, M_INNER:M_INNER + M_GROUPS * M_STATE]
    cmat = act[:, M_INNER + M_GROUPS * M_STATE:]
    dt = dt_ref[...]
    da = jnp.exp(dt * (-jnp.exp(alog_ref[...])))
    dt_e = _dot(_split3(dt), e64_ref[...])
    da_e = _dot(_split3(da), e64_ref[...])
    xdt = xs * dt_e

    hp = M_HPG * M_HEADDIM
    lane_group = lax.broadcasted_iota(jnp.int32, (M_GROUPS, M_INNER), 1) // hp
    row_group = lax.broadcasted_iota(jnp.int32, (M_GROUPS, M_INNER), 0)
    gmask = lane_group == row_group
    ones_rows = lax.broadcasted_iota(jnp.int32, (SUBLANES, M_STATE), 0) >= M_GROUPS

    cbx = []
    for g in range(M_GROUPS):
        cbg = jnp.sum(cmat[:, g * M_STATE:(g + 1) * M_STATE] * bmat[:, g * M_STATE:(g + 1) * M_STATE],
                      axis=-1, keepdims=True)
        cbx.append(jnp.broadcast_to(cbg, (bb, hp)))
    y = jnp.concatenate(cbx, axis=1) * xdt + dsk_ref[...] * xs

    y_off_rows = []
    for b in range(bb):
        xsel = jnp.where(gmask, jnp.broadcast_to(xdt[b:b + 1, :], (M_GROUPS, M_INNER)), 0.0)
        lhs = jnp.concatenate([xsel, *_pieces3(da_e[b:b + 1, :]), jnp.zeros((1, M_INNER), F32)],
                              axis=0).astype(BF16)
        bm4 = jnp.concatenate([bmat[b:b + 1, g * M_STATE:(g + 1) * M_STATE] for g in range(M_GROUPS)], axis=0)
        cm4 = jnp.concatenate([cmat[b:b + 1, g * M_STATE:(g + 1) * M_STATE] for g in range(M_GROUPS)], axis=0)
        zeros4 = jnp.zeros((SUBLANES - M_GROUPS, M_STATE), F32)
        rhs = jnp.concatenate([jnp.concatenate([bm4, zeros4], axis=0),
                               jnp.where(ones_rows, 1.0, 0.0)], axis=1).astype(BF16)
        res = lax.dot_general(lhs, rhs, _TN, preferred_element_type=F32)
        h0 = h0_ref[b]
        hout_ref[b] = res[:, M_STATE:] * h0 + res[:, :M_STATE]
        cm8 = jnp.concatenate([cm4, zeros4], axis=0).astype(BF16)
        yo = lax.dot_general(cm8, h0.astype(BF16), _NT, preferred_element_type=F32)
        y_off_rows.append(jnp.sum(jnp.where(gmask, yo[:M_GROUPS, :], 0.0), axis=0, keepdims=True))
    y = y + da_e * jnp.concatenate(y_off_rows, axis=0)

    y_ref[...] = _rms(y * _silu(z_ref[...]), nw_ref[...])


SSD_STEP_EVERY = 2


def _s5_ssd_step_kernel(*refs, nb, tl, sub_rows):
    n5i, nsi = S5_PIECES + 6, 11
    n5o, nso = S5_PIECES + 2, 3
    ins5, inss = refs[:n5i], refs[n5i:n5i + nsi]
    o = n5i + nsi
    outs5, outss = refs[o:o + n5o], refs[o + n5o:o + n5o + nso]
    sc5 = refs[o + n5o + nso:]
    s = pl.program_id(0) * pl.num_programs(1) + pl.program_id(1)

    def side():
        @pl.when(s % SSD_STEP_EVERY == 0)
        def _():
            _ssd_step_kernel(*inss, *outss)

    _s5_kernel(*ins5, *outs5, *sc5, nb=nb, tl=tl, sub_rows=sub_rows, side=side)


def _s5_ssd_step(u, h0r, h0i, b, a, c, d, zs, xbc, dt, cbuf, h0, cw, cb, alog, dsk, nw, e64, *, tl):
    nblk, rows, _ = u.shape
    pitch = _s5_pitch(tl)
    nb = rows // pitch
    sub_rows = min(S5_SUB_ROWS, tl * nb)
    nsub = tl * nb // sub_rows
    ns = zs.shape[0]
    bb, per = SSD_STEP_BB, SSD_STEP_EVERY
    assert sub_rows % nb == 0 and tl * nb % sub_rows == 0 and nsub % 2 == 0 and ns * per == bb * S5_CHUNKS * nblk
    chunk3 = lambda sh: pl.BlockSpec((None,) + sh, lambda c, t: (c, 0, 0))
    st_spec = pl.BlockSpec((nb, S5_NS), lambda c, t: (0, c))
    nu = S5_PIECES
    u_specs = [pl.BlockSpec((None, rows, LANES), functools.partial(lambda c, t, k: (t, 0, nu * c + k), k=k))
               for k in range(nu)]
    y5_spec = pl.BlockSpec((None, rows, LANES), lambda c, t: (t, 0, c))
    y5_shape = jax.ShapeDtypeStruct((nblk, rows, D_MODEL // nu), F32)
    step = lambda c, t: c * nblk + t
    row = lambda w: pl.BlockSpec((bb, w), lambda c, t: (step(c, t) // per, 0))
    conv_spec = pl.BlockSpec((M_CONV - 1, bb, M_CONV_DIM), lambda c, t: (0, step(c, t) // per, 0))
    st3_spec = pl.BlockSpec((bb, M_INNER, M_STATE), lambda c, t: (step(c, t) // per, 0, 0))
    res = pl.pallas_call(
        functools.partial(_s5_ssd_step_kernel, nb=nb, tl=tl, sub_rows=sub_rows),
        grid=(S5_CHUNKS, nblk),
        in_specs=u_specs + [st_spec, st_spec, chunk3((S5_CH, 2 * S5_STATE)), chunk3((1, 2 * S5_NS)),
                            chunk3((2 * S5_NS, S5_GROUP)), chunk3((1, S5_CH))]
                 + [row(M_INNER), row(M_CONV_DIM), row(LANES), conv_spec, st3_spec,
                    _const_spec(cw.shape), _const_spec(cb.shape), _const_spec(alog.shape),
                    _const_spec(dsk.shape), _const_spec(nw.shape), _const_spec(e64.shape)],
        out_specs=[y5_spec] * nu + [st_spec, st_spec] + [row(M_INNER), st3_spec, conv_spec],
        out_shape=[y5_shape] * nu + [jax.ShapeDtypeStruct(h0r.shape, F32), jax.ShapeDtypeStruct(h0i.shape, F32)]
                  + [jax.ShapeDtypeStruct((ns, M_INNER), F32), jax.ShapeDtypeStruct((ns, M_INNER, M_STATE), F32),
                     jax.ShapeDtypeStruct((M_CONV - 1, ns, M_CONV_DIM), F32)],
        scratch_shapes=[pltpu.VMEM((nb, 2 * S5_NS), F32)]
                       + [pltpu.VMEM((sub_rows, 2 * S5_NS), F32), pltpu.VMEM((sub_rows, S5_CH), F32)] * 2
                       + [pltpu.VMEM((S5_CH, 2 * S5_NS), BF16), pltpu.VMEM((2 * S5_NS, S5_CH), BF16)],
        compiler_params=_params(("arbitrary", "arbitrary")),
        name="s5_ssd_step",
    )(*([u] * nu), h0r, h0i, b, a, c, d, zs, xbc, dt, cbuf, h0, cw, cb, alog, dsk, nw, e64)
    return (res[:nu], res[nu], res[nu + 1]), (res[nu + 2], res[nu + 3], res[nu + 4])


MXU_WIDTH = 256
FFN_CHUNKS = (0, 6 * MXU_WIDTH, D_FF)


def _out_kernel(x_ref, *refs):
    y5_refs = refs[:S5_PIECES]
    (yb_ref, gate_ref, gtm_ref, shf_ref, scf_ref, gtf_ref, wglu_ref, bglu_ref, wb5_ref, wbs_ref, wout_ref,
     npm_ref, npf_ref, npo_ref, wfi_ref, wfo_ref, o_ref) = refs[S5_PIECES:]
    rows = x_ref.shape[0]
    y5 = jnp.concatenate([y5_refs[k][0:rows, c * LANES:(c + 1) * LANES]
                          for c in range(S5_CHUNKS) for k in range(S5_PIECES)], axis=1)
    ya = _gelu_tanh(y5)
    glu = ya * _sigmoid(_dot(ya.astype(BF16), wglu_ref[...]) + bglu_ref[...])
    merged = (_sigmoid(gate_ref[:, :D_MODEL]) * _dot(glu.astype(BF16), wb5_ref[...])
              + _sigmoid(gate_ref[:, D_MODEL:]) * _dot(yb_ref[...].astype(BF16), wbs_ref[...]))
    mix = _dot(merged.astype(BF16), wout_ref[...])
    x1 = x_ref[...] + gtm_ref[...] * _rms(mix, npm_ref[...])

    hb = (_rms(x1, npf_ref[...]) * (1.0 + scf_ref[...]) + shf_ref[...]).astype(BF16)
    f = jnp.zeros_like(x1)
    for j, e in zip(FFN_CHUNKS[:-1], FFN_CHUNKS[1:]):
        gg = _dot(hb, wfi_ref[:, j:e])
        uu = _dot(hb, wfi_ref[:, D_FF + j:D_FF + e])
        f = f + _dot((_silu(gg) * uu).astype(BF16), wfo_ref[j:e, :])
    o_ref[...] = x1 + gtf_ref[...] * _rms(f, npo_ref[...])


def _out_stage(x, y5, yb, gates, mod, ws, *, per_row, tm, rows_per_batch):
    t = x.shape[0]
    row = lambda n: pl.BlockSpec((tm, n), lambda i: (i, 0))
    mods = _mod_specs(per_row, tm, rows_per_batch // tm if not per_row else 1, (2, 3, 4, 5))
    _, y5_spec = _s5_layout(per_row, t, tm, rows_per_batch, D_MODEL // S5_PIECES)
    return pl.pallas_call(
        _out_kernel,
        grid=(t // tm,),
        in_specs=[row(D_MODEL)] + [y5_spec] * S5_PIECES + [row(M_INNER), row(2 * D_MODEL)] + mods
                 + [_const_spec(w.shape) for w in ws],
        out_specs=row(D_MODEL),
        out_shape=jax.ShapeDtypeStruct((t, D_MODEL), F32),
        compiler_params=_params(("parallel",)),
        name="out_stage",
    )(x, *y5, yb, gates, mod, mod, mod, mod, *ws)


def _pad_lanes(v):
    return jnp.pad(v.reshape(1, -1), ((0, 0), (0, LANES - v.shape[-1])))


def kernel(x_prompt, x_sample, state_s5_re, state_s5_im, state_ssm, state_conv, c_prompt, c_sample, w_ada, b_ada, norm_pre_mix, norm_post_mix, norm_pre_ffn, norm_post_ffn, w_in, s5_lam_re, s5_lam_im, s5_log_dt, s5_b_re, s5_b_im, s5_c_re, s5_c_im, s5_d, s5_w_glu, s5_b_glu, m_conv_w, m_conv_b, m_dt_bias, m_a_log, m_d, m_norm, w_branch_s5, w_branch_ssd, w_out, w_ffn_in, w_ffn_out):
    bp, seq, _ = x_prompt.shape
    bs = x_sample.shape[0]
    assert x_sample.shape[1] == 1 and w_ada.shape[0] == 1 and seq % M_CHUNK == 0
    row1 = lambda v: v.reshape(1, -1)

    ws_in = _split_w_in(w_in[0])
    cw, cb = m_conv_w[0], row1(m_conv_b[0])
    dtb, alog = _pad_lanes(m_dt_bias[0]), _pad_lanes(m_a_log[0])
    dsk = row1(jnp.repeat(m_d[0], M_HEADDIM))
    nw = row1(m_norm[0])
    e64 = _expand_mat()

    mod = _ada(jnp.concatenate([c_prompt, c_sample], axis=0), w_ada[0], b_ada[0])
    mod_p = mod[:bp].reshape(bp, 1, 6 * D_MODEL)
    mod_s = mod[bp:]

    ar, ai, bbr, bbi = _s5_disc(s5_lam_re[0], s5_lam_im[0], s5_log_dt[0], s5_b_re[0], s5_b_im[0])
    s5m = _s5_matrices(ar, ai, bbr, bbi, s5_c_re[0], s5_c_im[0], s5_d[0])
    g_pre = row1(norm_pre_mix[0])

    xp = x_prompt.reshape(bp * seq, D_MODEL)
    xs = x_sample.reshape(bs, D_MODEL)
    tm = min(ROW_TILE, seq)
    u, zs, act, dt, gates, tail = _in_proj(xp, mod_p, g_pre, ws_in, dtb, cw, cb,
                                           per_row=False, tm=tm, rows_per_batch=seq)
    u_s, zs_s, xbc_s, dt_s, gates_s = _in_proj(xs, mod_s, g_pre, ws_in, dtb, cw, cb,
                                               per_row=True, tm=bs, rows_per_batch=1)

    p_conv = tail[:, SUBLANES - (M_CONV - 1):, :]
    zeros_s5 = jnp.zeros((bp, S5_GROUPS * S5_STATE), F32)
    (y5, p_re, p_im), (yb_s, s_ssm, s_conv) = _s5_ssd_step(
        u, zeros_s5, zeros_s5, *s5m, zs_s, xbc_s, dt_s, jnp.swapaxes(state_conv[0], 0, 1),
        state_ssm[0].reshape(bs, M_INNER, M_STATE), cw, cb, alog, dsk, nw, e64, tl=tm)
    yb, p_ssm, (wglu, wb5, wbs, wo, wfi, wfo) = _ssd(
        zs, act, dt, alog, dsk, nw, e64,
        (s5_w_glu[0], w_branch_s5[0], w_branch_ssd[0], w_out[0], w_ffn_in[0], w_ffn_out[0]), nbatch=bp, seq=seq)
    ws_out = (wglu, row1(s5_b_glu[0]), wb5, wbs, wo, row1(norm_post_mix[0]),
              row1(norm_pre_ffn[0]), row1(norm_post_ffn[0]), wfi, wfo)
    y_prompt = _out_stage(xp, y5, yb, gates, mod_p, ws_out, per_row=False, tm=tm, rows_per_batch=seq)

    y5_s, s_re, s_im = _s5(u_s, state_s5_re[0].reshape(bs, -1), state_s5_im[0].reshape(bs, -1), *s5m, tl=1)
    y_sample = _out_stage(xs, y5_s, yb_s, gates_s, mod_s, ws_out, per_row=True, tm=bs, rows_per_batch=1)

    s5_shape = (1, -1, S5_GROUPS, S5_STATE)
    ssm_shape = (1, -1, M_HEADS, M_HEADDIM, M_STATE)
    return (y_prompt.reshape(bp, seq, D_MODEL), y_sample.reshape(bs, 1, D_MODEL),
            p_re.reshape(s5_shape), p_im.reshape(s5_shape), p_ssm.reshape(ssm_shape), p_conv[None],
            s_re.reshape(s5_shape), s_im.reshape(s5_shape), s_ssm.reshape(ssm_shape), jnp.swapaxes(s_conv, 0, 1)[None])
```

```python
import functools

import jax
import jax.numpy as jnp
from jax import lax
from jax.experimental import pallas as pl
from jax.experimental.pallas import tpu as pltpu

F32 = jnp.float32
BF16 = jnp.bfloat16

D_MODEL = 1024
EPS = 1e-6
S5_GROUP = 16
S5_GROUPS = 64
S5_STATE = 64
M_INNER = 2048
M_HEADDIM = 64
M_HEADS = 32
M_GROUPS = 4
M_HPG = 8
M_STATE = 128
M_CONV = 4
M_CONV_DIM = 3072
M_CHUNK = 128
D_FF = 2816
OFF_Z = D_MODEL
OFF_XBC = OFF_Z + M_INNER
OFF_DT = OFF_XBC + M_CONV_DIM
OFF_GA = OFF_DT + M_HEADS
IN_COLS = OFF_GA + 2 * D_MODEL

LANES = 128
SUBLANES = 8
VMEM_LIMIT = 56 * 1024 * 1024

S5_CH = 256
S5_NS = 1024
S5_CHUNKS = D_MODEL // S5_CH
S5_TILES = S5_NS // LANES
S5_PIECES = S5_CH // LANES
S5_SUB_ROWS = 256
ROW_TILE = 256
INPROJ_SLAB = 512
INPROJ_DEPTH = 3

_NT = (((1,), (1,)), ((), ()))
_TN = (((0,), (0,)), ((), ()))


def _const_spec(shape):
    nd = len(shape)
    return pl.BlockSpec(shape, lambda *_: (0,) * nd, pipeline_mode=pl.Buffered(1))


def _params(sem):
    return pltpu.CompilerParams(dimension_semantics=sem, vmem_limit_bytes=VMEM_LIMIT)


def _sigmoid(x):
    return 0.5 * jnp.tanh(0.5 * x) + 0.5


def _silu(x):
    hx = 0.5 * x
    return hx + hx * jnp.tanh(hx)


def _softplus(x):
    return jnp.maximum(x, 0.0) + jnp.log1p(jnp.exp(-jnp.abs(x)))


def _gelu_tanh(x):
    return 0.5 * x * (1.0 + jnp.tanh(0.7978845608028654 * (x + 0.044715 * (x * x * x))))


def _rms(x, g):
    return x * lax.rsqrt(jnp.mean(x * x, axis=-1, keepdims=True) + EPS) * g


def _dot(a, b):
    return jnp.dot(a, b, preferred_element_type=F32)


def _pieces3(x):
    hi = x.astype(BF16).astype(F32)
    r1 = x - hi
    mid = r1.astype(BF16).astype(F32)
    lo = (r1 - mid).astype(BF16).astype(F32)
    return hi, mid, lo


def _ordered_after(x, dep):
    zero = jnp.where(dep[:, 0:LANES] > jnp.inf, 1.0, 0.0)
    return x + jnp.concatenate([zero] * (x.shape[1] // LANES), axis=1)


def _split2(x):
    return jnp.concatenate(_pieces3(x)[:2], axis=-1).astype(BF16)


def _split3(x):
    return jnp.concatenate(_pieces3(x), axis=-1).astype(BF16)


def _ada_kernel(c_ref, w_ref, b_ref, o_ref):
    s = _silu(c_ref[...]).astype(BF16)
    o_ref[...] = _dot(s, w_ref[...].astype(BF16)) + b_ref[...]


def _ada(c, w, b):
    n = c.shape[0]
    tn = 2048
    return pl.pallas_call(
        _ada_kernel,
        grid=(6 * D_MODEL // tn,),
        in_specs=[_const_spec((n, D_MODEL)),
                  pl.BlockSpec((D_MODEL, tn), lambda j: (0, j)),
                  pl.BlockSpec((1, tn), lambda j: (0, j))],
        out_specs=pl.BlockSpec((n, tn), lambda j: (0, j)),
        out_shape=jax.ShapeDtypeStruct((n, 6 * D_MODEL), F32),
        compiler_params=_params(("arbitrary",)),
        name="ada",
    )(c, w, b.reshape(1, -1))


def _s5_disc_kernel(lr_ref, li_ref, ldt_ref, br_ref, bi_ref, ar_ref, ai_ref, bbr_ref, bbi_ref):
    lr = lr_ref[...]
    li = li_ref[...]
    dt = jnp.exp(ldt_ref[...])
    mag = jnp.exp(lr * dt)
    ar = mag * jnp.cos(li * dt)
    ai = mag * jnp.sin(li * dt)
    den = lr * lr + li * li
    nr = ar - 1.0
    f_re = (nr * lr + ai * li) / den
    f_im = (ai * lr - nr * li) / den
    br = br_ref[...]
    bi = bi_ref[...]
    ar_ref[...] = ar
    ai_ref[...] = ai
    bbr_ref[...] = f_re * br - f_im * bi
    bbi_ref[...] = f_re * bi + f_im * br


def _s5_disc(lam_re, lam_im, log_dt, b_re, b_im):
    g, n = lam_re.shape
    w = S5_GROUP * n
    tile = lambda a: jnp.tile(a, (1, S5_GROUP))
    ldt = jnp.broadcast_to(log_dt[:, None], (g, w))
    brt = jnp.swapaxes(b_re, 1, 2).reshape(g, w)
    bit = jnp.swapaxes(b_im, 1, 2).reshape(g, w)
    shp = jax.ShapeDtypeStruct((g, w), F32)
    ar, ai, bbr, bbi = pl.pallas_call(
        _s5_disc_kernel,
        out_shape=(shp, shp, shp, shp),
        name="s5_disc",
    )(tile(lam_re), tile(lam_im), ldt, brt, bit)
    return (ar[:, :n], ai[:, :n], bbr.reshape(g, S5_GROUP, n), bbi.reshape(g, S5_GROUP, n))


def _s5_matrices(abar_re, abar_im, bbar_re, bbar_im, c_re, c_im, d_skip):
    ns = 2 * S5_NS
    b = jnp.concatenate([bbar_re.reshape(D_MODEL, S5_STATE), bbar_im.reshape(D_MODEL, S5_STATE)],
                        axis=1).reshape(S5_CHUNKS, S5_CH, 2 * S5_STATE)

    def by_tile(c):
        return jnp.swapaxes(c, 1, 2).reshape(S5_CHUNKS, S5_TILES, 1, 2, S5_STATE, S5_GROUP)
    c = jnp.concatenate([by_tile(c_re), by_tile(-c_im)], axis=2).reshape(S5_CHUNKS, ns, S5_GROUP)

    by_tile_a = lambda v: v.reshape(S5_CHUNKS, S5_TILES, 1, LANES)
    a = jnp.concatenate([by_tile_a(abar_re), by_tile_a(abar_im)], axis=2).reshape(S5_CHUNKS, 1, ns)
    d = d_skip.reshape(S5_CHUNKS, 1, S5_CH)
    return b, a, c, d


def _s5_state_group(i):
    return 2 * (i // (2 * LANES)) + (i % LANES) // S5_STATE


def _s5_expand_operands(b_ref, c_ref, bm_ref, cm_ref):
    tw = 2 * LANES
    row_g = lax.broadcasted_iota(jnp.int32, (S5_CH, LANES), 0) // S5_GROUP
    lane_half = lax.broadcasted_iota(jnp.int32, (S5_CH, LANES), 1) // S5_STATE
    for part in range(2):
        bp = b_ref[:, part * S5_STATE:(part + 1) * S5_STATE]
        pattern = jnp.concatenate([bp, bp], axis=1)
        for j in range(S5_TILES):
            tile = jnp.where(row_g == 2 * j + lane_half, pattern, 0.0)
            bm_ref[:, j * tw + part * LANES:j * tw + (part + 1) * LANES] = tile.astype(BF16)
    spread = (lax.broadcasted_iota(jnp.int32, (S5_GROUP, S5_CH), 1) % S5_GROUP
              == lax.broadcasted_iota(jnp.int32, (S5_GROUP, S5_CH), 0)).astype(BF16)
    rows = 512
    for r0 in range(0, 2 * S5_NS, rows):
        tiled = _dot(c_ref[r0:r0 + rows, :].astype(BF16), spread)
        own = (_s5_state_group(lax.broadcasted_iota(jnp.int32, (rows, S5_CH), 0) + r0)
               == lax.broadcasted_iota(jnp.int32, (rows, S5_CH), 1) // S5_GROUP)
        cm_ref[r0:r0 + rows, :] = jnp.where(own, tiled, 0.0).astype(BF16)


def _split_w_in_kernel(wt_ref, wm_ref, wdt_ref, wg_ref):
    tk = wt_ref.shape[1]
    step = 1024
    for j in range(0, OFF_DT, step):
        wm_ref[:, j:j + step] = wt_ref[j:j + step, :].T.astype(BF16)
    head = lax.broadcasted_iota(jnp.int32, (tk, LANES), 1) < M_HEADS
    wdt_ref[...] = jnp.where(head, wt_ref[OFF_DT:OFF_DT + LANES, :].T, 0.0).astype(BF16)
    for j in range(0, 2 * D_MODEL, step):
        wg_ref[:, j:j + step] = wt_ref[OFF_GA + j:OFF_GA + j + step, :].T.astype(BF16)


def _split_w_in(w):
    k = w.shape[0]
    tk = 256
    outs = (OFF_DT, LANES, 2 * D_MODEL)
    return pl.pallas_call(
        _split_w_in_kernel,
        grid=(k // tk,),
        in_specs=[pl.BlockSpec((IN_COLS, tk), lambda i: (0, i))],
        out_specs=[pl.BlockSpec((tk, n), lambda i: (i, 0)) for n in outs],
        out_shape=[jax.ShapeDtypeStruct((k, n), BF16) for n in outs],
        compiler_params=_params(("parallel",)),
        name="split_w_in",
    )(jnp.swapaxes(w, 0, 1))


def _inproj_kernel(x_ref, sh_ref, sc_ref, g_ref, wm_ref, wdt_ref, wg_ref, dtb_ref, cw_ref, cb_ref,
                   u_ref, z_ref, xbc_ref, dt_ref, gate_ref, *rest, tiles_per_seq):
    accs = rest[-INPROJ_DEPTH:]
    h = _rms(x_ref[...], g_ref[...]) * (1.0 + sc_ref[...]) + sh_ref[...]
    hb = h.astype(BF16)
    rows = hb.shape[0]
    step = INPROJ_SLAB

    if u_ref.shape[0] > rows:
        u_ref[rows:, :] = jnp.zeros((u_ref.shape[0] - rows, D_MODEL), F32)

    if tiles_per_seq is None:
        conv_fn = None
    else:
        tail_ref, carry_ref = rest[:2]

        @pl.when(pl.program_id(0) % tiles_per_seq == 0)
        def _():
            carry_ref[...] = jnp.zeros_like(carry_ref)

        def conv_fn(r, sl):
            width = sl.stop - sl.start
            w = [cw_ref[k:k + 1, sl] for k in range(M_CONV)]
            bias = cb_ref[:, sl]
            row = lax.broadcasted_iota(jnp.int32, (SUBLANES, width), 0)
            xp = carry_ref[:, sl]
            farp = xp * w[1] + pltpu.roll(xp, 1, axis=0) * w[0]
            outs = []
            for i in range(rows // SUBLANES):
                xi = r[i * SUBLANES:(i + 1) * SUBLANES, :]
                if outs:
                    xi = _ordered_after(xi, outs[-1])
                s1 = jnp.where(row == 0, pltpu.roll(xp, 1, axis=0), pltpu.roll(xi, 1, axis=0))
                far = xi * w[1] + s1 * w[0]
                s2 = jnp.where(row < 2, pltpu.roll(farp, 2, axis=0), pltpu.roll(far, 2, axis=0))
                outs.append(_silu(xi * w[3] + s1 * w[2] + s2 + bias))
                xp, farp = xi, far
            carry_ref[:, sl] = r[rows - SUBLANES:, :]
            tail_ref[:, sl] = r[rows - SUBLANES:, :]
            return jnp.concatenate(outs, axis=0)

    def slabs(o_ref, w_ref, w_off, fn):
        n = o_ref.shape[-1]
        return [(o_ref, slice(j, j + min(step, n - j)), w_ref, w_off + j, fn) for j in range(0, n, step)]

    su = slabs(u_ref, wm_ref, 0, None)
    sz = slabs(z_ref, wm_ref, OFF_Z, None)
    sx = slabs(xbc_ref, wm_ref, OFF_XBC, conv_fn)
    sd = slabs(dt_ref, wdt_ref, 0, lambda r, sl: _softplus(r + dtb_ref[:, sl]))
    sg = slabs(gate_ref, wg_ref, 0, None)
    light = su + sz + sg + sd
    order = []
    while sx or light:
        if sx:
            order.append(sx.pop(0))
        if light:
            order.append(light.pop(0))

    def matmul(s):
        o_ref, sl, w_ref, w_col, _ = order[s]
        accs[s % INPROJ_DEPTH][:, 0:sl.stop - sl.start] = _dot(hb, w_ref[:, w_col:w_col + sl.stop - sl.start])

    for s in range(min(INPROJ_DEPTH - 1, len(order))):
        matmul(s)
    for s, (o_ref, sl, _, _, fn) in enumerate(order):
        if s + INPROJ_DEPTH - 1 < len(order):
            matmul(s + INPROJ_DEPTH - 1)
        r = accs[s % INPROJ_DEPTH][:, 0:sl.stop - sl.start]
        o_ref[0:rows, sl] = r if fn is None else fn(r, sl)


def _mod_specs(per_row, tm, rows_per_mod, cols):
    if per_row:
        return [pl.BlockSpec((tm, D_MODEL), functools.partial(lambda i, c: (i, c), c=c)) for c in cols]
    return [pl.BlockSpec((None, 1, D_MODEL), functools.partial(lambda i, c: (i // rows_per_mod, 0, c), c=c))
            for c in cols]


def _s5_layout(per_row, t, tm, rows_per_batch, width):
    if per_row:
        return (1, t, width), pl.BlockSpec((None, tm, width), lambda i: (0, i, 0))
    nblk, pitch = rows_per_batch // tm, _s5_pitch(tm)
    return ((nblk, (t // rows_per_batch) * pitch, width),
            pl.BlockSpec((None, pitch, width), lambda i: (i % nblk, i // nblk, 0)))


def _in_proj(x, mod, g, ws, dtb, cw, cb, *, per_row, tm, rows_per_batch):
    t = x.shape[0]
    row = lambda n: pl.BlockSpec((tm, n), lambda i: (i, 0))
    outs = (M_INNER, M_CONV_DIM, LANES, 2 * D_MODEL)
    u_shape, u_spec = _s5_layout(per_row, t, tm, rows_per_batch, D_MODEL)
    out_specs = [u_spec] + [row(n) for n in outs]
    out_shape = [jax.ShapeDtypeStruct(u_shape, F32)] + [jax.ShapeDtypeStruct((t, n), F32) for n in outs]
    scratch = []
    tiles_per_seq = None
    if not per_row:
        tiles_per_seq = rows_per_batch // tm
        out_specs.append(pl.BlockSpec((None, SUBLANES, M_CONV_DIM), lambda i: (i // tiles_per_seq, 0, 0)))
        out_shape.append(jax.ShapeDtypeStruct((t // rows_per_batch, SUBLANES, M_CONV_DIM), F32))
        scratch.append(pltpu.VMEM((SUBLANES, M_CONV_DIM), F32))
    scratch += [pltpu.VMEM((tm, INPROJ_SLAB), F32)] * INPROJ_DEPTH
    consts = (g,) + tuple(ws) + (dtb, cw, cb)
    return pl.pallas_call(
        functools.partial(_inproj_kernel, tiles_per_seq=tiles_per_seq),
        grid=(t // tm,),
        in_specs=[row(D_MODEL)] + _mod_specs(per_row, tm, rows_per_batch // tm if not per_row else 1, (0, 1))
                 + [_const_spec(w.shape) for w in consts],
        out_specs=out_specs,
        out_shape=out_shape,
        scratch_shapes=scratch,
        compiler_params=_params(("arbitrary",)),
        name="in_proj",
    )(x, mod, mod, *consts)


def _s5_pitch(tl):
    return tl if tl == 1 else tl + SUBLANES


def _s5_kernel(*refs, nb, tl, sub_rows, side=None):
    nu = S5_PIECES
    u_refs = refs[:nu]
    h0r_ref, h0i_ref, b_ref, a_ref, c_ref, d_ref = refs[nu:nu + 6]
    y_refs = refs[nu + 6:2 * nu + 6]
    hr_ref, hi_ref, hst_ref, bu0_ref, xs0_ref, bu1_ref, xs1_ref, bm_ref, cm_ref = refs[2 * nu + 6:]
    bufs = ((bu0_ref, xs0_ref), (bu1_ref, xs1_ref))
    tb = pl.program_id(1)
    tsub = sub_rows // nb
    tw = 2 * LANES
    pitch = _s5_pitch(tl)

    def step_rows(t):
        return pl.ds(t, nb, stride=pitch) if pitch > 1 else pl.ds(0, nb)

    @pl.when(tb == 0)
    def _():
        _s5_expand_operands(b_ref, c_ref, bm_ref, cm_ref)
        for j in range(S5_TILES):
            hst_ref[:, j * tw:j * tw + LANES] = h0r_ref[:, j * LANES:(j + 1) * LANES]
            hst_ref[:, j * tw + LANES:(j + 1) * tw] = h0i_ref[:, j * LANES:(j + 1) * LANES]

    ar = [jnp.broadcast_to(a_ref[:, j * tw:j * tw + LANES], (nb, LANES)) for j in range(S5_TILES)]
    ai = [jnp.broadcast_to(a_ref[:, j * tw + LANES:(j + 1) * tw], (nb, LANES)) for j in range(S5_TILES)]
    dsk = d_ref[...]

    def project_in(t0, bu_ref, xs_ref):
        xs = jnp.concatenate([jnp.concatenate([r[step_rows(t0 + i), :] for r in u_refs], axis=1)
                              for i in range(tsub)], axis=0)
        xs_ref[...] = xs
        lhs = xs.astype(BF16)
        for j in range(S5_TILES):
            bu_ref[:, j * tw:(j + 1) * tw] = _dot(lhs, bm_ref[:, j * tw:(j + 1) * tw])

    def scan_project_out(t0, bu_ref, xs_ref, carry):
        new, hs = [], []
        for j in range(S5_TILES):
            hr, hi = carry[2 * j], carry[2 * j + 1]
            rows = []
            for i in range(tsub):
                br = bu_ref[i * nb:(i + 1) * nb, j * tw:j * tw + LANES]
                bi = bu_ref[i * nb:(i + 1) * nb, j * tw + LANES:(j + 1) * tw]
                hr, hi = ar[j] * hr - ai[j] * hi + br, ar[j] * hi + ai[j] * hr + bi
                rows.append(jnp.concatenate([hr, hi], axis=1))
            new += [hr, hi]
            hs.append(jnp.concatenate(rows, axis=0).astype(BF16))
        y = _dot(jnp.concatenate(hs, axis=1), cm_ref[...]) + dsk * xs_ref[...]
        for i in range(tsub):
            for k, y_ref in enumerate(y_refs):
                y_ref[step_rows(t0 + i), :] = y[i * nb:(i + 1) * nb, k * LANES:(k + 1) * LANES]
        return tuple(new)

    h_in = []
    for j in range(S5_TILES):
        h_in += [hst_ref[:, j * tw:j * tw + LANES], hst_ref[:, j * tw + LANES:(j + 1) * tw]]
    nsub = tl // tsub
    (bu0, xs0), (bu1, xs1) = bufs
    project_in(0, bu0, xs0)
    if nsub == 1:
        h_out = scan_project_out(0, bu0, xs0, tuple(h_in))
    else:
        def pair(p, carry):
            t0 = 2 * p * tsub
            project_in(t0 + tsub, bu1, xs1)
            carry = scan_project_out(t0, bu0, xs0, carry)
            project_in(jnp.minimum(t0 + 2 * tsub, tl - tsub), bu0, xs0)
            return scan_project_out(t0 + tsub, bu1, xs1, carry)
        h_out = lax.fori_loop(0, nsub // 2, pair, tuple(h_in))
    for j in range(S5_TILES):
        hst_ref[:, j * tw:j * tw + LANES] = h_out[2 * j]
        hst_ref[:, j * tw + LANES:(j + 1) * tw] = h_out[2 * j + 1]

    if pitch > tl:
        for y_ref in y_refs:
            for b in range(nb):
                y_ref[b * pitch + tl:(b + 1) * pitch, :] = jnp.zeros((pitch - tl, LANES), F32)

    @pl.when(tb == pl.num_programs(1) - 1)
    def _():
        for j in range(S5_TILES):
            hr_ref[:, j * LANES:(j + 1) * LANES] = h_out[2 * j]
            hi_ref[:, j * LANES:(j + 1) * LANES] = h_out[2 * j + 1]

    if side is not None:
        side()


def _s5(u, h0r, h0i, b, a, c, d, *, tl):
    nblk, rows, _ = u.shape
    pitch = _s5_pitch(tl)
    nb = rows // pitch
    sub_rows = min(S5_SUB_ROWS, tl * nb)
    nsub = tl * nb // sub_rows
    assert sub_rows % nb == 0 and tl * nb % sub_rows == 0 and (nsub == 1 or nsub % 2 == 0)
    chunk3 = lambda s: pl.BlockSpec((None,) + s, lambda c, t: (c, 0, 0))
    st_spec = pl.BlockSpec((nb, S5_NS), lambda c, t: (0, c))
    nu = S5_PIECES
    u_specs = [pl.BlockSpec((None, rows, LANES), functools.partial(lambda c, t, k: (t, 0, nu * c + k), k=k))
               for k in range(nu)]
    y_spec = pl.BlockSpec((None, rows, LANES), lambda c, t: (t, 0, c))
    y_shape = jax.ShapeDtypeStruct((nblk, rows, D_MODEL // nu), F32)
    res = pl.pallas_call(
        functools.partial(_s5_kernel, nb=nb, tl=tl, sub_rows=sub_rows),
        grid=(S5_CHUNKS, nblk),
        in_specs=u_specs + [st_spec, st_spec, chunk3((S5_CH, 2 * S5_STATE)), chunk3((1, 2 * S5_NS)),
                            chunk3((2 * S5_NS, S5_GROUP)), chunk3((1, S5_CH))],
        out_specs=[y_spec] * nu + [st_spec, st_spec],
        out_shape=[y_shape] * nu + [jax.ShapeDtypeStruct(h0r.shape, F32), jax.ShapeDtypeStruct(h0i.shape, F32)],
        scratch_shapes=[pltpu.VMEM((nb, 2 * S5_NS), F32)]
                       + [pltpu.VMEM((sub_rows, 2 * S5_NS), F32), pltpu.VMEM((sub_rows, S5_CH), F32)] * 2
                       + [pltpu.VMEM((S5_CH, 2 * S5_NS), BF16), pltpu.VMEM((2 * S5_NS, S5_CH), BF16)],
        compiler_params=_params(("parallel", "arbitrary")),
        name="s5",
    )(*([u] * nu), h0r, h0i, b, a, c, d)
    return res[:nu], res[nu], res[nu + 1]


def _expand_mat():
    r = lax.broadcasted_iota(jnp.int32, (3 * LANES, M_HEADS * M_HEADDIM), 0) % LANES
    c = lax.broadcasted_iota(jnp.int32, (3 * LANES, M_HEADS * M_HEADDIM), 1) // M_HEADDIM
    return (r == c).astype(BF16)


LOG2E = 1.4426950408889634
SSD_CHUNKS_PER_STEP = 4


def _ssd_kernel(*refs, n_cast):
    z_ref, act_ref, dt_ref, alog_ref, dsk_ref, nw_ref, e64_ref = refs[:7]
    w_refs = refs[7:7 + n_cast]
    y_ref, hout_ref = refs[7 + n_cast:9 + n_cast]
    wb_refs = refs[9 + n_cast:9 + 2 * n_cast]
    ht_ref, xs_ref, xdte_ref, eacs_ref, yacc_ref = refs[9 + 2 * n_cast:]
    c = pl.program_id(1)
    for w_ref, wb_ref in zip(w_refs, wb_refs):
        wb_ref[...] = w_ref[...].astype(BF16)

    @pl.when(c == 0)
    def _():
        ht_ref[...] = jnp.zeros_like(ht_ref)

    for k in range(SSD_CHUNKS_PER_STEP):
        rows = pl.ds(k * M_CHUNK, M_CHUNK)
        _ssd_chunk(z_ref.at[rows], act_ref.at[rows], dt_ref.at[rows], alog_ref, dsk_ref, nw_ref, e64_ref,
                   y_ref.at[rows], ht_ref, xs_ref.at[k], xdte_ref.at[k], eacs_ref.at[k], yacc_ref.at[k])

    @pl.when(c == pl.num_programs(1) - 1)
    def _():
        for j in range(0, M_INNER, LANES):
            hout_ref[j:j + LANES, :] = ht_ref[:, j:j + LANES].T


def _ssd_chunk(z_ref, act_ref, dt_ref, alog_ref, dsk_ref, nw_ref, e64_ref,
               y_ref, ht_ref, xs_ref, xdte_ref, eacs_ref, yacc_ref):
    q = M_CHUNK
    cs = 512

    dt = dt_ref[...]
    a2 = dt * (-jnp.exp(alog_ref[...]) * LOG2E)
    ri = lax.broadcasted_iota(jnp.int32, (q, q), 0)
    ci = lax.broadcasted_iota(jnp.int32, (q, q), 1)
    causal = ri >= ci
    acs = jnp.dot(causal.astype(F32), a2, preferred_element_type=F32, precision=lax.Precision.HIGHEST)
    src_t = (acs - jnp.log2(dt)).T
    eacs2 = _split2(jnp.exp2(acs))
    dte2 = _split2(dt * jnp.exp2(acs[q - 1:q, :] - acs))
    for j in range(0, M_INNER, cs):
        sl = slice(j, j + cs)
        e = e64_ref[0:2 * LANES, sl]
        xs = act_ref[:, sl]
        xs_ref[:, sl] = xs.astype(BF16)
        xdte_ref[:, sl] = (xs * _dot(dte2, e)).astype(BF16)
        eacs_ref[:, sl] = _dot(eacs2, e)

    hp = M_HPG * M_HEADDIM
    lane = lax.broadcasted_iota(jnp.int32, (q, LANES), 1)
    first_head = lane < M_HEADDIM
    for g in range(M_GROUPS):
        gs = slice(g * hp, (g + 1) * hp)
        bm_t = act_ref[:, M_INNER + g * M_STATE:M_INNER + (g + 1) * M_STATE].T.astype(BF16)
        cm = act_ref[:, M_INNER + (M_GROUPS + g) * M_STATE:M_INNER + (M_GROUPS + g + 1) * M_STATE].astype(BF16)
        cbm = _dot(cm, bm_t)
        y_off = _dot(cm, ht_ref[:, gs].astype(BF16))
        st_t = _dot(bm_t, xdte_ref[:, gs])
        for pr in range(M_HPG // 2):
            h0 = g * M_HPG + 2 * pr
            ps = slice(h0 * M_HEADDIM, (h0 + 2) * M_HEADDIM)
            ms = []
            for h in (h0, h0 + 1):
                d2 = jnp.broadcast_to(acs[:, h:h + 1], (q, q)) - src_t[h:h + 1, :]
                ms.append((cbm * jnp.where(causal, jnp.exp2(d2), 0.0)).astype(BF16))
            xp = xs_ref[:, ps]
            zero = jnp.zeros_like(xp)
            rhs = jnp.concatenate([jnp.where(first_head, xp, zero), jnp.where(first_head, zero, xp)], axis=0)
            yd = _dot(jnp.concatenate(ms, axis=1), rhs)
            yacc_ref[:, ps] = yd + y_off[:, ps.start - gs.start:ps.stop - gs.start] * eacs_ref[:, ps] \
                + dsk_ref[:, ps] * act_ref[:, ps]
        ht_ref[:, gs] = eacs_ref[q - 1:q, gs] * ht_ref[:, gs] + st_t

    ss = jnp.zeros((q, 1), F32)
    for j in range(0, M_INNER, cs):
        sl = slice(j, j + cs)
        gt = yacc_ref[:, sl] * _silu(z_ref[:, sl])
        yacc_ref[:, sl] = gt
        ss = ss + jnp.sum(gt * gt, axis=-1, keepdims=True)
    inv = lax.rsqrt(ss * (1.0 / M_INNER) + EPS)
    for j in range(0, M_INNER, cs):
        sl = slice(j, j + cs)
        y_ref[:, sl] = yacc_ref[:, sl] * inv * nw_ref[:, sl]


def _cast_spec(w, nbatch, nc):
    rows, steps = w.shape[0], nbatch * nc
    n = next(n for n in range(steps, 0, -1) if rows % n == 0 and (rows // n) % (2 * SUBLANES) == 0)
    return pl.BlockSpec((rows // n, w.shape[1]), lambda b, c: (jnp.minimum(b * nc + c, n - 1), 0))


def _ssd(z, act, dt, alog, dsk, nw, e64, cast_ws, *, nbatch, seq):
    q = M_CHUNK
    k = SSD_CHUNKS_PER_STEP
    assert seq % (k * q) == 0
    nc = seq // (k * q)
    row = lambda n: pl.BlockSpec((k * q, n), lambda b, c: (b * nc + c, 0))
    cast_specs = [_cast_spec(w, nbatch, nc) for w in cast_ws]
    res = pl.pallas_call(
        functools.partial(_ssd_kernel, n_cast=len(cast_ws)),
        grid=(nbatch, nc),
        in_specs=[row(M_INNER), row(M_CONV_DIM), row(LANES), _const_spec(alog.shape),
                  _const_spec(dsk.shape), _const_spec(nw.shape), _const_spec(e64.shape)] + cast_specs,
        out_specs=[row(M_INNER), pl.BlockSpec((None, M_INNER, M_STATE), lambda b, c: (b, 0, 0))] + cast_specs,
        out_shape=[jax.ShapeDtypeStruct((nbatch * seq, M_INNER), F32),
                   jax.ShapeDtypeStruct((nbatch, M_INNER, M_STATE), F32)]
                  + [jax.ShapeDtypeStruct(w.shape, BF16) for w in cast_ws],
        scratch_shapes=[pltpu.VMEM((M_STATE, M_INNER), F32), pltpu.VMEM((k, q, M_INNER), BF16),
                        pltpu.VMEM((k, q, M_INNER), BF16), pltpu.VMEM((k, q, M_INNER), F32),
                        pltpu.VMEM((k, q, M_INNER), F32)],
        compiler_params=_params(("arbitrary", "arbitrary")),
        name="ssd",
    )(z, act, dt, alog, dsk, nw, e64, *cast_ws)
    return res[0], res[1], res[2:]


SSD_STEP_BB = 8


def _ssd_step_kernel(z_ref, xbc_ref, dt_ref, cbuf_ref, h0_ref, cw_ref, cb_ref, alog_ref, dsk_ref, nw_ref,
                     e64_ref, y_ref, hout_ref, convout_ref):
    bb = SSD_STEP_BB
    x_new = xbc_ref[...]
    conv = cb_ref[...] + x_new * cw_ref[M_CONV - 1:M_CONV, :]
    for k in range(M_CONV - 1):
        conv = conv + cbuf_ref[k] * cw_ref[k:k + 1, :]
    act = _silu(conv)
    for k in range(M_CONV - 2):
        convout_ref[k] = cbuf_ref[k + 1]
    convout_ref[M_CONV - 2] = x_new

    xs = act[:, :M_INNER]
    bmat = act[:, M_INNER:M_INNER + M_GROUPS * M_STATE]
    cmat = act[:, M_INNER + M_GROUPS * M_STATE:]
    dt = dt_ref[...]
    da = jnp.exp(dt * (-jnp.exp(alog_ref[...])))
    dt_e = _dot(_split3(dt), e64_ref[...])
    da_e = _dot(_split3(da), e64_ref[...])
    xdt = xs * dt_e

    hp = M_HPG * M_HEADDIM
    lane_group = lax.broadcasted_iota(jnp.int32, (M_GROUPS, M_INNER), 1) // hp
    row_group = lax.broadcasted_iota(jnp.int32, (M_GROUPS, M_INNER), 0)
    gmask = lane_group == row_group
    ones_rows = lax.broadcasted_iota(jnp.int32, (SUBLANES, M_STATE), 0) >= M_GROUPS

    cbx = []
    for g in range(M_GROUPS):
        cbg = jnp.sum(cmat[:, g * M_STATE:(g + 1) * M_STATE] * bmat[:, g * M_STATE:(g + 1) * M_STATE],
                      axis=-1, keepdims=True)
        cbx.append(jnp.broadcast_to(cbg, (bb, hp)))
    y = jnp.concatenate(cbx, axis=1) * xdt + dsk_ref[...] * xs

    y_off_rows = []
    for b in range(bb):
        xsel = jnp.where(gmask, jnp.broadcast_to(xdt[b:b + 1, :], (M_GROUPS, M_INNER)), 0.0)
        lhs = jnp.concatenate([xsel, *_pieces3(da_e[b:b + 1, :]), jnp.zeros((1, M_INNER), F32)],
                              axis=0).astype(BF16)
        bm4 = jnp.concatenate([bmat[b:b + 1, g * M_STATE:(g + 1) * M_STATE] for g in range(M_GROUPS)], axis=0)
        cm4 = jnp.concatenate([cmat[b:b + 1, g * M_STATE:(g + 1) * M_STATE] for g in range(M_GROUPS)], axis=0)
        zeros4 = jnp.zeros((SUBLANES - M_GROUPS, M_STATE), F32)
        rhs = jnp.concatenate([jnp.concatenate([bm4, zeros4], axis=0),
                               jnp.where(ones_rows, 1.0, 0.0)], axis=1).astype(BF16)
        res = lax.dot_general(lhs, rhs, _TN, preferred_element_type=F32)
        h0 = h0_ref[b]
        hout_ref[b] = res[:, M_STATE:] * h0 + res[:, :M_STATE]
        cm8 = jnp.concatenate([cm4, zeros4], axis=0).astype(BF16)
        yo = lax.dot_general(cm8, h0.astype(BF16), _NT, preferred_element_type=F32)
        y_off_rows.append(jnp.sum(jnp.where(gmask, yo[:M_GROUPS, :], 0.0), axis=0, keepdims=True))
    y = y + da_e * jnp.concatenate(y_off_rows, axis=0)

    y_ref[...] = _rms(y * _silu(z_ref[...]), nw_ref[...])


SSD_STEP_EVERY = 2


def _s5_ssd_step_kernel(*refs, nb, tl, sub_rows):
    n5i, nsi = S5_PIECES + 6, 11
    n5o, nso = S5_PIECES + 2, 3
    ins5, inss = refs[:n5i], refs[n5i:n5i + nsi]
    o = n5i + nsi
    outs5, outss = refs[o:o + n5o], refs[o + n5o:o + n5o + nso]
    sc5 = refs[o + n5o + nso:]
    s = pl.program_id(0) * pl.num_programs(1) + pl.program_id(1)

    def side():
        @pl.when(s % SSD_STEP_EVERY == 0)
        def _():
            _ssd_step_kernel(*inss, *outss)

    _s5_kernel(*ins5, *outs5, *sc5, nb=nb, tl=tl, sub_rows=sub_rows, side=side)


def _s5_ssd_step(u, h0r, h0i, b, a, c, d, zs, xbc, dt, cbuf, h0, cw, cb, alog, dsk, nw, e64, *, tl):
    nblk, rows, _ = u.shape
    pitch = _s5_pitch(tl)
    nb = rows // pitch
    sub_rows = min(S5_SUB_ROWS, tl * nb)
    nsub = tl * nb // sub_rows
    ns = zs.shape[0]
    bb, per = SSD_STEP_BB, SSD_STEP_EVERY
    assert sub_rows % nb == 0 and tl * nb % sub_rows == 0 and nsub % 2 == 0 and ns * per == bb * S5_CHUNKS * nblk
    chunk3 = lambda sh: pl.BlockSpec((None,) + sh, lambda c, t: (c, 0, 0))
    st_spec = pl.BlockSpec((nb, S5_NS), lambda c, t: (0, c))
    nu = S5_PIECES
    u_specs = [pl.BlockSpec((None, rows, LANES), functools.partial(lambda c, t, k: (t, 0, nu * c + k), k=k))
               for k in range(nu)]
    y5_spec = pl.BlockSpec((None, rows, LANES), lambda c, t: (t, 0, c))
    y5_shape = jax.ShapeDtypeStruct((nblk, rows, D_MODEL // nu), F32)
    step = lambda c, t: c * nblk + t
    row = lambda w: pl.BlockSpec((bb, w), lambda c, t: (step(c, t) // per, 0))
    conv_spec = pl.BlockSpec((M_CONV - 1, bb, M_CONV_DIM), lambda c, t: (0, step(c, t) // per, 0))
    st3_spec = pl.BlockSpec((bb, M_INNER, M_STATE), lambda c, t: (step(c, t) // per, 0, 0))
    res = pl.pallas_call(
        functools.partial(_s5_ssd_step_kernel, nb=nb, tl=tl, sub_rows=sub_rows),
        grid=(S5_CHUNKS, nblk),
        in_specs=u_specs + [st_spec, st_spec, chunk3((S5_CH, 2 * S5_STATE)), chunk3((1, 2 * S5_NS)),
                            chunk3((2 * S5_NS, S5_GROUP)), chunk3((1, S5_CH))]
                 + [row(M_INNER), row(M_CONV_DIM), row(LANES), conv_spec, st3_spec,
                    _const_spec(cw.shape), _const_spec(cb.shape), _const_spec(alog.shape),
                    _const_spec(dsk.shape), _const_spec(nw.shape), _const_spec(e64.shape)],
        out_specs=[y5_spec] * nu + [st_spec, st_spec] + [row(M_INNER), st3_spec, conv_spec],
        out_shape=[y5_shape] * nu + [jax.ShapeDtypeStruct(h0r.shape, F32), jax.ShapeDtypeStruct(h0i.shape, F32)]
                  + [jax.ShapeDtypeStruct((ns, M_INNER), F32), jax.ShapeDtypeStruct((ns, M_INNER, M_STATE), F32),
                     jax.ShapeDtypeStruct((M_CONV - 1, ns, M_CONV_DIM), F32)],
        scratch_shapes=[pltpu.VMEM((nb, 2 * S5_NS), F32)]
                       + [pltpu.VMEM((sub_rows, 2 * S5_NS), F32), pltpu.VMEM((sub_rows, S5_CH), F32)] * 2
                       + [pltpu.VMEM((S5_CH, 2 * S5_NS), BF16), pltpu.VMEM((2 * S5_NS, S5_CH), BF16)],
        compiler_params=_params(("arbitrary", "arbitrary")),
        name="s5_ssd_step",
    )(*([u] * nu), h0r, h0i, b, a, c, d, zs, xbc, dt, cbuf, h0, cw, cb, alog, dsk, nw, e64)
    return (res[:nu], res[nu], res[nu + 1]), (res[nu + 2], res[nu + 3], res[nu + 4])


MXU_WIDTH = 256
FFN_CHUNKS = (0, 6 * MXU_WIDTH, D_FF)


def _out_kernel(x_ref, *refs):
    y5_refs = refs[:S5_PIECES]
    (yb_ref, gate_ref, gtm_ref, shf_ref, scf_ref, gtf_ref, wglu_ref, bglu_ref, wb5_ref, wbs_ref, wout_ref,
     npm_ref, npf_ref, npo_ref, wfi_ref, wfo_ref, o_ref) = refs[S5_PIECES:]
    rows = x_ref.shape[0]
    y5 = jnp.concatenate([y5_refs[k][0:rows, c * LANES:(c + 1) * LANES]
                          for c in range(S5_CHUNKS) for k in range(S5_PIECES)], axis=1)
    ya = _gelu_tanh(y5)
    glu = ya * _sigmoid(_dot(ya.astype(BF16), wglu_ref[...]) + bglu_ref[...])
    merged = (_sigmoid(gate_ref[:, :D_MODEL]) * _dot(glu.astype(BF16), wb5_ref[...])
              + _sigmoid(gate_ref[:, D_MODEL:]) * _dot(yb_ref[...].astype(BF16), wbs_ref[...]))
    mix = _dot(merged.astype(BF16), wout_ref[...])
    x1 = x_ref[...] + gtm_ref[...] * _rms(mix, npm_ref[...])

    hb = (_rms(x1, npf_ref[...]) * (1.0 + scf_ref[...]) + shf_ref[...]).astype(BF16)
    f = jnp.zeros_like(x1)
    for j, e in zip(FFN_CHUNKS[:-1], FFN_CHUNKS[1:]):
        gg = _dot(hb, wfi_ref[:, j:e])
        uu = _dot(hb, wfi_ref[:, D_FF + j:D_FF + e])
        f = f + _dot((_silu(gg) * uu).astype(BF16), wfo_ref[j:e, :])
    o_ref[...] = x1 + gtf_ref[...] * _rms(f, npo_ref[...])


def _out_stage(x, y5, yb, gates, mod, ws, *, per_row, tm, rows_per_batch):
    t = x.shape[0]
    row = lambda n: pl.BlockSpec((tm, n), lambda i: (i, 0))
    mods = _mod_specs(per_row, tm, rows_per_batch // tm if not per_row else 1, (2, 3, 4, 5))
    _, y5_spec = _s5_layout(per_row, t, tm, rows_per_batch, D_MODEL // S5_PIECES)
    return pl.pallas_call(
        _out_kernel,
        grid=(t // tm,),
        in_specs=[row(D_MODEL)] + [y5_spec] * S5_PIECES + [row(M_INNER), row(2 * D_MODEL)] + mods
                 + [_const_spec(w.shape) for w in ws],
        out_specs=row(D_MODEL),
        out_shape=jax.ShapeDtypeStruct((t, D_MODEL), F32),
        compiler_params=_params(("parallel",)),
        name="out_stage",
    )(x, *y5, yb, gates, mod, mod, mod, mod, *ws)


def _pad_lanes(v):
    return jnp.pad(v.reshape(1, -1), ((0, 0), (0, LANES - v.shape[-1])))


def kernel(x_prompt, x_sample, state_s5_re, state_s5_im, state_ssm, state_conv, c_prompt, c_sample, w_ada, b_ada, norm_pre_mix, norm_post_mix, norm_pre_ffn, norm_post_ffn, w_in, s5_lam_re, s5_lam_im, s5_log_dt, s5_b_re, s5_b_im, s5_c_re, s5_c_im, s5_d, s5_w_glu, s5_b_glu, m_conv_w, m_conv_b, m_dt_bias, m_a_log, m_d, m_norm, w_branch_s5, w_branch_ssd, w_out, w_ffn_in, w_ffn_out):
    bp, seq, _ = x_prompt.shape
    bs = x_sample.shape[0]
    assert x_sample.shape[1] == 1 and w_ada.shape[0] == 1 and seq % M_CHUNK == 0
    row1 = lambda v: v.reshape(1, -1)

    ws_in = _split_w_in(w_in[0])
    cw, cb = m_conv_w[0], row1(m_conv_b[0])
    dtb, alog = _pad_lanes(m_dt_bias[0]), _pad_lanes(m_a_log[0])
    dsk = row1(jnp.repeat(m_d[0], M_HEADDIM))
    nw = row1(m_norm[0])
    e64 = _expand_mat()

    mod = _ada(jnp.concatenate([c_prompt, c_sample], axis=0), w_ada[0], b_ada[0])
    mod_p = mod[:bp].reshape(bp, 1, 6 * D_MODEL)
    mod_s = mod[bp:]

    ar, ai, bbr, bbi = _s5_disc(s5_lam_re[0], s5_lam_im[0], s5_log_dt[0], s5_b_re[0], s5_b_im[0])
    s5m = _s5_matrices(ar, ai, bbr, bbi, s5_c_re[0], s5_c_im[0], s5_d[0])
    g_pre = row1(norm_pre_mix[0])

    xp = x_prompt.reshape(bp * seq, D_MODEL)
    xs = x_sample.reshape(bs, D_MODEL)
    tm = min(ROW_TILE, seq)
    u, zs, act, dt, gates, tail = _in_proj(xp, mod_p, g_pre, ws_in, dtb, cw, cb,
                                           per_row=False, tm=tm, rows_per_batch=seq)
    u_s, zs_s, xbc_s, dt_s, gates_s = _in_proj(xs, mod_s, g_pre, ws_in, dtb, cw, cb,
                                               per_row=True, tm=bs, rows_per_batch=1)

    p_conv = tail[:, SUBLANES - (M_CONV - 1):, :]
    zeros_s5 = jnp.zeros((bp, S5_GROUPS * S5_STATE), F32)
    (y5, p_re, p_im), (yb_s, s_ssm, s_conv) = _s5_ssd_step(
        u, zeros_s5, zeros_s5, *s5m, zs_s, xbc_s, dt_s, jnp.swapaxes(state_conv[0], 0, 1),
        state_ssm[0].reshape(bs, M_INNER, M_STATE), cw, cb, alog, dsk, nw, e64, tl=tm)
    yb, p_ssm, (wglu, wb5, wbs, wo, wfi, wfo) = _ssd(
        zs, act, dt, alog, dsk, nw, e64,
        (s5_w_glu[0], w_branch_s5[0], w_branch_ssd[0], w_out[0], w_ffn_in[0], w_ffn_out[0]), nbatch=bp, seq=seq)
    ws_out = (wglu, row1(s5_b_glu[0]), wb5, wbs, wo, row1(norm_post_mix[0]),
              row1(norm_pre_ffn[0]), row1(norm_post_ffn[0]), wfi, wfo)
    y_prompt = _out_stage(xp, y5, yb, gates, mod_p, ws_out, per_row=False, tm=tm, rows_per_batch=seq)

    y5_s, s_re, s_im = _s5(u_s, state_s5_re[0].reshape(bs, -1), state_s5_im[0].reshape(bs, -1), *s5m, tl=1)
    y_sample = _out_stage(xs, y5_s, yb_s, gates_s, mod_s, ws_out, per_row=True, tm=bs, rows_per_batch=1)

    s5_shape = (1, -1, S5_GROUPS, S5_STATE)
    ssm_shape = (1, -1, M_HEADS, M_HEADDIM, M_STATE)
    return (y_prompt.reshape(bp, seq, D_MODEL), y_sample.reshape(bs, 1, D_MODEL),
            p_re.reshape(s5_shape), p_im.reshape(s5_shape), p_ssm.reshape(ssm_shape), p_conv[None],
            s_re.reshape(s5_shape), s_im.reshape(s5_shape), s_ssm.reshape(ssm_shape), jnp.swapaxes(s_conv, 0, 1)[None])
```

```python
import functools

import jax
import jax.numpy as jnp
from jax import lax
from jax.experimental import pallas as pl
from jax.experimental.pallas import tpu as pltpu

F32 = jnp.float32
BF16 = jnp.bfloat16

D_MODEL = 1024
EPS = 1e-6
S5_GROUP = 16
S5_GROUPS = 64
S5_STATE = 64
M_INNER = 2048
M_HEADDIM = 64
M_HEADS = 32
M_GROUPS = 4
M_HPG = 8
M_STATE = 128
M_CONV = 4
M_CONV_DIM = 3072
M_CHUNK = 128
D_FF = 2816
OFF_Z = D_MODEL
OFF_XBC = OFF_Z + M_INNER
OFF_DT = OFF_XBC + M_CONV_DIM
OFF_GA = OFF_DT + M_HEADS
IN_COLS = OFF_GA + 2 * D_MODEL

LANES = 128
SUBLANES = 8
VMEM_LIMIT = 56 * 1024 * 1024

S5_CH = 256
S5_NS = 1024
S5_CHUNKS = D_MODEL // S5_CH
S5_TILES = S5_NS // LANES
S5_PIECES = S5_CH // LANES
S5_SUB_ROWS = 256
ROW_TILE = 256
INPROJ_SLAB = 512
INPROJ_DEPTH = 3

_NT = (((1,), (1,)), ((), ()))
_TN = (((0,), (0,)), ((), ()))


def _const_spec(shape):
    nd = len(shape)
    return pl.BlockSpec(shape, lambda *_: (0,) * nd, pipeline_mode=pl.Buffered(1))


def _params(sem):
    return pltpu.CompilerParams(dimension_semantics=sem, vmem_limit_bytes=VMEM_LIMIT)


def _sigmoid(x):
    return 0.5 * jnp.tanh(0.5 * x) + 0.5


def _silu(x):
    hx = 0.5 * x
    return hx + hx * jnp.tanh(hx)


def _softplus(x):
    return jnp.maximum(x, 0.0) + jnp.log1p(jnp.exp(-jnp.abs(x)))


def _gelu_tanh(x):
    return 0.5 * x * (1.0 + jnp.tanh(0.7978845608028654 * (x + 0.044715 * (x * x * x))))


def _rms(x, g):
    return x * lax.rsqrt(jnp.mean(x * x, axis=-1, keepdims=True) + EPS) * g


def _dot(a, b):
    return jnp.dot(a, b, preferred_element_type=F32)


def _pieces3(x):
    hi = x.astype(BF16).astype(F32)
    r1 = x - hi
    mid = r1.astype(BF16).astype(F32)
    lo = (r1 - mid).astype(BF16).astype(F32)
    return hi, mid, lo


def _ordered_after(x, dep):
    zero = jnp.where(dep[:, 0:LANES] > jnp.inf, 1.0, 0.0)
    return x + jnp.concatenate([zero] * (x.shape[1] // LANES), axis=1)


def _split2(x):
    return jnp.concatenate(_pieces3(x)[:2], axis=-1).astype(BF16)


def _split3(x):
    return jnp.concatenate(_pieces3(x), axis=-1).astype(BF16)


def _ada_kernel(cp_ref, cs_ref, w_ref, b_ref, op_ref, os_ref):
    w = w_ref[...].astype(BF16)
    op_ref[:, 0, :] = _dot(_silu(cp_ref[...]).astype(BF16), w) + b_ref[...]
    os_ref[...] = _dot(_silu(cs_ref[...]).astype(BF16), w) + b_ref[...]


def _ada(c_prompt, c_sample, w, b):
    bp, bs = c_prompt.shape[0], c_sample.shape[0]
    tn = 2048
    return pl.pallas_call(
        _ada_kernel,
        grid=(6 * D_MODEL // tn,),
        in_specs=[_const_spec((bp, D_MODEL)), _const_spec((bs, D_MODEL)),
                  pl.BlockSpec((D_MODEL, tn), lambda j: (0, j)),
                  pl.BlockSpec((1, tn), lambda j: (0, j))],
        out_specs=[pl.BlockSpec((bp, 1, tn), lambda j: (0, 0, j)), pl.BlockSpec((bs, tn), lambda j: (0, j))],
        out_shape=[jax.ShapeDtypeStruct((bp, 1, 6 * D_MODEL), F32), jax.ShapeDtypeStruct((bs, 6 * D_MODEL), F32)],
        compiler_params=_params(("arbitrary",)),
        name="ada",
    )(c_prompt, c_sample, w, b.reshape(1, -1))


def _s5_disc_kernel(lr_ref, li_ref, ldt_ref, br_ref, bi_ref, ar_ref, ai_ref, bbr_ref, bbi_ref):
    lr = lr_ref[...]
    li = li_ref[...]
    dt = jnp.exp(ldt_ref[...])
    mag = jnp.exp(lr * dt)
    ar = mag * jnp.cos(li * dt)
    ai = mag * jnp.sin(li * dt)
    den = lr * lr + li * li
    nr = ar - 1.0
    f_re = (nr * lr + ai * li) / den
    f_im = (ai * lr - nr * li) / den
    br = br_ref[...]
    bi = bi_ref[...]
    ar_ref[...] = ar
    ai_ref[...] = ai
    bbr_ref[...] = f_re * br - f_im * bi
    bbi_ref[...] = f_re * bi + f_im * br


def _s5_disc(lam_re, lam_im, log_dt, b_re, b_im):
    g, n = lam_re.shape
    w = S5_GROUP * n
    tile = lambda a: jnp.tile(a, (1, S5_GROUP))
    ldt = jnp.broadcast_to(log_dt[:, None], (g, w))
    brt = jnp.swapaxes(b_re, 1, 2).reshape(g, w)
    bit = jnp.swapaxes(b_im, 1, 2).reshape(g, w)
    shp = jax.ShapeDtypeStruct((g, w), F32)
    ar, ai, bbr, bbi = pl.pallas_call(
        _s5_disc_kernel,
        out_shape=(shp, shp, shp, shp),
        name="s5_disc",
    )(tile(lam_re), tile(lam_im), ldt, brt, bit)
    return (ar[:, :n], ai[:, :n], bbr.reshape(g, S5_GROUP, n), bbi.reshape(g, S5_GROUP, n))


def _s5_matrices(abar_re, abar_im, bbar_re, bbar_im, c_re, c_im, d_skip):
    ns = 2 * S5_NS
    b = jnp.concatenate([bbar_re.reshape(D_MODEL, S5_STATE), bbar_im.reshape(D_MODEL, S5_STATE)],
                        axis=1).reshape(S5_CHUNKS, S5_CH, 2 * S5_STATE)

    def by_tile(c):
        return jnp.swapaxes(c, 1, 2).reshape(S5_CHUNKS, S5_TILES, 1, 2, S5_STATE, S5_GROUP)
    c = jnp.concatenate([by_tile(c_re), by_tile(-c_im)], axis=2).reshape(S5_CHUNKS, ns, S5_GROUP)

    by_tile_a = lambda v: v.reshape(S5_CHUNKS, S5_TILES, 1, LANES)
    a = jnp.concatenate([by_tile_a(abar_re), by_tile_a(abar_im)], axis=2).reshape(S5_CHUNKS, 1, ns)
    d = d_skip.reshape(S5_CHUNKS, 1, S5_CH)
    return b, a, c, d


def _s5_state_group(i):
    return 2 * (i // (2 * LANES)) + (i % LANES) // S5_STATE


def _s5_expand_operands(b_ref, c_ref, bm_ref, cm_ref):
    tw = 2 * LANES
    row_g = lax.broadcasted_iota(jnp.int32, (S5_CH, LANES), 0) // S5_GROUP
    lane_half = lax.broadcasted_iota(jnp.int32, (S5_CH, LANES), 1) // S5_STATE
    for part in range(2):
        bp = b_ref[:, part * S5_STATE:(part + 1) * S5_STATE]
        pattern = jnp.concatenate([bp, bp], axis=1)
        for j in range(S5_TILES):
            tile = jnp.where(row_g == 2 * j + lane_half, pattern, 0.0)
            bm_ref[:, j * tw + part * LANES:j * tw + (part + 1) * LANES] = tile.astype(BF16)
    spread = (lax.broadcasted_iota(jnp.int32, (S5_GROUP, S5_CH), 1) % S5_GROUP
              == lax.broadcasted_iota(jnp.int32, (S5_GROUP, S5_CH), 0)).astype(BF16)
    rows = 512
    for r0 in range(0, 2 * S5_NS, rows):
        tiled = _dot(c_ref[r0:r0 + rows, :].astype(BF16), spread)
        own = (_s5_state_group(lax.broadcasted_iota(jnp.int32, (rows, S5_CH), 0) + r0)
               == lax.broadcasted_iota(jnp.int32, (rows, S5_CH), 1) // S5_GROUP)
        cm_ref[r0:r0 + rows, :] = jnp.where(own, tiled, 0.0).astype(BF16)


def _split_w_in_kernel(wt_ref, wm_ref, wdt_ref, wg_ref):
    tk = wt_ref.shape[1]
    step = 1024
    for j in range(0, OFF_DT, step):
        wm_ref[:, j:j + step] = wt_ref[j:j + step, :].T.astype(BF16)
    head = lax.broadcasted_iota(jnp.int32, (tk, LANES), 1) < M_HEADS
    wdt_ref[...] = jnp.where(head, wt_ref[OFF_DT:OFF_DT + LANES, :].T, 0.0).astype(BF16)
    for j in range(0, 2 * D_MODEL, step):
        wg_ref[:, j:j + step] = wt_ref[OFF_GA + j:OFF_GA + j + step, :].T.astype(BF16)


def _split_w_in(w):
    k = w.shape[0]
    tk = 256
    outs = (OFF_DT, LANES, 2 * D_MODEL)
    return pl.pallas_call(
        _split_w_in_kernel,
        grid=(k // tk,),
        in_specs=[pl.BlockSpec((IN_COLS, tk), lambda i: (0, i))],
        out_specs=[pl.BlockSpec((tk, n), lambda i: (i, 0)) for n in outs],
        out_shape=[jax.ShapeDtypeStruct((k, n), BF16) for n in outs],
        compiler_params=_params(("parallel",)),
        name="split_w_in",
    )(jnp.swapaxes(w, 0, 1))


def _inproj_kernel(x_ref, sh_ref, sc_ref, g_ref, wm_ref, wdt_ref, wg_ref, dtb_ref, cw_ref, cb_ref,
                   u_ref, z_ref, xbc_ref, dt_ref, gate_ref, *rest, tiles_per_seq):
    accs = rest[-INPROJ_DEPTH:]
    h = _rms(x_ref[...], g_ref[...]) * (1.0 + sc_ref[...]) + sh_ref[...]
    hb = h.astype(BF16)
    rows = hb.shape[0]
    step = INPROJ_SLAB

    if u_ref.shape[0] > rows:
        u_ref[rows:, :] = jnp.zeros((u_ref.shape[0] - rows, D_MODEL), F32)

    if tiles_per_seq is None:
        conv_fn = None
    else:
        tail_ref, carry_ref = rest[:2]

        @pl.when(pl.program_id(0) % tiles_per_seq == 0)
        def _():
            carry_ref[...] = jnp.zeros_like(carry_ref)

        def conv_fn(r, sl):
            width = sl.stop - sl.start
            w = [cw_ref[k:k + 1, sl] for k in range(M_CONV)]
            bias = cb_ref[:, sl]
            row = lax.broadcasted_iota(jnp.int32, (SUBLANES, width), 0)
            xp = carry_ref[:, sl]
            farp = xp * w[1] + pltpu.roll(xp, 1, axis=0) * w[0]
            outs = []
            for i in range(rows // SUBLANES):
                xi = r[i * SUBLANES:(i + 1) * SUBLANES, :]
                if outs:
                    xi = _ordered_after(xi, outs[-1])
                s1 = jnp.where(row == 0, pltpu.roll(xp, 1, axis=0), pltpu.roll(xi, 1, axis=0))
                far = xi * w[1] + s1 * w[0]
                s2 = jnp.where(row < 2, pltpu.roll(farp, 2, axis=0), pltpu.roll(far, 2, axis=0))
                outs.append(_silu(xi * w[3] + s1 * w[2] + s2 + bias))
                xp, farp = xi, far
            carry_ref[:, sl] = r[rows - SUBLANES:, :]
            tail_ref[:, sl] = r[rows - SUBLANES:, :]
            return jnp.concatenate(outs, axis=0)

    def slabs(o_ref, w_ref, w_off, fn):
        n = o_ref.shape[-1]
        return [(o_ref, slice(j, j + min(step, n - j)), w_ref, w_off + j, fn) for j in range(0, n, step)]

    su = slabs(u_ref, wm_ref, 0, None)
    sz = slabs(z_ref, wm_ref, OFF_Z, None)
    sx = slabs(xbc_ref, wm_ref, OFF_XBC, conv_fn)
    sd = slabs(dt_ref, wdt_ref, 0, lambda r, sl: _softplus(r + dtb_ref[:, sl]))
    sg = slabs(gate_ref, wg_ref, 0, None)
    light = su + sz + sg + sd
    order = []
    while sx or light:
        if sx:
            order.append(sx.pop(0))
        if light:
            order.append(light.pop(0))

    def matmul(s):
        o_ref, sl, w_ref, w_col, _ = order[s]
        accs[s % INPROJ_DEPTH][:, 0:sl.stop - sl.start] = _dot(hb, w_ref[:, w_col:w_col + sl.stop - sl.start])

    for s in range(min(INPROJ_DEPTH - 1, len(order))):
        matmul(s)
    for s, (o_ref, sl, _, _, fn) in enumerate(order):
        if s + INPROJ_DEPTH - 1 < len(order):
            matmul(s + INPROJ_DEPTH - 1)
        r = accs[s % INPROJ_DEPTH][:, 0:sl.stop - sl.start]
        o_ref[0:rows, sl] = r if fn is None else fn(r, sl)


def _mod_specs(per_row, tm, rows_per_mod, cols):
    if per_row:
        return [pl.BlockSpec((tm, D_MODEL), functools.partial(lambda i, c: (i, c), c=c)) for c in cols]
    return [pl.BlockSpec((None, 1, D_MODEL), functools.partial(lambda i, c: (i // rows_per_mod, 0, c), c=c))
            for c in cols]


def _s5_layout(per_row, t, tm, rows_per_batch, width):
    if per_row:
        return (1, t, width), pl.BlockSpec((None, tm, width), lambda i: (0, i, 0))
    nblk, pitch = rows_per_batch // tm, _s5_pitch(tm)
    return ((nblk, (t // rows_per_batch) * pitch, width),
            pl.BlockSpec((None, pitch, width), lambda i: (i % nblk, i // nblk, 0)))


def _in_proj(x, mod, g, ws, dtb, cw, cb, *, per_row, tm, rows_per_batch):
    t = x.shape[0]
    row = lambda n: pl.BlockSpec((tm, n), lambda i: (i, 0))
    outs = (M_INNER, M_CONV_DIM, LANES, 2 * D_MODEL)
    u_shape, u_spec = _s5_layout(per_row, t, tm, rows_per_batch, D_MODEL)
    out_specs = [u_spec] + [row(n) for n in outs]
    out_shape = [jax.ShapeDtypeStruct(u_shape, F32)] + [jax.ShapeDtypeStruct((t, n), F32) for n in outs]
    scratch = []
    tiles_per_seq = None
    if not per_row:
        tiles_per_seq = rows_per_batch // tm
        out_specs.append(pl.BlockSpec((None, SUBLANES, M_CONV_DIM), lambda i: (i // tiles_per_seq, 0, 0)))
        out_shape.append(jax.ShapeDtypeStruct((t // rows_per_batch, SUBLANES, M_CONV_DIM), F32))
        scratch.append(pltpu.VMEM((SUBLANES, M_CONV_DIM), F32))
    scratch += [pltpu.VMEM((tm, INPROJ_SLAB), F32)] * INPROJ_DEPTH
    consts = (g,) + tuple(ws) + (dtb, cw, cb)
    return pl.pallas_call(
        functools.partial(_inproj_kernel, tiles_per_seq=tiles_per_seq),
        grid=(t // tm,),
        in_specs=[row(D_MODEL)] + _mod_specs(per_row, tm, rows_per_batch // tm if not per_row else 1, (0, 1))
                 + [_const_spec(w.shape) for w in consts],
        out_specs=out_specs,
        out_shape=out_shape,
        scratch_shapes=scratch,
        compiler_params=_params(("arbitrary",)),
        name="in_proj",
    )(x, mod, mod, *consts)


def _s5_pitch(tl):
    return tl if tl == 1 else tl + SUBLANES


def _s5_kernel(*refs, nb, tl, sub_rows, side=None):
    nu = S5_PIECES
    u_refs = refs[:nu]
    h0r_ref, h0i_ref, b_ref, a_ref, c_ref, d_ref = refs[nu:nu + 6]
    y_refs = refs[nu + 6:2 * nu + 6]
    hr_ref, hi_ref, hst_ref, bu0_ref, xs0_ref, bu1_ref, xs1_ref, bm_ref, cm_ref = refs[2 * nu + 6:]
    bufs = ((bu0_ref, xs0_ref), (bu1_ref, xs1_ref))
    tb = pl.program_id(1)
    tsub = sub_rows // nb
    tw = 2 * LANES
    pitch = _s5_pitch(tl)

    def step_rows(t):
        return pl.ds(t, nb, stride=pitch) if pitch > 1 else pl.ds(0, nb)

    @pl.when(tb == 0)
    def _():
        _s5_expand_operands(b_ref, c_ref, bm_ref, cm_ref)
        for j in range(S5_TILES):
            hst_ref[:, j * tw:j * tw + LANES] = h0r_ref[:, j * LANES:(j + 1) * LANES]
            hst_ref[:, j * tw + LANES:(j + 1) * tw] = h0i_ref[:, j * LANES:(j + 1) * LANES]

    ar = [jnp.broadcast_to(a_ref[:, j * tw:j * tw + LANES], (nb, LANES)) for j in range(S5_TILES)]
    ai = [jnp.broadcast_to(a_ref[:, j * tw + LANES:(j + 1) * tw], (nb, LANES)) for j in range(S5_TILES)]
    dsk = d_ref[...]

    def project_in(t0, bu_ref, xs_ref):
        xs = jnp.concatenate([jnp.concatenate([r[step_rows(t0 + i), :] for r in u_refs], axis=1)
                              for i in range(tsub)], axis=0)
        xs_ref[...] = xs
        lhs = xs.astype(BF16)
        for j in range(S5_TILES):
            bu_ref[:, j * tw:(j + 1) * tw] = _dot(lhs, bm_ref[:, j * tw:(j + 1) * tw])

    def scan_project_out(t0, bu_ref, xs_ref, carry):
        new, hs = [], []
        for j in range(S5_TILES):
            hr, hi = carry[2 * j], carry[2 * j + 1]
            rows = []
            for i in range(tsub):
                br = bu_ref[i * nb:(i + 1) * nb, j * tw:j * tw + LANES]
                bi = bu_ref[i * nb:(i + 1) * nb, j * tw + LANES:(j + 1) * tw]
                hr, hi = ar[j] * hr - ai[j] * hi + br, ar[j] * hi + ai[j] * hr + bi
                rows.append(jnp.concatenate([hr, hi], axis=1))
            new += [hr, hi]
            hs.append(jnp.concatenate(rows, axis=0).astype(BF16))
        y = _dot(jnp.concatenate(hs, axis=1), cm_ref[...]) + dsk * xs_ref[...]
        for i in range(tsub):
            for k, y_ref in enumerate(y_refs):
                y_ref[step_rows(t0 + i), :] = y[i * nb:(i + 1) * nb, k * LANES:(k + 1) * LANES]
        return tuple(new)

    h_in = []
    for j in range(S5_TILES):
        h_in += [hst_ref[:, j * tw:j * tw + LANES], hst_ref[:, j * tw + LANES:(j + 1) * tw]]
    nsub = tl // tsub
    (bu0, xs0), (bu1, xs1) = bufs
    project_in(0, bu0, xs0)
    if nsub == 1:
        h_out = scan_project_out(0, bu0, xs0, tuple(h_in))
    else:
        def pair(p, carry):
            t0 = 2 * p * tsub
            project_in(t0 + tsub, bu1, xs1)
            carry = scan_project_out(t0, bu0, xs0, carry)
            project_in(jnp.minimum(t0 + 2 * tsub, tl - tsub), bu0, xs0)
            return scan_project_out(t0 + tsub, bu1, xs1, carry)
        h_out = lax.fori_loop(0, nsub // 2, pair, tuple(h_in))
    for j in range(S5_TILES):
        hst_ref[:, j * tw:j * tw + LANES] = h_out[2 * j]
        hst_ref[:, j * tw + LANES:(j + 1) * tw] = h_out[2 * j + 1]

    if pitch > tl:
        for y_ref in y_refs:
            for b in range(nb):
                y_ref[b * pitch + tl:(b + 1) * pitch, :] = jnp.zeros((pitch - tl, LANES), F32)

    @pl.when(tb == pl.num_programs(1) - 1)
    def _():
        for j in range(S5_TILES):
            hr_ref[:, j * LANES:(j + 1) * LANES] = h_out[2 * j]
            hi_ref[:, j * LANES:(j + 1) * LANES] = h_out[2 * j + 1]

    if side is not None:
        side()


def _s5(u, h0r, h0i, b, a, c, d, *, tl):
    nblk, rows, _ = u.shape
    pitch = _s5_pitch(tl)
    nb = rows // pitch
    sub_rows = min(S5_SUB_ROWS, tl * nb)
    nsub = tl * nb // sub_rows
    assert sub_rows % nb == 0 and tl * nb % sub_rows == 0 and (nsub == 1 or nsub % 2 == 0)
    chunk3 = lambda s: pl.BlockSpec((None,) + s, lambda c, t: (c, 0, 0))
    st_spec = pl.BlockSpec((nb, S5_NS), lambda c, t: (0, c))
    nu = S5_PIECES
    u_specs = [pl.BlockSpec((None, rows, LANES), functools.partial(lambda c, t, k: (t, 0, nu * c + k), k=k))
               for k in range(nu)]
    y_spec = pl.BlockSpec((None, rows, LANES), lambda c, t: (t, 0, c))
    y_shape = jax.ShapeDtypeStruct((nblk, rows, D_MODEL // nu), F32)
    res = pl.pallas_call(
        functools.partial(_s5_kernel, nb=nb, tl=tl, sub_rows=sub_rows),
        grid=(S5_CHUNKS, nblk),
        in_specs=u_specs + [st_spec, st_spec, chunk3((S5_CH, 2 * S5_STATE)), chunk3((1, 2 * S5_NS)),
                            chunk3((2 * S5_NS, S5_GROUP)), chunk3((1, S5_CH))],
        out_specs=[y_spec] * nu + [st_spec, st_spec],
        out_shape=[y_shape] * nu + [jax.ShapeDtypeStruct(h0r.shape, F32), jax.ShapeDtypeStruct(h0i.shape, F32)],
        scratch_shapes=[pltpu.VMEM((nb, 2 * S5_NS), F32)]
                       + [pltpu.VMEM((sub_rows, 2 * S5_NS), F32), pltpu.VMEM((sub_rows, S5_CH), F32)] * 2
                       + [pltpu.VMEM((S5_CH, 2 * S5_NS), BF16), pltpu.VMEM((2 * S5_NS, S5_CH), BF16)],
        compiler_params=_params(("parallel", "arbitrary")),
        name="s5",
    )(*([u] * nu), h0r, h0i, b, a, c, d)
    return res[:nu], res[nu], res[nu + 1]


def _expand_mat():
    r = lax.broadcasted_iota(jnp.int32, (3 * LANES, M_HEADS * M_HEADDIM), 0) % LANES
    c = lax.broadcasted_iota(jnp.int32, (3 * LANES, M_HEADS * M_HEADDIM), 1) // M_HEADDIM
    return (r == c).astype(BF16)


LOG2E = 1.4426950408889634
SSD_CHUNKS_PER_STEP = 4


def _ssd_kernel(*refs, n_cast):
    z_ref, act_ref, dt_ref, alog_ref, dsk_ref, nw_ref, e64_ref = refs[:7]
    w_refs = refs[7:7 + n_cast]
    y_ref, hout_ref = refs[7 + n_cast:9 + n_cast]
    wb_refs = refs[9 + n_cast:9 + 2 * n_cast]
    ht_ref, xs_ref, xdte_ref, eacs_ref, yacc_ref = refs[9 + 2 * n_cast:]
    c = pl.program_id(1)
    for w_ref, wb_ref in zip(w_refs, wb_refs):
        wb_ref[...] = w_ref[...].astype(BF16)

    @pl.when(c == 0)
    def _():
        ht_ref[...] = jnp.zeros_like(ht_ref)

    for k in range(SSD_CHUNKS_PER_STEP):
        rows = pl.ds(k * M_CHUNK, M_CHUNK)
        _ssd_chunk(z_ref.at[rows], act_ref.at[rows], dt_ref.at[rows], alog_ref, dsk_ref, nw_ref, e64_ref,
                   y_ref.at[rows], ht_ref, xs_ref.at[k], xdte_ref.at[k], eacs_ref.at[k], yacc_ref.at[k])

    @pl.when(c == pl.num_programs(1) - 1)
    def _():
        for j in range(0, M_INNER, LANES):
            hout_ref[j:j + LANES, :] = ht_ref[:, j:j + LANES].T


def _ssd_chunk(z_ref, act_ref, dt_ref, alog_ref, dsk_ref, nw_ref, e64_ref,
               y_ref, ht_ref, xs_ref, xdte_ref, eacs_ref, yacc_ref):
    q = M_CHUNK
    cs = 512

    dt = dt_ref[...]
    a2 = dt * (-jnp.exp(alog_ref[...]) * LOG2E)
    ri = lax.broadcasted_iota(jnp.int32, (q, q), 0)
    ci = lax.broadcasted_iota(jnp.int32, (q, q), 1)
    causal = ri >= ci
    acs = jnp.dot(causal.astype(F32), a2, preferred_element_type=F32, precision=lax.Precision.HIGHEST)
    src_t = (acs - jnp.log2(dt)).T
    eacs2 = _split2(jnp.exp2(acs))
    dte2 = _split2(dt * jnp.exp2(acs[q - 1:q, :] - acs))
    for j in range(0, M_INNER, cs):
        sl = slice(j, j + cs)
        e = e64_ref[0:2 * LANES, sl]
        xs = act_ref[:, sl]
        xs_ref[:, sl] = xs.astype(BF16)
        xdte_ref[:, sl] = (xs * _dot(dte2, e)).astype(BF16)
        eacs_ref[:, sl] = _dot(eacs2, e)

    hp = M_HPG * M_HEADDIM
    lane = lax.broadcasted_iota(jnp.int32, (q, LANES), 1)
    first_head = lane < M_HEADDIM
    for g in range(M_GROUPS):
        gs = slice(g * hp, (g + 1) * hp)
        bm_t = act_ref[:, M_INNER + g * M_STATE:M_INNER + (g + 1) * M_STATE].T.astype(BF16)
        cm = act_ref[:, M_INNER + (M_GROUPS + g) * M_STATE:M_INNER + (M_GROUPS + g + 1) * M_STATE].astype(BF16)
        cbm = _dot(cm, bm_t)
        y_off = _dot(cm, ht_ref[:, gs].astype(BF16))
        st_t = _dot(bm_t, xdte_ref[:, gs])
        for pr in range(M_HPG // 2):
            h0 = g * M_HPG + 2 * pr
            ps = slice(h0 * M_HEADDIM, (h0 + 2) * M_HEADDIM)
            ms = []
            for h in (h0, h0 + 1):
                d2 = jnp.broadcast_to(acs[:, h:h + 1], (q, q)) - src_t[h:h + 1, :]
                ms.append((cbm * jnp.where(causal, jnp.exp2(d2), 0.0)).astype(BF16))
            xp = xs_ref[:, ps]
            zero = jnp.zeros_like(xp)
            rhs = jnp.concatenate([jnp.where(first_head, xp, zero), jnp.where(first_head, zero, xp)], axis=0)
            yd = _dot(jnp.concatenate(ms, axis=1), rhs)
            yacc_ref[:, ps] = yd + y_off[:, ps.start - gs.start:ps.stop - gs.start] * eacs_ref[:, ps] \
                + dsk_ref[:, ps] * act_ref[:, ps]
        ht_ref[:, gs] = eacs_ref[q - 1:q, gs] * ht_ref[:, gs] + st_t

    ss = jnp.zeros((q, 1), F32)
    for j in range(0, M_INNER, cs):
        sl = slice(j, j + cs)
        gt = yacc_ref[:, sl] * _silu(z_ref[:, sl])
        yacc_ref[:, sl] = gt
        ss = ss + jnp.sum(gt * gt, axis=-1, keepdims=True)
    inv = lax.rsqrt(ss * (1.0 / M_INNER) + EPS)
    for j in range(0, M_INNER, cs):
        sl = slice(j, j + cs)
        y_ref[:, sl] = yacc_ref[:, sl] * inv * nw_ref[:, sl]


def _cast_spec(w, nbatch, nc):
    rows, steps = w.shape[0], nbatch * nc
    n = next(n for n in range(steps, 0, -1) if rows % n == 0 and (rows // n) % (2 * SUBLANES) == 0)
    return pl.BlockSpec((rows // n, w.shape[1]), lambda b, c: (jnp.minimum(b * nc + c, n - 1), 0))


def _ssd(z, act, dt, alog, dsk, nw, e64, cast_ws, *, nbatch, seq):
    q = M_CHUNK
    k = SSD_CHUNKS_PER_STEP
    assert seq % (k * q) == 0
    nc = seq // (k * q)
    row = lambda n: pl.BlockSpec((k * q, n), lambda b, c: (b * nc + c, 0))
    cast_specs = [_cast_spec(w, nbatch, nc) for w in cast_ws]
    res = pl.pallas_call(
        functools.partial(_ssd_kernel, n_cast=len(cast_ws)),
        grid=(nbatch, nc),
        in_specs=[row(M_INNER), row(M_CONV_DIM), row(LANES), _const_spec(alog.shape),
                  _const_spec(dsk.shape), _const_spec(nw.shape), _const_spec(e64.shape)] + cast_specs,
        out_specs=[row(M_INNER), pl.BlockSpec((None, M_INNER, M_STATE), lambda b, c: (b, 0, 0))] + cast_specs,
        out_shape=[jax.ShapeDtypeStruct((nbatch * seq, M_INNER), F32),
                   jax.ShapeDtypeStruct((nbatch, M_INNER, M_STATE), F32)]
                  + [jax.ShapeDtypeStruct(w.shape, BF16) for w in cast_ws],
        scratch_shapes=[pltpu.VMEM((M_STATE, M_INNER), F32), pltpu.VMEM((k, q, M_INNER), BF16),
                        pltpu.VMEM((k, q, M_INNER), BF16), pltpu.VMEM((k, q, M_INNER), F32),
                        pltpu.VMEM((k, q, M_INNER), F32)],
        compiler_params=_params(("arbitrary", "arbitrary")),
        name="ssd",
    )(z, act, dt, alog, dsk, nw, e64, *cast_ws)
    return res[0], res[1], res[2:]


SSD_STEP_BB = 8


def _ssd_step_kernel(z_ref, xbc_ref, dt_ref, cbuf_ref, h0_ref, cw_ref, cb_ref, alog_ref, dsk_ref, nw_ref,
                     e64_ref, y_ref, hout_ref, convout_ref):
    bb = SSD_STEP_BB
    x_new = xbc_ref[...]
    conv = cb_ref[...] + x_new * cw_ref[M_CONV - 1:M_CONV, :]
    for k in range(M_CONV - 1):
        conv = conv + cbuf_ref[k] * cw_ref[k:k + 1, :]
    act = _silu(conv)
    for k in range(M_CONV - 2):
        convout_ref[k] = cbuf_ref[k + 1]
    convout_ref[M_CONV - 2] = x_new

    xs = act[:, :M_INNER]
    bmat = act[:, M_INNER:M_INNER + M_GROUPS * M_STATE]
    cmat = act[:, M_INNER + M_GROUPS * M_STATE:]
    dt = dt_ref[...]
    da = jnp.exp(dt * (-jnp.exp(alog_ref[...])))
    dt_e = _dot(_split3(dt), e64_ref[...])
    da_e = _dot(_split3(da), e64_ref[...])
    xdt = xs * dt_e

    hp = M_HPG * M_HEADDIM
    lane_group = lax.broadcasted_iota(jnp.int32, (M_GROUPS, M_INNER), 1) // hp
    row_group = lax.broadcasted_iota(jnp.int32, (M_GROUPS, M_INNER), 0)
    gmask = lane_group == row_group
    ones_rows = lax.broadcasted_iota(jnp.int32, (SUBLANES, M_STATE), 0) >= M_GROUPS

    cbx = []
    for g in range(M_GROUPS):
        cbg = jnp.sum(cmat[:, g * M_STATE:(g + 1) * M_STATE] * bmat[:, g * M_STATE:(g + 1) * M_STATE],
                      axis=-1, keepdims=True)
        cbx.append(jnp.broadcast_to(cbg, (bb, hp)))
    y = jnp.concatenate(cbx, axis=1) * xdt + dsk_ref[...] * xs

    y_off_rows = []
    for b in range(bb):
        xsel = jnp.where(gmask, jnp.broadcast_to(xdt[b:b + 1, :], (M_GROUPS, M_INNER)), 0.0)
        lhs = jnp.concatenate([xsel, *_pieces3(da_e[b:b + 1, :]), jnp.zeros((1, M_INNER), F32)],
                              axis=0).astype(BF16)
        bm4 = jnp.concatenate([bmat[b:b + 1, g * M_STATE:(g + 1) * M_STATE] for g in range(M_GROUPS)], axis=0)
        cm4 = jnp.concatenate([cmat[b:b + 1, g * M_STATE:(g + 1) * M_STATE] for g in range(M_GROUPS)], axis=0)
        zeros4 = jnp.zeros((SUBLANES - M_GROUPS, M_STATE), F32)
        rhs = jnp.concatenate([jnp.concatenate([bm4, zeros4], axis=0),
                               jnp.where(ones_rows, 1.0, 0.0)], axis=1).astype(BF16)
        res = lax.dot_general(lhs, rhs, _TN, preferred_element_type=F32)
        h0 = h0_ref[b]
        hout_ref[b] = res[:, M_STATE:] * h0 + res[:, :M_STATE]
        cm8 = jnp.concatenate([cm4, zeros4], axis=0).astype(BF16)
        yo = lax.dot_general(cm8, h0.astype(BF16), _NT, preferred_element_type=F32)
        y_off_rows.append(jnp.sum(jnp.where(gmask, yo[:M_GROUPS, :], 0.0), axis=0, keepdims=True))
    y = y + da_e * jnp.concatenate(y_off_rows, axis=0)

    y_ref[...] = _rms(y * _silu(z_ref[...]), nw_ref[...])


SSD_STEP_EVERY = 2


def _s5_ssd_step_kernel(*refs, nb, tl, sub_rows):
    n5i, nsi = S5_PIECES + 6, 11
    n5o, nso = S5_PIECES + 2, 3
    ins5, inss = refs[:n5i], refs[n5i:n5i + nsi]
    o = n5i + nsi
    outs5, outss = refs[o:o + n5o], refs[o + n5o:o + n5o + nso]
    sc5 = refs[o + n5o + nso:]
    s = pl.program_id(0) * pl.num_programs(1) + pl.program_id(1)

    def side():
        @pl.when(s % SSD_STEP_EVERY == 0)
        def _():
            _ssd_step_kernel(*inss, *outss)

    _s5_kernel(*ins5, *outs5, *sc5, nb=nb, tl=tl, sub_rows=sub_rows, side=side)


def _s5_ssd_step(u, h0r, h0i, b, a, c, d, zs, xbc, dt, cbuf, h0, cw, cb, alog, dsk, nw, e64, *, tl):
    nblk, rows, _ = u.shape
    pitch = _s5_pitch(tl)
    nb = rows // pitch
    sub_rows = min(S5_SUB_ROWS, tl * nb)
    nsub = tl * nb // sub_rows
    ns = zs.shape[0]
    bb, per = SSD_STEP_BB, SSD_STEP_EVERY
    assert sub_rows % nb == 0 and tl * nb % sub_rows == 0 and nsub % 2 == 0 and ns * per == bb * S5_CHUNKS * nblk
    chunk3 = lambda sh: pl.BlockSpec((None,) + sh, lambda c, t: (c, 0, 0))
    st_spec = pl.BlockSpec((nb, S5_NS), lambda c, t: (0, c))
    nu = S5_PIECES
    u_specs = [pl.BlockSpec((None, rows, LANES), functools.partial(lambda c, t, k: (t, 0, nu * c + k), k=k))
               for k in range(nu)]
    y5_spec = pl.BlockSpec((None, rows, LANES), lambda c, t: (t, 0, c))
    y5_shape = jax.ShapeDtypeStruct((nblk, rows, D_MODEL // nu), F32)
    step = lambda c, t: c * nblk + t
    row = lambda w: pl.BlockSpec((bb, w), lambda c, t: (step(c, t) // per, 0))
    conv_spec = pl.BlockSpec((M_CONV - 1, bb, M_CONV_DIM), lambda c, t: (0, step(c, t) // per, 0))
    st3_spec = pl.BlockSpec((bb, M_INNER, M_STATE), lambda c, t: (step(c, t) // per, 0, 0))
    res = pl.pallas_call(
        functools.partial(_s5_ssd_step_kernel, nb=nb, tl=tl, sub_rows=sub_rows),
        grid=(S5_CHUNKS, nblk),
        in_specs=u_specs + [st_spec, st_spec, chunk3((S5_CH, 2 * S5_STATE)), chunk3((1, 2 * S5_NS)),
                            chunk3((2 * S5_NS, S5_GROUP)), chunk3((1, S5_CH))]
                 + [row(M_INNER), row(M_CONV_DIM), row(LANES), conv_spec, st3_spec,
                    _const_spec(cw.shape), _const_spec(cb.shape), _const_spec(alog.shape),
                    _const_spec(dsk.shape), _const_spec(nw.shape), _const_spec(e64.shape)],
        out_specs=[y5_spec] * nu + [st_spec, st_spec] + [row(M_INNER), st3_spec, conv_spec],
        out_shape=[y5_shape] * nu + [jax.ShapeDtypeStruct(h0r.shape, F32), jax.ShapeDtypeStruct(h0i.shape, F32)]
                  + [jax.ShapeDtypeStruct((ns, M_INNER), F32), jax.ShapeDtypeStruct((ns, M_INNER, M_STATE), F32),
                     jax.ShapeDtypeStruct((M_CONV - 1, ns, M_CONV_DIM), F32)],
        scratch_shapes=[pltpu.VMEM((nb, 2 * S5_NS), F32)]
                       + [pltpu.VMEM((sub_rows, 2 * S5_NS), F32), pltpu.VMEM((sub_rows, S5_CH), F32)] * 2
                       + [pltpu.VMEM((S5_CH, 2 * S5_NS), BF16), pltpu.VMEM((2 * S5_NS, S5_CH), BF16)],
        compiler_params=_params(("arbitrary", "arbitrary")),
        name="s5_ssd_step",
    )(*([u] * nu), h0r, h0i, b, a, c, d, zs, xbc, dt, cbuf, h0, cw, cb, alog, dsk, nw, e64)
    return (res[:nu], res[nu], res[nu + 1]), (res[nu + 2], res[nu + 3], res[nu + 4])


MXU_WIDTH = 256
FFN_CHUNKS = (0, 6 * MXU_WIDTH, D_FF)


def _out_kernel(x_ref, *refs):
    y5_refs = refs[:S5_PIECES]
    (yb_ref, gate_ref, gtm_ref, shf_ref, scf_ref, gtf_ref, wglu_ref, bglu_ref, wb5_ref, wbs_ref, wout_ref,
     npm_ref, npf_ref, npo_ref, wfi_ref, wfo_ref, o_ref) = refs[S5_PIECES:]
    rows = x_ref.shape[0]
    y5 = jnp.concatenate([y5_refs[k][0:rows, c * LANES:(c + 1) * LANES]
                          for c in range(S5_CHUNKS) for k in range(S5_PIECES)], axis=1)
    ya = _gelu_tanh(y5)
    glu = ya * _sigmoid(_dot(ya.astype(BF16), wglu_ref[...]) + bglu_ref[...])
    merged = (_sigmoid(gate_ref[:, :D_MODEL]) * _dot(glu.astype(BF16), wb5_ref[...])
              + _sigmoid(gate_ref[:, D_MODEL:]) * _dot(yb_ref[...].astype(BF16), wbs_ref[...]))
    mix = _dot(merged.astype(BF16), wout_ref[...])
    x1 = x_ref[...] + gtm_ref[...] * _rms(mix, npm_ref[...])

    hb = (_rms(x1, npf_ref[...]) * (1.0 + scf_ref[...]) + shf_ref[...]).astype(BF16)
    f = jnp.zeros_like(x1)
    for j, e in zip(FFN_CHUNKS[:-1], FFN_CHUNKS[1:]):
        gg = _dot(hb, wfi_ref[:, j:e])
        uu = _dot(hb, wfi_ref[:, D_FF + j:D_FF + e])
        f = f + _dot((_silu(gg) * uu).astype(BF16), wfo_ref[j:e, :])
    o_ref[...] = x1 + gtf_ref[...] * _rms(f, npo_ref[...])


def _out_stage(x, y5, yb, gates, mod, ws, *, per_row, tm, rows_per_batch):
    t = x.shape[0]
    row = lambda n: pl.BlockSpec((tm, n), lambda i: (i, 0))
    mods = _mod_specs(per_row, tm, rows_per_batch // tm if not per_row else 1, (2, 3, 4, 5))
    _, y5_spec = _s5_layout(per_row, t, tm, rows_per_batch, D_MODEL // S5_PIECES)
    return pl.pallas_call(
        _out_kernel,
        grid=(t // tm,),
        in_specs=[row(D_MODEL)] + [y5_spec] * S5_PIECES + [row(M_INNER), row(2 * D_MODEL)] + mods
                 + [_const_spec(w.shape) for w in ws],
        out_specs=row(D_MODEL),
        out_shape=jax.ShapeDtypeStruct((t, D_MODEL), F32),
        compiler_params=_params(("parallel",)),
        name="out_stage",
    )(x, *y5, yb, gates, mod, mod, mod, mod, *ws)


def _pad_lanes(v):
    return jnp.pad(v.reshape(1, -1), ((0, 0), (0, LANES - v.shape[-1])))


def kernel(x_prompt, x_sample, state_s5_re, state_s5_im, state_ssm, state_conv, c_prompt, c_sample, w_ada, b_ada, norm_pre_mix, norm_post_mix, norm_pre_ffn, norm_post_ffn, w_in, s5_lam_re, s5_lam_im, s5_log_dt, s5_b_re, s5_b_im, s5_c_re, s5_c_im, s5_d, s5_w_glu, s5_b_glu, m_conv_w, m_conv_b, m_dt_bias, m_a_log, m_d, m_norm, w_branch_s5, w_branch_ssd, w_out, w_ffn_in, w_ffn_out):
    bp, seq, _ = x_prompt.shape
    bs = x_sample.shape[0]
    assert x_sample.shape[1] == 1 and w_ada.shape[0] == 1 and seq % M_CHUNK == 0
    row1 = lambda v: v.reshape(1, -1)

    ws_in = _split_w_in(w_in[0])
    cw, cb = m_conv_w[0], row1(m_conv_b[0])
    dtb, alog = _pad_lanes(m_dt_bias[0]), _pad_lanes(m_a_log[0])
    dsk = row1(jnp.repeat(m_d[0], M_HEADDIM))
    nw = row1(m_norm[0])
    e64 = _expand_mat()

    mod_p, mod_s = _ada(c_prompt, c_sample, w_ada[0], b_ada[0])

    ar, ai, bbr, bbi = _s5_disc(s5_lam_re[0], s5_lam_im[0], s5_log_dt[0], s5_b_re[0], s5_b_im[0])
    s5m = _s5_matrices(ar, ai, bbr, bbi, s5_c_re[0], s5_c_im[0], s5_d[0])
    g_pre = row1(norm_pre_mix[0])

    xp = x_prompt.reshape(bp * seq, D_MODEL)
    xs = x_sample.reshape(bs, D_MODEL)
    tm = min(ROW_TILE, seq)
    u, zs, act, dt, gates, tail = _in_proj(xp, mod_p, g_pre, ws_in, dtb, cw, cb,
                                           per_row=False, tm=tm, rows_per_batch=seq)
    u_s, zs_s, xbc_s, dt_s, gates_s = _in_proj(xs, mod_s, g_pre, ws_in, dtb, cw, cb,
                                               per_row=True, tm=bs, rows_per_batch=1)

    p_conv = tail[:, SUBLANES - (M_CONV - 1):, :]
    zeros_s5 = jnp.zeros((bp, S5_GROUPS * S5_STATE), F32)
    (y5, p_re, p_im), (yb_s, s_ssm, s_conv) = _s5_ssd_step(
        u, zeros_s5, zeros_s5, *s5m, zs_s, xbc_s, dt_s, jnp.swapaxes(state_conv[0], 0, 1),
        state_ssm[0].reshape(bs, M_INNER, M_STATE), cw, cb, alog, dsk, nw, e64, tl=tm)
    yb, p_ssm, (wglu, wb5, wbs, wo, wfi, wfo) = _ssd(
        zs, act, dt, alog, dsk, nw, e64,
        (s5_w_glu[0], w_branch_s5[0], w_branch_ssd[0], w_out[0], w_ffn_in[0], w_ffn_out[0]), nbatch=bp, seq=seq)
    ws_out = (wglu, row1(s5_b_glu[0]), wb5, wbs, wo, row1(norm_post_mix[0]),
              row1(norm_pre_ffn[0]), row1(norm_post_ffn[0]), wfi, wfo)
    y_prompt = _out_stage(xp, y5, yb, gates, mod_p, ws_out, per_row=False, tm=tm, rows_per_batch=seq)

    y5_s, s_re, s_im = _s5(u_s, state_s5_re[0].reshape(bs, -1), state_s5_im[0].reshape(bs, -1), *s5m, tl=1)
    y_sample = _out_stage(xs, y5_s, yb_s, gates_s, mod_s, ws_out, per_row=True, tm=bs, rows_per_batch=1)

    s5_shape = (1, -1, S5_GROUPS, S5_STATE)
    ssm_shape = (1, -1, M_HEADS, M_HEADDIM, M_STATE)
    return (y_prompt.reshape(bp, seq, D_MODEL), y_sample.reshape(bs, 1, D_MODEL),
            p_re.reshape(s5_shape), p_im.reshape(s5_shape), p_ssm.reshape(ssm_shape), p_conv[None],
            s_re.reshape(s5_shape), s_im.reshape(s5_shape), s_ssm.reshape(ssm_shape), jnp.swapaxes(s_conv, 0, 1)[None])
```

```python
import functools

import jax
import jax.numpy as jnp
from jax import lax
from jax.experimental import pallas as pl
from jax.experimental.pallas import tpu as pltpu

F32 = jnp.float32
BF16 = jnp.bfloat16

D_MODEL = 1024
EPS = 1e-6
S5_GROUP = 16
S5_GROUPS = 64
S5_STATE = 64
M_INNER = 2048
M_HEADDIM = 64
M_HEADS = 32
M_GROUPS = 4
M_HPG = 8
M_STATE = 128
M_CONV = 4
M_CONV_DIM = 3072
M_CHUNK = 128
D_FF = 2816
OFF_Z = D_MODEL
OFF_XBC = OFF_Z + M_INNER
OFF_DT = OFF_XBC + M_CONV_DIM
OFF_GA = OFF_DT + M_HEADS
IN_COLS = OFF_GA + 2 * D_MODEL

LANES = 128
SUBLANES = 8
VMEM_LIMIT = 56 * 1024 * 1024

S5_CH = 256
S5_NS = 1024
S5_CHUNKS = D_MODEL // S5_CH
S5_TILES = S5_NS // LANES
S5_PIECES = S5_CH // LANES
S5_SUB_ROWS = 256
ROW_TILE = 256
INPROJ_SLAB = 512
INPROJ_DEPTH = 3

_NT = (((1,), (1,)), ((), ()))
_TN = (((0,), (0,)), ((), ()))


def _const_spec(shape):
    nd = len(shape)
    return pl.BlockSpec(shape, lambda *_: (0,) * nd, pipeline_mode=pl.Buffered(1))


def _params(sem):
    return pltpu.CompilerParams(dimension_semantics=sem, vmem_limit_bytes=VMEM_LIMIT)


def _sigmoid(x):
    return 0.5 * jnp.tanh(0.5 * x) + 0.5


def _silu(x):
    hx = 0.5 * x
    return hx + hx * jnp.tanh(hx)


def _softplus(x):
    return jnp.maximum(x, 0.0) + jnp.log1p(jnp.exp(-jnp.abs(x)))


def _gelu_tanh(x):
    return 0.5 * x * (1.0 + jnp.tanh(0.7978845608028654 * (x + 0.044715 * (x * x * x))))


def _rms(x, g):
    return x * lax.rsqrt(jnp.mean(x * x, axis=-1, keepdims=True) + EPS) * g


def _dot(a, b):
    return jnp.dot(a, b, preferred_element_type=F32)


def _pieces3(x):
    hi = x.astype(BF16).astype(F32)
    r1 = x - hi
    mid = r1.astype(BF16).astype(F32)
    lo = (r1 - mid).astype(BF16).astype(F32)
    return hi, mid, lo


def _ordered_after(x, dep):
    zero = jnp.where(dep[:, 0:LANES] > jnp.inf, 1.0, 0.0)
    return x + jnp.concatenate([zero] * (x.shape[1] // LANES), axis=1)


def _split2(x):
    return jnp.concatenate(_pieces3(x)[:2], axis=-1).astype(BF16)


def _split3(x):
    return jnp.concatenate(_pieces3(x), axis=-1).astype(BF16)


def _ada_kernel(cp_ref, cs_ref, w_ref, b_ref, op_ref, os_ref):
    w = w_ref[...].astype(BF16)
    op_ref[:, 0, :] = _dot(_silu(cp_ref[...]).astype(BF16), w) + b_ref[...]
    os_ref[...] = _dot(_silu(cs_ref[...]).astype(BF16), w) + b_ref[...]


def _ada(c_prompt, c_sample, w, b):
    bp, bs = c_prompt.shape[0], c_sample.shape[0]
    tn = 2048
    return pl.pallas_call(
        _ada_kernel,
        grid=(6 * D_MODEL // tn,),
        in_specs=[_const_spec((bp, D_MODEL)), _const_spec((bs, D_MODEL)),
                  pl.BlockSpec((D_MODEL, tn), lambda j: (0, j)),
                  pl.BlockSpec((1, tn), lambda j: (0, j))],
        out_specs=[pl.BlockSpec((bp, 1, tn), lambda j: (0, 0, j)), pl.BlockSpec((bs, tn), lambda j: (0, j))],
        out_shape=[jax.ShapeDtypeStruct((bp, 1, 6 * D_MODEL), F32), jax.ShapeDtypeStruct((bs, 6 * D_MODEL), F32)],
        compiler_params=_params(("arbitrary",)),
        name="ada",
    )(c_prompt, c_sample, w, b.reshape(1, -1))


def _s5_disc_kernel(lr_ref, li_ref, ldt_ref, br_ref, bi_ref, ar_ref, ai_ref, bbr_ref, bbi_ref):
    lr = lr_ref[...]
    li = li_ref[...]
    dt = jnp.exp(ldt_ref[...])
    mag = jnp.exp(lr * dt)
    ar = mag * jnp.cos(li * dt)
    ai = mag * jnp.sin(li * dt)
    den = lr * lr + li * li
    nr = ar - 1.0
    f_re = (nr * lr + ai * li) / den
    f_im = (ai * lr - nr * li) / den
    br = br_ref[...]
    bi = bi_ref[...]
    ar_ref[...] = ar
    ai_ref[...] = ai
    bbr_ref[...] = f_re * br - f_im * bi
    bbi_ref[...] = f_re * bi + f_im * br


def _s5_disc(lam_re, lam_im, log_dt, b_re, b_im):
    g, n = lam_re.shape
    w = S5_GROUP * n
    tile = lambda a: jnp.tile(a, (1, S5_GROUP))
    ldt = jnp.broadcast_to(log_dt[:, None], (g, w))
    brt = jnp.swapaxes(b_re, 1, 2).reshape(g, w)
    bit = jnp.swapaxes(b_im, 1, 2).reshape(g, w)
    shp = jax.ShapeDtypeStruct((g, w), F32)
    ar, ai, bbr, bbi = pl.pallas_call(
        _s5_disc_kernel,
        out_shape=(shp, shp, shp, shp),
        name="s5_disc",
    )(tile(lam_re), tile(lam_im), ldt, brt, bit)
    return (ar[:, :n], ai[:, :n], bbr.reshape(g, S5_GROUP, n), bbi.reshape(g, S5_GROUP, n))


def _s5_matrices(abar_re, abar_im, bbar_re, bbar_im, c_re, c_im, d_skip):
    ns = 2 * S5_NS
    b = jnp.concatenate([bbar_re.reshape(D_MODEL, S5_STATE), bbar_im.reshape(D_MODEL, S5_STATE)],
                        axis=1).reshape(S5_CHUNKS, S5_CH, 2 * S5_STATE)

    def by_tile(c):
        return jnp.swapaxes(c, 1, 2).reshape(S5_CHUNKS, S5_TILES, 1, 2, S5_STATE, S5_GROUP)
    c = jnp.concatenate([by_tile(c_re), by_tile(-c_im)], axis=2).reshape(S5_CHUNKS, ns, S5_GROUP)

    by_tile_a = lambda v: v.reshape(S5_CHUNKS, S5_TILES, 1, LANES)
    a = jnp.concatenate([by_tile_a(abar_re), by_tile_a(abar_im)], axis=2).reshape(S5_CHUNKS, 1, ns)
    d = d_skip.reshape(S5_CHUNKS, 1, S5_CH)
    return b, a, c, d


def _s5_state_group(i):
    return 2 * (i // (2 * LANES)) + (i % LANES) // S5_STATE


def _s5_expand_operands(b_ref, c_ref, bm_ref, cm_ref):
    tw = 2 * LANES
    row_g = lax.broadcasted_iota(jnp.int32, (S5_CH, LANES), 0) // S5_GROUP
    lane_half = lax.broadcasted_iota(jnp.int32, (S5_CH, LANES), 1) // S5_STATE
    for part in range(2):
        bp = b_ref[:, part * S5_STATE:(part + 1) * S5_STATE]
        pattern = jnp.concatenate([bp, bp], axis=1)
        for j in range(S5_TILES):
            tile = jnp.where(row_g == 2 * j + lane_half, pattern, 0.0)
            bm_ref[:, j * tw + part * LANES:j * tw + (part + 1) * LANES] = tile.astype(BF16)
    spread = (lax.broadcasted_iota(jnp.int32, (S5_GROUP, S5_CH), 1) % S5_GROUP
              == lax.broadcasted_iota(jnp.int32, (S5_GROUP, S5_CH), 0)).astype(BF16)
    rows = 512
    for r0 in range(0, 2 * S5_NS, rows):
        tiled = _dot(c_ref[r0:r0 + rows, :].astype(BF16), spread)
        own = (_s5_state_group(lax.broadcasted_iota(jnp.int32, (rows, S5_CH), 0) + r0)
               == lax.broadcasted_iota(jnp.int32, (rows, S5_CH), 1) // S5_GROUP)
        cm_ref[r0:r0 + rows, :] = jnp.where(own, tiled, 0.0).astype(BF16)


def _split_w_in_kernel(wt_ref, wm_ref, wdt_ref, wg_ref):
    tk = wt_ref.shape[1]
    step = 1024
    for j in range(0, OFF_DT, step):
        wm_ref[:, j:j + step] = wt_ref[j:j + step, :].T.astype(BF16)
    head = lax.broadcasted_iota(jnp.int32, (tk, LANES), 1) < M_HEADS
    wdt_ref[...] = jnp.where(head, wt_ref[OFF_DT:OFF_DT + LANES, :].T, 0.0).astype(BF16)
    for j in range(0, 2 * D_MODEL, step):
        wg_ref[:, j:j + step] = wt_ref[OFF_GA + j:OFF_GA + j + step, :].T.astype(BF16)


def _split_w_in(w):
    k = w.shape[0]
    tk = 256
    outs = (OFF_DT, LANES, 2 * D_MODEL)
    return pl.pallas_call(
        _split_w_in_kernel,
        grid=(k // tk,),
        in_specs=[pl.BlockSpec((IN_COLS, tk), lambda i: (0, i))],
        out_specs=[pl.BlockSpec((tk, n), lambda i: (i, 0)) for n in outs],
        out_shape=[jax.ShapeDtypeStruct((k, n), BF16) for n in outs],
        compiler_params=_params(("parallel",)),
        name="split_w_in",
    )(jnp.swapaxes(w, 0, 1))


def _inproj_kernel(x_ref, sh_ref, sc_ref, g_ref, wm_ref, wdt_ref, wg_ref, dtb_ref, cw_ref, cb_ref,
                   u_ref, z_ref, xbc_ref, dt_ref, gate_ref, *rest, tiles_per_seq):
    accs = rest[-INPROJ_DEPTH:]
    h = _rms(x_ref[...], g_ref[...]) * (1.0 + sc_ref[...]) + sh_ref[...]
    hb = h.astype(BF16)
    rows = hb.shape[0]
    step = INPROJ_SLAB

    if u_ref.shape[0] > rows:
        u_ref[rows:, :] = jnp.zeros((u_ref.shape[0] - rows, D_MODEL), F32)

    if tiles_per_seq is None:
        conv_fn = None
    else:
        tail_ref, carry_ref = rest[:2]

        @pl.when(pl.program_id(0) % tiles_per_seq == 0)
        def _():
            carry_ref[...] = jnp.zeros_like(carry_ref)

        def conv_fn(r, sl):
            width = sl.stop - sl.start
            w = [cw_ref[k:k + 1, sl] for k in range(M_CONV)]
            bias = cb_ref[:, sl]
            row = lax.broadcasted_iota(jnp.int32, (SUBLANES, width), 0)
            xp = carry_ref[:, sl]
            farp = xp * w[1] + pltpu.roll(xp, 1, axis=0) * w[0]
            outs = []
            for i in range(rows // SUBLANES):
                xi = r[i * SUBLANES:(i + 1) * SUBLANES, :]
                if outs:
                    xi = _ordered_after(xi, outs[-1])
                s1 = jnp.where(row == 0, pltpu.roll(xp, 1, axis=0), pltpu.roll(xi, 1, axis=0))
                far = xi * w[1] + s1 * w[0]
                s2 = jnp.where(row < 2, pltpu.roll(farp, 2, axis=0), pltpu.roll(far, 2, axis=0))
                outs.append(_silu(xi * w[3] + s1 * w[2] + s2 + bias))
                xp, farp = xi, far
            carry_ref[:, sl] = r[rows - SUBLANES:, :]
            tail_ref[:, sl] = r[rows - SUBLANES:, :]
            return jnp.concatenate(outs, axis=0)

    def slabs(o_ref, w_ref, w_off, fn):
        n = o_ref.shape[-1]
        return [(o_ref, slice(j, j + min(step, n - j)), w_ref, w_off + j, fn) for j in range(0, n, step)]

    su = slabs(u_ref, wm_ref, 0, None)
    sz = slabs(z_ref, wm_ref, OFF_Z, None)
    sx = slabs(xbc_ref, wm_ref, OFF_XBC, conv_fn)
    sd = slabs(dt_ref, wdt_ref, 0, lambda r, sl: _softplus(r + dtb_ref[:, sl]))
    sg = slabs(gate_ref, wg_ref, 0, None)
    light = su + sz + sg + sd
    order = []
    while sx or light:
        if sx:
            order.append(sx.pop(0))
        if light:
            order.append(light.pop(0))

    def matmul(s):
        o_ref, sl, w_ref, w_col, _ = order[s]
        accs[s % INPROJ_DEPTH][:, 0:sl.stop - sl.start] = _dot(hb, w_ref[:, w_col:w_col + sl.stop - sl.start])

    for s in range(min(INPROJ_DEPTH - 1, len(order))):
        matmul(s)
    for s, (o_ref, sl, _, _, fn) in enumerate(order):
        if s + INPROJ_DEPTH - 1 < len(order):
            matmul(s + INPROJ_DEPTH - 1)
        r = accs[s % INPROJ_DEPTH][:, 0:sl.stop - sl.start]
        o_ref[0:rows, sl] = r if fn is None else fn(r, sl)


def _mod_specs(per_row, tm, rows_per_mod, cols):
    if per_row:
        return [pl.BlockSpec((tm, D_MODEL), functools.partial(lambda i, c: (i, c), c=c)) for c in cols]
    return [pl.BlockSpec((None, 1, D_MODEL), functools.partial(lambda i, c: (i // rows_per_mod, 0, c), c=c))
            for c in cols]


def _s5_layout(per_row, t, tm, rows_per_batch, width):
    if per_row:
        return (1, t, width), pl.BlockSpec((None, tm, width), lambda i: (0, i, 0))
    nblk, pitch = rows_per_batch // tm, _s5_pitch(tm)
    return ((nblk, (t // rows_per_batch) * pitch, width),
            pl.BlockSpec((None, pitch, width), lambda i: (i % nblk, i // nblk, 0)))


def _in_proj(x, mod, g, ws, dtb, cw, cb, *, per_row, tm, rows_per_batch):
    t = x.shape[0]
    row = lambda n: pl.BlockSpec((tm, n), lambda i: (i, 0))
    outs = (M_INNER, M_CONV_DIM, LANES, 2 * D_MODEL)
    u_shape, u_spec = _s5_layout(per_row, t, tm, rows_per_batch, D_MODEL)
    out_specs = [u_spec] + [row(n) for n in outs]
    out_shape = [jax.ShapeDtypeStruct(u_shape, F32)] + [jax.ShapeDtypeStruct((t, n), F32) for n in outs]
    scratch = []
    tiles_per_seq = None
    if not per_row:
        tiles_per_seq = rows_per_batch // tm
        out_specs.append(pl.BlockSpec((None, SUBLANES, M_CONV_DIM), lambda i: (i // tiles_per_seq, 0, 0)))
        out_shape.append(jax.ShapeDtypeStruct((t // rows_per_batch, SUBLANES, M_CONV_DIM), F32))
        scratch.append(pltpu.VMEM((SUBLANES, M_CONV_DIM), F32))
    scratch += [pltpu.VMEM((tm, INPROJ_SLAB), F32)] * INPROJ_DEPTH
    consts = (g,) + tuple(ws) + (dtb, cw, cb)
    return pl.pallas_call(
        functools.partial(_inproj_kernel, tiles_per_seq=tiles_per_seq),
        grid=(t // tm,),
        in_specs=[row(D_MODEL)] + _mod_specs(per_row, tm, rows_per_batch // tm if not per_row else 1, (0, 1))
                 + [_const_spec(w.shape) for w in consts],
        out_specs=out_specs,
        out_shape=out_shape,
        scratch_shapes=scratch,
        compiler_params=_params(("arbitrary",)),
        name="in_proj",
    )(x, mod, mod, *consts)


def _s5_pitch(tl):
    return tl if tl == 1 else tl + SUBLANES


def _s5_kernel(*refs, nb, tl, sub_rows, side=None, states_t=False):
    nu = S5_PIECES
    u_refs = refs[:nu]
    h0r_ref, h0i_ref, b_ref, a_ref, c_ref, d_ref = refs[nu:nu + 6]
    y_refs = refs[nu + 6:2 * nu + 6]
    hr_ref, hi_ref, hst_ref, bu0_ref, xs0_ref, bu1_ref, xs1_ref, bm_ref, cm_ref = refs[2 * nu + 6:]
    bufs = ((bu0_ref, xs0_ref), (bu1_ref, xs1_ref))
    tb = pl.program_id(1)
    tsub = sub_rows // nb
    tw = 2 * LANES
    pitch = _s5_pitch(tl)

    def step_rows(t):
        return pl.ds(t, nb, stride=pitch) if pitch > 1 else pl.ds(0, nb)

    @pl.when(tb == 0)
    def _():
        _s5_expand_operands(b_ref, c_ref, bm_ref, cm_ref)
        for j in range(S5_TILES):
            js = slice(j * LANES, (j + 1) * LANES)
            hst_ref[:, j * tw:j * tw + LANES] = h0r_ref[js, :].T if states_t else h0r_ref[:, js]
            hst_ref[:, j * tw + LANES:(j + 1) * tw] = h0i_ref[js, :].T if states_t else h0i_ref[:, js]

    ar = [jnp.broadcast_to(a_ref[:, j * tw:j * tw + LANES], (nb, LANES)) for j in range(S5_TILES)]
    ai = [jnp.broadcast_to(a_ref[:, j * tw + LANES:(j + 1) * tw], (nb, LANES)) for j in range(S5_TILES)]
    dsk = d_ref[...]

    def project_in(t0, bu_ref, xs_ref):
        xs = jnp.concatenate([jnp.concatenate([r[step_rows(t0 + i), :] for r in u_refs], axis=1)
                              for i in range(tsub)], axis=0)
        xs_ref[...] = xs
        lhs = xs.astype(BF16)
        for j in range(S5_TILES):
            bu_ref[:, j * tw:(j + 1) * tw] = _dot(lhs, bm_ref[:, j * tw:(j + 1) * tw])

    def scan_project_out(t0, bu_ref, xs_ref, carry):
        new, hs = [], []
        for j in range(S5_TILES):
            hr, hi = carry[2 * j], carry[2 * j + 1]
            rows = []
            for i in range(tsub):
                br = bu_ref[i * nb:(i + 1) * nb, j * tw:j * tw + LANES]
                bi = bu_ref[i * nb:(i + 1) * nb, j * tw + LANES:(j + 1) * tw]
                hr, hi = ar[j] * hr - ai[j] * hi + br, ar[j] * hi + ai[j] * hr + bi
                rows.append(jnp.concatenate([hr, hi], axis=1))
            new += [hr, hi]
            hs.append(jnp.concatenate(rows, axis=0).astype(BF16))
        y = _dot(jnp.concatenate(hs, axis=1), cm_ref[...]) + dsk * xs_ref[...]
        for i in range(tsub):
            for k, y_ref in enumerate(y_refs):
                y_ref[step_rows(t0 + i), :] = y[i * nb:(i + 1) * nb, k * LANES:(k + 1) * LANES]
        return tuple(new)

    h_in = []
    for j in range(S5_TILES):
        h_in += [hst_ref[:, j * tw:j * tw + LANES], hst_ref[:, j * tw + LANES:(j + 1) * tw]]
    nsub = tl // tsub
    (bu0, xs0), (bu1, xs1) = bufs
    project_in(0, bu0, xs0)
    if nsub == 1:
        h_out = scan_project_out(0, bu0, xs0, tuple(h_in))
    else:
        def pair(p, carry):
            t0 = 2 * p * tsub
            project_in(t0 + tsub, bu1, xs1)
            carry = scan_project_out(t0, bu0, xs0, carry)
            project_in(jnp.minimum(t0 + 2 * tsub, tl - tsub), bu0, xs0)
            return scan_project_out(t0 + tsub, bu1, xs1, carry)
        h_out = lax.fori_loop(0, nsub // 2, pair, tuple(h_in))
    for j in range(S5_TILES):
        hst_ref[:, j * tw:j * tw + LANES] = h_out[2 * j]
        hst_ref[:, j * tw + LANES:(j + 1) * tw] = h_out[2 * j + 1]

    if pitch > tl:
        for y_ref in y_refs:
            for b in range(nb):
                y_ref[b * pitch + tl:(b + 1) * pitch, :] = jnp.zeros((pitch - tl, LANES), F32)

    @pl.when(tb == pl.num_programs(1) - 1)
    def _():
        for j in range(S5_TILES):
            js = slice(j * LANES, (j + 1) * LANES)
            if states_t:
                hr_ref[js, :] = h_out[2 * j].T
                hi_ref[js, :] = h_out[2 * j + 1].T
            else:
                hr_ref[:, js] = h_out[2 * j]
                hi_ref[:, js] = h_out[2 * j + 1]

    if side is not None:
        side()


def _s5(u, h0r, h0i, b, a, c, d, *, tl, states_t=False):
    nblk, rows, _ = u.shape
    pitch = _s5_pitch(tl)
    nb = rows // pitch
    sub_rows = min(S5_SUB_ROWS, tl * nb)
    nsub = tl * nb // sub_rows
    assert sub_rows % nb == 0 and tl * nb % sub_rows == 0 and (nsub == 1 or nsub % 2 == 0)
    chunk3 = lambda s: pl.BlockSpec((None,) + s, lambda c, t: (c, 0, 0))
    assert not states_t or nb == LANES
    st_spec = (pl.BlockSpec((S5_NS, nb), lambda c, t: (c, 0)) if states_t
               else pl.BlockSpec((nb, S5_NS), lambda c, t: (0, c)))
    nu = S5_PIECES
    u_specs = [pl.BlockSpec((None, rows, LANES), functools.partial(lambda c, t, k: (t, 0, nu * c + k), k=k))
               for k in range(nu)]
    y_spec = pl.BlockSpec((None, rows, LANES), lambda c, t: (t, 0, c))
    y_shape = jax.ShapeDtypeStruct((nblk, rows, D_MODEL // nu), F32)
    res = pl.pallas_call(
        functools.partial(_s5_kernel, nb=nb, tl=tl, sub_rows=sub_rows, states_t=states_t),
        grid=(S5_CHUNKS, nblk),
        in_specs=u_specs + [st_spec, st_spec, chunk3((S5_CH, 2 * S5_STATE)), chunk3((1, 2 * S5_NS)),
                            chunk3((2 * S5_NS, S5_GROUP)), chunk3((1, S5_CH))],
        out_specs=[y_spec] * nu + [st_spec, st_spec],
        out_shape=[y_shape] * nu + [jax.ShapeDtypeStruct(h0r.shape, F32), jax.ShapeDtypeStruct(h0i.shape, F32)],
        scratch_shapes=[pltpu.VMEM((nb, 2 * S5_NS), F32)]
                       + [pltpu.VMEM((sub_rows, 2 * S5_NS), F32), pltpu.VMEM((sub_rows, S5_CH), F32)] * 2
                       + [pltpu.VMEM((S5_CH, 2 * S5_NS), BF16), pltpu.VMEM((2 * S5_NS, S5_CH), BF16)],
        compiler_params=_params(("parallel", "arbitrary")),
        name="s5",
    )(*([u] * nu), h0r, h0i, b, a, c, d)
    return res[:nu], res[nu], res[nu + 1]


def _expand_mat():
    r = lax.broadcasted_iota(jnp.int32, (3 * LANES, M_HEADS * M_HEADDIM), 0) % LANES
    c = lax.broadcasted_iota(jnp.int32, (3 * LANES, M_HEADS * M_HEADDIM), 1) // M_HEADDIM
    return (r == c).astype(BF16)


LOG2E = 1.4426950408889634
SSD_CHUNKS_PER_STEP = 4


def _ssd_kernel(*refs, n_cast):
    z_ref, act_ref, dt_ref, alog_ref, dsk_ref, nw_ref, e64_ref = refs[:7]
    w_refs = refs[7:7 + n_cast]
    y_ref, hout_ref = refs[7 + n_cast:9 + n_cast]
    wb_refs = refs[9 + n_cast:9 + 2 * n_cast]
    ht_ref, xs_ref, xdte_ref, eacs_ref, yacc_ref = refs[9 + 2 * n_cast:]
    c = pl.program_id(1)
    for w_ref, wb_ref in zip(w_refs, wb_refs):
        wb_ref[...] = w_ref[...].astype(BF16)

    @pl.when(c == 0)
    def _():
        ht_ref[...] = jnp.zeros_like(ht_ref)

    for k in range(SSD_CHUNKS_PER_STEP):
        rows = pl.ds(k * M_CHUNK, M_CHUNK)
        _ssd_chunk(z_ref.at[rows], act_ref.at[rows], dt_ref.at[rows], alog_ref, dsk_ref, nw_ref, e64_ref,
                   y_ref.at[rows], ht_ref, xs_ref.at[k], xdte_ref.at[k], eacs_ref.at[k], yacc_ref.at[k])

    @pl.when(c == pl.num_programs(1) - 1)
    def _():
        for j in range(0, M_INNER, LANES):
            hout_ref[j:j + LANES, :] = ht_ref[:, j:j + LANES].T


def _ssd_chunk(z_ref, act_ref, dt_ref, alog_ref, dsk_ref, nw_ref, e64_ref,
               y_ref, ht_ref, xs_ref, xdte_ref, eacs_ref, yacc_ref):
    q = M_CHUNK
    cs = 512

    dt = dt_ref[...]
    a2 = dt * (-jnp.exp(alog_ref[...]) * LOG2E)
    ri = lax.broadcasted_iota(jnp.int32, (q, q), 0)
    ci = lax.broadcasted_iota(jnp.int32, (q, q), 1)
    causal = ri >= ci
    acs = jnp.dot(causal.astype(F32), a2, preferred_element_type=F32, precision=lax.Precision.HIGHEST)
    src_t = (acs - jnp.log2(dt)).T
    eacs2 = _split2(jnp.exp2(acs))
    dte2 = _split2(dt * jnp.exp2(acs[q - 1:q, :] - acs))
    for j in range(0, M_INNER, cs):
        sl = slice(j, j + cs)
        e = e64_ref[0:2 * LANES, sl]
        xs = act_ref[:, sl]
        xs_ref[:, sl] = xs.astype(BF16)
        xdte_ref[:, sl] = (xs * _dot(dte2, e)).astype(BF16)
        eacs_ref[:, sl] = _dot(eacs2, e)

    hp = M_HPG * M_HEADDIM
    lane = lax.broadcasted_iota(jnp.int32, (q, LANES), 1)
    first_head = lane < M_HEADDIM
    for g in range(M_GROUPS):
        gs = slice(g * hp, (g + 1) * hp)
        bm_t = act_ref[:, M_INNER + g * M_STATE:M_INNER + (g + 1) * M_STATE].T.astype(BF16)
        cm = act_ref[:, M_INNER + (M_GROUPS + g) * M_STATE:M_INNER + (M_GROUPS + g + 1) * M_STATE].astype(BF16)
        cbm = _dot(cm, bm_t)
        y_off = _dot(cm, ht_ref[:, gs].astype(BF16))
        st_t = _dot(bm_t, xdte_ref[:, gs])
        for pr in range(M_HPG // 2):
            h0 = g * M_HPG + 2 * pr
            ps = slice(h0 * M_HEADDIM, (h0 + 2) * M_HEADDIM)
            ms = []
            for h in (h0, h0 + 1):
                d2 = jnp.broadcast_to(acs[:, h:h + 1], (q, q)) - src_t[h:h + 1, :]
                ms.append((cbm * jnp.where(causal, jnp.exp2(d2), 0.0)).astype(BF16))
            xp = xs_ref[:, ps]
            zero = jnp.zeros_like(xp)
            rhs = jnp.concatenate([jnp.where(first_head, xp, zero), jnp.where(first_head, zero, xp)], axis=0)
            yd = _dot(jnp.concatenate(ms, axis=1), rhs)
            yacc_ref[:, ps] = yd + y_off[:, ps.start - gs.start:ps.stop - gs.start] * eacs_ref[:, ps] \
                + dsk_ref[:, ps] * act_ref[:, ps]
        ht_ref[:, gs] = eacs_ref[q - 1:q, gs] * ht_ref[:, gs] + st_t

    ss = jnp.zeros((q, 1), F32)
    for j in range(0, M_INNER, cs):
        sl = slice(j, j + cs)
        gt = yacc_ref[:, sl] * _silu(z_ref[:, sl])
        yacc_ref[:, sl] = gt
        ss = ss + jnp.sum(gt * gt, axis=-1, keepdims=True)
    inv = lax.rsqrt(ss * (1.0 / M_INNER) + EPS)
    for j in range(0, M_INNER, cs):
        sl = slice(j, j + cs)
        y_ref[:, sl] = yacc_ref[:, sl] * inv * nw_ref[:, sl]


def _cast_spec(w, nbatch, nc):
    rows, steps = w.shape[0], nbatch * nc
    n = next(n for n in range(steps, 0, -1) if rows % n == 0 and (rows // n) % (2 * SUBLANES) == 0)
    return pl.BlockSpec((rows // n, w.shape[1]), lambda b, c: (jnp.minimum(b * nc + c, n - 1), 0))


def _ssd(z, act, dt, alog, dsk, nw, e64, cast_ws, *, nbatch, seq):
    q = M_CHUNK
    k = SSD_CHUNKS_PER_STEP
    assert seq % (k * q) == 0
    nc = seq // (k * q)
    row = lambda n: pl.BlockSpec((k * q, n), lambda b, c: (b * nc + c, 0))
    cast_specs = [_cast_spec(w, nbatch, nc) for w in cast_ws]
    res = pl.pallas_call(
        functools.partial(_ssd_kernel, n_cast=len(cast_ws)),
        grid=(nbatch, nc),
        in_specs=[row(M_INNER), row(M_CONV_DIM), row(LANES), _const_spec(alog.shape),
                  _const_spec(dsk.shape), _const_spec(nw.shape), _const_spec(e64.shape)] + cast_specs,
        out_specs=[row(M_INNER), pl.BlockSpec((None, M_INNER, M_STATE), lambda b, c: (b, 0, 0))] + cast_specs,
        out_shape=[jax.ShapeDtypeStruct((nbatch * seq, M_INNER), F32),
                   jax.ShapeDtypeStruct((nbatch, M_INNER, M_STATE), F32)]
                  + [jax.ShapeDtypeStruct(w.shape, BF16) for w in cast_ws],
        scratch_shapes=[pltpu.VMEM((M_STATE, M_INNER), F32), pltpu.VMEM((k, q, M_INNER), BF16),
                        pltpu.VMEM((k, q, M_INNER), BF16), pltpu.VMEM((k, q, M_INNER), F32),
                        pltpu.VMEM((k, q, M_INNER), F32)],
        compiler_params=_params(("arbitrary", "arbitrary")),
        name="ssd",
    )(z, act, dt, alog, dsk, nw, e64, *cast_ws)
    return res[0], res[1], res[2:]


SSD_STEP_BB = 8


def _ssd_step_kernel(z_ref, xbc_ref, dt_ref, cbuf_ref, h0_ref, cw_ref, cb_ref, alog_ref, dsk_ref, nw_ref,
                     e64_ref, y_ref, hout_ref, convout_ref):
    bb = SSD_STEP_BB
    x_new = xbc_ref[...]
    conv = cb_ref[...] + x_new * cw_ref[M_CONV - 1:M_CONV, :]
    for k in range(M_CONV - 1):
        conv = conv + cbuf_ref[k] * cw_ref[k:k + 1, :]
    act = _silu(conv)
    for k in range(M_CONV - 2):
        convout_ref[k] = cbuf_ref[k + 1]
    convout_ref[M_CONV - 2] = x_new

    xs = act[:, :M_INNER]
    bmat = act[:, M_INNER:M_INNER + M_GROUPS * M_STATE]
    cmat = act[:, M_INNER + M_GROUPS * M_STATE:]
    dt = dt_ref[...]
    da = jnp.exp(dt * (-jnp.exp(alog_ref[...])))
    dt_e = _dot(_split3(dt), e64_ref[...])
    da_e = _dot(_split3(da), e64_ref[...])
    xdt = xs * dt_e

    hp = M_HPG * M_HEADDIM
    lane_group = lax.broadcasted_iota(jnp.int32, (M_GROUPS, M_INNER), 1) // hp
    row_group = lax.broadcasted_iota(jnp.int32, (M_GROUPS, M_INNER), 0)
    gmask = lane_group == row_group
    ones_rows = lax.broadcasted_iota(jnp.int32, (SUBLANES, M_STATE), 0) >= M_GROUPS

    cbx = []
    for g in range(M_GROUPS):
        cbg = jnp.sum(cmat[:, g * M_STATE:(g + 1) * M_STATE] * bmat[:, g * M_STATE:(g + 1) * M_STATE],
                      axis=-1, keepdims=True)
        cbx.append(jnp.broadcast_to(cbg, (bb, hp)))
    y = jnp.concatenate(cbx, axis=1) * xdt + dsk_ref[...] * xs

    y_off_rows = []
    for b in range(bb):
        xsel = jnp.where(gmask, jnp.broadcast_to(xdt[b:b + 1, :], (M_GROUPS, M_INNER)), 0.0)
        lhs = jnp.concatenate([xsel, *_pieces3(da_e[b:b + 1, :]), jnp.zeros((1, M_INNER), F32)],
                              axis=0).astype(BF16)
        bm4 = jnp.concatenate([bmat[b:b + 1, g * M_STATE:(g + 1) * M_STATE] for g in range(M_GROUPS)], axis=0)
        cm4 = jnp.concatenate([cmat[b:b + 1, g * M_STATE:(g + 1) * M_STATE] for g in range(M_GROUPS)], axis=0)
        zeros4 = jnp.zeros((SUBLANES - M_GROUPS, M_STATE), F32)
        rhs = jnp.concatenate([jnp.concatenate([bm4, zeros4], axis=0),
                               jnp.where(ones_rows, 1.0, 0.0)], axis=1).astype(BF16)
        res = lax.dot_general(lhs, rhs, _TN, preferred_element_type=F32)
        h0 = h0_ref[b]
        hout_ref[b] = res[:, M_STATE:] * h0 + res[:, :M_STATE]
        cm8 = jnp.concatenate([cm4, zeros4], axis=0).astype(BF16)
        yo = lax.dot_general(cm8, h0.astype(BF16), _NT, preferred_element_type=F32)
        y_off_rows.append(jnp.sum(jnp.where(gmask, yo[:M_GROUPS, :], 0.0), axis=0, keepdims=True))
    y = y + da_e * jnp.concatenate(y_off_rows, axis=0)

    y_ref[...] = _rms(y * _silu(z_ref[...]), nw_ref[...])


SSD_STEP_EVERY = 2


def _s5_ssd_step_kernel(*refs, nb, tl, sub_rows):
    n5i, nsi = S5_PIECES + 6, 11
    n5o, nso = S5_PIECES + 2, 3
    ins5, inss = refs[:n5i], refs[n5i:n5i + nsi]
    o = n5i + nsi
    outs5, outss = refs[o:o + n5o], refs[o + n5o:o + n5o + nso]
    sc5 = refs[o + n5o + nso:]
    s = pl.program_id(0) * pl.num_programs(1) + pl.program_id(1)

    def side():
        @pl.when(s % SSD_STEP_EVERY == 0)
        def _():
            _ssd_step_kernel(*inss, *outss)

    _s5_kernel(*ins5, *outs5, *sc5, nb=nb, tl=tl, sub_rows=sub_rows, side=side)


def _s5_ssd_step(u, h0r, h0i, b, a, c, d, zs, xbc, dt, cbuf, h0, cw, cb, alog, dsk, nw, e64, *, tl):
    nblk, rows, _ = u.shape
    pitch = _s5_pitch(tl)
    nb = rows // pitch
    sub_rows = min(S5_SUB_ROWS, tl * nb)
    nsub = tl * nb // sub_rows
    ns = zs.shape[0]
    bb, per = SSD_STEP_BB, SSD_STEP_EVERY
    assert sub_rows % nb == 0 and tl * nb % sub_rows == 0 and nsub % 2 == 0 and ns * per == bb * S5_CHUNKS * nblk
    chunk3 = lambda sh: pl.BlockSpec((None,) + sh, lambda c, t: (c, 0, 0))
    st_spec = pl.BlockSpec((nb, S5_NS), lambda c, t: (0, c))
    nu = S5_PIECES
    u_specs = [pl.BlockSpec((None, rows, LANES), functools.partial(lambda c, t, k: (t, 0, nu * c + k), k=k))
               for k in range(nu)]
    y5_spec = pl.BlockSpec((None, rows, LANES), lambda c, t: (t, 0, c))
    y5_shape = jax.ShapeDtypeStruct((nblk, rows, D_MODEL // nu), F32)
    step = lambda c, t: c * nblk + t
    row = lambda w: pl.BlockSpec((bb, w), lambda c, t: (step(c, t) // per, 0))
    conv_spec = pl.BlockSpec((M_CONV - 1, bb, M_CONV_DIM), lambda c, t: (0, step(c, t) // per, 0))
    st3_spec = pl.BlockSpec((bb, M_INNER, M_STATE), lambda c, t: (step(c, t) // per, 0, 0))
    res = pl.pallas_call(
        functools.partial(_s5_ssd_step_kernel, nb=nb, tl=tl, sub_rows=sub_rows),
        grid=(S5_CHUNKS, nblk),
        in_specs=u_specs + [st_spec, st_spec, chunk3((S5_CH, 2 * S5_STATE)), chunk3((1, 2 * S5_NS)),
                            chunk3((2 * S5_NS, S5_GROUP)), chunk3((1, S5_CH))]
                 + [row(M_INNER), row(M_CONV_DIM), row(LANES), conv_spec, st3_spec,
                    _const_spec(cw.shape), _const_spec(cb.shape), _const_spec(alog.shape),
                    _const_spec(dsk.shape), _const_spec(nw.shape), _const_spec(e64.shape)],
        out_specs=[y5_spec] * nu + [st_spec, st_spec] + [row(M_INNER), st3_spec, conv_spec],
        out_shape=[y5_shape] * nu + [jax.ShapeDtypeStruct(h0r.shape, F32), jax.ShapeDtypeStruct(h0i.shape, F32)]
                  + [jax.ShapeDtypeStruct((ns, M_INNER), F32), jax.ShapeDtypeStruct((ns, M_INNER, M_STATE), F32),
                     jax.ShapeDtypeStruct((M_CONV - 1, ns, M_CONV_DIM), F32)],
        scratch_shapes=[pltpu.VMEM((nb, 2 * S5_NS), F32)]
                       + [pltpu.VMEM((sub_rows, 2 * S5_NS), F32), pltpu.VMEM((sub_rows, S5_CH), F32)] * 2
                       + [pltpu.VMEM((S5_CH, 2 * S5_NS), BF16), pltpu.VMEM((2 * S5_NS, S5_CH), BF16)],
        compiler_params=_params(("arbitrary", "arbitrary")),
        name="s5_ssd_step",
    )(*([u] * nu), h0r, h0i, b, a, c, d, zs, xbc, dt, cbuf, h0, cw, cb, alog, dsk, nw, e64)
    return (res[:nu], res[nu], res[nu + 1]), (res[nu + 2], res[nu + 3], res[nu + 4])


MXU_WIDTH = 256
FFN_CHUNKS = (0, 6 * MXU_WIDTH, D_FF)


def _out_kernel(x_ref, *refs):
    y5_refs = refs[:S5_PIECES]
    (yb_ref, gate_ref, gtm_ref, shf_ref, scf_ref, gtf_ref, wglu_ref, bglu_ref, wb5_ref, wbs_ref, wout_ref,
     npm_ref, npf_ref, npo_ref, wfi_ref, wfo_ref, o_ref) = refs[S5_PIECES:]
    rows = x_ref.shape[0]
    y5 = jnp.concatenate([y5_refs[k][0:rows, c * LANES:(c + 1) * LANES]
                          for c in range(S5_CHUNKS) for k in range(S5_PIECES)], axis=1)
    ya = _gelu_tanh(y5)
    glu = ya * _sigmoid(_dot(ya.astype(BF16), wglu_ref[...]) + bglu_ref[...])
    merged = (_sigmoid(gate_ref[:, :D_MODEL]) * _dot(glu.astype(BF16), wb5_ref[...])
              + _sigmoid(gate_ref[:, D_MODEL:]) * _dot(yb_ref[...].astype(BF16), wbs_ref[...]))
    mix = _dot(merged.astype(BF16), wout_ref[...])
    x1 = x_ref[...] + gtm_ref[...] * _rms(mix, npm_ref[...])

    hb = (_rms(x1, npf_ref[...]) * (1.0 + scf_ref[...]) + shf_ref[...]).astype(BF16)
    f = jnp.zeros_like(x1)
    for j, e in zip(FFN_CHUNKS[:-1], FFN_CHUNKS[1:]):
        gg = _dot(hb, wfi_ref[:, j:e])
        uu = _dot(hb, wfi_ref[:, D_FF + j:D_FF + e])
        f = f + _dot((_silu(gg) * uu).astype(BF16), wfo_ref[j:e, :])
    o_ref[...] = x1 + gtf_ref[...] * _rms(f, npo_ref[...])


def _out_stage(x, y5, yb, gates, mod, ws, *, per_row, tm, rows_per_batch):
    t = x.shape[0]
    row = lambda n: pl.BlockSpec((tm, n), lambda i: (i, 0))
    mods = _mod_specs(per_row, tm, rows_per_batch // tm if not per_row else 1, (2, 3, 4, 5))
    _, y5_spec = _s5_layout(per_row, t, tm, rows_per_batch, D_MODEL // S5_PIECES)
    return pl.pallas_call(
        _out_kernel,
        grid=(t // tm,),
        in_specs=[row(D_MODEL)] + [y5_spec] * S5_PIECES + [row(M_INNER), row(2 * D_MODEL)] + mods
                 + [_const_spec(w.shape) for w in ws],
        out_specs=row(D_MODEL),
        out_shape=jax.ShapeDtypeStruct((t, D_MODEL), F32),
        compiler_params=_params(("parallel",)),
        name="out_stage",
    )(x, *y5, yb, gates, mod, mod, mod, mod, *ws)


def _pad_lanes(v):
    return jnp.pad(v.reshape(1, -1), ((0, 0), (0, LANES - v.shape[-1])))


def kernel(x_prompt, x_sample, state_s5_re, state_s5_im, state_ssm, state_conv, c_prompt, c_sample, w_ada, b_ada, norm_pre_mix, norm_post_mix, norm_pre_ffn, norm_post_ffn, w_in, s5_lam_re, s5_lam_im, s5_log_dt, s5_b_re, s5_b_im, s5_c_re, s5_c_im, s5_d, s5_w_glu, s5_b_glu, m_conv_w, m_conv_b, m_dt_bias, m_a_log, m_d, m_norm, w_branch_s5, w_branch_ssd, w_out, w_ffn_in, w_ffn_out):
    bp, seq, _ = x_prompt.shape
    bs = x_sample.shape[0]
    assert x_sample.shape[1] == 1 and w_ada.shape[0] == 1 and seq % M_CHUNK == 0
    row1 = lambda v: v.reshape(1, -1)

    ws_in = _split_w_in(w_in[0])
    cw, cb = m_conv_w[0], row1(m_conv_b[0])
    dtb, alog = _pad_lanes(m_dt_bias[0]), _pad_lanes(m_a_log[0])
    dsk = row1(jnp.repeat(m_d[0], M_HEADDIM))
    nw = row1(m_norm[0])
    e64 = _expand_mat()

    mod_p, mod_s = _ada(c_prompt, c_sample, w_ada[0], b_ada[0])

    ar, ai, bbr, bbi = _s5_disc(s5_lam_re[0], s5_lam_im[0], s5_log_dt[0], s5_b_re[0], s5_b_im[0])
    s5m = _s5_matrices(ar, ai, bbr, bbi, s5_c_re[0], s5_c_im[0], s5_d[0])
    g_pre = row1(norm_pre_mix[0])

    xp = x_prompt.reshape(bp * seq, D_MODEL)
    xs = x_sample.reshape(bs, D_MODEL)
    tm = min(ROW_TILE, seq)
    u, zs, act, dt, gates, tail = _in_proj(xp, mod_p, g_pre, ws_in, dtb, cw, cb,
                                           per_row=False, tm=tm, rows_per_batch=seq)
    u_s, zs_s, xbc_s, dt_s, gates_s = _in_proj(xs, mod_s, g_pre, ws_in, dtb, cw, cb,
                                               per_row=True, tm=bs, rows_per_batch=1)

    p_conv = tail[:, SUBLANES - (M_CONV - 1):, :]
    zeros_s5 = jnp.zeros((bp, S5_GROUPS * S5_STATE), F32)
    (y5, p_re, p_im), (yb_s, s_ssm, s_conv) = _s5_ssd_step(
        u, zeros_s5, zeros_s5, *s5m, zs_s, xbc_s, dt_s, jnp.swapaxes(state_conv[0], 0, 1),
        state_ssm[0].reshape(bs, M_INNER, M_STATE), cw, cb, alog, dsk, nw, e64, tl=tm)
    yb, p_ssm, (wglu, wb5, wbs, wo, wfi, wfo) = _ssd(
        zs, act, dt, alog, dsk, nw, e64,
        (s5_w_glu[0], w_branch_s5[0], w_branch_ssd[0], w_out[0], w_ffn_in[0], w_ffn_out[0]), nbatch=bp, seq=seq)
    ws_out = (wglu, row1(s5_b_glu[0]), wb5, wbs, wo, row1(norm_post_mix[0]),
              row1(norm_pre_ffn[0]), row1(norm_post_ffn[0]), wfi, wfo)
    y_prompt = _out_stage(xp, y5, yb, gates, mod_p, ws_out, per_row=False, tm=tm, rows_per_batch=seq)

    states_t = bs == LANES
    to_kernel = lambda v: (jnp.transpose(v, (1, 2, 0)).reshape(-1, bs) if states_t else v.reshape(bs, -1))
    from_kernel = lambda v: (jnp.transpose(v.reshape(S5_GROUPS, S5_STATE, bs), (2, 0, 1)) if states_t
                             else v.reshape(bs, S5_GROUPS, S5_STATE))[None]
    y5_s, s_re, s_im = _s5(u_s, to_kernel(state_s5_re[0]), to_kernel(state_s5_im[0]), *s5m, tl=1,
                           states_t=states_t)
    y_sample = _out_stage(xs, y5_s, yb_s, gates_s, mod_s, ws_out, per_row=True, tm=bs, rows_per_batch=1)

    s5_shape = (1, -1, S5_GROUPS, S5_STATE)
    ssm_shape = (1, -1, M_HEADS, M_HEADDIM, M_STATE)
    return (y_prompt.reshape(bp, seq, D_MODEL), y_sample.reshape(bs, 1, D_MODEL),
            p_re.reshape(s5_shape), p_im.reshape(s5_shape), p_ssm.reshape(ssm_shape), p_conv[None],
            from_kernel(s_re), from_kernel(s_im), s_ssm.reshape(ssm_shape), jnp.swapaxes(s_conv, 0, 1)[None])
```

```python
import functools

import jax
import jax.numpy as jnp
from jax import lax
from jax.experimental import pallas as pl
from jax.experimental.pallas import tpu as pltpu

F32 = jnp.float32
BF16 = jnp.bfloat16

D_MODEL = 1024
EPS = 1e-6
S5_GROUP = 16
S5_GROUPS = 64
S5_STATE = 64
M_INNER = 2048
M_HEADDIM = 64
M_HEADS = 32
M_GROUPS = 4
M_HPG = 8
M_STATE = 128
M_CONV = 4
M_CONV_DIM = 3072
M_CHUNK = 128
D_FF = 2816
OFF_Z = D_MODEL
OFF_XBC = OFF_Z + M_INNER
OFF_DT = OFF_XBC + M_CONV_DIM
OFF_GA = OFF_DT + M_HEADS
IN_COLS = OFF_GA + 2 * D_MODEL

LANES = 128
SUBLANES = 8
VMEM_LIMIT = 56 * 1024 * 1024

S5_CH = 256
S5_NS = 1024
S5_CHUNKS = D_MODEL // S5_CH
S5_TILES = S5_NS // LANES
S5_PIECES = S5_CH // LANES
S5_SUB_ROWS = 256
ROW_TILE = 256
INPROJ_SLAB = 512
INPROJ_DEPTH = 3

_NT = (((1,), (1,)), ((), ()))
_TN = (((0,), (0,)), ((), ()))


def _const_spec(shape):
    nd = len(shape)
    return pl.BlockSpec(shape, lambda *_: (0,) * nd, pipeline_mode=pl.Buffered(1))


def _params(sem):
    return pltpu.CompilerParams(dimension_semantics=sem, vmem_limit_bytes=VMEM_LIMIT)


def _sigmoid(x):
    return 0.5 * jnp.tanh(0.5 * x) + 0.5


def _silu(x):
    hx = 0.5 * x
    return hx + hx * jnp.tanh(hx)


def _softplus(x):
    return jnp.maximum(x, 0.0) + jnp.log1p(jnp.exp(-jnp.abs(x)))


def _gelu_tanh(x):
    return 0.5 * x * (1.0 + jnp.tanh(0.7978845608028654 * (x + 0.044715 * (x * x * x))))


def _rms(x, g):
    return x * lax.rsqrt(jnp.mean(x * x, axis=-1, keepdims=True) + EPS) * g


def _dot(a, b):
    return jnp.dot(a, b, preferred_element_type=F32)


def _pieces3(x):
    hi = x.astype(BF16).astype(F32)
    r1 = x - hi
    mid = r1.astype(BF16).astype(F32)
    lo = (r1 - mid).astype(BF16).astype(F32)
    return hi, mid, lo


def _ordered_after(x, dep):
    zero = jnp.where(dep[:, 0:LANES] > jnp.inf, 1.0, 0.0)
    return x + jnp.concatenate([zero] * (x.shape[1] // LANES), axis=1)


def _split2(x):
    return jnp.concatenate(_pieces3(x)[:2], axis=-1).astype(BF16)


def _split3(x):
    return jnp.concatenate(_pieces3(x), axis=-1).astype(BF16)


def _ada_kernel(cp_ref, cs_ref, w_ref, b_ref, op_ref, os_ref):
    w = w_ref[...].astype(BF16)
    op_ref[:, 0, :] = _dot(_silu(cp_ref[...]).astype(BF16), w) + b_ref[...]
    os_ref[...] = _dot(_silu(cs_ref[...]).astype(BF16), w) + b_ref[...]


def _ada(c_prompt, c_sample, w, b):
    bp, bs = c_prompt.shape[0], c_sample.shape[0]
    tn = 1024
    return pl.pallas_call(
        _ada_kernel,
        grid=(6 * D_MODEL // tn,),
        in_specs=[_const_spec((bp, D_MODEL)), _const_spec((bs, D_MODEL)),
                  pl.BlockSpec((D_MODEL, tn), lambda j: (0, j)),
                  pl.BlockSpec((1, tn), lambda j: (0, j))],
        out_specs=[pl.BlockSpec((bp, 1, tn), lambda j: (0, 0, j)), pl.BlockSpec((bs, tn), lambda j: (0, j))],
        out_shape=[jax.ShapeDtypeStruct((bp, 1, 6 * D_MODEL), F32), jax.ShapeDtypeStruct((bs, 6 * D_MODEL), F32)],
        compiler_params=_params(("arbitrary",)),
        name="ada",
    )(c_prompt, c_sample, w, b.reshape(1, -1))


def _s5_disc_kernel(lr_ref, li_ref, ldt_ref, br_ref, bi_ref, ar_ref, ai_ref, bbr_ref, bbi_ref):
    lr = lr_ref[...]
    li = li_ref[...]
    dt = jnp.exp(ldt_ref[...])
    mag = jnp.exp(lr * dt)
    ar = mag * jnp.cos(li * dt)
    ai = mag * jnp.sin(li * dt)
    den = lr * lr + li * li
    nr = ar - 1.0
    f_re = (nr * lr + ai * li) / den
    f_im = (ai * lr - nr * li) / den
    br = br_ref[...]
    bi = bi_ref[...]
    ar_ref[...] = ar
    ai_ref[...] = ai
    bbr_ref[...] = f_re * br - f_im * bi
    bbi_ref[...] = f_re * bi + f_im * br


def _s5_disc(lam_re, lam_im, log_dt, b_re, b_im):
    g, n = lam_re.shape
    w = S5_GROUP * n
    tile = lambda a: jnp.tile(a, (1, S5_GROUP))
    ldt = jnp.broadcast_to(log_dt[:, None], (g, w))
    brt = jnp.swapaxes(b_re, 1, 2).reshape(g, w)
    bit = jnp.swapaxes(b_im, 1, 2).reshape(g, w)
    shp = jax.ShapeDtypeStruct((g, w), F32)
    ar, ai, bbr, bbi = pl.pallas_call(
        _s5_disc_kernel,
        out_shape=(shp, shp, shp, shp),
        name="s5_disc",
    )(tile(lam_re), tile(lam_im), ldt, brt, bit)
    return (ar[:, :n], ai[:, :n], bbr.reshape(g, S5_GROUP, n), bbi.reshape(g, S5_GROUP, n))


def _s5_matrices(abar_re, abar_im, bbar_re, bbar_im, c_re, c_im, d_skip):
    ns = 2 * S5_NS
    b = jnp.concatenate([bbar_re.reshape(D_MODEL, S5_STATE), bbar_im.reshape(D_MODEL, S5_STATE)],
                        axis=1).reshape(S5_CHUNKS, S5_CH, 2 * S5_STATE)

    def by_tile(c):
        return jnp.swapaxes(c, 1, 2).reshape(S5_CHUNKS, S5_TILES, 1, 2, S5_STATE, S5_GROUP)
    c = jnp.concatenate([by_tile(c_re), by_tile(-c_im)], axis=2).reshape(S5_CHUNKS, ns, S5_GROUP)

    by_tile_a = lambda v: v.reshape(S5_CHUNKS, S5_TILES, 1, LANES)
    a = jnp.concatenate([by_tile_a(abar_re), by_tile_a(abar_im)], axis=2).reshape(S5_CHUNKS, 1, ns)
    d = d_skip.reshape(S5_CHUNKS, 1, S5_CH)
    return b, a, c, d


def _s5_state_group(i):
    return 2 * (i // (2 * LANES)) + (i % LANES) // S5_STATE


def _s5_expand_operands(b_ref, c_ref, bm_ref, cm_ref):
    tw = 2 * LANES
    row_g = lax.broadcasted_iota(jnp.int32, (S5_CH, LANES), 0) // S5_GROUP
    lane_half = lax.broadcasted_iota(jnp.int32, (S5_CH, LANES), 1) // S5_STATE
    for part in range(2):
        bp = b_ref[:, part * S5_STATE:(part + 1) * S5_STATE]
        pattern = jnp.concatenate([bp, bp], axis=1)
        for j in range(S5_TILES):
            tile = jnp.where(row_g == 2 * j + lane_half, pattern, 0.0)
            bm_ref[:, j * tw + part * LANES:j * tw + (part + 1) * LANES] = tile.astype(BF16)
    spread = (lax.broadcasted_iota(jnp.int32, (S5_GROUP, S5_CH), 1) % S5_GROUP
              == lax.broadcasted_iota(jnp.int32, (S5_GROUP, S5_CH), 0)).astype(BF16)
    rows = 512
    for r0 in range(0, 2 * S5_NS, rows):
        tiled = _dot(c_ref[r0:r0 + rows, :].astype(BF16), spread)
        own = (_s5_state_group(lax.broadcasted_iota(jnp.int32, (rows, S5_CH), 0) + r0)
               == lax.broadcasted_iota(jnp.int32, (rows, S5_CH), 1) // S5_GROUP)
        cm_ref[r0:r0 + rows, :] = jnp.where(own, tiled, 0.0).astype(BF16)


def _split_w_in_kernel(wt_ref, wm_ref, wdt_ref, wg_ref):
    tk = wt_ref.shape[1]
    step = 1024
    for j in range(0, OFF_DT, step):
        wm_ref[:, j:j + step] = wt_ref[j:j + step, :].T.astype(BF16)
    head = lax.broadcasted_iota(jnp.int32, (tk, LANES), 1) < M_HEADS
    wdt_ref[...] = jnp.where(head, wt_ref[OFF_DT:OFF_DT + LANES, :].T, 0.0).astype(BF16)
    for j in range(0, 2 * D_MODEL, step):
        wg_ref[:, j:j + step] = wt_ref[OFF_GA + j:OFF_GA + j + step, :].T.astype(BF16)


def _split_w_in(w):
    k = w.shape[0]
    tk = 256
    outs = (OFF_DT, LANES, 2 * D_MODEL)
    return pl.pallas_call(
        _split_w_in_kernel,
        grid=(k // tk,),
        in_specs=[pl.BlockSpec((IN_COLS, tk), lambda i: (0, i))],
        out_specs=[pl.BlockSpec((tk, n), lambda i: (i, 0)) for n in outs],
        out_shape=[jax.ShapeDtypeStruct((k, n), BF16) for n in outs],
        compiler_params=_params(("parallel",)),
        name="split_w_in",
    )(jnp.swapaxes(w, 0, 1))


def _inproj_kernel(x_ref, sh_ref, sc_ref, g_ref, wm_ref, wdt_ref, wg_ref, dtb_ref, cw_ref, cb_ref,
                   u_ref, z_ref, xbc_ref, dt_ref, gate_ref, *rest, tiles_per_seq):
    accs = rest[-INPROJ_DEPTH:]
    h = _rms(x_ref[...], g_ref[...]) * (1.0 + sc_ref[...]) + sh_ref[...]
    hb = h.astype(BF16)
    rows = hb.shape[0]
    step = INPROJ_SLAB

    if u_ref.shape[0] > rows:
        u_ref[rows:, :] = jnp.zeros((u_ref.shape[0] - rows, D_MODEL), F32)

    if tiles_per_seq is None:
        conv_fn = None
    else:
        tail_ref, carry_ref = rest[:2]

        @pl.when(pl.program_id(0) % tiles_per_seq == 0)
        def _():
            carry_ref[...] = jnp.zeros_like(carry_ref)

        def conv_fn(r, sl):
            width = sl.stop - sl.start
            w = [cw_ref[k:k + 1, sl] for k in range(M_CONV)]
            bias = cb_ref[:, sl]
            row = lax.broadcasted_iota(jnp.int32, (SUBLANES, width), 0)
            xp = carry_ref[:, sl]
            farp = xp * w[1] + pltpu.roll(xp, 1, axis=0) * w[0]
            outs = []
            for i in range(rows // SUBLANES):
                xi = r[i * SUBLANES:(i + 1) * SUBLANES, :]
                if outs:
                    xi = _ordered_after(xi, outs[-1])
                s1 = jnp.where(row == 0, pltpu.roll(xp, 1, axis=0), pltpu.roll(xi, 1, axis=0))
                far = xi * w[1] + s1 * w[0]
                s2 = jnp.where(row < 2, pltpu.roll(farp, 2, axis=0), pltpu.roll(far, 2, axis=0))
                outs.append(_silu(xi * w[3] + s1 * w[2] + s2 + bias))
                xp, farp = xi, far
            carry_ref[:, sl] = r[rows - SUBLANES:, :]
            tail_ref[:, sl] = r[rows - SUBLANES:, :]
            return jnp.concatenate(outs, axis=0)

    def slabs(o_ref, w_ref, w_off, fn):
        n = o_ref.shape[-1]
        return [(o_ref, slice(j, j + min(step, n - j)), w_ref, w_off + j, fn) for j in range(0, n, step)]

    su = slabs(u_ref, wm_ref, 0, None)
    sz = slabs(z_ref, wm_ref, OFF_Z, None)
    sx = slabs(xbc_ref, wm_ref, OFF_XBC, conv_fn)
    sd = slabs(dt_ref, wdt_ref, 0, lambda r, sl: _softplus(r + dtb_ref[:, sl]))
    sg = slabs(gate_ref, wg_ref, 0, None)
    light = su + sz + sg + sd
    order = []
    while sx or light:
        if sx:
            order.append(sx.pop(0))
        if light:
            order.append(light.pop(0))

    def matmul(s):
        o_ref, sl, w_ref, w_col, _ = order[s]
        accs[s % INPROJ_DEPTH][:, 0:sl.stop - sl.start] = _dot(hb, w_ref[:, w_col:w_col + sl.stop - sl.start])

    for s in range(min(INPROJ_DEPTH - 1, len(order))):
        matmul(s)
    for s, (o_ref, sl, _, _, fn) in enumerate(order):
        if s + INPROJ_DEPTH - 1 < len(order):
            matmul(s + INPROJ_DEPTH - 1)
        r = accs[s % INPROJ_DEPTH][:, 0:sl.stop - sl.start]
        o_ref[0:rows, sl] = r if fn is None else fn(r, sl)


def _mod_specs(per_row, tm, rows_per_mod, cols):
    if per_row:
        return [pl.BlockSpec((tm, D_MODEL), functools.partial(lambda i, c: (i, c), c=c)) for c in cols]
    return [pl.BlockSpec((None, 1, D_MODEL), functools.partial(lambda i, c: (i // rows_per_mod, 0, c), c=c))
            for c in cols]


def _s5_layout(per_row, t, tm, rows_per_batch, width):
    if per_row:
        return (1, t, width), pl.BlockSpec((None, tm, width), lambda i: (0, i, 0))
    nblk, pitch = rows_per_batch // tm, _s5_pitch(tm)
    return ((nblk, (t // rows_per_batch) * pitch, width),
            pl.BlockSpec((None, pitch, width), lambda i: (i % nblk, i // nblk, 0)))


def _in_proj(x, mod, g, ws, dtb, cw, cb, *, per_row, tm, rows_per_batch):
    t = x.shape[0]
    row = lambda n: pl.BlockSpec((tm, n), lambda i: (i, 0))
    outs = (M_INNER, M_CONV_DIM, LANES, 2 * D_MODEL)
    u_shape, u_spec = _s5_layout(per_row, t, tm, rows_per_batch, D_MODEL)
    out_specs = [u_spec] + [row(n) for n in outs]
    out_shape = [jax.ShapeDtypeStruct(u_shape, F32)] + [jax.ShapeDtypeStruct((t, n), F32) for n in outs]
    scratch = []
    tiles_per_seq = None
    if not per_row:
        tiles_per_seq = rows_per_batch // tm
        out_specs.append(pl.BlockSpec((None, SUBLANES, M_CONV_DIM), lambda i: (i // tiles_per_seq, 0, 0)))
        out_shape.append(jax.ShapeDtypeStruct((t // rows_per_batch, SUBLANES, M_CONV_DIM), F32))
        scratch.append(pltpu.VMEM((SUBLANES, M_CONV_DIM), F32))
    scratch += [pltpu.VMEM((tm, INPROJ_SLAB), F32)] * INPROJ_DEPTH
    consts = (g,) + tuple(ws) + (dtb, cw, cb)
    return pl.pallas_call(
        functools.partial(_inproj_kernel, tiles_per_seq=tiles_per_seq),
        grid=(t // tm,),
        in_specs=[row(D_MODEL)] + _mod_specs(per_row, tm, rows_per_batch // tm if not per_row else 1, (0, 1))
                 + [_const_spec(w.shape) for w in consts],
        out_specs=out_specs,
        out_shape=out_shape,
        scratch_shapes=scratch,
        compiler_params=_params(("arbitrary",)),
        name="in_proj",
    )(x, mod, mod, *consts)


def _s5_pitch(tl):
    return tl if tl == 1 else tl + SUBLANES


def _s5_kernel(*refs, nb, tl, sub_rows, side=None, states_t=False):
    nu = S5_PIECES
    u_refs = refs[:nu]
    h0r_ref, h0i_ref, b_ref, a_ref, c_ref, d_ref = refs[nu:nu + 6]
    y_refs = refs[nu + 6:2 * nu + 6]
    hr_ref, hi_ref, hst_ref, bu0_ref, xs0_ref, bu1_ref, xs1_ref, bm_ref, cm_ref = refs[2 * nu + 6:]
    bufs = ((bu0_ref, xs0_ref), (bu1_ref, xs1_ref))
    tb = pl.program_id(1)
    tsub = sub_rows // nb
    tw = 2 * LANES
    pitch = _s5_pitch(tl)

    def step_rows(t):
        return pl.ds(t, nb, stride=pitch) if pitch > 1 else pl.ds(0, nb)

    @pl.when(tb == 0)
    def _():
        _s5_expand_operands(b_ref, c_ref, bm_ref, cm_ref)
        for j in range(S5_TILES):
            js = slice(j * LANES, (j + 1) * LANES)
            hst_ref[:, j * tw:j * tw + LANES] = h0r_ref[js, :].T if states_t else h0r_ref[:, js]
            hst_ref[:, j * tw + LANES:(j + 1) * tw] = h0i_ref[js, :].T if states_t else h0i_ref[:, js]

    ar = [jnp.broadcast_to(a_ref[:, j * tw:j * tw + LANES], (nb, LANES)) for j in range(S5_TILES)]
    ai = [jnp.broadcast_to(a_ref[:, j * tw + LANES:(j + 1) * tw], (nb, LANES)) for j in range(S5_TILES)]
    dsk = d_ref[...]

    def project_in(t0, bu_ref, xs_ref):
        xs = jnp.concatenate([jnp.concatenate([r[step_rows(t0 + i), :] for r in u_refs], axis=1)
                              for i in range(tsub)], axis=0)
        xs_ref[...] = xs
        lhs = xs.astype(BF16)
        for j in range(S5_TILES):
            bu_ref[:, j * tw:(j + 1) * tw] = _dot(lhs, bm_ref[:, j * tw:(j + 1) * tw])

    def scan_project_out(t0, bu_ref, xs_ref, carry):
        new, hs = [], []
        for j in range(S5_TILES):
            hr, hi = carry[2 * j], carry[2 * j + 1]
            rows = []
            for i in range(tsub):
                br = bu_ref[i * nb:(i + 1) * nb, j * tw:j * tw + LANES]
                bi = bu_ref[i * nb:(i + 1) * nb, j * tw + LANES:(j + 1) * tw]
                hr, hi = ar[j] * hr - ai[j] * hi + br, ar[j] * hi + ai[j] * hr + bi
                rows.append(jnp.concatenate([hr, hi], axis=1))
            new += [hr, hi]
            hs.append(jnp.concatenate(rows, axis=0).astype(BF16))
        y = _dot(jnp.concatenate(hs, axis=1), cm_ref[...]) + dsk * xs_ref[...]
        for i in range(tsub):
            for k, y_ref in enumerate(y_refs):
                y_ref[step_rows(t0 + i), :] = y[i * nb:(i + 1) * nb, k * LANES:(k + 1) * LANES]
        return tuple(new)

    h_in = []
    for j in range(S5_TILES):
        h_in += [hst_ref[:, j * tw:j * tw + LANES], hst_ref[:, j * tw + LANES:(j + 1) * tw]]
    nsub = tl // tsub
    (bu0, xs0), (bu1, xs1) = bufs
    project_in(0, bu0, xs0)
    if nsub == 1:
        h_out = scan_project_out(0, bu0, xs0, tuple(h_in))
    else:
        def pair(p, carry):
            t0 = 2 * p * tsub
            project_in(t0 + tsub, bu1, xs1)
            carry = scan_project_out(t0, bu0, xs0, carry)
            project_in(jnp.minimum(t0 + 2 * tsub, tl - tsub), bu0, xs0)
            return scan_project_out(t0 + tsub, bu1, xs1, carry)
        h_out = lax.fori_loop(0, nsub // 2, pair, tuple(h_in))
    for j in range(S5_TILES):
        hst_ref[:, j * tw:j * tw + LANES] = h_out[2 * j]
        hst_ref[:, j * tw + LANES:(j + 1) * tw] = h_out[2 * j + 1]

    if pitch > tl:
        for y_ref in y_refs:
            for b in range(nb):
                y_ref[b * pitch + tl:(b + 1) * pitch, :] = jnp.zeros((pitch - tl, LANES), F32)

    @pl.when(tb == pl.num_programs(1) - 1)
    def _():
        for j in range(S5_TILES):
            js = slice(j * LANES, (j + 1) * LANES)
            if states_t:
                hr_ref[js, :] = h_out[2 * j].T
                hi_ref[js, :] = h_out[2 * j + 1].T
            else:
                hr_ref[:, js] = h_out[2 * j]
                hi_ref[:, js] = h_out[2 * j + 1]

    if side is not None:
        side()


def _s5(u, h0r, h0i, b, a, c, d, *, tl, states_t=False):
    nblk, rows, _ = u.shape
    pitch = _s5_pitch(tl)
    nb = rows // pitch
    sub_rows = min(S5_SUB_ROWS, tl * nb)
    nsub = tl * nb // sub_rows
    assert sub_rows % nb == 0 and tl * nb % sub_rows == 0 and (nsub == 1 or nsub % 2 == 0)
    chunk3 = lambda s: pl.BlockSpec((None,) + s, lambda c, t: (c, 0, 0))
    assert not states_t or nb == LANES
    st_spec = (pl.BlockSpec((S5_NS, nb), lambda c, t: (c, 0)) if states_t
               else pl.BlockSpec((nb, S5_NS), lambda c, t: (0, c)))
    nu = S5_PIECES
    u_specs = [pl.BlockSpec((None, rows, LANES), functools.partial(lambda c, t, k: (t, 0, nu * c + k), k=k))
               for k in range(nu)]
    y_spec = pl.BlockSpec((None, rows, LANES), lambda c, t: (t, 0, c))
    y_shape = jax.ShapeDtypeStruct((nblk, rows, D_MODEL // nu), F32)
    res = pl.pallas_call(
        functools.partial(_s5_kernel, nb=nb, tl=tl, sub_rows=sub_rows, states_t=states_t),
        grid=(S5_CHUNKS, nblk),
        in_specs=u_specs + [st_spec, st_spec, chunk3((S5_CH, 2 * S5_STATE)), chunk3((1, 2 * S5_NS)),
                            chunk3((2 * S5_NS, S5_GROUP)), chunk3((1, S5_CH))],
        out_specs=[y_spec] * nu + [st_spec, st_spec],
        out_shape=[y_shape] * nu + [jax.ShapeDtypeStruct(h0r.shape, F32), jax.ShapeDtypeStruct(h0i.shape, F32)],
        scratch_shapes=[pltpu.VMEM((nb, 2 * S5_NS), F32)]
                       + [pltpu.VMEM((sub_rows, 2 * S5_NS), F32), pltpu.VMEM((sub_rows, S5_CH), F32)] * 2
                       + [pltpu.VMEM((S5_CH, 2 * S5_NS), BF16), pltpu.VMEM((2 * S5_NS, S5_CH), BF16)],
        compiler_params=_params(("parallel", "arbitrary")),
        name="s5",
    )(*([u] * nu), h0r, h0i, b, a, c, d)
    return res[:nu], res[nu], res[nu + 1]


def _expand_mat():
    r = lax.broadcasted_iota(jnp.int32, (3 * LANES, M_HEADS * M_HEADDIM), 0) % LANES
    c = lax.broadcasted_iota(jnp.int32, (3 * LANES, M_HEADS * M_HEADDIM), 1) // M_HEADDIM
    return (r == c).astype(BF16)


LOG2E = 1.4426950408889634
SSD_CHUNKS_PER_STEP = 4


def _ssd_kernel(*refs, n_cast):
    z_ref, act_ref, dt_ref, alog_ref, dsk_ref, nw_ref, e64_ref = refs[:7]
    w_refs = refs[7:7 + n_cast]
    y_ref, hout_ref = refs[7 + n_cast:9 + n_cast]
    wb_refs = refs[9 + n_cast:9 + 2 * n_cast]
    ht_ref, xs_ref, xdte_ref, eacs_ref, yacc_ref = refs[9 + 2 * n_cast:]
    c = pl.program_id(1)
    for w_ref, wb_ref in zip(w_refs, wb_refs):
        wb_ref[...] = w_ref[...].astype(BF16)

    @pl.when(c == 0)
    def _():
        ht_ref[...] = jnp.zeros_like(ht_ref)

    for k in range(SSD_CHUNKS_PER_STEP):
        rows = pl.ds(k * M_CHUNK, M_CHUNK)
        _ssd_chunk(z_ref.at[rows], act_ref.at[rows], dt_ref.at[rows], alog_ref, dsk_ref, nw_ref, e64_ref,
                   y_ref.at[rows], ht_ref, xs_ref.at[k], xdte_ref.at[k], eacs_ref.at[k], yacc_ref.at[k])

    @pl.when(c == pl.num_programs(1) - 1)
    def _():
        for j in range(0, M_INNER, LANES):
            hout_ref[j:j + LANES, :] = ht_ref[:, j:j + LANES].T


def _ssd_chunk(z_ref, act_ref, dt_ref, alog_ref, dsk_ref, nw_ref, e64_ref,
               y_ref, ht_ref, xs_ref, xdte_ref, eacs_ref, yacc_ref):
    q = M_CHUNK
    cs = 512

    dt = dt_ref[...]
    a2 = dt * (-jnp.exp(alog_ref[...]) * LOG2E)
    ri = lax.broadcasted_iota(jnp.int32, (q, q), 0)
    ci = lax.broadcasted_iota(jnp.int32, (q, q), 1)
    causal = ri >= ci
    acs = jnp.dot(causal.astype(F32), a2, preferred_element_type=F32, precision=lax.Precision.HIGHEST)
    src_t = (acs - jnp.log2(dt)).T
    eacs2 = _split2(jnp.exp2(acs))
    dte2 = _split2(dt * jnp.exp2(acs[q - 1:q, :] - acs))
    for j in range(0, M_INNER, cs):
        sl = slice(j, j + cs)
        e = e64_ref[0:2 * LANES, sl]
        xs = act_ref[:, sl]
        xs_ref[:, sl] = xs.astype(BF16)
        xdte_ref[:, sl] = (xs * _dot(dte2, e)).astype(BF16)
        eacs_ref[:, sl] = _dot(eacs2, e)

    hp = M_HPG * M_HEADDIM
    lane = lax.broadcasted_iota(jnp.int32, (q, LANES), 1)
    first_head = lane < M_HEADDIM
    for g in range(M_GROUPS):
        gs = slice(g * hp, (g + 1) * hp)
        bm_t = act_ref[:, M_INNER + g * M_STATE:M_INNER + (g + 1) * M_STATE].T.astype(BF16)
        cm = act_ref[:, M_INNER + (M_GROUPS + g) * M_STATE:M_INNER + (M_GROUPS + g + 1) * M_STATE].astype(BF16)
        cbm = _dot(cm, bm_t)
        y_off = _dot(cm, ht_ref[:, gs].astype(BF16))
        st_t = _dot(bm_t, xdte_ref[:, gs])
        for pr in range(M_HPG // 2):
            h0 = g * M_HPG + 2 * pr
            ps = slice(h0 * M_HEADDIM, (h0 + 2) * M_HEADDIM)
            ms = []
            for h in (h0, h0 + 1):
                d2 = jnp.broadcast_to(acs[:, h:h + 1], (q, q)) - src_t[h:h + 1, :]
                ms.append((cbm * jnp.where(causal, jnp.exp2(d2), 0.0)).astype(BF16))
            xp = xs_ref[:, ps]
            zero = jnp.zeros_like(xp)
            rhs = jnp.concatenate([jnp.where(first_head, xp, zero), jnp.where(first_head, zero, xp)], axis=0)
            yd = _dot(jnp.concatenate(ms, axis=1), rhs)
            yacc_ref[:, ps] = yd + y_off[:, ps.start - gs.start:ps.stop - gs.start] * eacs_ref[:, ps] \
                + dsk_ref[:, ps] * act_ref[:, ps]
        ht_ref[:, gs] = eacs_ref[q - 1:q, gs] * ht_ref[:, gs] + st_t

    ss = jnp.zeros((q, 1), F32)
    for j in range(0, M_INNER, cs):
        sl = slice(j, j + cs)
        gt = yacc_ref[:, sl] * _silu(z_ref[:, sl])
        yacc_ref[:, sl] = gt
        ss = ss + jnp.sum(gt * gt, axis=-1, keepdims=True)
    inv = lax.rsqrt(ss * (1.0 / M_INNER) + EPS)
    for j in range(0, M_INNER, cs):
        sl = slice(j, j + cs)
        y_ref[:, sl] = yacc_ref[:, sl] * inv * nw_ref[:, sl]


def _cast_spec(w, nbatch, nc):
    rows, steps = w.shape[0], nbatch * nc
    n = next(n for n in range(steps, 0, -1) if rows % n == 0 and (rows // n) % (2 * SUBLANES) == 0)
    return pl.BlockSpec((rows // n, w.shape[1]), lambda b, c: (jnp.minimum(b * nc + c, n - 1), 0))


def _ssd(z, act, dt, alog, dsk, nw, e64, cast_ws, *, nbatch, seq):
    q = M_CHUNK
    k = SSD_CHUNKS_PER_STEP
    assert seq % (k * q) == 0
    nc = seq // (k * q)
    row = lambda n: pl.BlockSpec((k * q, n), lambda b, c: (b * nc + c, 0))
    cast_specs = [_cast_spec(w, nbatch, nc) for w in cast_ws]
    res = pl.pallas_call(
        functools.partial(_ssd_kernel, n_cast=len(cast_ws)),
        grid=(nbatch, nc),
        in_specs=[row(M_INNER), row(M_CONV_DIM), row(LANES), _const_spec(alog.shape),
                  _const_spec(dsk.shape), _const_spec(nw.shape), _const_spec(e64.shape)] + cast_specs,
        out_specs=[row(M_INNER), pl.BlockSpec((None, M_INNER, M_STATE), lambda b, c: (b, 0, 0))] + cast_specs,
        out_shape=[jax.ShapeDtypeStruct((nbatch * seq, M_INNER), F32),
                   jax.ShapeDtypeStruct((nbatch, M_INNER, M_STATE), F32)]
                  + [jax.ShapeDtypeStruct(w.shape, BF16) for w in cast_ws],
        scratch_shapes=[pltpu.VMEM((M_STATE, M_INNER), F32), pltpu.VMEM((k, q, M_INNER), BF16),
                        pltpu.VMEM((k, q, M_INNER), BF16), pltpu.VMEM((k, q, M_INNER), F32),
                        pltpu.VMEM((k, q, M_INNER), F32)],
        compiler_params=_params(("arbitrary", "arbitrary")),
        name="ssd",
    )(z, act, dt, alog, dsk, nw, e64, *cast_ws)
    return res[0], res[1], res[2:]


SSD_STEP_BB = 8


def _ssd_step_kernel(z_ref, xbc_ref, dt_ref, cbuf_ref, h0_ref, cw_ref, cb_ref, alog_ref, dsk_ref, nw_ref,
                     e64_ref, y_ref, hout_ref, convout_ref):
    bb = SSD_STEP_BB
    x_new = xbc_ref[...]
    conv = cb_ref[...] + x_new * cw_ref[M_CONV - 1:M_CONV, :]
    for k in range(M_CONV - 1):
        conv = conv + cbuf_ref[k] * cw_ref[k:k + 1, :]
    act = _silu(conv)
    for k in range(M_CONV - 2):
        convout_ref[k] = cbuf_ref[k + 1]
    convout_ref[M_CONV - 2] = x_new

    xs = act[:, :M_INNER]
    bmat = act[:, M_INNER:M_INNER + M_GROUPS * M_STATE]
    cmat = act[:, M_INNER + M_GROUPS * M_STATE:]
    dt = dt_ref[...]
    da = jnp.exp(dt * (-jnp.exp(alog_ref[...])))
    dt_e = _dot(_split3(dt), e64_ref[...])
    da_e = _dot(_split3(da), e64_ref[...])
    xdt = xs * dt_e

    hp = M_HPG * M_HEADDIM
    lane_group = lax.broadcasted_iota(jnp.int32, (M_GROUPS, M_INNER), 1) // hp
    row_group = lax.broadcasted_iota(jnp.int32, (M_GROUPS, M_INNER), 0)
    gmask = lane_group == row_group
    ones_rows = lax.broadcasted_iota(jnp.int32, (SUBLANES, M_STATE), 0) >= M_GROUPS

    cbx = []
    for g in range(M_GROUPS):
        cbg = jnp.sum(cmat[:, g * M_STATE:(g + 1) * M_STATE] * bmat[:, g * M_STATE:(g + 1) * M_STATE],
                      axis=-1, keepdims=True)
        cbx.append(jnp.broadcast_to(cbg, (bb, hp)))
    y = jnp.concatenate(cbx, axis=1) * xdt + dsk_ref[...] * xs

    y_off_rows = []
    for b in range(bb):
        xsel = jnp.where(gmask, jnp.broadcast_to(xdt[b:b + 1, :], (M_GROUPS, M_INNER)), 0.0)
        lhs = jnp.concatenate([xsel, *_pieces3(da_e[b:b + 1, :]), jnp.zeros((1, M_INNER), F32)],
                              axis=0).astype(BF16)
        bm4 = jnp.concatenate([bmat[b:b + 1, g * M_STATE:(g + 1) * M_STATE] for g in range(M_GROUPS)], axis=0)
        cm4 = jnp.concatenate([cmat[b:b + 1, g * M_STATE:(g + 1) * M_STATE] for g in range(M_GROUPS)], axis=0)
        zeros4 = jnp.zeros((SUBLANES - M_GROUPS, M_STATE), F32)
        rhs = jnp.concatenate([jnp.concatenate([bm4, zeros4], axis=0),
                               jnp.where(ones_rows, 1.0, 0.0)], axis=1).astype(BF16)
        res = lax.dot_general(lhs, rhs, _TN, preferred_element_type=F32)
        h0 = h0_ref[b]
        hout_ref[b] = res[:, M_STATE:] * h0 + res[:, :M_STATE]
        cm8 = jnp.concatenate([cm4, zeros4], axis=0).astype(BF16)
        yo = lax.dot_general(cm8, h0.astype(BF16), _NT, preferred_element_type=F32)
        y_off_rows.append(jnp.sum(jnp.where(gmask, yo[:M_GROUPS, :], 0.0), axis=0, keepdims=True))
    y = y + da_e * jnp.concatenate(y_off_rows, axis=0)

    y_ref[...] = _rms(y * _silu(z_ref[...]), nw_ref[...])


SSD_STEP_EVERY = 2


def _s5_ssd_step_kernel(*refs, nb, tl, sub_rows):
    n5i, nsi = S5_PIECES + 6, 11
    n5o, nso = S5_PIECES + 2, 3
    ins5, inss = refs[:n5i], refs[n5i:n5i + nsi]
    o = n5i + nsi
    outs5, outss = refs[o:o + n5o], refs[o + n5o:o + n5o + nso]
    sc5 = refs[o + n5o + nso:]
    s = pl.program_id(0) * pl.num_programs(1) + pl.program_id(1)

    def side():
        @pl.when(s % SSD_STEP_EVERY == 0)
        def _():
            _ssd_step_kernel(*inss, *outss)

    _s5_kernel(*ins5, *outs5, *sc5, nb=nb, tl=tl, sub_rows=sub_rows, side=side)


def _s5_ssd_step(u, h0r, h0i, b, a, c, d, zs, xbc, dt, cbuf, h0, cw, cb, alog, dsk, nw, e64, *, tl):
    nblk, rows, _ = u.shape
    pitch = _s5_pitch(tl)
    nb = rows // pitch
    sub_rows = min(S5_SUB_ROWS, tl * nb)
    nsub = tl * nb // sub_rows
    ns = zs.shape[0]
    bb, per = SSD_STEP_BB, SSD_STEP_EVERY
    assert sub_rows % nb == 0 and tl * nb % sub_rows == 0 and nsub % 2 == 0 and ns * per == bb * S5_CHUNKS * nblk
    chunk3 = lambda sh: pl.BlockSpec((None,) + sh, lambda c, t: (c, 0, 0))
    st_spec = pl.BlockSpec((nb, S5_NS), lambda c, t: (0, c))
    nu = S5_PIECES
    u_specs = [pl.BlockSpec((None, rows, LANES), functools.partial(lambda c, t, k: (t, 0, nu * c + k), k=k))
               for k in range(nu)]
    y5_spec = pl.BlockSpec((None, rows, LANES), lambda c, t: (t, 0, c))
    y5_shape = jax.ShapeDtypeStruct((nblk, rows, D_MODEL // nu), F32)
    step = lambda c, t: c * nblk + t
    row = lambda w: pl.BlockSpec((bb, w), lambda c, t: (step(c, t) // per, 0))
    conv_spec = pl.BlockSpec((M_CONV - 1, bb, M_CONV_DIM), lambda c, t: (0, step(c, t) // per, 0))
    st3_spec = pl.BlockSpec((bb, M_INNER, M_STATE), lambda c, t: (step(c, t) // per, 0, 0))
    res = pl.pallas_call(
        functools.partial(_s5_ssd_step_kernel, nb=nb, tl=tl, sub_rows=sub_rows),
        grid=(S5_CHUNKS, nblk),
        in_specs=u_specs + [st_spec, st_spec, chunk3((S5_CH, 2 * S5_STATE)), chunk3((1, 2 * S5_NS)),
                            chunk3((2 * S5_NS, S5_GROUP)), chunk3((1, S5_CH))]
                 + [row(M_INNER), row(M_CONV_DIM), row(LANES), conv_spec, st3_spec,
                    _const_spec(cw.shape), _const_spec(cb.shape), _const_spec(alog.shape),
                    _const_spec(dsk.shape), _const_spec(nw.shape), _const_spec(e64.shape)],
        out_specs=[y5_spec] * nu + [st_spec, st_spec] + [row(M_INNER), st3_spec, conv_spec],
        out_shape=[y5_shape] * nu + [jax.ShapeDtypeStruct(h0r.shape, F32), jax.ShapeDtypeStruct(h0i.shape, F32)]
                  + [jax.ShapeDtypeStruct((ns, M_INNER), F32), jax.ShapeDtypeStruct((ns, M_INNER, M_STATE), F32),
                     jax.ShapeDtypeStruct((M_CONV - 1, ns, M_CONV_DIM), F32)],
        scratch_shapes=[pltpu.VMEM((nb, 2 * S5_NS), F32)]
                       + [pltpu.VMEM((sub_rows, 2 * S5_NS), F32), pltpu.VMEM((sub_rows, S5_CH), F32)] * 2
                       + [pltpu.VMEM((S5_CH, 2 * S5_NS), BF16), pltpu.VMEM((2 * S5_NS, S5_CH), BF16)],
        compiler_params=_params(("arbitrary", "arbitrary")),
        name="s5_ssd_step",
    )(*([u] * nu), h0r, h0i, b, a, c, d, zs, xbc, dt, cbuf, h0, cw, cb, alog, dsk, nw, e64)
    return (res[:nu], res[nu], res[nu + 1]), (res[nu + 2], res[nu + 3], res[nu + 4])


MXU_WIDTH = 256
FFN_CHUNKS = (0, 6 * MXU_WIDTH, D_FF)


def _out_kernel(x_ref, *refs):
    y5_refs = refs[:S5_PIECES]
    (yb_ref, gate_ref, gtm_ref, shf_ref, scf_ref, gtf_ref, wglu_ref, bglu_ref, wb5_ref, wbs_ref, wout_ref,
     npm_ref, npf_ref, npo_ref, wfi_ref, wfo_ref, o_ref) = refs[S5_PIECES:]
    rows = x_ref.shape[0]
    y5 = jnp.concatenate([y5_refs[k][0:rows, c * LANES:(c + 1) * LANES]
                          for c in range(S5_CHUNKS) for k in range(S5_PIECES)], axis=1)
    ya = _gelu_tanh(y5)
    glu = ya * _sigmoid(_dot(ya.astype(BF16), wglu_ref[...]) + bglu_ref[...])
    merged = (_sigmoid(gate_ref[:, :D_MODEL]) * _dot(glu.astype(BF16), wb5_ref[...])
              + _sigmoid(gate_ref[:, D_MODEL:]) * _dot(yb_ref[...].astype(BF16), wbs_ref[...]))
    mix = _dot(merged.astype(BF16), wout_ref[...])
    x1 = x_ref[...] + gtm_ref[...] * _rms(mix, npm_ref[...])

    hb = (_rms(x1, npf_ref[...]) * (1.0 + scf_ref[...]) + shf_ref[...]).astype(BF16)
    f = jnp.zeros_like(x1)
    for j, e in zip(FFN_CHUNKS[:-1], FFN_CHUNKS[1:]):
        gg = _dot(hb, wfi_ref[:, j:e])
        uu = _dot(hb, wfi_ref[:, D_FF + j:D_FF + e])
        f = f + _dot((_silu(gg) * uu).astype(BF16), wfo_ref[j:e, :])
    o_ref[...] = x1 + gtf_ref[...] * _rms(f, npo_ref[...])


def _out_stage(x, y5, yb, gates, mod, ws, *, per_row, tm, rows_per_batch):
    t = x.shape[0]
    row = lambda n: pl.BlockSpec((tm, n), lambda i: (i, 0))
    mods = _mod_specs(per_row, tm, rows_per_batch // tm if not per_row else 1, (2, 3, 4, 5))
    _, y5_spec = _s5_layout(per_row, t, tm, rows_per_batch, D_MODEL // S5_PIECES)
    return pl.pallas_call(
        _out_kernel,
        grid=(t // tm,),
        in_specs=[row(D_MODEL)] + [y5_spec] * S5_PIECES + [row(M_INNER), row(2 * D_MODEL)] + mods
                 + [_const_spec(w.shape) for w in ws],
        out_specs=row(D_MODEL),
        out_shape=jax.ShapeDtypeStruct((t, D_MODEL), F32),
        compiler_params=_params(("parallel",)),
        name="out_stage",
    )(x, *y5, yb, gates, mod, mod, mod, mod, *ws)


def _pad_lanes(v):
    return jnp.pad(v.reshape(1, -1), ((0, 0), (0, LANES - v.shape[-1])))


def kernel(x_prompt, x_sample, state_s5_re, state_s5_im, state_ssm, state_conv, c_prompt, c_sample, w_ada, b_ada, norm_pre_mix, norm_post_mix, norm_pre_ffn, norm_post_ffn, w_in, s5_lam_re, s5_lam_im, s5_log_dt, s5_b_re, s5_b_im, s5_c_re, s5_c_im, s5_d, s5_w_glu, s5_b_glu, m_conv_w, m_conv_b, m_dt_bias, m_a_log, m_d, m_norm, w_branch_s5, w_branch_ssd, w_out, w_ffn_in, w_ffn_out):
    bp, seq, _ = x_prompt.shape
    bs = x_sample.shape[0]
    assert x_sample.shape[1] == 1 and w_ada.shape[0] == 1 and seq % M_CHUNK == 0
    row1 = lambda v: v.reshape(1, -1)

    ws_in = _split_w_in(w_in[0])
    cw, cb = m_conv_w[0], row1(m_conv_b[0])
    dtb, alog = _pad_lanes(m_dt_bias[0]), _pad_lanes(m_a_log[0])
    dsk = row1(jnp.repeat(m_d[0], M_HEADDIM))
    nw = row1(m_norm[0])
    e64 = _expand_mat()

    mod_p, mod_s = _ada(c_prompt, c_sample, w_ada[0], b_ada[0])

    ar, ai, bbr, bbi = _s5_disc(s5_lam_re[0], s5_lam_im[0], s5_log_dt[0], s5_b_re[0], s5_b_im[0])
    s5m = _s5_matrices(ar, ai, bbr, bbi, s5_c_re[0], s5_c_im[0], s5_d[0])
    g_pre = row1(norm_pre_mix[0])

    xp = x_prompt.reshape(bp * seq, D_MODEL)
    xs = x_sample.reshape(bs, D_MODEL)
    tm = min(ROW_TILE, seq)
    u, zs, act, dt, gates, tail = _in_proj(xp, mod_p, g_pre, ws_in, dtb, cw, cb,
                                           per_row=False, tm=tm, rows_per_batch=seq)
    u_s, zs_s, xbc_s, dt_s, gates_s = _in_proj(xs, mod_s, g_pre, ws_in, dtb, cw, cb,
                                               per_row=True, tm=bs, rows_per_batch=1)

    p_conv = tail[:, SUBLANES - (M_CONV - 1):, :]
    zeros_s5 = jnp.zeros((bp, S5_GROUPS * S5_STATE), F32)
    (y5, p_re, p_im), (yb_s, s_ssm, s_conv) = _s5_ssd_step(
        u, zeros_s5, zeros_s5, *s5m, zs_s, xbc_s, dt_s, jnp.swapaxes(state_conv[0], 0, 1),
        state_ssm[0].reshape(bs, M_INNER, M_STATE), cw, cb, alog, dsk, nw, e64, tl=tm)
    yb, p_ssm, (wglu, wb5, wbs, wo, wfi, wfo) = _ssd(
        zs, act, dt, alog, dsk, nw, e64,
        (s5_w_glu[0], w_branch_s5[0], w_branch_ssd[0], w_out[0], w_ffn_in[0], w_ffn_out[0]), nbatch=bp, seq=seq)
    ws_out = (wglu, row1(s5_b_glu[0]), wb5, wbs, wo, row1(norm_post_mix[0]),
              row1(norm_pre_ffn[0]), row1(norm_post_ffn[0]), wfi, wfo)
    y_prompt = _out_stage(xp, y5, yb, gates, mod_p, ws_out, per_row=False, tm=tm, rows_per_batch=seq)

    states_t = bs == LANES
    to_kernel = lambda v: (jnp.transpose(v, (1, 2, 0)).reshape(-1, bs) if states_t else v.reshape(bs, -1))
    from_kernel = lambda v: (jnp.transpose(v.reshape(S5_GROUPS, S5_STATE, bs), (2, 0, 1)) if states_t
                             else v.reshape(bs, S5_GROUPS, S5_STATE))[None]
    y5_s, s_re, s_im = _s5(u_s, to_kernel(state_s5_re[0]), to_kernel(state_s5_im[0]), *s5m, tl=1,
                           states_t=states_t)
    y_sample = _out_stage(xs, y5_s, yb_s, gates_s, mod_s, ws_out, per_row=True, tm=bs, rows_per_batch=1)

    s5_shape = (1, -1, S5_GROUPS, S5_STATE)
    ssm_shape = (1, -1, M_HEADS, M_HEADDIM, M_STATE)
    return (y_prompt.reshape(bp, seq, D_MODEL), y_sample.reshape(bs, 1, D_MODEL),
            p_re.reshape(s5_shape), p_im.reshape(s5_shape), p_ssm.reshape(ssm_shape), p_conv[None],
            from_kernel(s_re), from_kernel(s_im), s_ssm.reshape(ssm_shape), jnp.swapaxes(s_conv, 0, 1)[None])
```

```python
import functools

import jax
import jax.numpy as jnp
from jax import lax
from jax.experimental import pallas as pl
from jax.experimental.pallas import tpu as pltpu

F32 = jnp.float32
BF16 = jnp.bfloat16

D_MODEL = 1024
EPS = 1e-6
S5_GROUP = 16
S5_GROUPS = 64
S5_STATE = 64
M_INNER = 2048
M_HEADDIM = 64
M_HEADS = 32
M_GROUPS = 4
M_HPG = 8
M_STATE = 128
M_CONV = 4
M_CONV_DIM = 3072
M_CHUNK = 128
D_FF = 2816
OFF_Z = D_MODEL
OFF_XBC = OFF_Z + M_INNER
OFF_DT = OFF_XBC + M_CONV_DIM
OFF_GA = OFF_DT + M_HEADS
IN_COLS = OFF_GA + 2 * D_MODEL

LANES = 128
SUBLANES = 8
VMEM_LIMIT = 56 * 1024 * 1024

S5_CH = 256
S5_NS = 1024
S5_CHUNKS = D_MODEL // S5_CH
S5_TILES = S5_NS // LANES
S5_PIECES = S5_CH // LANES
S5_SUB_ROWS = 256
ROW_TILE = 256
INPROJ_SLAB = 512
INPROJ_DEPTH = 3

_NT = (((1,), (1,)), ((), ()))
_TN = (((0,), (0,)), ((), ()))


def _const_spec(shape):
    nd = len(shape)
    return pl.BlockSpec(shape, lambda *_: (0,) * nd, pipeline_mode=pl.Buffered(1))


def _params(sem):
    return pltpu.CompilerParams(dimension_semantics=sem, vmem_limit_bytes=VMEM_LIMIT)


def _sigmoid(x):
    return 0.5 * jnp.tanh(0.5 * x) + 0.5


def _silu(x):
    hx = 0.5 * x
    return hx + hx * jnp.tanh(hx)


def _softplus(x):
    return jnp.maximum(x, 0.0) + jnp.log1p(jnp.exp(-jnp.abs(x)))


def _gelu_tanh(x):
    return 0.5 * x * (1.0 + jnp.tanh(0.7978845608028654 * (x + 0.044715 * (x * x * x))))


def _rms(x, g):
    return x * lax.rsqrt(jnp.mean(x * x, axis=-1, keepdims=True) + EPS) * g


def _dot(a, b):
    return jnp.dot(a, b, preferred_element_type=F32)


def _pieces3(x):
    hi = x.astype(BF16).astype(F32)
    r1 = x - hi
    mid = r1.astype(BF16).astype(F32)
    lo = (r1 - mid).astype(BF16).astype(F32)
    return hi, mid, lo


def _ordered_after(x, dep):
    zero = jnp.where(dep[:, 0:LANES] > jnp.inf, 1.0, 0.0)
    return x + jnp.concatenate([zero] * (x.shape[1] // LANES), axis=1)


def _split2(x):
    return jnp.concatenate(_pieces3(x)[:2], axis=-1).astype(BF16)


def _split3(x):
    return jnp.concatenate(_pieces3(x), axis=-1).astype(BF16)


def _ada_kernel(cp_ref, cs_ref, w_ref, b_ref, *rest):
    disc_refs, (op_ref, os_ref) = rest[:5] + rest[7:], rest[5:7]

    @pl.when(pl.program_id(0) == 0)
    def _():
        _s5_disc_kernel(*disc_refs)

    w = w_ref[...].astype(BF16)
    op_ref[:, 0, :] = _dot(_silu(cp_ref[...]).astype(BF16), w) + b_ref[...]
    os_ref[...] = _dot(_silu(cs_ref[...]).astype(BF16), w) + b_ref[...]


def _ada_s5_disc(c_prompt, c_sample, w, b, lam_re, lam_im, log_dt, b_re, b_im):
    bp, bs = c_prompt.shape[0], c_sample.shape[0]
    tn = 2048
    disc_in = _s5_disc_operands(lam_re, lam_im, log_dt, b_re, b_im)
    g, n = lam_re.shape
    disc_shape = disc_in[0].shape
    mod_p, mod_s, ar, ai, bbr, bbi = pl.pallas_call(
        _ada_kernel,
        grid=(6 * D_MODEL // tn,),
        in_specs=[_const_spec((bp, D_MODEL)), _const_spec((bs, D_MODEL)),
                  pl.BlockSpec((D_MODEL, tn), lambda j: (0, j)),
                  pl.BlockSpec((1, tn), lambda j: (0, j))] + [_const_spec(disc_shape)] * 5,
        out_specs=[pl.BlockSpec((bp, 1, tn), lambda j: (0, 0, j)), pl.BlockSpec((bs, tn), lambda j: (0, j))]
        + [pl.BlockSpec(disc_shape, lambda j: (0, 0))] * 4,
        out_shape=[jax.ShapeDtypeStruct((bp, 1, 6 * D_MODEL), F32), jax.ShapeDtypeStruct((bs, 6 * D_MODEL), F32)]
        + [jax.ShapeDtypeStruct(disc_shape, F32)] * 4,
        compiler_params=_params(("arbitrary",)),
        name="ada_s5_disc",
    )(c_prompt, c_sample, w, b.reshape(1, -1), *disc_in)
    return mod_p, mod_s, (ar[:, :n], ai[:, :n], bbr.reshape(g, S5_GROUP, n), bbi.reshape(g, S5_GROUP, n))


def _s5_disc_kernel(lr_ref, li_ref, ldt_ref, br_ref, bi_ref, ar_ref, ai_ref, bbr_ref, bbi_ref):
    lr = lr_ref[...]
    li = li_ref[...]
    dt = jnp.exp(ldt_ref[...])
    mag = jnp.exp(lr * dt)
    ar = mag * jnp.cos(li * dt)
    ai = mag * jnp.sin(li * dt)
    den = lr * lr + li * li
    nr = ar - 1.0
    f_re = (nr * lr + ai * li) / den
    f_im = (ai * lr - nr * li) / den
    br = br_ref[...]
    bi = bi_ref[...]
    ar_ref[...] = ar
    ai_ref[...] = ai
    bbr_ref[...] = f_re * br - f_im * bi
    bbi_ref[...] = f_re * bi + f_im * br


def _s5_disc_operands(lam_re, lam_im, log_dt, b_re, b_im):
    g, n = lam_re.shape
    w = S5_GROUP * n
    tile = lambda a: jnp.tile(a, (1, S5_GROUP))
    ldt = jnp.broadcast_to(log_dt[:, None], (g, w))
    brt = jnp.swapaxes(b_re, 1, 2).reshape(g, w)
    bit = jnp.swapaxes(b_im, 1, 2).reshape(g, w)
    return tile(lam_re), tile(lam_im), ldt, brt, bit


def _s5_matrices(abar_re, abar_im, bbar_re, bbar_im, c_re, c_im, d_skip):
    ns = 2 * S5_NS
    b = jnp.concatenate([bbar_re.reshape(D_MODEL, S5_STATE), bbar_im.reshape(D_MODEL, S5_STATE)],
                        axis=1).reshape(S5_CHUNKS, S5_CH, 2 * S5_STATE)

    def by_tile(c):
        return jnp.swapaxes(c, 1, 2).reshape(S5_CHUNKS, S5_TILES, 1, 2, S5_STATE, S5_GROUP)
    c = jnp.concatenate([by_tile(c_re), by_tile(-c_im)], axis=2).reshape(S5_CHUNKS, ns, S5_GROUP)

    by_tile_a = lambda v: v.reshape(S5_CHUNKS, S5_TILES, 1, LANES)
    a = jnp.concatenate([by_tile_a(abar_re), by_tile_a(abar_im)], axis=2).reshape(S5_CHUNKS, 1, ns)
    d = d_skip.reshape(S5_CHUNKS, 1, S5_CH)
    return b, a, c, d


def _s5_state_group(i):
    return 2 * (i // (2 * LANES)) + (i % LANES) // S5_STATE


def _s5_expand_operands(b_ref, c_ref, bm_ref, cm_ref):
    tw = 2 * LANES
    row_g = lax.broadcasted_iota(jnp.int32, (S5_CH, LANES), 0) // S5_GROUP
    lane_half = lax.broadcasted_iota(jnp.int32, (S5_CH, LANES), 1) // S5_STATE
    for part in range(2):
        bp = b_ref[:, part * S5_STATE:(part + 1) * S5_STATE]
        pattern = jnp.concatenate([bp, bp], axis=1)
        for j in range(S5_TILES):
            tile = jnp.where(row_g == 2 * j + lane_half, pattern, 0.0)
            bm_ref[:, j * tw + part * LANES:j * tw + (part + 1) * LANES] = tile.astype(BF16)
    spread = (lax.broadcasted_iota(jnp.int32, (S5_GROUP, S5_CH), 1) % S5_GROUP
              == lax.broadcasted_iota(jnp.int32, (S5_GROUP, S5_CH), 0)).astype(BF16)
    rows = 512
    for r0 in range(0, 2 * S5_NS, rows):
        tiled = _dot(c_ref[r0:r0 + rows, :].astype(BF16), spread)
        own = (_s5_state_group(lax.broadcasted_iota(jnp.int32, (rows, S5_CH), 0) + r0)
               == lax.broadcasted_iota(jnp.int32, (rows, S5_CH), 1) // S5_GROUP)
        cm_ref[r0:r0 + rows, :] = jnp.where(own, tiled, 0.0).astype(BF16)


def _split_w_in_kernel(wt_ref, wm_ref, wdt_ref, wg_ref):
    tk = wt_ref.shape[1]
    step = 1024
    for j in range(0, OFF_DT, step):
        wm_ref[:, j:j + step] = wt_ref[j:j + step, :].T.astype(BF16)
    head = lax.broadcasted_iota(jnp.int32, (tk, LANES), 1) < M_HEADS
    wdt_ref[...] = jnp.where(head, wt_ref[OFF_DT:OFF_DT + LANES, :].T, 0.0).astype(BF16)
    for j in range(0, 2 * D_MODEL, step):
        wg_ref[:, j:j + step] = wt_ref[OFF_GA + j:OFF_GA + j + step, :].T.astype(BF16)


def _split_w_in(w):
    k = w.shape[0]
    tk = 256
    outs = (OFF_DT, LANES, 2 * D_MODEL)
    return pl.pallas_call(
        _split_w_in_kernel,
        grid=(k // tk,),
        in_specs=[pl.BlockSpec((IN_COLS, tk), lambda i: (0, i))],
        out_specs=[pl.BlockSpec((tk, n), lambda i: (i, 0)) for n in outs],
        out_shape=[jax.ShapeDtypeStruct((k, n), BF16) for n in outs],
        compiler_params=_params(("parallel",)),
        name="split_w_in",
    )(jnp.swapaxes(w, 0, 1))


def _inproj_kernel(x_ref, sh_ref, sc_ref, g_ref, wm_ref, wdt_ref, wg_ref, dtb_ref, cw_ref, cb_ref,
                   u_ref, z_ref, xbc_ref, dt_ref, gate_ref, *rest, tiles_per_seq):
    accs = rest[-INPROJ_DEPTH:]
    h = _rms(x_ref[...], g_ref[...]) * (1.0 + sc_ref[...]) + sh_ref[...]
    hb = h.astype(BF16)
    rows = hb.shape[0]
    step = INPROJ_SLAB

    if u_ref.shape[0] > rows:
        u_ref[rows:, :] = jnp.zeros((u_ref.shape[0] - rows, D_MODEL), F32)

    if tiles_per_seq is None:
        conv_fn = None
    else:
        tail_ref, carry_ref = rest[:2]

        @pl.when(pl.program_id(0) % tiles_per_seq == 0)
        def _():
            carry_ref[...] = jnp.zeros_like(carry_ref)

        def conv_fn(r, sl):
            width = sl.stop - sl.start
            w = [cw_ref[k:k + 1, sl] for k in range(M_CONV)]
            bias = cb_ref[:, sl]
            row = lax.broadcasted_iota(jnp.int32, (SUBLANES, width), 0)
            xp = carry_ref[:, sl]
            farp = xp * w[1] + pltpu.roll(xp, 1, axis=0) * w[0]
            outs = []
            for i in range(rows // SUBLANES):
                xi = r[i * SUBLANES:(i + 1) * SUBLANES, :]
                if outs:
                    xi = _ordered_after(xi, outs[-1])
                s1 = jnp.where(row == 0, pltpu.roll(xp, 1, axis=0), pltpu.roll(xi, 1, axis=0))
                far = xi * w[1] + s1 * w[0]
                s2 = jnp.where(row < 2, pltpu.roll(farp, 2, axis=0), pltpu.roll(far, 2, axis=0))
                outs.append(_silu(xi * w[3] + s1 * w[2] + s2 + bias))
                xp, farp = xi, far
            carry_ref[:, sl] = r[rows - SUBLANES:, :]
            tail_ref[:, sl] = r[rows - SUBLANES:, :]
            return jnp.concatenate(outs, axis=0)

    def slabs(o_ref, w_ref, w_off, fn):
        n = o_ref.shape[-1]
        return [(o_ref, slice(j, j + min(step, n - j)), w_ref, w_off + j, fn) for j in range(0, n, step)]

    su = slabs(u_ref, wm_ref, 0, None)
    sz = slabs(z_ref, wm_ref, OFF_Z, None)
    sx = slabs(xbc_ref, wm_ref, OFF_XBC, conv_fn)
    sd = slabs(dt_ref, wdt_ref, 0, lambda r, sl: _softplus(r + dtb_ref[:, sl]))
    sg = slabs(gate_ref, wg_ref, 0, None)
    light = su + sz + sg + sd
    order = []
    while sx or light:
        if sx:
            order.append(sx.pop(0))
        if light:
            order.append(light.pop(0))

    def matmul(s):
        o_ref, sl, w_ref, w_col, _ = order[s]
        accs[s % INPROJ_DEPTH][:, 0:sl.stop - sl.start] = _dot(hb, w_ref[:, w_col:w_col + sl.stop - sl.start])

    for s in range(min(INPROJ_DEPTH - 1, len(order))):
        matmul(s)
    for s, (o_ref, sl, _, _, fn) in enumerate(order):
        if s + INPROJ_DEPTH - 1 < len(order):
            matmul(s + INPROJ_DEPTH - 1)
        r = accs[s % INPROJ_DEPTH][:, 0:sl.stop - sl.start]
        o_ref[0:rows, sl] = r if fn is None else fn(r, sl)


def _mod_specs(per_row, tm, rows_per_mod, cols):
    if per_row:
        return [pl.BlockSpec((tm, D_MODEL), functools.partial(lambda i, c: (i, c), c=c)) for c in cols]
    return [pl.BlockSpec((None, 1, D_MODEL), functools.partial(lambda i, c: (i // rows_per_mod, 0, c), c=c))
            for c in cols]


def _s5_layout(per_row, t, tm, rows_per_batch, width):
    if per_row:
        return (1, t, width), pl.BlockSpec((None, tm, width), lambda i: (0, i, 0))
    nblk, pitch = rows_per_batch // tm, _s5_pitch(tm)
    return ((nblk, (t // rows_per_batch) * pitch, width),
            pl.BlockSpec((None, pitch, width), lambda i: (i % nblk, i // nblk, 0)))


def _in_proj(x, mod, g, ws, dtb, cw, cb, *, per_row, tm, rows_per_batch):
    t = x.shape[0]
    row = lambda n: pl.BlockSpec((tm, n), lambda i: (i, 0))
    outs = (M_INNER, M_CONV_DIM, LANES, 2 * D_MODEL)
    u_shape, u_spec = _s5_layout(per_row, t, tm, rows_per_batch, D_MODEL)
    out_specs = [u_spec] + [row(n) for n in outs]
    out_shape = [jax.ShapeDtypeStruct(u_shape, F32)] + [jax.ShapeDtypeStruct((t, n), F32) for n in outs]
    scratch = []
    tiles_per_seq = None
    if not per_row:
        tiles_per_seq = rows_per_batch // tm
        out_specs.append(pl.BlockSpec((None, SUBLANES, M_CONV_DIM), lambda i: (i // tiles_per_seq, 0, 0)))
        out_shape.append(jax.ShapeDtypeStruct((t // rows_per_batch, SUBLANES, M_CONV_DIM), F32))
        scratch.append(pltpu.VMEM((SUBLANES, M_CONV_DIM), F32))
    scratch += [pltpu.VMEM((tm, INPROJ_SLAB), F32)] * INPROJ_DEPTH
    consts = (g,) + tuple(ws) + (dtb, cw, cb)
    return pl.pallas_call(
        functools.partial(_inproj_kernel, tiles_per_seq=tiles_per_seq),
        grid=(t // tm,),
        in_specs=[row(D_MODEL)] + _mod_specs(per_row, tm, rows_per_batch // tm if not per_row else 1, (0, 1))
                 + [_const_spec(w.shape) for w in consts],
        out_specs=out_specs,
        out_shape=out_shape,
        scratch_shapes=scratch,
        compiler_params=_params(("arbitrary",)),
        name="in_proj",
    )(x, mod, mod, *consts)


def _s5_pitch(tl):
    return tl if tl == 1 else tl + SUBLANES


def _s5_kernel(*refs, nb, tl, sub_rows, side=None, states_t=False):
    nu = S5_PIECES
    u_refs = refs[:nu]
    h0r_ref, h0i_ref, b_ref, a_ref, c_ref, d_ref = refs[nu:nu + 6]
    y_refs = refs[nu + 6:2 * nu + 6]
    hr_ref, hi_ref, hst_ref, bu0_ref, xs0_ref, bu1_ref, xs1_ref, bm_ref, cm_ref = refs[2 * nu + 6:]
    bufs = ((bu0_ref, xs0_ref), (bu1_ref, xs1_ref))
    tb = pl.program_id(1)
    tsub = sub_rows // nb
    tw = 2 * LANES
    pitch = _s5_pitch(tl)

    def step_rows(t):
        return pl.ds(t, nb, stride=pitch) if pitch > 1 else pl.ds(0, nb)

    @pl.when(tb == 0)
    def _():
        _s5_expand_operands(b_ref, c_ref, bm_ref, cm_ref)
        for j in range(S5_TILES):
            js = slice(j * LANES, (j + 1) * LANES)
            hst_ref[:, j * tw:j * tw + LANES] = h0r_ref[js, :].T if states_t else h0r_ref[:, js]
            hst_ref[:, j * tw + LANES:(j + 1) * tw] = h0i_ref[js, :].T if states_t else h0i_ref[:, js]

    ar = [jnp.broadcast_to(a_ref[:, j * tw:j * tw + LANES], (nb, LANES)) for j in range(S5_TILES)]
    ai = [jnp.broadcast_to(a_ref[:, j * tw + LANES:(j + 1) * tw], (nb, LANES)) for j in range(S5_TILES)]
    dsk = d_ref[...]

    def project_in(t0, bu_ref, xs_ref):
        xs = jnp.concatenate([jnp.concatenate([r[step_rows(t0 + i), :] for r in u_refs], axis=1)
                              for i in range(tsub)], axis=0)
        xs_ref[...] = xs
        lhs = xs.astype(BF16)
        for j in range(S5_TILES):
            bu_ref[:, j * tw:(j + 1) * tw] = _dot(lhs, bm_ref[:, j * tw:(j + 1) * tw])

    def scan_project_out(t0, bu_ref, xs_ref, carry):
        new, hs = [], []
        for j in range(S5_TILES):
            hr, hi = carry[2 * j], carry[2 * j + 1]
            rows = []
            for i in range(tsub):
                br = bu_ref[i * nb:(i + 1) * nb, j * tw:j * tw + LANES]
                bi = bu_ref[i * nb:(i + 1) * nb, j * tw + LANES:(j + 1) * tw]
                hr, hi = ar[j] * hr - ai[j] * hi + br, ar[j] * hi + ai[j] * hr + bi
                rows.append(jnp.concatenate([hr, hi], axis=1))
            new += [hr, hi]
            hs.append(jnp.concatenate(rows, axis=0).astype(BF16))
        y = _dot(jnp.concatenate(hs, axis=1), cm_ref[...]) + dsk * xs_ref[...]
        for i in range(tsub):
            for k, y_ref in enumerate(y_refs):
                y_ref[step_rows(t0 + i), :] = y[i * nb:(i + 1) * nb, k * LANES:(k + 1) * LANES]
        return tuple(new)

    h_in = []
    for j in range(S5_TILES):
        h_in += [hst_ref[:, j * tw:j * tw + LANES], hst_ref[:, j * tw + LANES:(j + 1) * tw]]
    nsub = tl // tsub
    (bu0, xs0), (bu1, xs1) = bufs
    project_in(0, bu0, xs0)
    if nsub == 1:
        h_out = scan_project_out(0, bu0, xs0, tuple(h_in))
    else:
        def pair(p, carry):
            t0 = 2 * p * tsub
            project_in(t0 + tsub, bu1, xs1)
            carry = scan_project_out(t0, bu0, xs0, carry)
            project_in(jnp.minimum(t0 + 2 * tsub, tl - tsub), bu0, xs0)
            return scan_project_out(t0 + tsub, bu1, xs1, carry)
        h_out = lax.fori_loop(0, nsub // 2, pair, tuple(h_in))
    for j in range(S5_TILES):
        hst_ref[:, j * tw:j * tw + LANES] = h_out[2 * j]
        hst_ref[:, j * tw + LANES:(j + 1) * tw] = h_out[2 * j + 1]

    if pitch > tl:
        for y_ref in y_refs:
            for b in range(nb):
                y_ref[b * pitch + tl:(b + 1) * pitch, :] = jnp.zeros((pitch - tl, LANES), F32)

    @pl.when(tb == pl.num_programs(1) - 1)
    def _():
        for j in range(S5_TILES):
            js = slice(j * LANES, (j + 1) * LANES)
            if states_t:
                hr_ref[js, :] = h_out[2 * j].T
                hi_ref[js, :] = h_out[2 * j + 1].T
            else:
                hr_ref[:, js] = h_out[2 * j]
                hi_ref[:, js] = h_out[2 * j + 1]

    if side is not None:
        side()


def _s5(u, h0r, h0i, b, a, c, d, *, tl, states_t=False):
    nblk, rows, _ = u.shape
    pitch = _s5_pitch(tl)
    nb = rows // pitch
    sub_rows = min(S5_SUB_ROWS, tl * nb)
    nsub = tl * nb // sub_rows
    assert sub_rows % nb == 0 and tl * nb % sub_rows == 0 and (nsub == 1 or nsub % 2 == 0)
    chunk3 = lambda s: pl.BlockSpec((None,) + s, lambda c, t: (c, 0, 0))
    assert not states_t or nb == LANES
    st_spec = (pl.BlockSpec((S5_NS, nb), lambda c, t: (c, 0)) if states_t
               else pl.BlockSpec((nb, S5_NS), lambda c, t: (0, c)))
    nu = S5_PIECES
    u_specs = [pl.BlockSpec((None, rows, LANES), functools.partial(lambda c, t, k: (t, 0, nu * c + k), k=k))
               for k in range(nu)]
    y_spec = pl.BlockSpec((None, rows, LANES), lambda c, t: (t, 0, c))
    y_shape = jax.ShapeDtypeStruct((nblk, rows, D_MODEL // nu), F32)
    res = pl.pallas_call(
        functools.partial(_s5_kernel, nb=nb, tl=tl, sub_rows=sub_rows, states_t=states_t),
        grid=(S5_CHUNKS, nblk),
        in_specs=u_specs + [st_spec, st_spec, chunk3((S5_CH, 2 * S5_STATE)), chunk3((1, 2 * S5_NS)),
                            chunk3((2 * S5_NS, S5_GROUP)), chunk3((1, S5_CH))],
        out_specs=[y_spec] * nu + [st_spec, st_spec],
        out_shape=[y_shape] * nu + [jax.ShapeDtypeStruct(h0r.shape, F32), jax.ShapeDtypeStruct(h0i.shape, F32)],
        scratch_shapes=[pltpu.VMEM((nb, 2 * S5_NS), F32)]
                       + [pltpu.VMEM((sub_rows, 2 * S5_NS), F32), pltpu.VMEM((sub_rows, S5_CH), F32)] * 2
                       + [pltpu.VMEM((S5_CH, 2 * S5_NS), BF16), pltpu.VMEM((2 * S5_NS, S5_CH), BF16)],
        compiler_params=_params(("parallel", "arbitrary")),
        name="s5",
    )(*([u] * nu), h0r, h0i, b, a, c, d)
    return res[:nu], res[nu], res[nu + 1]


def _expand_mat():
    r = lax.broadcasted_iota(jnp.int32, (3 * LANES, M_HEADS * M_HEADDIM), 0) % LANES
    c = lax.broadcasted_iota(jnp.int32, (3 * LANES, M_HEADS * M_HEADDIM), 1) // M_HEADDIM
    return (r == c).astype(BF16)


LOG2E = 1.4426950408889634
SSD_CHUNKS_PER_STEP = 4


def _ssd_kernel(*refs, n_cast):
    z_ref, act_ref, dt_ref, alog_ref, dsk_ref, nw_ref, e64_ref = refs[:7]
    w_refs = refs[7:7 + n_cast]
    y_ref, hout_ref = refs[7 + n_cast:9 + n_cast]
    wb_refs = refs[9 + n_cast:9 + 2 * n_cast]
    ht_ref, xs_ref, xdte_ref, eacs_ref, yacc_ref = refs[9 + 2 * n_cast:]
    c = pl.program_id(1)
    for w_ref, wb_ref in zip(w_refs, wb_refs):
        wb_ref[...] = w_ref[...].astype(BF16)

    @pl.when(c == 0)
    def _():
        ht_ref[...] = jnp.zeros_like(ht_ref)

    for k in range(SSD_CHUNKS_PER_STEP):
        rows = pl.ds(k * M_CHUNK, M_CHUNK)
        _ssd_chunk(z_ref.at[rows], act_ref.at[rows], dt_ref.at[rows], alog_ref, dsk_ref, nw_ref, e64_ref,
                   y_ref.at[rows], ht_ref, xs_ref.at[k], xdte_ref.at[k], eacs_ref.at[k], yacc_ref.at[k])

    @pl.when(c == pl.num_programs(1) - 1)
    def _():
        for j in range(0, M_INNER, LANES):
            hout_ref[j:j + LANES, :] = ht_ref[:, j:j + LANES].T


def _ssd_chunk(z_ref, act_ref, dt_ref, alog_ref, dsk_ref, nw_ref, e64_ref,
               y_ref, ht_ref, xs_ref, xdte_ref, eacs_ref, yacc_ref):
    q = M_CHUNK
    cs = 512

    dt = dt_ref[...]
    a2 = dt * (-jnp.exp(alog_ref[...]) * LOG2E)
    ri = lax.broadcasted_iota(jnp.int32, (q, q), 0)
    ci = lax.broadcasted_iota(jnp.int32, (q, q), 1)
    causal = ri >= ci
    acs = jnp.dot(causal.astype(F32), a2, preferred_element_type=F32, precision=lax.Precision.HIGHEST)
    src_t = (acs - jnp.log2(dt)).T
    eacs2 = _split2(jnp.exp2(acs))
    dte2 = _split2(dt * jnp.exp2(acs[q - 1:q, :] - acs))
    for j in range(0, M_INNER, cs):
        sl = slice(j, j + cs)
        e = e64_ref[0:2 * LANES, sl]
        xs = act_ref[:, sl]
        xs_ref[:, sl] = xs.astype(BF16)
        xdte_ref[:, sl] = (xs * _dot(dte2, e)).astype(BF16)
        eacs_ref[:, sl] = _dot(eacs2, e)

    hp = M_HPG * M_HEADDIM
    lane = lax.broadcasted_iota(jnp.int32, (q, LANES), 1)
    first_head = lane < M_HEADDIM
    for g in range(M_GROUPS):
        gs = slice(g * hp, (g + 1) * hp)
        bm_t = act_ref[:, M_INNER + g * M_STATE:M_INNER + (g + 1) * M_STATE].T.astype(BF16)
        cm = act_ref[:, M_INNER + (M_GROUPS + g) * M_STATE:M_INNER + (M_GROUPS + g + 1) * M_STATE].astype(BF16)
        cbm = _dot(cm, bm_t)
        y_off = _dot(cm, ht_ref[:, gs].astype(BF16))
        st_t = _dot(bm_t, xdte_ref[:, gs])
        for pr in range(M_HPG // 2):
            h0 = g * M_HPG + 2 * pr
            ps = slice(h0 * M_HEADDIM, (h0 + 2) * M_HEADDIM)
            ms = []
            for h in (h0, h0 + 1):
                d2 = jnp.broadcast_to(acs[:, h:h + 1], (q, q)) - src_t[h:h + 1, :]
                ms.append((cbm * jnp.where(causal, jnp.exp2(d2), 0.0)).astype(BF16))
            xp = xs_ref[:, ps]
            zero = jnp.zeros_like(xp)
            rhs = jnp.concatenate([jnp.where(first_head, xp, zero), jnp.where(first_head, zero, xp)], axis=0)
            yd = _dot(jnp.concatenate(ms, axis=1), rhs)
            yacc_ref[:, ps] = yd + y_off[:, ps.start - gs.start:ps.stop - gs.start] * eacs_ref[:, ps] \
                + dsk_ref[:, ps] * act_ref[:, ps]
        ht_ref[:, gs] = eacs_ref[q - 1:q, gs] * ht_ref[:, gs] + st_t

    ss = jnp.zeros((q, 1), F32)
    for j in range(0, M_INNER, cs):
        sl = slice(j, j + cs)
        gt = yacc_ref[:, sl] * _silu(z_ref[:, sl])
        yacc_ref[:, sl] = gt
        ss = ss + jnp.sum(gt * gt, axis=-1, keepdims=True)
    inv = lax.rsqrt(ss * (1.0 / M_INNER) + EPS)
    for j in range(0, M_INNER, cs):
        sl = slice(j, j + cs)
        y_ref[:, sl] = yacc_ref[:, sl] * inv * nw_ref[:, sl]


def _cast_spec(w, nbatch, nc):
    rows, steps = w.shape[0], nbatch * nc
    n = next(n for n in range(steps, 0, -1) if rows % n == 0 and (rows // n) % (2 * SUBLANES) == 0)
    return pl.BlockSpec((rows // n, w.shape[1]), lambda b, c: (jnp.minimum(b * nc + c, n - 1), 0))


def _ssd(z, act, dt, alog, dsk, nw, e64, cast_ws, *, nbatch, seq):
    q = M_CHUNK
    k = SSD_CHUNKS_PER_STEP
    assert seq % (k * q) == 0
    nc = seq // (k * q)
    row = lambda n: pl.BlockSpec((k * q, n), lambda b, c: (b * nc + c, 0))
    cast_specs = [_cast_spec(w, nbatch, nc) for w in cast_ws]
    res = pl.pallas_call(
        functools.partial(_ssd_kernel, n_cast=len(cast_ws)),
        grid=(nbatch, nc),
        in_specs=[row(M_INNER), row(M_CONV_DIM), row(LANES), _const_spec(alog.shape),
                  _const_spec(dsk.shape), _const_spec(nw.shape), _const_spec(e64.shape)] + cast_specs,
        out_specs=[row(M_INNER), pl.BlockSpec((None, M_INNER, M_STATE), lambda b, c: (b, 0, 0))] + cast_specs,
        out_shape=[jax.ShapeDtypeStruct((nbatch * seq, M_INNER), F32),
                   jax.ShapeDtypeStruct((nbatch, M_INNER, M_STATE), F32)]
                  + [jax.ShapeDtypeStruct(w.shape, BF16) for w in cast_ws],
        scratch_shapes=[pltpu.VMEM((M_STATE, M_INNER), F32), pltpu.VMEM((k, q, M_INNER), BF16),
                        pltpu.VMEM((k, q, M_INNER), BF16), pltpu.VMEM((k, q, M_INNER), F32),
                        pltpu.VMEM((k, q, M_INNER), F32)],
        compiler_params=_params(("arbitrary", "arbitrary")),
        name="ssd",
    )(z, act, dt, alog, dsk, nw, e64, *cast_ws)
    return res[0], res[1], res[2:]


SSD_STEP_BB = 8


def _ssd_step_kernel(z_ref, xbc_ref, dt_ref, cbuf_ref, h0_ref, cw_ref, cb_ref, alog_ref, dsk_ref, nw_ref,
                     e64_ref, y_ref, hout_ref, convout_ref):
    bb = SSD_STEP_BB
    x_new = xbc_ref[...]
    conv = cb_ref[...] + x_new * cw_ref[M_CONV - 1:M_CONV, :]
    for k in range(M_CONV - 1):
        conv = conv + cbuf_ref[k] * cw_ref[k:k + 1, :]
    act = _silu(conv)
    for k in range(M_CONV - 2):
        convout_ref[k] = cbuf_ref[k + 1]
    convout_ref[M_CONV - 2] = x_new

    xs = act[:, :M_INNER]
    bmat = act[:, M_INNER:M_INNER + M_GROUPS * M_STATE]
    cmat = act[:, M_INNER + M_GROUPS * M_STATE:]
    dt = dt_ref[...]
    da = jnp.exp(dt * (-jnp.exp(alog_ref[...])))
    dt_e = _dot(_split3(dt), e64_ref[...])
    da_e = _dot(_split3(da), e64_ref[...])
    xdt = xs * dt_e

    hp = M_HPG * M_HEADDIM
    lane_group = lax.broadcasted_iota(jnp.int32, (M_GROUPS, M_INNER), 1) // hp
    row_group = lax.broadcasted_iota(jnp.int32, (M_GROUPS, M_INNER), 0)
    gmask = lane_group == row_group
    ones_rows = lax.broadcasted_iota(jnp.int32, (SUBLANES, M_STATE), 0) >= M_GROUPS

    cbx = []
    for g in range(M_GROUPS):
        cbg = jnp.sum(cmat[:, g * M_STATE:(g + 1) * M_STATE] * bmat[:, g * M_STATE:(g + 1) * M_STATE],
                      axis=-1, keepdims=True)
        cbx.append(jnp.broadcast_to(cbg, (bb, hp)))
    y = jnp.concatenate(cbx, axis=1) * xdt + dsk_ref[...] * xs

    y_off_rows = []
    for b in range(bb):
        xsel = jnp.where(gmask, jnp.broadcast_to(xdt[b:b + 1, :], (M_GROUPS, M_INNER)), 0.0)
        lhs = jnp.concatenate([xsel, *_pieces3(da_e[b:b + 1, :]), jnp.zeros((1, M_INNER), F32)],
                              axis=0).astype(BF16)
        bm4 = jnp.concatenate([bmat[b:b + 1, g * M_STATE:(g + 1) * M_STATE] for g in range(M_GROUPS)], axis=0)
        cm4 = jnp.concatenate([cmat[b:b + 1, g * M_STATE:(g + 1) * M_STATE] for g in range(M_GROUPS)], axis=0)
        zeros4 = jnp.zeros((SUBLANES - M_GROUPS, M_STATE), F32)
        rhs = jnp.concatenate([jnp.concatenate([bm4, zeros4], axis=0),
                               jnp.where(ones_rows, 1.0, 0.0)], axis=1).astype(BF16)
        res = lax.dot_general(lhs, rhs, _TN, preferred_element_type=F32)
        h0 = h0_ref[b]
        hout_ref[b] = res[:, M_STATE:] * h0 + res[:, :M_STATE]
        cm8 = jnp.concatenate([cm4, zeros4], axis=0).astype(BF16)
        yo = lax.dot_general(cm8, h0.astype(BF16), _NT, preferred_element_type=F32)
        y_off_rows.append(jnp.sum(jnp.where(gmask, yo[:M_GROUPS, :], 0.0), axis=0, keepdims=True))
    y = y + da_e * jnp.concatenate(y_off_rows, axis=0)

    y_ref[...] = _rms(y * _silu(z_ref[...]), nw_ref[...])


SSD_STEP_EVERY = 2


def _s5_ssd_step_kernel(*refs, nb, tl, sub_rows):
    n5i, nsi = S5_PIECES + 6, 11
    n5o, nso = S5_PIECES + 2, 3
    ins5, inss = refs[:n5i], refs[n5i:n5i + nsi]
    o = n5i + nsi
    outs5, outss = refs[o:o + n5o], refs[o + n5o:o + n5o + nso]
    sc5 = refs[o + n5o + nso:]
    s = pl.program_id(0) * pl.num_programs(1) + pl.program_id(1)

    def side():
        @pl.when(s % SSD_STEP_EVERY == 0)
        def _():
            _ssd_step_kernel(*inss, *outss)

    _s5_kernel(*ins5, *outs5, *sc5, nb=nb, tl=tl, sub_rows=sub_rows, side=side)


def _s5_ssd_step(u, h0r, h0i, b, a, c, d, zs, xbc, dt, cbuf, h0, cw, cb, alog, dsk, nw, e64, *, tl):
    nblk, rows, _ = u.shape
    pitch = _s5_pitch(tl)
    nb = rows // pitch
    sub_rows = min(S5_SUB_ROWS, tl * nb)
    nsub = tl * nb // sub_rows
    ns = zs.shape[0]
    bb, per = SSD_STEP_BB, SSD_STEP_EVERY
    assert sub_rows % nb == 0 and tl * nb % sub_rows == 0 and nsub % 2 == 0 and ns * per == bb * S5_CHUNKS * nblk
    chunk3 = lambda sh: pl.BlockSpec((None,) + sh, lambda c, t: (c, 0, 0))
    st_spec = pl.BlockSpec((nb, S5_NS), lambda c, t: (0, c))
    nu = S5_PIECES
    u_specs = [pl.BlockSpec((None, rows, LANES), functools.partial(lambda c, t, k: (t, 0, nu * c + k), k=k))
               for k in range(nu)]
    y5_spec = pl.BlockSpec((None, rows, LANES), lambda c, t: (t, 0, c))
    y5_shape = jax.ShapeDtypeStruct((nblk, rows, D_MODEL // nu), F32)
    step = lambda c, t: c * nblk + t
    row = lambda w: pl.BlockSpec((bb, w), lambda c, t: (step(c, t) // per, 0))
    conv_spec = pl.BlockSpec((M_CONV - 1, bb, M_CONV_DIM), lambda c, t: (0, step(c, t) // per, 0))
    st3_spec = pl.BlockSpec((bb, M_INNER, M_STATE), lambda c, t: (step(c, t) // per, 0, 0))
    res = pl.pallas_call(
        functools.partial(_s5_ssd_step_kernel, nb=nb, tl=tl, sub_rows=sub_rows),
        grid=(S5_CHUNKS, nblk),
        in_specs=u_specs + [st_spec, st_spec, chunk3((S5_CH, 2 * S5_STATE)), chunk3((1, 2 * S5_NS)),
                            chunk3((2 * S5_NS, S5_GROUP)), chunk3((1, S5_CH))]
                 + [row(M_INNER), row(M_CONV_DIM), row(LANES), conv_spec, st3_spec,
                    _const_spec(cw.shape), _const_spec(cb.shape), _const_spec(alog.shape),
                    _const_spec(dsk.shape), _const_spec(nw.shape), _const_spec(e64.shape)],
        out_specs=[y5_spec] * nu + [st_spec, st_spec] + [row(M_INNER), st3_spec, conv_spec],
        out_shape=[y5_shape] * nu + [jax.ShapeDtypeStruct(h0r.shape, F32), jax.ShapeDtypeStruct(h0i.shape, F32)]
                  + [jax.ShapeDtypeStruct((ns, M_INNER), F32), jax.ShapeDtypeStruct((ns, M_INNER, M_STATE), F32),
                     jax.ShapeDtypeStruct((M_CONV - 1, ns, M_CONV_DIM), F32)],
        scratch_shapes=[pltpu.VMEM((nb, 2 * S5_NS), F32)]
                       + [pltpu.VMEM((sub_rows, 2 * S5_NS), F32), pltpu.VMEM((sub_rows, S5_CH), F32)] * 2
                       + [pltpu.VMEM((S5_CH, 2 * S5_NS), BF16), pltpu.VMEM((2 * S5_NS, S5_CH), BF16)],
        compiler_params=_params(("arbitrary", "arbitrary")),
        name="s5_ssd_step",
    )(*([u] * nu), h0r, h0i, b, a, c, d, zs, xbc, dt, cbuf, h0, cw, cb, alog, dsk, nw, e64)
    return (res[:nu], res[nu], res[nu + 1]), (res[nu + 2], res[nu + 3], res[nu + 4])


MXU_WIDTH = 256
FFN_CHUNKS = (0, 6 * MXU_WIDTH, D_FF)


def _out_kernel(x_ref, *refs):
    y5_refs = refs[:S5_PIECES]
    (yb_ref, gate_ref, gtm_ref, shf_ref, scf_ref, gtf_ref, wglu_ref, bglu_ref, wb5_ref, wbs_ref, wout_ref,
     npm_ref, npf_ref, npo_ref, wfi_ref, wfo_ref, o_ref) = refs[S5_PIECES:]
    rows = x_ref.shape[0]
    y5 = jnp.concatenate([y5_refs[k][0:rows, c * LANES:(c + 1) * LANES]
                          for c in range(S5_CHUNKS) for k in range(S5_PIECES)], axis=1)
    ya = _gelu_tanh(y5)
    glu = ya * _sigmoid(_dot(ya.astype(BF16), wglu_ref[...]) + bglu_ref[...])
    merged = (_sigmoid(gate_ref[:, :D_MODEL]) * _dot(glu.astype(BF16), wb5_ref[...])
              + _sigmoid(gate_ref[:, D_MODEL:]) * _dot(yb_ref[...].astype(BF16), wbs_ref[...]))
    mix = _dot(merged.astype(BF16), wout_ref[...])
    x1 = x_ref[...] + gtm_ref[...] * _rms(mix, npm_ref[...])

    hb = (_rms(x1, npf_ref[...]) * (1.0 + scf_ref[...]) + shf_ref[...]).astype(BF16)
    f = jnp.zeros_like(x1)
    for j, e in zip(FFN_CHUNKS[:-1], FFN_CHUNKS[1:]):
        gg = _dot(hb, wfi_ref[:, j:e])
        uu = _dot(hb, wfi_ref[:, D_FF + j:D_FF + e])
        f = f + _dot((_silu(gg) * uu).astype(BF16), wfo_ref[j:e, :])
    o_ref[...] = x1 + gtf_ref[...] * _rms(f, npo_ref[...])


def _out_stage(x, y5, yb, gates, mod, ws, *, per_row, tm, rows_per_batch):
    t = x.shape[0]
    row = lambda n: pl.BlockSpec((tm, n), lambda i: (i, 0))
    mods = _mod_specs(per_row, tm, rows_per_batch // tm if not per_row else 1, (2, 3, 4, 5))
    _, y5_spec = _s5_layout(per_row, t, tm, rows_per_batch, D_MODEL // S5_PIECES)
    return pl.pallas_call(
        _out_kernel,
        grid=(t // tm,),
        in_specs=[row(D_MODEL)] + [y5_spec] * S5_PIECES + [row(M_INNER), row(2 * D_MODEL)] + mods
                 + [_const_spec(w.shape) for w in ws],
        out_specs=row(D_MODEL),
        out_shape=jax.ShapeDtypeStruct((t, D_MODEL), F32),
        compiler_params=_params(("parallel",)),
        name="out_stage",
    )(x, *y5, yb, gates, mod, mod, mod, mod, *ws)


def _pad_lanes(v):
    return jnp.pad(v.reshape(1, -1), ((0, 0), (0, LANES - v.shape[-1])))


def kernel(x_prompt, x_sample, state_s5_re, state_s5_im, state_ssm, state_conv, c_prompt, c_sample, w_ada, b_ada, norm_pre_mix, norm_post_mix, norm_pre_ffn, norm_post_ffn, w_in, s5_lam_re, s5_lam_im, s5_log_dt, s5_b_re, s5_b_im, s5_c_re, s5_c_im, s5_d, s5_w_glu, s5_b_glu, m_conv_w, m_conv_b, m_dt_bias, m_a_log, m_d, m_norm, w_branch_s5, w_branch_ssd, w_out, w_ffn_in, w_ffn_out):
    bp, seq, _ = x_prompt.shape
    bs = x_sample.shape[0]
    assert x_sample.shape[1] == 1 and w_ada.shape[0] == 1 and seq % M_CHUNK == 0
    row1 = lambda v: v.reshape(1, -1)

    ws_in = _split_w_in(w_in[0])
    cw, cb = m_conv_w[0], row1(m_conv_b[0])
    dtb, alog = _pad_lanes(m_dt_bias[0]), _pad_lanes(m_a_log[0])
    dsk = row1(jnp.repeat(m_d[0], M_HEADDIM))
    nw = row1(m_norm[0])
    e64 = _expand_mat()

    mod_p, mod_s, (ar, ai, bbr, bbi) = _ada_s5_disc(c_prompt, c_sample, w_ada[0], b_ada[0], s5_lam_re[0], s5_lam_im[0],
                                                    s5_log_dt[0], s5_b_re[0], s5_b_im[0])
    s5m =_s5_matrices(ar, ai, bbr, bbi, s5_c_re[0], s5_c_im[0], s5_d[0])
    g_pre = row1(norm_pre_mix[0])

    xp = x_prompt.reshape(bp * seq, D_MODEL)
    xs = x_sample.reshape(bs, D_MODEL)
    tm = min(ROW_TILE, seq)
    u, zs, act, dt, gates, tail = _in_proj(xp, mod_p, g_pre, ws_in, dtb, cw, cb,
                                           per_row=False, tm=tm, rows_per_batch=seq)
    u_s, zs_s, xbc_s, dt_s, gates_s = _in_proj(xs, mod_s, g_pre, ws_in, dtb, cw, cb,
                                               per_row=True, tm=bs, rows_per_batch=1)

    p_conv = tail[:, SUBLANES - (M_CONV - 1):, :]
    zeros_s5 = jnp.zeros((bp, S5_GROUPS * S5_STATE), F32)
    (y5, p_re, p_im), (yb_s, s_ssm, s_conv) = _s5_ssd_step(
        u, zeros_s5, zeros_s5, *s5m, zs_s, xbc_s, dt_s, jnp.swapaxes(state_conv[0], 0, 1),
        state_ssm[0].reshape(bs, M_INNER, M_STATE), cw, cb, alog, dsk, nw, e64, tl=tm)
    yb, p_ssm, (wglu, wb5, wbs, wo, wfi, wfo) = _ssd(
        zs, act, dt, alog, dsk, nw, e64,
        (s5_w_glu[0], w_branch_s5[0], w_branch_ssd[0], w_out[0], w_ffn_in[0], w_ffn_out[0]), nbatch=bp, seq=seq)
    ws_out = (wglu, row1(s5_b_glu[0]), wb5, wbs, wo, row1(norm_post_mix[0]),
              row1(norm_pre_ffn[0]), row1(norm_post_ffn[0]), wfi, wfo)
    y_prompt = _out_stage(xp, y5, yb, gates, mod_p, ws_out, per_row=False, tm=tm, rows_per_batch=seq)

    states_t = bs == LANES
    to_kernel = lambda v: (jnp.transpose(v, (1, 2, 0)).reshape(-1, bs) if states_t else v.reshape(bs, -1))
    from_kernel = lambda v: (jnp.transpose(v.reshape(S5_GROUPS, S5_STATE, bs), (2, 0, 1)) if states_t
                             else v.reshape(bs, S5_GROUPS, S5_STATE))[None]
    y5_s, s_re, s_im = _s5(u_s, to_kernel(state_s5_re[0]), to_kernel(state_s5_im[0]), *s5m, tl=1,
                           states_t=states_t)
    y_sample = _out_stage(xs, y5_s, yb_s, gates_s, mod_s, ws_out, per_row=True, tm=bs, rows_per_batch=1)

    s5_shape = (1, -1, S5_GROUPS, S5_STATE)
    ssm_shape = (1, -1, M_HEADS, M_HEADDIM, M_STATE)
    return (y_prompt.reshape(bp, seq, D_MODEL), y_sample.reshape(bs, 1, D_MODEL),
            p_re.reshape(s5_shape), p_im.reshape(s5_shape), p_ssm.reshape(ssm_shape), p_conv[None],
            from_kernel(s_re), from_kernel(s_im), s_ssm.reshape(ssm_shape), jnp.swapaxes(s_conv, 0, 1)[None])
```

```python
import functools

import jax
import jax.numpy as jnp
from jax import lax
from jax.experimental import pallas as pl
from jax.experimental.pallas import tpu as pltpu

F32 = jnp.float32
BF16 = jnp.bfloat16

D_MODEL = 1024
EPS = 1e-6
S5_GROUP = 16
S5_GROUPS = 64
S5_STATE = 64
M_INNER = 2048
M_HEADDIM = 64
M_HEADS = 32
M_GROUPS = 4
M_HPG = 8
M_STATE = 128
M_CONV = 4
M_CONV_DIM = 3072
M_CHUNK = 128
D_FF = 2816
OFF_Z = D_MODEL
OFF_XBC = OFF_Z + M_INNER
OFF_DT = OFF_XBC + M_CONV_DIM
OFF_GA = OFF_DT + M_HEADS
IN_COLS = OFF_GA + 2 * D_MODEL

LANES = 128
SUBLANES = 8
VMEM_LIMIT = 56 * 1024 * 1024

S5_CH = 256
S5_NS = 1024
S5_CHUNKS = D_MODEL // S5_CH
S5_TILES = S5_NS // LANES
S5_PIECES = S5_CH // LANES
S5_SUB_ROWS = 256
ROW_TILE = 256
INPROJ_SLAB = 512
INPROJ_DEPTH = 3

_NT = (((1,), (1,)), ((), ()))
_TN = (((0,), (0,)), ((), ()))


def _const_spec(shape):
    nd = len(shape)
    return pl.BlockSpec(shape, lambda *_: (0,) * nd, pipeline_mode=pl.Buffered(1))


def _params(sem):
    return pltpu.CompilerParams(dimension_semantics=sem, vmem_limit_bytes=VMEM_LIMIT)


def _sigmoid(x):
    return 0.5 * jnp.tanh(0.5 * x) + 0.5


def _silu(x):
    hx = 0.5 * x
    return hx + hx * jnp.tanh(hx)


def _softplus(x):
    return jnp.maximum(x, 0.0) + jnp.log1p(jnp.exp(-jnp.abs(x)))


def _gelu_tanh(x):
    return 0.5 * x * (1.0 + jnp.tanh(0.7978845608028654 * (x + 0.044715 * (x * x * x))))


def _rms(x, g):
    return x * lax.rsqrt(jnp.mean(x * x, axis=-1, keepdims=True) + EPS) * g


def _dot(a, b):
    return jnp.dot(a, b, preferred_element_type=F32)


def _pieces3(x):
    hi = x.astype(BF16).astype(F32)
    r1 = x - hi
    mid = r1.astype(BF16).astype(F32)
    lo = (r1 - mid).astype(BF16).astype(F32)
    return hi, mid, lo


def _ordered_after(x, dep):
    zero = jnp.where(dep[:, 0:LANES] > jnp.inf, 1.0, 0.0)
    return x + jnp.concatenate([zero] * (x.shape[1] // LANES), axis=1)


def _split2(x):
    return jnp.concatenate(_pieces3(x)[:2], axis=-1).astype(BF16)


def _split3(x):
    return jnp.concatenate(_pieces3(x), axis=-1).astype(BF16)


def _ada_kernel(cp_ref, cs_ref, w_ref, b_ref, op_ref, os_ref):
    w = w_ref[...].astype(BF16)
    op_ref[:, 0, :] = _dot(_silu(cp_ref[...]).astype(BF16), w) + b_ref[...]
    os_ref[...] = _dot(_silu(cs_ref[...]).astype(BF16), w) + b_ref[...]


def _ada(c_prompt, c_sample, w, b):
    bp, bs = c_prompt.shape[0], c_sample.shape[0]
    tn = 2048
    return pl.pallas_call(
        _ada_kernel,
        grid=(6 * D_MODEL // tn,),
        in_specs=[_const_spec((bp, D_MODEL)), _const_spec((bs, D_MODEL)),
                  pl.BlockSpec((D_MODEL, tn), lambda j: (0, j)),
                  pl.BlockSpec((1, tn), lambda j: (0, j))],
        out_specs=[pl.BlockSpec((bp, 1, tn), lambda j: (0, 0, j)), pl.BlockSpec((bs, tn), lambda j: (0, j))],
        out_shape=[jax.ShapeDtypeStruct((bp, 1, 6 * D_MODEL), F32), jax.ShapeDtypeStruct((bs, 6 * D_MODEL), F32)],
        compiler_params=_params(("arbitrary",)),
        name="ada",
    )(c_prompt, c_sample, w, b.reshape(1, -1))


def _s5_disc_kernel(lr_ref, li_ref, ldt_ref, br_ref, bi_ref, ar_ref, ai_ref, bbr_ref, bbi_ref):
    lr = lr_ref[...]
    li = li_ref[...]
    dt = jnp.exp(ldt_ref[...])
    mag = jnp.exp(lr * dt)
    ar = mag * jnp.cos(li * dt)
    ai = mag * jnp.sin(li * dt)
    den = lr * lr + li * li
    nr = ar - 1.0
    f_re = (nr * lr + ai * li) / den
    f_im = (ai * lr - nr * li) / den
    br = br_ref[...]
    bi = bi_ref[...]
    ar_ref[...] = ar
    ai_ref[...] = ai
    bbr_ref[...] = f_re * br - f_im * bi
    bbi_ref[...] = f_re * bi + f_im * br


def _s5_disc(lam_re, lam_im, log_dt, b_re, b_im):
    g, n = lam_re.shape
    w = S5_GROUP * n
    tile = lambda a: jnp.tile(a, (1, S5_GROUP))
    ldt = jnp.broadcast_to(log_dt[:, None], (g, w))
    brt = jnp.swapaxes(b_re, 1, 2).reshape(g, w)
    bit = jnp.swapaxes(b_im, 1, 2).reshape(g, w)
    shp = jax.ShapeDtypeStruct((g, w), F32)
    ar, ai, bbr, bbi = pl.pallas_call(
        _s5_disc_kernel,
        out_shape=(shp, shp, shp, shp),
        name="s5_disc",
    )(tile(lam_re), tile(lam_im), ldt, brt, bit)
    return (ar[:, :n], ai[:, :n], bbr.reshape(g, S5_GROUP, n), bbi.reshape(g, S5_GROUP, n))


def _s5_matrices(abar_re, abar_im, bbar_re, bbar_im, c_re, c_im, d_skip):
    ns = 2 * S5_NS
    b = jnp.concatenate([bbar_re.reshape(D_MODEL, S5_STATE), bbar_im.reshape(D_MODEL, S5_STATE)],
                        axis=1).reshape(S5_CHUNKS, S5_CH, 2 * S5_STATE)

    def by_tile(c):
        return jnp.swapaxes(c, 1, 2).reshape(S5_CHUNKS, S5_TILES, 1, 2, S5_STATE, S5_GROUP)
    c = jnp.concatenate([by_tile(c_re), by_tile(-c_im)], axis=2).reshape(S5_CHUNKS, ns, S5_GROUP)

    by_tile_a = lambda v: v.reshape(S5_CHUNKS, S5_TILES, 1, LANES)
    a = jnp.concatenate([by_tile_a(abar_re), by_tile_a(abar_im)], axis=2).reshape(S5_CHUNKS, 1, ns)
    d = d_skip.reshape(S5_CHUNKS, 1, S5_CH)
    return b, a, c, d


def _s5_state_group(i):
    return 2 * (i // (2 * LANES)) + (i % LANES) // S5_STATE


def _s5_expand_operands(b_ref, c_ref, bm_ref, cm_ref):
    tw = 2 * LANES
    row_g = lax.broadcasted_iota(jnp.int32, (S5_CH, LANES), 0) // S5_GROUP
    lane_half = lax.broadcasted_iota(jnp.int32, (S5_CH, LANES), 1) // S5_STATE
    for part in range(2):
        bp = b_ref[:, part * S5_STATE:(part + 1) * S5_STATE]
        pattern = jnp.concatenate([bp, bp], axis=1)
        for j in range(S5_TILES):
            tile = jnp.where(row_g == 2 * j + lane_half, pattern, 0.0)
            bm_ref[:, j * tw + part * LANES:j * tw + (part + 1) * LANES] = tile.astype(BF16)
    spread = (lax.broadcasted_iota(jnp.int32, (S5_GROUP, S5_CH), 1) % S5_GROUP
              == lax.broadcasted_iota(jnp.int32, (S5_GROUP, S5_CH), 0)).astype(BF16)
    rows = 512
    for r0 in range(0, 2 * S5_NS, rows):
        tiled = _dot(c_ref[r0:r0 + rows, :].astype(BF16), spread)
        own = (_s5_state_group(lax.broadcasted_iota(jnp.int32, (rows, S5_CH), 0) + r0)
               == lax.broadcasted_iota(jnp.int32, (rows, S5_CH), 1) // S5_GROUP)
        cm_ref[r0:r0 + rows, :] = jnp.where(own, tiled, 0.0).astype(BF16)


def _split_w_in_kernel(wt_ref, wm_ref, wdt_ref, wg_ref):
    tk = wt_ref.shape[1]
    step = 1024
    for j in range(0, OFF_DT, step):
        wm_ref[:, j:j + step] = wt_ref[j:j + step, :].T.astype(BF16)
    head = lax.broadcasted_iota(jnp.int32, (tk, LANES), 1) < M_HEADS
    wdt_ref[...] = jnp.where(head, wt_ref[OFF_DT:OFF_DT + LANES, :].T, 0.0).astype(BF16)
    for j in range(0, 2 * D_MODEL, step):
        wg_ref[:, j:j + step] = wt_ref[OFF_GA + j:OFF_GA + j + step, :].T.astype(BF16)


def _split_w_in(w):
    k = w.shape[0]
    tk = 256
    outs = (OFF_DT, LANES, 2 * D_MODEL)
    return pl.pallas_call(
        _split_w_in_kernel,
        grid=(k // tk,),
        in_specs=[pl.BlockSpec((IN_COLS, tk), lambda i: (0, i))],
        out_specs=[pl.BlockSpec((tk, n), lambda i: (i, 0)) for n in outs],
        out_shape=[jax.ShapeDtypeStruct((k, n), BF16) for n in outs],
        compiler_params=_params(("parallel",)),
        name="split_w_in",
    )(jnp.swapaxes(w, 0, 1))


def _inproj_kernel(x_ref, sh_ref, sc_ref, g_ref, wm_ref, wdt_ref, wg_ref, dtb_ref, cw_ref, cb_ref,
                   u_ref, z_ref, xbc_ref, dt_ref, gate_ref, *rest, tiles_per_seq):
    accs = rest[-INPROJ_DEPTH:]
    h = _rms(x_ref[...], g_ref[...]) * (1.0 + sc_ref[...]) + sh_ref[...]
    hb = h.astype(BF16)
    rows = hb.shape[0]
    step = INPROJ_SLAB

    if u_ref.shape[0] > rows:
        u_ref[rows:, :] = jnp.zeros((u_ref.shape[0] - rows, D_MODEL), F32)

    if tiles_per_seq is None:
        conv_fn = None
    else:
        tail_ref, carry_ref = rest[:2]

        @pl.when(pl.program_id(0) % tiles_per_seq == 0)
        def _():
            carry_ref[...] = jnp.zeros_like(carry_ref)

        def conv_fn(r, sl):
            width = sl.stop - sl.start
            w = [cw_ref[k:k + 1, sl] for k in range(M_CONV)]
            bias = cb_ref[:, sl]
            row = lax.broadcasted_iota(jnp.int32, (SUBLANES, width), 0)
            xp = carry_ref[:, sl]
            farp = xp * w[1] + pltpu.roll(xp, 1, axis=0) * w[0]
            outs = []
            for i in range(rows // SUBLANES):
                xi = r[i * SUBLANES:(i + 1) * SUBLANES, :]
                if outs:
                    xi = _ordered_after(xi, outs[-1])
                s1 = jnp.where(row == 0, pltpu.roll(xp, 1, axis=0), pltpu.roll(xi, 1, axis=0))
                far = xi * w[1] + s1 * w[0]
                s2 = jnp.where(row < 2, pltpu.roll(farp, 2, axis=0), pltpu.roll(far, 2, axis=0))
                outs.append(_silu(xi * w[3] + s1 * w[2] + s2 + bias))
                xp, farp = xi, far
            carry_ref[:, sl] = r[rows - SUBLANES:, :]
            tail_ref[:, sl] = r[rows - SUBLANES:, :]
            return jnp.concatenate(outs, axis=0)

    def slabs(o_ref, w_ref, w_off, fn):
        n = o_ref.shape[-1]
        return [(o_ref, slice(j, j + min(step, n - j)), w_ref, w_off + j, fn) for j in range(0, n, step)]

    su = slabs(u_ref, wm_ref, 0, None)
    sz = slabs(z_ref, wm_ref, OFF_Z, None)
    sx = slabs(xbc_ref, wm_ref, OFF_XBC, conv_fn)
    sd = slabs(dt_ref, wdt_ref, 0, lambda r, sl: _softplus(r + dtb_ref[:, sl]))
    sg = slabs(gate_ref, wg_ref, 0, None)
    light = su + sz + sg + sd
    order = []
    while sx or light:
        if sx:
            order.append(sx.pop(0))
        if light:
            order.append(light.pop(0))

    def matmul(s):
        o_ref, sl, w_ref, w_col, _ = order[s]
        accs[s % INPROJ_DEPTH][:, 0:sl.stop - sl.start] = _dot(hb, w_ref[:, w_col:w_col + sl.stop - sl.start])

    for s in range(min(INPROJ_DEPTH - 1, len(order))):
        matmul(s)
    for s, (o_ref, sl, _, _, fn) in enumerate(order):
        if s + INPROJ_DEPTH - 1 < len(order):
            matmul(s + INPROJ_DEPTH - 1)
        r = accs[s % INPROJ_DEPTH][:, 0:sl.stop - sl.start]
        o_ref[0:rows, sl] = r if fn is None else fn(r, sl)


def _mod_specs(per_row, tm, rows_per_mod, cols):
    if per_row:
        return [pl.BlockSpec((tm, D_MODEL), functools.partial(lambda i, c: (i, c), c=c)) for c in cols]
    return [pl.BlockSpec((None, 1, D_MODEL), functools.partial(lambda i, c: (i // rows_per_mod, 0, c), c=c))
            for c in cols]


def _x_spec(x, tm):
    if x.ndim == 3:
        return pl.BlockSpec((tm, None, D_MODEL), lambda i: (i, 0, 0))
    return pl.BlockSpec((tm, D_MODEL), lambda i: (i, 0))


def _s5_layout(per_row, t, tm, rows_per_batch, width):
    if per_row:
        return (1, t, width), pl.BlockSpec((None, tm, width), lambda i: (0, i, 0))
    nblk, pitch = rows_per_batch // tm, _s5_pitch(tm)
    return ((nblk, (t // rows_per_batch) * pitch, width),
            pl.BlockSpec((None, pitch, width), lambda i: (i % nblk, i // nblk, 0)))


def _in_proj(x, mod, g, ws, dtb, cw, cb, *, per_row, tm, rows_per_batch):
    t = x.shape[0]
    row = lambda n: pl.BlockSpec((tm, n), lambda i: (i, 0))
    outs = (M_INNER, M_CONV_DIM, LANES, 2 * D_MODEL)
    u_shape, u_spec = _s5_layout(per_row, t, tm, rows_per_batch, D_MODEL)
    out_specs = [u_spec] + [row(n) for n in outs]
    out_shape = [jax.ShapeDtypeStruct(u_shape, F32)] + [jax.ShapeDtypeStruct((t, n), F32) for n in outs]
    scratch = []
    tiles_per_seq = None
    if not per_row:
        tiles_per_seq = rows_per_batch // tm
        out_specs.append(pl.BlockSpec((None, SUBLANES, M_CONV_DIM), lambda i: (i // tiles_per_seq, 0, 0)))
        out_shape.append(jax.ShapeDtypeStruct((t // rows_per_batch, SUBLANES, M_CONV_DIM), F32))
        scratch.append(pltpu.VMEM((SUBLANES, M_CONV_DIM), F32))
    scratch += [pltpu.VMEM((tm, INPROJ_SLAB), F32)] * INPROJ_DEPTH
    consts = (g,) + tuple(ws) + (dtb, cw, cb)
    return pl.pallas_call(
        functools.partial(_inproj_kernel, tiles_per_seq=tiles_per_seq),
        grid=(t // tm,),
        in_specs=[_x_spec(x, tm)] + _mod_specs(per_row, tm, rows_per_batch // tm if not per_row else 1, (0, 1))
                 + [_const_spec(w.shape) for w in consts],
        out_specs=out_specs,
        out_shape=out_shape,
        scratch_shapes=scratch,
        compiler_params=_params(("arbitrary",)),
        name="in_proj",
    )(x, mod, mod, *consts)


def _s5_pitch(tl):
    return tl if tl == 1 else tl + SUBLANES


def _s5_kernel(*refs, nb, tl, sub_rows, side=None, states_t=False):
    nu = S5_PIECES
    u_refs = refs[:nu]
    h0r_ref, h0i_ref, b_ref, a_ref, c_ref, d_ref = refs[nu:nu + 6]
    y_refs = refs[nu + 6:2 * nu + 6]
    hr_ref, hi_ref, hst_ref, bu0_ref, xs0_ref, bu1_ref, xs1_ref, bm_ref, cm_ref = refs[2 * nu + 6:]
    bufs = ((bu0_ref, xs0_ref), (bu1_ref, xs1_ref))
    tb = pl.program_id(1)
    tsub = sub_rows // nb
    tw = 2 * LANES
    pitch = _s5_pitch(tl)

    def step_rows(t):
        return pl.ds(t, nb, stride=pitch) if pitch > 1 else pl.ds(0, nb)

    @pl.when(tb == 0)
    def _():
        _s5_expand_operands(b_ref, c_ref, bm_ref, cm_ref)
        for j in range(S5_TILES):
            js = slice(j * LANES, (j + 1) * LANES)
            hst_ref[:, j * tw:j * tw + LANES] = h0r_ref[js, :].T if states_t else h0r_ref[:, js]
            hst_ref[:, j * tw + LANES:(j + 1) * tw] = h0i_ref[js, :].T if states_t else h0i_ref[:, js]

    ar = [jnp.broadcast_to(a_ref[:, j * tw:j * tw + LANES], (nb, LANES)) for j in range(S5_TILES)]
    ai = [jnp.broadcast_to(a_ref[:, j * tw + LANES:(j + 1) * tw], (nb, LANES)) for j in range(S5_TILES)]
    dsk = d_ref[...]

    def project_in(t0, bu_ref, xs_ref):
        xs = jnp.concatenate([jnp.concatenate([r[step_rows(t0 + i), :] for r in u_refs], axis=1)
                              for i in range(tsub)], axis=0)
        xs_ref[...] = xs
        lhs = xs.astype(BF16)
        for j in range(S5_TILES):
            bu_ref[:, j * tw:(j + 1) * tw] = _dot(lhs, bm_ref[:, j * tw:(j + 1) * tw])

    def scan_project_out(t0, bu_ref, xs_ref, carry):
        new, hs = [], []
        for j in range(S5_TILES):
            hr, hi = carry[2 * j], carry[2 * j + 1]
            rows = []
            for i in range(tsub):
                br = bu_ref[i * nb:(i + 1) * nb, j * tw:j * tw + LANES]
                bi = bu_ref[i * nb:(i + 1) * nb, j * tw + LANES:(j + 1) * tw]
                hr, hi = ar[j] * hr - ai[j] * hi + br, ar[j] * hi + ai[j] * hr + bi
                rows.append(jnp.concatenate([hr, hi], axis=1))
            new += [hr, hi]
            hs.append(jnp.concatenate(rows, axis=0).astype(BF16))
        y = _dot(jnp.concatenate(hs, axis=1), cm_ref[...]) + dsk * xs_ref[...]
        for i in range(tsub):
            for k, y_ref in enumerate(y_refs):
                y_ref[step_rows(t0 + i), :] = y[i * nb:(i + 1) * nb, k * LANES:(k + 1) * LANES]
        return tuple(new)

    h_in = []
    for j in range(S5_TILES):
        h_in += [hst_ref[:, j * tw:j * tw + LANES], hst_ref[:, j * tw + LANES:(j + 1) * tw]]
    nsub = tl // tsub
    (bu0, xs0), (bu1, xs1) = bufs
    project_in(0, bu0, xs0)
    if nsub == 1:
        h_out = scan_project_out(0, bu0, xs0, tuple(h_in))
    else:
        def pair(p, carry):
            t0 = 2 * p * tsub
            project_in(t0 + tsub, bu1, xs1)
            carry = scan_project_out(t0, bu0, xs0, carry)
            project_in(jnp.minimum(t0 + 2 * tsub, tl - tsub), bu0, xs0)
            return scan_project_out(t0 + tsub, bu1, xs1, carry)
        h_out = lax.fori_loop(0, nsub // 2, pair, tuple(h_in))
    for j in range(S5_TILES):
        hst_ref[:, j * tw:j * tw + LANES] = h_out[2 * j]
        hst_ref[:, j * tw + LANES:(j + 1) * tw] = h_out[2 * j + 1]

    if pitch > tl:
        for y_ref in y_refs:
            for b in range(nb):
                y_ref[b * pitch + tl:(b + 1) * pitch, :] = jnp.zeros((pitch - tl, LANES), F32)

    @pl.when(tb == pl.num_programs(1) - 1)
    def _():
        for j in range(S5_TILES):
            js = slice(j * LANES, (j + 1) * LANES)
            if states_t:
                hr_ref[js, :] = h_out[2 * j].T
                hi_ref[js, :] = h_out[2 * j + 1].T
            else:
                hr_ref[:, js] = h_out[2 * j]
                hi_ref[:, js] = h_out[2 * j + 1]

    if side is not None:
        side()


def _s5(u, h0r, h0i, b, a, c, d, *, tl, states_t=False):
    nblk, rows, _ = u.shape
    pitch = _s5_pitch(tl)
    nb = rows // pitch
    sub_rows = min(S5_SUB_ROWS, tl * nb)
    nsub = tl * nb // sub_rows
    assert sub_rows % nb == 0 and tl * nb % sub_rows == 0 and (nsub == 1 or nsub % 2 == 0)
    chunk3 = lambda s: pl.BlockSpec((None,) + s, lambda c, t: (c, 0, 0))
    assert not states_t or nb == LANES
    st_spec = (pl.BlockSpec((S5_NS, nb), lambda c, t: (c, 0)) if states_t
               else pl.BlockSpec((nb, S5_NS), lambda c, t: (0, c)))
    nu = S5_PIECES
    u_specs = [pl.BlockSpec((None, rows, LANES), functools.partial(lambda c, t, k: (t, 0, nu * c + k), k=k))
               for k in range(nu)]
    y_spec = pl.BlockSpec((None, rows, LANES), lambda c, t: (t, 0, c))
    y_shape = jax.ShapeDtypeStruct((nblk, rows, D_MODEL // nu), F32)
    res = pl.pallas_call(
        functools.partial(_s5_kernel, nb=nb, tl=tl, sub_rows=sub_rows, states_t=states_t),
        grid=(S5_CHUNKS, nblk),
        in_specs=u_specs + [st_spec, st_spec, chunk3((S5_CH, 2 * S5_STATE)), chunk3((1, 2 * S5_NS)),
                            chunk3((2 * S5_NS, S5_GROUP)), chunk3((1, S5_CH))],
        out_specs=[y_spec] * nu + [st_spec, st_spec],
        out_shape=[y_shape] * nu + [jax.ShapeDtypeStruct(h0r.shape, F32), jax.ShapeDtypeStruct(h0i.shape, F32)],
        scratch_shapes=[pltpu.VMEM((nb, 2 * S5_NS), F32)]
                       + [pltpu.VMEM((sub_rows, 2 * S5_NS), F32), pltpu.VMEM((sub_rows, S5_CH), F32)] * 2
                       + [pltpu.VMEM((S5_CH, 2 * S5_NS), BF16), pltpu.VMEM((2 * S5_NS, S5_CH), BF16)],
        compiler_params=_params(("parallel", "arbitrary")),
        name="s5",
    )(*([u] * nu), h0r, h0i, b, a, c, d)
    return res[:nu], res[nu], res[nu + 1]


def _expand_mat():
    r = lax.broadcasted_iota(jnp.int32, (3 * LANES, M_HEADS * M_HEADDIM), 0) % LANES
    c = lax.broadcasted_iota(jnp.int32, (3 * LANES, M_HEADS * M_HEADDIM), 1) // M_HEADDIM
    return (r == c).astype(BF16)


LOG2E = 1.4426950408889634
SSD_CHUNKS_PER_STEP = 4


def _ssd_kernel(*refs, n_cast):
    z_ref, act_ref, dt_ref, alog_ref, dsk_ref, nw_ref, e64_ref = refs[:7]
    w_refs = refs[7:7 + n_cast]
    y_ref, hout_ref = refs[7 + n_cast:9 + n_cast]
    wb_refs = refs[9 + n_cast:9 + 2 * n_cast]
    ht_ref, xs_ref, xdte_ref, eacs_ref, yacc_ref = refs[9 + 2 * n_cast:]
    c = pl.program_id(1)
    for w_ref, wb_ref in zip(w_refs, wb_refs):
        wb_ref[...] = w_ref[...].astype(BF16)

    @pl.when(c == 0)
    def _():
        ht_ref[...] = jnp.zeros_like(ht_ref)

    for k in range(SSD_CHUNKS_PER_STEP):
        rows = pl.ds(k * M_CHUNK, M_CHUNK)
        _ssd_chunk(z_ref.at[rows], act_ref.at[rows], dt_ref.at[rows], alog_ref, dsk_ref, nw_ref, e64_ref,
                   y_ref.at[rows], ht_ref, xs_ref.at[k], xdte_ref.at[k], eacs_ref.at[k], yacc_ref.at[k])

    @pl.when(c == pl.num_programs(1) - 1)
    def _():
        for j in range(0, M_INNER, LANES):
            hout_ref[j:j + LANES, :] = ht_ref[:, j:j + LANES].T


def _ssd_chunk(z_ref, act_ref, dt_ref, alog_ref, dsk_ref, nw_ref, e64_ref,
               y_ref, ht_ref, xs_ref, xdte_ref, eacs_ref, yacc_ref):
    q = M_CHUNK
    cs = 512

    dt = dt_ref[...]
    a2 = dt * (-jnp.exp(alog_ref[...]) * LOG2E)
    ri = lax.broadcasted_iota(jnp.int32, (q, q), 0)
    ci = lax.broadcasted_iota(jnp.int32, (q, q), 1)
    causal = ri >= ci
    acs = jnp.dot(causal.astype(F32), a2, preferred_element_type=F32, precision=lax.Precision.HIGHEST)
    src_t = (acs - jnp.log2(dt)).T
    eacs2 = _split2(jnp.exp2(acs))
    dte2 = _split2(dt * jnp.exp2(acs[q - 1:q, :] - acs))
    for j in range(0, M_INNER, cs):
        sl = slice(j, j + cs)
        e = e64_ref[0:2 * LANES, sl]
        xs = act_ref[:, sl]
        xs_ref[:, sl] = xs.astype(BF16)
        xdte_ref[:, sl] = (xs * _dot(dte2, e)).astype(BF16)
        eacs_ref[:, sl] = _dot(eacs2, e)

    hp = M_HPG * M_HEADDIM
    lane = lax.broadcasted_iota(jnp.int32, (q, LANES), 1)
    first_head = lane < M_HEADDIM
    for g in range(M_GROUPS):
        gs = slice(g * hp, (g + 1) * hp)
        bm_t = act_ref[:, M_INNER + g * M_STATE:M_INNER + (g + 1) * M_STATE].T.astype(BF16)
        cm = act_ref[:, M_INNER + (M_GROUPS + g) * M_STATE:M_INNER + (M_GROUPS + g + 1) * M_STATE].astype(BF16)
        cbm = _dot(cm, bm_t)
        y_off = _dot(cm, ht_ref[:, gs].astype(BF16))
        st_t = _dot(bm_t, xdte_ref[:, gs])
        for pr in range(M_HPG // 2):
            h0 = g * M_HPG + 2 * pr
            ps = slice(h0 * M_HEADDIM, (h0 + 2) * M_HEADDIM)
            ms = []
            for h in (h0, h0 + 1):
                d2 = jnp.broadcast_to(acs[:, h:h + 1], (q, q)) - src_t[h:h + 1, :]
                ms.append((cbm * jnp.where(causal, jnp.exp2(d2), 0.0)).astype(BF16))
            xp = xs_ref[:, ps]
            zero = jnp.zeros_like(xp)
            rhs = jnp.concatenate([jnp.where(first_head, xp, zero), jnp.where(first_head, zero, xp)], axis=0)
            yd = _dot(jnp.concatenate(ms, axis=1), rhs)
            yacc_ref[:, ps] = yd + y_off[:, ps.start - gs.start:ps.stop - gs.start] * eacs_ref[:, ps] \
                + dsk_ref[:, ps] * act_ref[:, ps]
        ht_ref[:, gs] = eacs_ref[q - 1:q, gs] * ht_ref[:, gs] + st_t

    ss = jnp.zeros((q, 1), F32)
    for j in range(0, M_INNER, cs):
        sl = slice(j, j + cs)
        gt = yacc_ref[:, sl] * _silu(z_ref[:, sl])
        yacc_ref[:, sl] = gt
        ss = ss + jnp.sum(gt * gt, axis=-1, keepdims=True)
    inv = lax.rsqrt(ss * (1.0 / M_INNER) + EPS)
    for j in range(0, M_INNER, cs):
        sl = slice(j, j + cs)
        y_ref[:, sl] = yacc_ref[:, sl] * inv * nw_ref[:, sl]


def _cast_spec(w, nbatch, nc):
    rows, steps = w.shape[0], nbatch * nc
    n = next(n for n in range(steps, 0, -1) if rows % n == 0 and (rows // n) % (2 * SUBLANES) == 0)
    return pl.BlockSpec((rows // n, w.shape[1]), lambda b, c: (jnp.minimum(b * nc + c, n - 1), 0))


def _ssd(z, act, dt, alog, dsk, nw, e64, cast_ws, *, nbatch, seq):
    q = M_CHUNK
    k = SSD_CHUNKS_PER_STEP
    assert seq % (k * q) == 0
    nc = seq // (k * q)
    row = lambda n: pl.BlockSpec((k * q, n), lambda b, c: (b * nc + c, 0))
    cast_specs = [_cast_spec(w, nbatch, nc) for w in cast_ws]
    res = pl.pallas_call(
        functools.partial(_ssd_kernel, n_cast=len(cast_ws)),
        grid=(nbatch, nc),
        in_specs=[row(M_INNER), row(M_CONV_DIM), row(LANES), _const_spec(alog.shape),
                  _const_spec(dsk.shape), _const_spec(nw.shape), _const_spec(e64.shape)] + cast_specs,
        out_specs=[row(M_INNER), pl.BlockSpec((None, M_INNER, M_STATE), lambda b, c: (b, 0, 0))] + cast_specs,
        out_shape=[jax.ShapeDtypeStruct((nbatch * seq, M_INNER), F32),
                   jax.ShapeDtypeStruct((nbatch, M_INNER, M_STATE), F32)]
                  + [jax.ShapeDtypeStruct(w.shape, BF16) for w in cast_ws],
        scratch_shapes=[pltpu.VMEM((M_STATE, M_INNER), F32), pltpu.VMEM((k, q, M_INNER), BF16),
                        pltpu.VMEM((k, q, M_INNER), BF16), pltpu.VMEM((k, q, M_INNER), F32),
                        pltpu.VMEM((k, q, M_INNER), F32)],
        compiler_params=_params(("arbitrary", "arbitrary")),
        name="ssd",
    )(z, act, dt, alog, dsk, nw, e64, *cast_ws)
    return res[0], res[1], res[2:]


SSD_STEP_BB = 8


def _ssd_step_kernel(z_ref, xbc_ref, dt_ref, cbuf_ref, h0_ref, cw_ref, cb_ref, alog_ref, dsk_ref, nw_ref,
                     e64_ref, y_ref, hout_ref, convout_ref):
    bb = SSD_STEP_BB
    x_new = xbc_ref[...]
    conv = cb_ref[...] + x_new * cw_ref[M_CONV - 1:M_CONV, :]
    for k in range(M_CONV - 1):
        conv = conv + cbuf_ref[k] * cw_ref[k:k + 1, :]
    act = _silu(conv)
    for k in range(M_CONV - 2):
        convout_ref[k] = cbuf_ref[k + 1]
    convout_ref[M_CONV - 2] = x_new

    xs = act[:, :M_INNER]
    bmat = act[:, M_INNER:M_INNER + M_GROUPS * M_STATE]
    cmat = act[:, M_INNER + M_GROUPS * M_STATE:]
    dt = dt_ref[...]
    da = jnp.exp(dt * (-jnp.exp(alog_ref[...])))
    dt_e = _dot(_split3(dt), e64_ref[...])
    da_e = _dot(_split3(da), e64_ref[...])
    xdt = xs * dt_e

    hp = M_HPG * M_HEADDIM
    lane_group = lax.broadcasted_iota(jnp.int32, (M_GROUPS, M_INNER), 1) // hp
    row_group = lax.broadcasted_iota(jnp.int32, (M_GROUPS, M_INNER), 0)
    gmask = lane_group == row_group
    ones_rows = lax.broadcasted_iota(jnp.int32, (SUBLANES, M_STATE), 0) >= M_GROUPS

    cbx = []
    for g in range(M_GROUPS):
        cbg = jnp.sum(cmat[:, g * M_STATE:(g + 1) * M_STATE] * bmat[:, g * M_STATE:(g + 1) * M_STATE],
                      axis=-1, keepdims=True)
        cbx.append(jnp.broadcast_to(cbg, (bb, hp)))
    y = jnp.concatenate(cbx, axis=1) * xdt + dsk_ref[...] * xs

    y_off_rows = []
    for b in range(bb):
        xsel = jnp.where(gmask, jnp.broadcast_to(xdt[b:b + 1, :], (M_GROUPS, M_INNER)), 0.0)
        lhs = jnp.concatenate([xsel, *_pieces3(da_e[b:b + 1, :]), jnp.zeros((1, M_INNER), F32)],
                              axis=0).astype(BF16)
        bm4 = jnp.concatenate([bmat[b:b + 1, g * M_STATE:(g + 1) * M_STATE] for g in range(M_GROUPS)], axis=0)
        cm4 = jnp.concatenate([cmat[b:b + 1, g * M_STATE:(g + 1) * M_STATE] for g in range(M_GROUPS)], axis=0)
        zeros4 = jnp.zeros((SUBLANES - M_GROUPS, M_STATE), F32)
        rhs = jnp.concatenate([jnp.concatenate([bm4, zeros4], axis=0),
                               jnp.where(ones_rows, 1.0, 0.0)], axis=1).astype(BF16)
        res = lax.dot_general(lhs, rhs, _TN, preferred_element_type=F32)
        h0 = h0_ref[b]
        hout_ref[b] = res[:, M_STATE:] * h0 + res[:, :M_STATE]
        cm8 = jnp.concatenate([cm4, zeros4], axis=0).astype(BF16)
        yo = lax.dot_general(cm8, h0.astype(BF16), _NT, preferred_element_type=F32)
        y_off_rows.append(jnp.sum(jnp.where(gmask, yo[:M_GROUPS, :], 0.0), axis=0, keepdims=True))
    y = y + da_e * jnp.concatenate(y_off_rows, axis=0)

    y_ref[...] = _rms(y * _silu(z_ref[...]), nw_ref[...])


SSD_STEP_EVERY = 2


def _s5_ssd_step_kernel(*refs, nb, tl, sub_rows):
    n5i, nsi = S5_PIECES + 6, 11
    n5o, nso = S5_PIECES + 2, 3
    ins5, inss = refs[:n5i], refs[n5i:n5i + nsi]
    o = n5i + nsi
    outs5, outss = refs[o:o + n5o], refs[o + n5o:o + n5o + nso]
    sc5 = refs[o + n5o + nso:]
    s = pl.program_id(0) * pl.num_programs(1) + pl.program_id(1)

    def side():
        @pl.when(s % SSD_STEP_EVERY == 0)
        def _():
            _ssd_step_kernel(*inss, *outss)

    _s5_kernel(*ins5, *outs5, *sc5, nb=nb, tl=tl, sub_rows=sub_rows, side=side)


def _s5_ssd_step(u, h0r, h0i, b, a, c, d, zs, xbc, dt, cbuf, h0, cw, cb, alog, dsk, nw, e64, *, tl):
    nblk, rows, _ = u.shape
    pitch = _s5_pitch(tl)
    nb = rows // pitch
    sub_rows = min(S5_SUB_ROWS, tl * nb)
    nsub = tl * nb // sub_rows
    ns = zs.shape[0]
    bb, per = SSD_STEP_BB, SSD_STEP_EVERY
    assert sub_rows % nb == 0 and tl * nb % sub_rows == 0 and nsub % 2 == 0 and ns * per == bb * S5_CHUNKS * nblk
    chunk3 = lambda sh: pl.BlockSpec((None,) + sh, lambda c, t: (c, 0, 0))
    st_spec = pl.BlockSpec((nb, S5_NS), lambda c, t: (0, c))
    nu = S5_PIECES
    u_specs = [pl.BlockSpec((None, rows, LANES), functools.partial(lambda c, t, k: (t, 0, nu * c + k), k=k))
               for k in range(nu)]
    y5_spec = pl.BlockSpec((None, rows, LANES), lambda c, t: (t, 0, c))
    y5_shape = jax.ShapeDtypeStruct((nblk, rows, D_MODEL // nu), F32)
    step = lambda c, t: c * nblk + t
    row = lambda w: pl.BlockSpec((bb, w), lambda c, t: (step(c, t) // per, 0))
    conv_spec = pl.BlockSpec((M_CONV - 1, bb, M_CONV_DIM), lambda c, t: (0, step(c, t) // per, 0))
    st3_spec = pl.BlockSpec((bb, M_INNER, M_STATE), lambda c, t: (step(c, t) // per, 0, 0))
    res = pl.pallas_call(
        functools.partial(_s5_ssd_step_kernel, nb=nb, tl=tl, sub_rows=sub_rows),
        grid=(S5_CHUNKS, nblk),
        in_specs=u_specs + [st_spec, st_spec, chunk3((S5_CH, 2 * S5_STATE)), chunk3((1, 2 * S5_NS)),
                            chunk3((2 * S5_NS, S5_GROUP)), chunk3((1, S5_CH))]
                 + [row(M_INNER), row(M_CONV_DIM), row(LANES), conv_spec, st3_spec,
                    _const_spec(cw.shape), _const_spec(cb.shape), _const_spec(alog.shape),
                    _const_spec(dsk.shape), _const_spec(nw.shape), _const_spec(e64.shape)],
        out_specs=[y5_spec] * nu + [st_spec, st_spec] + [row(M_INNER), st3_spec, conv_spec],
        out_shape=[y5_shape] * nu + [jax.ShapeDtypeStruct(h0r.shape, F32), jax.ShapeDtypeStruct(h0i.shape, F32)]
                  + [jax.ShapeDtypeStruct((ns, M_INNER), F32), jax.ShapeDtypeStruct((ns, M_INNER, M_STATE), F32),
                     jax.ShapeDtypeStruct((M_CONV - 1, ns, M_CONV_DIM), F32)],
        scratch_shapes=[pltpu.VMEM((nb, 2 * S5_NS), F32)]
                       + [pltpu.VMEM((sub_rows, 2 * S5_NS), F32), pltpu.VMEM((sub_rows, S5_CH), F32)] * 2
                       + [pltpu.VMEM((S5_CH, 2 * S5_NS), BF16), pltpu.VMEM((2 * S5_NS, S5_CH), BF16)],
        compiler_params=_params(("arbitrary", "arbitrary")),
        name="s5_ssd_step",
    )(*([u] * nu), h0r, h0i, b, a, c, d, zs, xbc, dt, cbuf, h0, cw, cb, alog, dsk, nw, e64)
    return (res[:nu], res[nu], res[nu + 1]), (res[nu + 2], res[nu + 3], res[nu + 4])


MXU_WIDTH = 256
FFN_CHUNKS = (0, 6 * MXU_WIDTH, D_FF)


def _out_kernel(x_ref, *refs):
    y5_refs = refs[:S5_PIECES]
    (yb_ref, gate_ref, gtm_ref, shf_ref, scf_ref, gtf_ref, wglu_ref, bglu_ref, wb5_ref, wbs_ref, wout_ref,
     npm_ref, npf_ref, npo_ref, wfi_ref, wfo_ref, o_ref) = refs[S5_PIECES:]
    rows = x_ref.shape[0]
    y5 = jnp.concatenate([y5_refs[k][0:rows, c * LANES:(c + 1) * LANES]
                          for c in range(S5_CHUNKS) for k in range(S5_PIECES)], axis=1)
    ya = _gelu_tanh(y5)
    glu = ya * _sigmoid(_dot(ya.astype(BF16), wglu_ref[...]) + bglu_ref[...])
    merged = (_sigmoid(gate_ref[:, :D_MODEL]) * _dot(glu.astype(BF16), wb5_ref[...])
              + _sigmoid(gate_ref[:, D_MODEL:]) * _dot(yb_ref[...].astype(BF16), wbs_ref[...]))
    mix = _dot(merged.astype(BF16), wout_ref[...])
    x1 = x_ref[...] + gtm_ref[...] * _rms(mix, npm_ref[...])

    hb = (_rms(x1, npf_ref[...]) * (1.0 + scf_ref[...]) + shf_ref[...]).astype(BF16)
    f = jnp.zeros_like(x1)
    for j, e in zip(FFN_CHUNKS[:-1], FFN_CHUNKS[1:]):
        gg = _dot(hb, wfi_ref[:, j:e])
        uu = _dot(hb, wfi_ref[:, D_FF + j:D_FF + e])
        f = f + _dot((_silu(gg) * uu).astype(BF16), wfo_ref[j:e, :])
    o_ref[...] = x1 + gtf_ref[...] * _rms(f, npo_ref[...])


def _out_stage(x, y5, yb, gates, mod, ws, *, per_row, tm, rows_per_batch):
    t = x.shape[0]
    row = lambda n: pl.BlockSpec((tm, n), lambda i: (i, 0))
    mods = _mod_specs(per_row, tm, rows_per_batch // tm if not per_row else 1, (2, 3, 4, 5))
    _, y5_spec = _s5_layout(per_row, t, tm, rows_per_batch, D_MODEL // S5_PIECES)
    return pl.pallas_call(
        _out_kernel,
        grid=(t // tm,),
        in_specs=[_x_spec(x, tm)] + [y5_spec] * S5_PIECES + [row(M_INNER), row(2 * D_MODEL)] + mods
                 + [_const_spec(w.shape) for w in ws],
        out_specs=row(D_MODEL),
        out_shape=jax.ShapeDtypeStruct((t, D_MODEL), F32),
        compiler_params=_params(("parallel",)),
        name="out_stage",
    )(x, *y5, yb, gates, mod, mod, mod, mod, *ws)


def _pad_lanes(v):
    return jnp.pad(v.reshape(1, -1), ((0, 0), (0, LANES - v.shape[-1])))


def kernel(x_prompt, x_sample, state_s5_re, state_s5_im, state_ssm, state_conv, c_prompt, c_sample, w_ada, b_ada, norm_pre_mix, norm_post_mix, norm_pre_ffn, norm_post_ffn, w_in, s5_lam_re, s5_lam_im, s5_log_dt, s5_b_re, s5_b_im, s5_c_re, s5_c_im, s5_d, s5_w_glu, s5_b_glu, m_conv_w, m_conv_b, m_dt_bias, m_a_log, m_d, m_norm, w_branch_s5, w_branch_ssd, w_out, w_ffn_in, w_ffn_out):
    bp, seq, _ = x_prompt.shape
    bs = x_sample.shape[0]
    assert x_sample.shape[1] == 1 and w_ada.shape[0] == 1 and seq % M_CHUNK == 0
    row1 = lambda v: v.reshape(1, -1)

    ws_in = _split_w_in(w_in[0])
    cw, cb = m_conv_w[0], row1(m_conv_b[0])
    dtb, alog = _pad_lanes(m_dt_bias[0]), _pad_lanes(m_a_log[0])
    dsk = row1(jnp.repeat(m_d[0], M_HEADDIM))
    nw = row1(m_norm[0])
    e64 = _expand_mat()

    mod_p, mod_s = _ada(c_prompt, c_sample, w_ada[0], b_ada[0])

    ar, ai, bbr, bbi = _s5_disc(s5_lam_re[0], s5_lam_im[0], s5_log_dt[0], s5_b_re[0], s5_b_im[0])
    s5m = _s5_matrices(ar, ai, bbr, bbi, s5_c_re[0], s5_c_im[0], s5_d[0])
    g_pre = row1(norm_pre_mix[0])

    xp = x_prompt.reshape(bp * seq, D_MODEL)
    xs = x_sample
    tm = min(ROW_TILE, seq)
    u, zs, act, dt, gates, tail = _in_proj(xp, mod_p, g_pre, ws_in, dtb, cw, cb,
                                           per_row=False, tm=tm, rows_per_batch=seq)
    u_s, zs_s, xbc_s, dt_s, gates_s = _in_proj(xs, mod_s, g_pre, ws_in, dtb, cw, cb,
                                               per_row=True, tm=bs, rows_per_batch=1)

    p_conv = tail[:, SUBLANES - (M_CONV - 1):, :]
    zeros_s5 = jnp.zeros((bp, S5_GROUPS * S5_STATE), F32)
    (y5, p_re, p_im), (yb_s, s_ssm, s_conv) = _s5_ssd_step(
        u, zeros_s5, zeros_s5, *s5m, zs_s, xbc_s, dt_s, jnp.swapaxes(state_conv[0], 0, 1),
        state_ssm[0].reshape(bs, M_INNER, M_STATE), cw, cb, alog, dsk, nw, e64, tl=tm)
    yb, p_ssm, (wglu, wb5, wbs, wo, wfi, wfo) = _ssd(
        zs, act, dt, alog, dsk, nw, e64,
        (s5_w_glu[0], w_branch_s5[0], w_branch_ssd[0], w_out[0], w_ffn_in[0], w_ffn_out[0]), nbatch=bp, seq=seq)
    ws_out = (wglu, row1(s5_b_glu[0]), wb5, wbs, wo, row1(norm_post_mix[0]),
              row1(norm_pre_ffn[0]), row1(norm_post_ffn[0]), wfi, wfo)
    y_prompt = _out_stage(xp, y5, yb, gates, mod_p, ws_out, per_row=False, tm=tm, rows_per_batch=seq)

    states_t = bs == LANES
    to_kernel = lambda v: (jnp.transpose(v, (1, 2, 0)).reshape(-1, bs) if states_t else v.reshape(bs, -1))
    from_kernel = lambda v: (jnp.transpose(v.reshape(S5_GROUPS, S5_STATE, bs), (2, 0, 1)) if states_t
                             else v.reshape(bs, S5_GROUPS, S5_STATE))[None]
    y5_s, s_re, s_im = _s5(u_s, to_kernel(state_s5_re[0]), to_kernel(state_s5_im[0]), *s5m, tl=1,
                           states_t=states_t)
    y_sample = _out_stage(xs, y5_s, yb_s, gates_s, mod_s, ws_out, per_row=True, tm=bs, rows_per_batch=1)

    s5_shape = (1, -1, S5_GROUPS, S5_STATE)
    ssm_shape = (1, -1, M_HEADS, M_HEADDIM, M_STATE)
    return (y_prompt.reshape(bp, seq, D_MODEL), y_sample.reshape(bs, 1, D_MODEL),
            p_re.reshape(s5_shape), p_im.reshape(s5_shape), p_ssm.reshape(ssm_shape), p_conv[None],
            from_kernel(s_re), from_kernel(s_im), s_ssm.reshape(ssm_shape), jnp.swapaxes(s_conv, 0, 1)[None])
```
